```python
import jax, jax.numpy as jnp
from jax import lax
import numpy as np

D_MODEL = 1024
BATCH = 8
SEQ = 16384
DEPTH = 4

D_PLE = 256
D_FF = 4 * D_MODEL
HEAD_DIM = 64
D_A = 3 * D_MODEL // 8
D_B = 3 * D_MODEL // 8
D_C = D_MODEL - D_A - D_B
N_HEADS_A = D_A // HEAD_DIM
N_HEADS_B = D_B // HEAD_DIM
POOL_WINDOWS = (2, 4, 8, 16)
N_POOL_GROUPS = len(POOL_WINDOWS)
D_POOL_GROUP = D_C // N_POOL_GROUPS
CHUNK = 128
CONV_WIDTH = 3
D_IN = 2 * D_A + 3 * D_B + D_C
SPLITS = (D_A, 2 * D_A, 2 * D_A + D_B, 2 * D_A + 2 * D_B, 2 * D_A + 3 * D_B)
RMS_EPS = 1e-6
LN_EPS = 1e-5

kernel_name = "hybrid_sgu_conv_pool_trunk"


def rms_norm(x, g):
    xf = x.astype(jnp.float32)
    y = xf * lax.rsqrt(jnp.mean(xf * xf, axis=-1, keepdims=True) + RMS_EPS)
    return (y * g.astype(jnp.float32)).astype(x.dtype)


def spatial_gating(u, v, w_s, b_s, ln_g, ln_b):
    bsz, t, _ = u.shape
    n = t // CHUNK
    u = jax.nn.gelu(u, approximate=False)
    v = jax.nn.gelu(v, approximate=False)
    vf = v.reshape(bsz, n, CHUNK, N_HEADS_A, HEAD_DIM).astype(jnp.float32)
    mu = jnp.mean(vf, axis=-1, keepdims=True)
    var = jnp.mean(jnp.square(vf - mu), axis=-1, keepdims=True)
    vn = ((vf - mu) * lax.rsqrt(var + LN_EPS)
          * ln_g.reshape(N_HEADS_A, HEAD_DIM).astype(jnp.float32)
          + ln_b.reshape(N_HEADS_A, HEAD_DIM).astype(jnp.float32)).astype(u.dtype)
    mask = jnp.tril(jnp.ones((CHUNK, CHUNK), dtype=bool))
    w = jnp.where(mask[None], w_s, jnp.zeros((), w_s.dtype)).astype(u.dtype)
    mixed = jnp.einsum('hts,bnshd->bnthd', w, vn) + b_s.T.astype(u.dtype)[:, :, None]
    return u * mixed.reshape(bsz, t, D_A)


def short_conv(z, gate_b, gate_c, conv_w):
    h = gate_c * z
    y = lax.conv_general_dilated(
        h, conv_w.astype(h.dtype)[:, None, :], window_strides=(1,),
        padding=[(CONV_WIDTH - 1, 0)],
        dimension_numbers=('NWC', 'WIO', 'NWC'), feature_group_count=D_B)
    return gate_b * y


def multiscale_pool(z, w_pool, pool_scale):
    bsz, t, _ = z.shape
    zf = z.astype(jnp.float32)
    cs = jnp.cumsum(zf, axis=1)
    pos_count = jnp.arange(1, t + 1, dtype=jnp.float32)
    outs = []
    for g, win in enumerate(POOL_WINDOWS):
        sl = slice(g * D_POOL_GROUP, (g + 1) * D_POOL_GROUP)
        c = cs[..., sl]
        lag = jnp.pad(c, ((0, 0), (win, 0), (0, 0)))[:, :t]
        mean = (c - lag) / jnp.minimum(pos_count, float(win))[None, :, None]
        outs.append(mean - zf[..., sl])
    pooled = jnp.stack(outs, axis=2).astype(z.dtype)
    y = jnp.einsum('btgc,gcd->btgd', pooled, w_pool)
    return y.reshape(bsz, t, D_C) * pool_scale


def _fwd_setup_inputs(seed: int = 0) -> dict:
    key = jax.random.key(seed)
    ks = jax.random.split(key, 20)
    f32 = jnp.float32
    nrm = lambda k, shape, scale: jax.random.normal(k, shape, f32) * scale
    return {
        "x": nrm(ks[0], (BATCH, SEQ, D_MODEL), 1.0),
        "p": nrm(ks[1], (DEPTH, BATCH, SEQ, D_PLE), 1.0),
        "norm_mix_g": 1.0 + nrm(ks[2], (DEPTH, D_MODEL), 0.05),
        "w_in": nrm(ks[3], (DEPTH, D_MODEL, D_IN), D_MODEL ** -0.5),
        "sgu_w": nrm(ks[4], (DEPTH, N_HEADS_A, CHUNK, CHUNK), CHUNK ** -0.5),
        "sgu_b": 1.0 + nrm(ks[5], (DEPTH, N_HEADS_A, CHUNK), 0.1),
        "sgu_ln_g": 1.0 + nrm(ks[6], (DEPTH, D_A), 0.05),
        "sgu_ln_b": nrm(ks[7], (DEPTH, D_A), 0.02),
        "conv_w": nrm(ks[8], (DEPTH, CONV_WIDTH, D_B), CONV_WIDTH ** -0.5),
        "pool_w": nrm(ks[9], (DEPTH, N_POOL_GROUPS, D_POOL_GROUP, D_POOL_GROUP), D_POOL_GROUP ** -0.5),
        "pool_scale": 1.0 + nrm(ks[10], (DEPTH, D_C), 0.1),
        "w_out": nrm(ks[11], (DEPTH, D_MODEL, D_MODEL), D_MODEL ** -0.5),
        "norm_ff_g": 1.0 + nrm(ks[12], (DEPTH, D_MODEL), 0.05),
        "w_ff1": nrm(ks[13], (DEPTH, D_MODEL, D_FF), D_MODEL ** -0.5),
        "w_ff2": nrm(ks[14], (DEPTH, D_FF, D_MODEL), D_FF ** -0.5),
        "norm_ple_g": 1.0 + nrm(ks[15], (DEPTH, D_MODEL), 0.05),
        "w_ple_gate": nrm(ks[16], (DEPTH, D_MODEL, D_MODEL), D_MODEL ** -0.5),
        "w_ple_proj": nrm(ks[17], (DEPTH, D_PLE, D_MODEL), D_PLE ** -0.5),
        "final_g": 1.0 + nrm(ks[18], (D_MODEL,), 0.05),
    }


def _fwd_reference(x, p, norm_mix_g, w_in, sgu_w, sgu_b, sgu_ln_g, sgu_ln_b, conv_w,
              pool_w, pool_scale, w_out, norm_ff_g, w_ff1, w_ff2, norm_ple_g,
              w_ple_gate, w_ple_proj, final_g):
    for i in range(DEPTH):
        h = rms_norm(x, norm_mix_g[i])
        proj = h @ w_in[i]
        u_a, v_a, z_b, g_b, g_c, z_c = jnp.split(proj, SPLITS, axis=-1)
        y_a = spatial_gating(u_a, v_a, sgu_w[i], sgu_b[i], sgu_ln_g[i], sgu_ln_b[i])
        y_b = short_conv(z_b, g_b, g_c, conv_w[i])
        y_c = multiscale_pool(z_c, pool_w[i], pool_scale[i])
        x = x + jnp.concatenate([y_a, y_b, y_c], axis=-1) @ w_out[i]
        h = rms_norm(x, norm_ff_g[i])
        x = x + jnp.square(jax.nn.relu(h @ w_ff1[i])) @ w_ff2[i]
        gate = jax.nn.sigmoid(rms_norm(x, norm_ple_g[i]) @ w_ple_gate[i])
        x = x + (p[i] @ w_ple_proj[i]) * gate
    return rms_norm(x, final_g)


import jax as _jax
import jax.numpy as _jnp

TWIN_FORMAT = 'train_step'
FWD_PARAMS = ['x', 'p', 'norm_mix_g', 'w_in', 'sgu_w', 'sgu_b', 'sgu_ln_g', 'sgu_ln_b', 'conv_w', 'pool_w', 'pool_scale', 'w_out', 'norm_ff_g', 'w_ff1', 'w_ff2', 'norm_ple_g', 'w_ple_gate', 'w_ple_proj', 'final_g']
TWIN_WEIGHTS = ['norm_mix_g', 'w_in', 'sgu_w', 'sgu_b', 'sgu_ln_g', 'sgu_ln_b', 'conv_w', 'pool_w', 'pool_scale', 'w_out', 'norm_ff_g', 'w_ff1', 'w_ff2', 'norm_ple_g', 'w_ple_gate', 'w_ple_proj', 'final_g']
TWIN_DIFF_INPUT = 'x'
TWIN_INPUTS = ['x', 'p', 'norm_mix_g', 'w_in', 'sgu_w', 'sgu_b', 'sgu_ln_g', 'sgu_ln_b', 'conv_w', 'pool_w', 'pool_scale', 'w_out', 'norm_ff_g', 'w_ff1', 'w_ff2', 'norm_ple_g', 'w_ple_gate', 'w_ple_proj', 'final_g', 'loss_target', 'm_norm_mix_g', 'm_w_in', 'm_sgu_w', 'm_sgu_b', 'm_sgu_ln_g', 'm_sgu_ln_b', 'm_conv_w', 'm_pool_w', 'm_pool_scale', 'm_w_out', 'm_norm_ff_g', 'm_w_ff1', 'm_w_ff2', 'm_norm_ple_g', 'm_w_ple_gate', 'm_w_ple_proj', 'm_final_g', 'v_norm_mix_g', 'v_w_in', 'v_sgu_w', 'v_sgu_b', 'v_sgu_ln_g', 'v_sgu_ln_b', 'v_conv_w', 'v_pool_w', 'v_pool_scale', 'v_w_out', 'v_norm_ff_g', 'v_w_ff1', 'v_w_ff2', 'v_norm_ple_g', 'v_w_ple_gate', 'v_w_ple_proj', 'v_final_g']
TWIN_OUTPUTS = ['loss', 'grad_x', 'grad_norm_mix_g', 'grad_w_in', 'grad_sgu_w', 'grad_sgu_b', 'grad_sgu_ln_g', 'grad_sgu_ln_b', 'grad_conv_w', 'grad_pool_w', 'grad_pool_scale', 'grad_w_out', 'grad_norm_ff_g', 'grad_w_ff1', 'grad_w_ff2', 'grad_norm_ple_g', 'grad_w_ple_gate', 'grad_w_ple_proj', 'grad_final_g', 'delta_norm_mix_g', 'delta_w_in', 'delta_sgu_w', 'delta_sgu_b', 'delta_sgu_ln_g', 'delta_sgu_ln_b', 'delta_conv_w', 'delta_pool_w', 'delta_pool_scale', 'delta_w_out', 'delta_norm_ff_g', 'delta_w_ff1', 'delta_w_ff2', 'delta_norm_ple_g', 'delta_w_ple_gate', 'delta_w_ple_proj', 'delta_final_g', 'new_m_norm_mix_g', 'new_m_w_in', 'new_m_sgu_w', 'new_m_sgu_b', 'new_m_sgu_ln_g', 'new_m_sgu_ln_b', 'new_m_conv_w', 'new_m_pool_w', 'new_m_pool_scale', 'new_m_w_out', 'new_m_norm_ff_g', 'new_m_w_ff1', 'new_m_w_ff2', 'new_m_norm_ple_g', 'new_m_w_ple_gate', 'new_m_w_ple_proj', 'new_m_final_g', 'new_v_norm_mix_g', 'new_v_w_in', 'new_v_sgu_w', 'new_v_sgu_b', 'new_v_sgu_ln_g', 'new_v_sgu_ln_b', 'new_v_conv_w', 'new_v_pool_w', 'new_v_pool_scale', 'new_v_w_out', 'new_v_norm_ff_g', 'new_v_w_ff1', 'new_v_w_ff2', 'new_v_norm_ple_g', 'new_v_w_ple_gate', 'new_v_w_ple_proj', 'new_v_final_g']
TWIN_LEAF_KINDS = {'loss': 'loss', 'grad_x': 'grad_x', 'grad_norm_mix_g': 'grad_w', 'grad_w_in': 'grad_w', 'grad_sgu_w': 'grad_w', 'grad_sgu_b': 'grad_w', 'grad_sgu_ln_g': 'grad_w', 'grad_sgu_ln_b': 'grad_w', 'grad_conv_w': 'grad_w', 'grad_pool_w': 'grad_w', 'grad_pool_scale': 'grad_w', 'grad_w_out': 'grad_w', 'grad_norm_ff_g': 'grad_w', 'grad_w_ff1': 'grad_w', 'grad_w_ff2': 'grad_w', 'grad_norm_ple_g': 'grad_w', 'grad_w_ple_gate': 'grad_w', 'grad_w_ple_proj': 'grad_w', 'grad_final_g': 'grad_w', 'delta_norm_mix_g': 'delta_w', 'delta_w_in': 'delta_w', 'delta_sgu_w': 'delta_w', 'delta_sgu_b': 'delta_w', 'delta_sgu_ln_g': 'delta_w', 'delta_sgu_ln_b': 'delta_w', 'delta_conv_w': 'delta_w', 'delta_pool_w': 'delta_w', 'delta_pool_scale': 'delta_w', 'delta_w_out': 'delta_w', 'delta_norm_ff_g': 'delta_w', 'delta_w_ff1': 'delta_w', 'delta_w_ff2': 'delta_w', 'delta_norm_ple_g': 'delta_w', 'delta_w_ple_gate': 'delta_w', 'delta_w_ple_proj': 'delta_w', 'delta_final_g': 'delta_w', 'new_m_norm_mix_g': 'new_m', 'new_m_w_in': 'new_m', 'new_m_sgu_w': 'new_m', 'new_m_sgu_b': 'new_m', 'new_m_sgu_ln_g': 'new_m', 'new_m_sgu_ln_b': 'new_m', 'new_m_conv_w': 'new_m', 'new_m_pool_w': 'new_m', 'new_m_pool_scale': 'new_m', 'new_m_w_out': 'new_m', 'new_m_norm_ff_g': 'new_m', 'new_m_w_ff1': 'new_m', 'new_m_w_ff2': 'new_m', 'new_m_norm_ple_g': 'new_m', 'new_m_w_ple_gate': 'new_m', 'new_m_w_ple_proj': 'new_m', 'new_m_final_g': 'new_m', 'new_v_norm_mix_g': 'new_v', 'new_v_w_in': 'new_v', 'new_v_sgu_w': 'new_v', 'new_v_sgu_b': 'new_v', 'new_v_sgu_ln_g': 'new_v', 'new_v_sgu_ln_b': 'new_v', 'new_v_conv_w': 'new_v', 'new_v_pool_w': 'new_v', 'new_v_pool_scale': 'new_v', 'new_v_w_out': 'new_v', 'new_v_norm_ff_g': 'new_v', 'new_v_w_ff1': 'new_v', 'new_v_w_ff2': 'new_v', 'new_v_norm_ple_g': 'new_v', 'new_v_w_ple_gate': 'new_v', 'new_v_w_ple_proj': 'new_v', 'new_v_final_g': 'new_v'}


def _forward(args):
    return _fwd_reference(*[args[k] for k in FWD_PARAMS])


def _output_shape():
    def fwd():
        inp = _fwd_setup_inputs(0)
        return _fwd_reference(*[inp[k] for k in FWD_PARAMS])
    out = _jax.eval_shape(fwd)
    return out.shape, out.dtype

N_MICROBATCH = 1
ADAM_LR = 0.001
ADAM_B1 = 0.9
ADAM_B2 = 0.999
ADAM_EPS = 1e-08
ADAM_WD = 0.01
ADAM_STEP = 10
PER_EXAMPLE_BATCH_AXIS = {'x': 0, 'p': 1, 'loss_target': 0}
SHARED_INPUTS = []
_WEIGHT_DTYPES = {'norm_mix_g': _jnp.float32, 'w_in': _jnp.float32, 'sgu_w': _jnp.float32, 'sgu_b': _jnp.float32, 'sgu_ln_g': _jnp.float32, 'sgu_ln_b': _jnp.float32, 'conv_w': _jnp.float32, 'pool_w': _jnp.float32, 'pool_scale': _jnp.float32, 'w_out': _jnp.float32, 'norm_ff_g': _jnp.float32, 'w_ff1': _jnp.float32, 'w_ff2': _jnp.float32, 'norm_ple_g': _jnp.float32, 'w_ple_gate': _jnp.float32, 'w_ple_proj': _jnp.float32, 'final_g': _jnp.float32}
MOMENT_SCALE = {'norm_mix_g': 3.065255e-01, 'w_in': 2.001146e-01, 'sgu_w': 7.253987e-02, 'sgu_b': 1.084523e-01, 'sgu_ln_g': 1.168372e-01, 'sgu_ln_b': 1.056976e-01, 'conv_w': 2.356726e-01, 'pool_w': 1.988403e-01, 'pool_scale': 2.369402e-01, 'w_out': 2.366399e-01, 'norm_ff_g': 2.875773e-01, 'w_ff1': 1.380147e-01, 'w_ff2': 4.465301e-01, 'norm_ple_g': 3.906434e-02, 'w_ple_gate': 3.880249e-02, 'w_ple_proj': 9.042440e-02, 'final_g': 1.298462e+02}


def _to_microbatches(a, axis):
    t = _jnp.moveaxis(a, axis, 0)
    t = t.reshape((N_MICROBATCH, t.shape[0] // N_MICROBATCH) + t.shape[1:])
    return _jnp.moveaxis(t, 1, axis + 1)


def setup_inputs(seed: int = 0) -> dict:
    inp = _fwd_setup_inputs(seed)
    key = _jax.random.fold_in(_jax.random.key(seed), 7919)
    shape, _ = _output_shape()
    out = dict(inp)
    out["loss_target"] = _jax.random.normal(_jax.random.fold_in(key, 0), shape, _jnp.float32)
    for i, name in enumerate(TWIN_WEIGHTS):
        w = inp[name].astype(_jnp.float32)
        if MOMENT_SCALE is None:
            s = _jnp.sqrt(_jnp.mean(_jnp.square(w)) + 1e-30)
        else:
            s = MOMENT_SCALE[name]
        km, kv = _jax.random.split(_jax.random.fold_in(key, i + 1))
        out[name] = w
        out["m_" + name] = s * _jax.random.normal(km, w.shape, _jnp.float32)
        out["v_" + name] = (s * s) * _jax.random.uniform(kv, w.shape, _jnp.float32, 0.5, 1.5)
    if N_MICROBATCH > 1:
        for name, axis in PER_EXAMPLE_BATCH_AXIS.items():
            out[name] = _to_microbatches(out[name], axis)
    return {'x': out['x'], 'p': out['p'], 'norm_mix_g': out['norm_mix_g'], 'w_in': out['w_in'], 'sgu_w': out['sgu_w'], 'sgu_b': out['sgu_b'], 'sgu_ln_g': out['sgu_ln_g'], 'sgu_ln_b': out['sgu_ln_b'], 'conv_w': out['conv_w'], 'pool_w': out['pool_w'], 'pool_scale': out['pool_scale'], 'w_out': out['w_out'], 'norm_ff_g': out['norm_ff_g'], 'w_ff1': out['w_ff1'], 'w_ff2': out['w_ff2'], 'norm_ple_g': out['norm_ple_g'], 'w_ple_gate': out['w_ple_gate'], 'w_ple_proj': out['w_ple_proj'], 'final_g': out['final_g'], 'loss_target': out['loss_target'], 'm_norm_mix_g': out['m_norm_mix_g'], 'm_w_in': out['m_w_in'], 'm_sgu_w': out['m_sgu_w'], 'm_sgu_b': out['m_sgu_b'], 'm_sgu_ln_g': out['m_sgu_ln_g'], 'm_sgu_ln_b': out['m_sgu_ln_b'], 'm_conv_w': out['m_conv_w'], 'm_pool_w': out['m_pool_w'], 'm_pool_scale': out['m_pool_scale'], 'm_w_out': out['m_w_out'], 'm_norm_ff_g': out['m_norm_ff_g'], 'm_w_ff1': out['m_w_ff1'], 'm_w_ff2': out['m_w_ff2'], 'm_norm_ple_g': out['m_norm_ple_g'], 'm_w_ple_gate': out['m_w_ple_gate'], 'm_w_ple_proj': out['m_w_ple_proj'], 'm_final_g': out['m_final_g'], 'v_norm_mix_g': out['v_norm_mix_g'], 'v_w_in': out['v_w_in'], 'v_sgu_w': out['v_sgu_w'], 'v_sgu_b': out['v_sgu_b'], 'v_sgu_ln_g': out['v_sgu_ln_g'], 'v_sgu_ln_b': out['v_sgu_ln_b'], 'v_conv_w': out['v_conv_w'], 'v_pool_w': out['v_pool_w'], 'v_pool_scale': out['v_pool_scale'], 'v_w_out': out['v_w_out'], 'v_norm_ff_g': out['v_norm_ff_g'], 'v_w_ff1': out['v_w_ff1'], 'v_w_ff2': out['v_w_ff2'], 'v_norm_ple_g': out['v_norm_ple_g'], 'v_w_ple_gate': out['v_w_ple_gate'], 'v_w_ple_proj': out['v_w_ple_proj'], 'v_final_g': out['v_final_g']}


def _loss(weights, diff, rest, loss_target):
    with _jax.named_scope("forward"):
        args = {**rest, TWIN_DIFF_INPUT: diff, **{k: w.astype(_WEIGHT_DTYPES[k]) for k, w in weights.items()}}
        y = _forward(args)
    with _jax.named_scope("loss_head"):
        err = _jnp.square(y.astype(_jnp.float32) - loss_target)
        return 0.5 * _jnp.sum(_jnp.mean(err, axis=-1)) if err.ndim else 0.5 * err


def _adamw(w, g, m, v):
    m = ADAM_B1 * m + (1.0 - ADAM_B1) * g
    v = ADAM_B2 * v + (1.0 - ADAM_B2) * _jnp.square(g)
    m_hat = m / (1.0 - ADAM_B1 ** ADAM_STEP)
    v_hat = v / (1.0 - ADAM_B2 ** ADAM_STEP)
    delta = -ADAM_LR * (m_hat / (_jnp.sqrt(v_hat) + ADAM_EPS) + ADAM_WD * w)
    return delta, m, v


def reference(x, p, norm_mix_g, w_in, sgu_w, sgu_b, sgu_ln_g, sgu_ln_b, conv_w, pool_w, pool_scale, w_out, norm_ff_g, w_ff1, w_ff2, norm_ple_g, w_ple_gate, w_ple_proj, final_g, loss_target, m_norm_mix_g, m_w_in, m_sgu_w, m_sgu_b, m_sgu_ln_g, m_sgu_ln_b, m_conv_w, m_pool_w, m_pool_scale, m_w_out, m_norm_ff_g, m_w_ff1, m_w_ff2, m_norm_ple_g, m_w_ple_gate, m_w_ple_proj, m_final_g, v_norm_mix_g, v_w_in, v_sgu_w, v_sgu_b, v_sgu_ln_g, v_sgu_ln_b, v_conv_w, v_pool_w, v_pool_scale, v_w_out, v_norm_ff_g, v_w_ff1, v_w_ff2, v_norm_ple_g, v_w_ple_gate, v_w_ple_proj, v_final_g):
    given = dict(x=x, p=p, norm_mix_g=norm_mix_g, w_in=w_in, sgu_w=sgu_w, sgu_b=sgu_b, sgu_ln_g=sgu_ln_g, sgu_ln_b=sgu_ln_b, conv_w=conv_w, pool_w=pool_w, pool_scale=pool_scale, w_out=w_out, norm_ff_g=norm_ff_g, w_ff1=w_ff1, w_ff2=w_ff2, norm_ple_g=norm_ple_g, w_ple_gate=w_ple_gate, w_ple_proj=w_ple_proj, final_g=final_g, loss_target=loss_target, m_norm_mix_g=m_norm_mix_g, m_w_in=m_w_in, m_sgu_w=m_sgu_w, m_sgu_b=m_sgu_b, m_sgu_ln_g=m_sgu_ln_g, m_sgu_ln_b=m_sgu_ln_b, m_conv_w=m_conv_w, m_pool_w=m_pool_w, m_pool_scale=m_pool_scale, m_w_out=m_w_out, m_norm_ff_g=m_norm_ff_g, m_w_ff1=m_w_ff1, m_w_ff2=m_w_ff2, m_norm_ple_g=m_norm_ple_g, m_w_ple_gate=m_w_ple_gate, m_w_ple_proj=m_w_ple_proj, m_final_g=m_final_g, v_norm_mix_g=v_norm_mix_g, v_w_in=v_w_in, v_sgu_w=v_sgu_w, v_sgu_b=v_sgu_b, v_sgu_ln_g=v_sgu_ln_g, v_sgu_ln_b=v_sgu_ln_b, v_conv_w=v_conv_w, v_pool_w=v_pool_w, v_pool_scale=v_pool_scale, v_w_out=v_w_out, v_norm_ff_g=v_norm_ff_g, v_w_ff1=v_w_ff1, v_w_ff2=v_w_ff2, v_norm_ple_g=v_norm_ple_g, v_w_ple_gate=v_w_ple_gate, v_w_ple_proj=v_w_ple_proj, v_final_g=v_final_g)
    weights = {n: given[n] for n in TWIN_WEIGHTS}
    shared = {n: given[n] for n in SHARED_INPUTS}
    per_example = {n: given[n] for n in ['x', 'p']}
    grad_fn = _jax.value_and_grad(_loss, argnums=(0, 1))

    def one_microbatch(ex, loss_target):
        ex = dict(ex)
        diff = ex.pop(TWIN_DIFF_INPUT)
        return grad_fn(weights, diff, {**shared, **ex}, loss_target)

    if N_MICROBATCH == 1:
        loss, (grad_w, grad_x) = one_microbatch(per_example, given["loss_target"])
    else:
        def body(carry, xs):
            loss_sum, grad_sum = carry
            l_k, (gw_k, gx_k) = one_microbatch(xs[0], xs[1])
            with _jax.named_scope("update"):
                return (loss_sum + l_k, _jax.tree.map(_jnp.add, grad_sum, gw_k)), gx_k

        init = (_jnp.zeros((), _jnp.float32), _jax.tree.map(_jnp.zeros_like, weights))
        (loss, grad_w), grad_x = _jax.lax.scan(body, init, (per_example, given["loss_target"]))
    with _jax.named_scope("update"):
        delta_w, new_m, new_v = {}, {}, {}
        for n in TWIN_WEIGHTS:
            delta_w[n], new_m[n], new_v[n] = _adamw(weights[n], grad_w[n], given["m_" + n], given["v_" + n])
    return (loss, grad_x, *[grad_w[n] for n in TWIN_WEIGHTS], *[delta_w[n] for n in TWIN_WEIGHTS],
            *[new_m[n] for n in TWIN_WEIGHTS], *[new_v[n] for n in TWIN_WEIGHTS])
```

```python
import functools
import math

import jax
import jax.numpy as jnp
from jax import lax
from jax.experimental import pallas as pl
from jax.experimental.pallas import tpu as pltpu

F32 = jnp.float32
BF16 = jnp.bfloat16

D = 1024
D_IN = 2176
D_A = 384
D_B = 384
D_C = 256
D_FF = 4096
D_PLE = 256
DEPTH = 4
CHUNK = 128
HALO = 16
FF_BLK = 1024
N_DEV = 8
RMS_EPS = 1e-6
LN_EPS = 1e-5
ADAM_LR = 0.001
ADAM_B1 = 0.9
ADAM_B2 = 0.999
ADAM_EPS = 1e-08
ADAM_WD = 0.01
ADAM_STEP = 10

TM_MIX_FWD = 512
TM_FFN_FWD = 512
TM_LOSS = 512
TM_PLE_BWD = 512
TM_FFN_BWD = 256
TM_MIX_BWD = 256
V7X_VMEM_LIMIT = 56 * 1024 * 1024

ANY = pl.BlockSpec(memory_space=pl.ANY)
MESH = pl.DeviceIdType.MESH


def _params(vmem=V7X_VMEM_LIMIT):
    return pltpu.CompilerParams(dimension_semantics=("arbitrary",), vmem_limit_bytes=vmem)


def _full(shape):
    nd = len(shape)
    return pl.BlockSpec(shape, lambda i: (0,) * nd)


def _rows(tm, cols):
    return pl.BlockSpec((tm, cols), lambda i: (i, 0))


def _mm(a, b):
    return jnp.dot(a, b, preferred_element_type=F32)


def _mm_nt(a, b):
    return lax.dot_general(a, b, (((1,), (1,)), ((), ())), preferred_element_type=F32)


def _mm_tn(a, b):
    return lax.dot_general(a, b, (((0,), (0,)), ((), ())), preferred_element_type=F32)


def _erf(x):
    ax = jnp.abs(x)
    t = 1.0 / (1.0 + 0.3275911 * ax)
    poly = t * (0.254829592 + t * (-0.284496736 + t * (1.421413741 + t * (-1.453152027 + t * 1.061405429))))
    y = 1.0 - poly * jnp.exp(-ax * ax)
    return jnp.where(x < 0, -y, y)


def _gelu_and_grad(x):
    cdf = 0.5 * (1.0 + _erf(x * (1.0 / math.sqrt(2.0))))
    pdf = jnp.exp(-0.5 * x * x) * (1.0 / math.sqrt(2.0 * math.pi))
    return x * cdf, cdf + x * pdf


def _rms(x, g):
    rstd = lax.rsqrt(jnp.mean(x * x, axis=-1, keepdims=True) + RMS_EPS)
    xhat = x * rstd
    return xhat * g, xhat, rstd


def _rms_bwd(dy, g, xhat, rstd):
    dg = jnp.sum(dy * xhat, axis=0, keepdims=True)
    dxh = dy * g
    dx = rstd * (dxh - xhat * jnp.mean(dxh * xhat, axis=-1, keepdims=True))
    return dx, dg


def _shift_down(ext, k):
    return pltpu.roll(ext, k, 0)[HALO:, :]


def _shift_up(ext, k):
    n = ext.shape[0]
    return pltpu.roll(ext, n - k, 0)[: n - HALO, :]


def _pool_select(s2, s4, s8, s16):
    lane = lax.broadcasted_iota(jnp.int32, s2.shape, 1)
    return jnp.where(lane < 64, s2, jnp.where(lane < 128, s4, jnp.where(lane < 192, s8, s16)))


def _pool_inv_count(tile_start, tm):
    pos = lax.broadcasted_iota(jnp.int32, (tm, D_C), 0) + tile_start + 1
    lane = lax.broadcasted_iota(jnp.int32, (tm, D_C), 1)
    win = jnp.where(lane < 64, 2, jnp.where(lane < 128, 4, jnp.where(lane < 192, 8, 16)))
    return 1.0 / jnp.minimum(pos, win).astype(F32)


def _head_halves(a):
    lane = lax.broadcasted_iota(jnp.int32, a.shape, 1)
    even = (lane & 64) == 0
    return jnp.where(even, a, 0.0).astype(BF16), jnp.where(even, 0.0, a).astype(BF16)


def _head_stack(lo, hi, j, nch):
    return jnp.concatenate(
        [jnp.concatenate([lo[c * CHUNK:(c + 1) * CHUNK, j * 128:(j + 1) * 128], hi[c * CHUNK:(c + 1) * CHUNK, j * 128:(j + 1) * 128]], axis=0)
         for c in range(nch)], axis=1)


def _chunks_to_lanes(a, j, nch):
    return jnp.concatenate([a[c * CHUNK:(c + 1) * CHUNK, j * 128:(j + 1) * 128] for c in range(nch)], axis=1)


def _lanes_to_chunks(o, nch):
    return jnp.concatenate([o[:, c * CHUNK:(c + 1) * CHUNK] for c in range(nch)], axis=0)


def _mixers_fwd(pf, halo_hc, halo_zc, tile_start, prm):
    tm = pf.shape[0]
    nch = tm // CHUNK
    u, v = pf[:, 0:D_A], pf[:, D_A:2 * D_A]
    zb, gb, gc = pf[:, 768:1152], pf[:, 1152:1536], pf[:, 1536:1920]
    zc = pf[:, 1920:2176]
    r = {}
    gu, r["dgelu_u"] = _gelu_and_grad(u)
    gv, r["dgelu_v"] = _gelu_and_grad(v)
    pmat = prm["pmat"][...]
    mu = _mm(gv.astype(BF16), pmat)
    dv = gv - mu
    var = _mm((dv * dv).astype(BF16), pmat)
    rstd = lax.rsqrt(var + LN_EPS)
    xh = dv * rstd
    vlo, vhi = _head_halves(xh * prm["ln_g"][...] + prm["ln_b"][...])
    cols, v2s = [], []
    for j in range(3):
        v2 = _head_stack(vlo, vhi, j, nch)
        v2s.append(v2)
        cols.append(_lanes_to_chunks(_mm(prm["wcat"][j], v2), nch))
    mixed = jnp.concatenate(cols, axis=1) + prm["bmat"][...]
    ya = gu * mixed
    r.update(gu=gu, mixed=mixed, v2s=v2s, xh=xh, ln_rstd=rstd)
    w0, w1, w2 = prm["conv_w"][0:1, :], prm["conv_w"][1:2, :], prm["conv_w"][2:3, :]
    hc = gc * zb
    ext = jnp.concatenate([halo_hc, hc], axis=0)
    h1, h2 = _shift_down(ext, 1), _shift_down(ext, 2)
    yc = w2 * hc + w1 * h1 + w0 * h2
    yb = gb * yc
    r.update(hc=hc, h1=h1, h2=h2, yc=yc, zb=zb, gb=gb, gc=gc, w0=w0, w1=w1, w2=w2)
    ext = jnp.concatenate([halo_zc, zc], axis=0)
    s2 = ext + pltpu.roll(ext, 1, 0)
    s4 = s2 + pltpu.roll(s2, 2, 0)
    s8 = s4 + pltpu.roll(s4, 4, 0)
    s16 = s8 + pltpu.roll(s8, 8, 0)
    inv = _pool_inv_count(tile_start, tm)
    pooled = _pool_select(s2, s4, s8, s16)[HALO:, :] * inv - zc
    pooledb = pooled.astype(BF16)
    pm = _mm(pooledb, prm["bd"][...])
    scale = prm["pool_scale"][...]
    ycm = pm * scale
    r.update(inv=inv, pooledb=pooledb, pm=pm, scale=scale, zc=zc)
    r["ycat"] = jnp.concatenate([ya, yb, ycm], axis=1)
    return r


_MIX_PARAM_NAMES = ("pmat", "ln_g", "ln_b", "wcat", "bmat", "conv_w", "bd", "pool_scale")


def _mix_param_specs(tm):
    return [_full((D_A, D_A)), _full((1, D_A)), _full((1, D_A)), _full((3, CHUNK, 2 * CHUNK)), _full((tm, D_A)),
            _full((3, D_B)), _full((D_C, D_C)), _full((1, D_C))]


def _mix_fwd_call(x, g_mix, w_in, w_out, mp, tm, layer):
    t = x.shape[0]
    nt = t // tm

    def body(x_ref, g_ref, pmat, ln_g, ln_b, wcat, bmat, conv_w, bd, pool_scale, win_hbm, wout_hbm,
             x1_ref, proj_ref, win_s, wout_s, halo_hc, halo_zc):
        i = pl.program_id(0)

        @pl.when(i == 0)
        def _():
            pltpu.sync_copy(win_hbm, win_s)
            pltpu.sync_copy(wout_hbm, wout_s)
            halo_hc[...] = jnp.zeros_like(halo_hc)
            halo_zc[...] = jnp.zeros_like(halo_zc)

        prm = dict(pmat=pmat, ln_g=ln_g, ln_b=ln_b, wcat=wcat, bmat=bmat, conv_w=conv_w, bd=bd, pool_scale=pool_scale)
        xv = x_ref[...]
        h, _, _ = _rms(xv, g_ref[...])
        pf = _mm(h.astype(BF16), win_s[...])
        proj_ref[...] = pf.astype(BF16)
        r = _mixers_fwd(pf, halo_hc[...], halo_zc[...], i * tm, prm)
        halo_hc[...] = r["hc"][tm - HALO:, :]
        halo_zc[...] = r["zc"][tm - HALO:, :]
        x1_ref[...] = xv + _mm(r["ycat"].astype(BF16), wout_s[...])

    return pl.pallas_call(
        body, name=f"mix_fwd_{layer}", grid=(nt,),
        in_specs=[_rows(tm, D), _full((1, D))] + _mix_param_specs(tm) + [ANY, ANY],
        out_specs=[_rows(tm, D), _rows(tm, D_IN)],
        out_shape=[jax.ShapeDtypeStruct((t, D), F32), jax.ShapeDtypeStruct((t, D_IN), BF16)],
        scratch_shapes=[pltpu.VMEM((D, D_IN), BF16), pltpu.VMEM((D, D), BF16),
                        pltpu.VMEM((HALO, D_B), F32), pltpu.VMEM((HALO, D_C), F32)],
        compiler_params=_params(),
    )(x, g_mix, *[mp[k] for k in _MIX_PARAM_NAMES], w_in, w_out)


def _ffn_fwd_call(x1, p, g_ff, g_ple, w1, w2, wg, wp, tm, layer):
    t = x1.shape[0]
    nt = t // tm

    def body(x1_ref, p_ref, gff_ref, gple_ref, w1_hbm, w2_hbm, wg_hbm, wp_hbm,
             x2_ref, x3_ref, r_ref, gate_ref, w1_s, w2_s, wg_s, wp_s):
        i = pl.program_id(0)

        @pl.when(i == 0)
        def _():
            pltpu.sync_copy(w1_hbm, w1_s)
            pltpu.sync_copy(w2_hbm, w2_s)
            pltpu.sync_copy(wg_hbm, wg_s)
            pltpu.sync_copy(wp_hbm, wp_s)

        x1v = x1_ref[...]
        h2, _, _ = _rms(x1v, gff_ref[...])
        h2b = h2.astype(BF16)
        acc = x1v
        for j in range(D_FF // FF_BLK):
            blk = slice(j * FF_BLK, (j + 1) * FF_BLK)
            rj = jnp.maximum(_mm(h2b, w1_s[:, blk]), 0.0)
            r_ref[:, blk] = rj.astype(BF16)
            acc = acc + _mm((rj * rj).astype(BF16), w2_s[blk, :])
        x2_ref[...] = acc
        n3, _, _ = _rms(acc, gple_ref[...])
        gate = jax.nn.sigmoid(_mm(n3.astype(BF16), wg_s[...]))
        gate_ref[...] = gate.astype(BF16)
        pp = _mm(p_ref[...].astype(BF16), wp_s[...])
        x3_ref[...] = acc + pp * gate

    return pl.pallas_call(
        body, name=f"ffn_fwd_{layer}", grid=(nt,),
        in_specs=[_rows(tm, D), _rows(tm, D_PLE), _full((1, D)), _full((1, D)), ANY, ANY, ANY, ANY],
        out_specs=[_rows(tm, D), _rows(tm, D), _rows(tm, D_FF), _rows(tm, D)],
        out_shape=[jax.ShapeDtypeStruct((t, D), F32), jax.ShapeDtypeStruct((t, D), F32),
                   jax.ShapeDtypeStruct((t, D_FF), BF16), jax.ShapeDtypeStruct((t, D), BF16)],
        scratch_shapes=[pltpu.VMEM((D, D_FF), BF16), pltpu.VMEM((D_FF, D), BF16),
                        pltpu.VMEM((D, D), BF16), pltpu.VMEM((D_PLE, D), BF16)],
        compiler_params=_params(),
    )(x1, p, g_ff, g_ple, w1, w2, wg, wp)


def _loss_call(xl, target, final_g, tm):
    t = xl.shape[0]
    nt = t // tm

    def body(x_ref, t_ref, g_ref, sq_ref, dx_ref, dg_ref):
        i = pl.program_id(0)

        @pl.when(i == 0)
        def _():
            sq_ref[...] = jnp.zeros_like(sq_ref)
            dg_ref[...] = jnp.zeros_like(dg_ref)

        g = g_ref[...]
        y, xhat, rstd = _rms(x_ref[...], g)
        err = y - t_ref[...]
        sq_ref[...] += jnp.sum(err * err, axis=0, keepdims=True)
        dx, dg = _rms_bwd(err * (1.0 / D), g, xhat, rstd)
        dx_ref[...] = dx
        dg_ref[...] += dg

    return pl.pallas_call(
        body, name="loss_head", grid=(nt,),
        in_specs=[_rows(tm, D), _rows(tm, D), _full((1, D))],
        out_specs=[_full((1, D)), _rows(tm, D), _full((1, D))],
        out_shape=[jax.ShapeDtypeStruct((1, D), F32), jax.ShapeDtypeStruct((t, D), F32), jax.ShapeDtypeStruct((1, D), F32)],
        compiler_params=_params(),
    )(xl, target, final_g)


def _ple_bwd_call(dx3, x2, gate, p, g_ple, wg, wp, tm, layer):
    t = dx3.shape[0]
    nt = t // tm

    def body(dx3_ref, x2_ref, gate_ref, p_ref, g_ref, wg_hbm, wp_hbm,
             dx2_ref, dg_ref, dwg_hbm, dwp_hbm, wg_s, wp_s, dwg_acc, dwp_acc):
        i = pl.program_id(0)

        @pl.when(i == 0)
        def _():
            pltpu.sync_copy(wg_hbm, wg_s)
            pltpu.sync_copy(wp_hbm, wp_s)
            dwg_acc[...] = jnp.zeros_like(dwg_acc)
            dwp_acc[...] = jnp.zeros_like(dwp_acc)
            dg_ref[...] = jnp.zeros_like(dg_ref)

        g = g_ref[...]
        dx3v = dx3_ref[...]
        gatev = gate_ref[...].astype(F32)
        pb = p_ref[...].astype(BF16)
        pp = _mm(pb, wp_s[...])
        dwp_acc[...] += _mm_tn(pb, (dx3v * gatev).astype(BF16))
        dgpre = (dx3v * pp * gatev * (1.0 - gatev)).astype(BF16)
        n3, xhat, rstd = _rms(x2_ref[...], g)
        dwg_acc[...] += _mm_tn(n3.astype(BF16), dgpre)
        dn3 = _mm_nt(dgpre, wg_s[...])
        dx, dg = _rms_bwd(dn3, g, xhat, rstd)
        dx2_ref[...] = dx3v + dx
        dg_ref[...] += dg

        @pl.when(i == nt - 1)
        def _():
            pltpu.sync_copy(dwg_acc, dwg_hbm)
            pltpu.sync_copy(dwp_acc, dwp_hbm)

    return pl.pallas_call(
        body, name=f"ple_bwd_{layer}", grid=(nt,),
        in_specs=[_rows(tm, D), _rows(tm, D), _rows(tm, D), _rows(tm, D_PLE), _full((1, D)), ANY, ANY],
        out_specs=[_rows(tm, D), _full((1, D)), ANY, ANY],
        out_shape=[jax.ShapeDtypeStruct((t, D), F32), jax.ShapeDtypeStruct((1, D), F32),
                   jax.ShapeDtypeStruct((D, D), F32), jax.ShapeDtypeStruct((D_PLE, D), F32)],
        scratch_shapes=[pltpu.VMEM((D, D), BF16), pltpu.VMEM((D_PLE, D), BF16),
                        pltpu.VMEM((D, D), F32), pltpu.VMEM((D_PLE, D), F32)],
        compiler_params=_params(),
    )(dx3, x2, gate, p, g_ple, wg, wp)


def _ffn_bwd_hidden_call(dx2, r, w2, tm, layer):
    t = dx2.shape[0]
    nt = t // tm

    def body(dx2_ref, r_ref, w2_hbm, da_ref, dw2_hbm, w2_s, dw2_acc):
        i = pl.program_id(0)

        @pl.when(i == 0)
        def _():
            pltpu.sync_copy(w2_hbm, w2_s)
            dw2_acc[...] = jnp.zeros_like(dw2_acc)

        dxb = dx2_ref[...].astype(BF16)
        for j in range(D_FF // FF_BLK):
            blk = slice(j * FF_BLK, (j + 1) * FF_BLK)
            rj = r_ref[:, blk].astype(F32)
            ds = _mm_nt(dxb, w2_s[blk, :])
            da_ref[:, blk] = (2.0 * rj * ds).astype(BF16)
            dw2_acc[blk, :] += _mm_tn((rj * rj).astype(BF16), dxb)

        @pl.when(i == nt - 1)
        def _():
            pltpu.sync_copy(dw2_acc, dw2_hbm)

    return pl.pallas_call(
        body, name=f"ffn_bwd_hidden_{layer}", grid=(nt,),
        in_specs=[_rows(tm, D), _rows(tm, D_FF), ANY],
        out_specs=[_rows(tm, D_FF), ANY],
        out_shape=[jax.ShapeDtypeStruct((t, D_FF), BF16), jax.ShapeDtypeStruct((D_FF, D), F32)],
        scratch_shapes=[pltpu.VMEM((D_FF, D), BF16), pltpu.VMEM((D_FF, D), F32)],
        compiler_params=_params(),
    )(dx2, r, w2)


def _ffn_bwd_input_call(da, x1, dx2, g_ff, w1, tm, layer):
    t = dx2.shape[0]
    nt = t // tm

    def body(da_ref, x1_ref, dx2_ref, g_ref, w1_hbm, dx1_ref, dg_ref, dw1_hbm, w1_s, dw1_acc):
        i = pl.program_id(0)

        @pl.when(i == 0)
        def _():
            pltpu.sync_copy(w1_hbm, w1_s)
            dw1_acc[...] = jnp.zeros_like(dw1_acc)
            dg_ref[...] = jnp.zeros_like(dg_ref)

        g = g_ref[...]
        h2, xhat, rstd = _rms(x1_ref[...], g)
        h2b = h2.astype(BF16)
        dh2 = jnp.zeros((tm, D), F32)
        for j in range(D_FF // FF_BLK):
            blk = slice(j * FF_BLK, (j + 1) * FF_BLK)
            daj = da_ref[:, blk]
            dh2 = dh2 + _mm_nt(daj, w1_s[:, blk])
            dw1_acc[:, blk] += _mm_tn(h2b, daj)
        dx, dg = _rms_bwd(dh2, g, xhat, rstd)
        dx1_ref[...] = dx2_ref[...] + dx
        dg_ref[...] += dg

        @pl.when(i == nt - 1)
        def _():
            pltpu.sync_copy(dw1_acc, dw1_hbm)

    return pl.pallas_call(
        body, name=f"ffn_bwd_input_{layer}", grid=(nt,),
        in_specs=[_rows(tm, D_FF), _rows(tm, D), _rows(tm, D), _full((1, D)), ANY],
        out_specs=[_rows(tm, D), _full((1, D)), ANY],
        out_shape=[jax.ShapeDtypeStruct((t, D), F32), jax.ShapeDtypeStruct((1, D), F32), jax.ShapeDtypeStruct((D, D_FF), F32)],
        scratch_shapes=[pltpu.VMEM((D, D_FF), BF16), pltpu.VMEM((D, D_FF), F32)],
        compiler_params=_params(),
    )(da, x1, dx2, g_ff, w1)


def _mix_bwd_call(dx1, x, proj, g_mix, w_in, w_out, mp, wtcat, trilcat, headsel, tm, layer):
    t = dx1.shape[0]
    nt = t // tm
    nch = tm // CHUNK
    hb = tm // HALO

    def rev(i):
        return nt - 1 - i

    def body(dx1_ref, x_ref, proj_ref, halo_ref, g_ref, pmat, ln_g, ln_b, wcat, bmat, conv_w, bd, pool_scale,
             wtcat_ref, tril_ref, sel_ref, win_hbm, wout_hbm,
             dx_ref, dg_ref, dwcat_ref, dsb_ref, dlng_ref, dlnb_ref, dconv_ref, dbd_ref, dscale_ref, dwin_hbm, dwout_hbm,
             win_s, wout_s, dwin_acc, dwout_acc, dbm_acc, carry_yc, carry_q, dvn_s):
        i = pl.program_id(0)
        ri = nt - 1 - i

        @pl.when(i == 0)
        def _():
            pltpu.sync_copy(win_hbm, win_s)
            pltpu.sync_copy(wout_hbm, wout_s)
            for ref in (dwin_acc, dwout_acc, dbm_acc, carry_yc, carry_q, dg_ref, dwcat_ref, dlng_ref, dlnb_ref,
                        dconv_ref, dbd_ref, dscale_ref):
                ref[...] = jnp.zeros_like(ref)

        prm = dict(pmat=pmat, ln_g=ln_g, ln_b=ln_b, wcat=wcat, bmat=bmat, conv_w=conv_w, bd=bd, pool_scale=pool_scale)
        g = g_ref[...]
        h, xhat, rstd = _rms(x_ref[...], g)
        hb16 = h.astype(BF16)
        dx1v = dx1_ref[...]
        dx1b = dx1v.astype(BF16)
        pf = proj_ref[...].astype(F32)
        ph = halo_ref[...].astype(F32) * (ri > 0).astype(F32)
        r = _mixers_fwd(pf, ph[:, 1536:1920] * ph[:, 768:1152], ph[:, 1920:2176], ri * tm, prm)

        dwout_acc[...] += _mm_tn(r["ycat"].astype(BF16), dx1b)
        dycat = _mm_nt(dx1b, wout_s[...])
        dya, dyb, dyc = dycat[:, 0:D_A], dycat[:, D_A:D_A + D_B], dycat[:, D_A + D_B:D]

        dgu = dya * r["mixed"]
        dmix = dya * r["gu"]
        dmix_b = dmix.astype(BF16)
        dlo, dhi = _head_halves(dmix)
        dbm = dmix[0:CHUNK, :]
        for c in range(1, nch):
            dbm = dbm + dmix[c * CHUNK:(c + 1) * CHUNK, :]
        dbm_acc[...] += dbm
        for j in range(3):
            dm_all = _chunks_to_lanes(dmix_b, j, nch)
            dwcat_ref[j] += _mm_nt(dm_all, r["v2s"][j])
            dm2 = _head_stack(dlo, dhi, j, nch)
            dvn_s[:, j * 128:(j + 1) * 128] = _lanes_to_chunks(_mm(wtcat_ref[j], dm2), nch)
        dvn = dvn_s[...]
        xh = r["xh"]
        dlng_ref[...] += jnp.sum(dvn * xh, axis=0, keepdims=True)
        dlnb_ref[...] += jnp.sum(dvn, axis=0, keepdims=True)
        dxh = dvn * ln_g[...]
        pm_ = pmat[...]
        m1 = _mm(dxh.astype(BF16), pm_)
        m2 = _mm((dxh * xh).astype(BF16), pm_)
        dgv = r["ln_rstd"] * (dxh - m1 - xh * m2)
        du = dgu * r["dgelu_u"]
        dv = dgv * r["dgelu_v"]

        dgb = dyb * r["yc"]
        dyc2 = dyb * r["gb"]
        dconv_ref[0:1, :] += jnp.sum(dyc2 * r["h2"], axis=0, keepdims=True)
        dconv_ref[1:2, :] += jnp.sum(dyc2 * r["h1"], axis=0, keepdims=True)
        dconv_ref[2:3, :] += jnp.sum(dyc2 * r["hc"], axis=0, keepdims=True)
        ext = jnp.concatenate([dyc2, carry_yc[...]], axis=0)
        dhc = r["w2"] * dyc2 + r["w1"] * _shift_up(ext, 1) + r["w0"] * _shift_up(ext, 2)
        carry_yc[...] = dyc2[0:HALO, :]
        dgc = dhc * r["zb"]
        dzb = dhc * r["gc"]

        dscale_ref[...] += jnp.sum(dyc * r["pm"], axis=0, keepdims=True)
        dpm = (dyc * r["scale"]).astype(BF16)
        dbd_ref[...] += _mm_tn(r["pooledb"], dpm)
        dpooled = _mm_nt(dpm, bd[...])
        q = dpooled * r["inv"]
        ext = jnp.concatenate([q, carry_q[...]], axis=0)
        n = tm + HALO
        r2 = ext + pltpu.roll(ext, n - 1, 0)
        r4 = r2 + pltpu.roll(r2, n - 2, 0)
        r8 = r4 + pltpu.roll(r4, n - 4, 0)
        r16 = r8 + pltpu.roll(r8, n - 8, 0)
        dzc = _pool_select(r2, r4, r8, r16)[0:tm, :] - dpooled
        carry_q[...] = q[0:HALO, :]

        dproj = jnp.concatenate([du, dv, dzb, dgb, dgc, dzc], axis=1).astype(BF16)
        dwin_acc[...] += _mm_tn(hb16, dproj)
        dh = _mm_nt(dproj, win_s[...])
        dx, dg = _rms_bwd(dh, g, xhat, rstd)
        dx_ref[...] = dx1v + dx
        dg_ref[...] += dg

        @pl.when(i == nt - 1)
        def _():
            pltpu.sync_copy(dwin_acc, dwin_hbm)
            pltpu.sync_copy(dwout_acc, dwout_hbm)
            for j in range(3):
                dwcat_ref[j] = dwcat_ref[j] * tril_ref[...]
            acc = dbm_acc[...]
            hi = acc.astype(BF16)
            lo = (acc - hi.astype(F32)).astype(BF16)
            dsb_ref[...] = _mm(hi, sel_ref[...]) + _mm(lo, sel_ref[...])

    return pl.pallas_call(
        body, name=f"mix_bwd_{layer}", grid=(nt,),
        in_specs=[pl.BlockSpec((tm, D), lambda i: (rev(i), 0)), pl.BlockSpec((tm, D), lambda i: (rev(i), 0)),
                  pl.BlockSpec((tm, D_IN), lambda i: (rev(i), 0)),
                  pl.BlockSpec((HALO, D_IN), lambda i: (jnp.maximum(rev(i) * hb - 1, 0), 0)),
                  _full((1, D))] + _mix_param_specs(tm)
                 + [_full((3, CHUNK, 2 * CHUNK)), _full((CHUNK, 2 * CHUNK)), _full((D_A, CHUNK)), ANY, ANY],
        out_specs=[pl.BlockSpec((tm, D), lambda i: (rev(i), 0)), _full((1, D)), _full((3, CHUNK, 2 * CHUNK)),
                   _full((CHUNK, CHUNK)), _full((1, D_A)), _full((1, D_A)), _full((3, D_B)), _full((D_C, D_C)),
                   _full((1, D_C)), ANY, ANY],
        out_shape=[jax.ShapeDtypeStruct((t, D), F32), jax.ShapeDtypeStruct((1, D), F32),
                   jax.ShapeDtypeStruct((3, CHUNK, 2 * CHUNK), F32), jax.ShapeDtypeStruct((CHUNK, CHUNK), F32),
                   jax.ShapeDtypeStruct((1, D_A), F32), jax.ShapeDtypeStruct((1, D_A), F32),
                   jax.ShapeDtypeStruct((3, D_B), F32), jax.ShapeDtypeStruct((D_C, D_C), F32),
                   jax.ShapeDtypeStruct((1, D_C), F32), jax.ShapeDtypeStruct((D, D_IN), F32),
                   jax.ShapeDtypeStruct((D, D), F32)],
        scratch_shapes=[pltpu.VMEM((D, D_IN), BF16), pltpu.VMEM((D, D), BF16),
                        pltpu.VMEM((D, D_IN), F32), pltpu.VMEM((D, D), F32), pltpu.VMEM((CHUNK, D_A), F32),
                        pltpu.VMEM((HALO, D_B), F32), pltpu.VMEM((HALO, D_C), F32), pltpu.VMEM((tm, D_A), F32)],
        compiler_params=_params(),
    )(dx1, x, proj, proj, g_mix, *[mp[k] for k in _MIX_PARAM_NAMES], wtcat, trilcat, headsel, w_in, w_out)


def _coords():
    return lax.axis_index("x"), lax.axis_index("y"), lax.axis_index("c")


def _all_gather_call(shards):
    na = len(shards)

    def body(*refs):
        ins, outs = refs[:na], refs[na:2 * na]
        send_sems, recv_sems, local_sems = refs[2 * na:]
        x, y, c = _coords()
        me, sibling = (x, y, c), (x, y, 1 - c)
        chips = [(1 - x, y), (x, 1 - y), (1 - x, 1 - y)]

        def slot(a, px, py, pc):
            return outs[a].at[4 * px + 2 * py + pc]

        def copy(a, k, block, to, src=None):
            return pltpu.make_async_remote_copy(
                src_ref=slot(a, *block) if src is None else src, dst_ref=slot(a, *block),
                send_sem=send_sems.at[a, k], recv_sem=recv_sems.at[a, k], device_id=to, device_id_type=MESH)

        mine = [pltpu.make_async_copy(ins[a], slot(a, *me), local_sems.at[a]) for a in range(na)]
        for cp in mine:
            cp.start()
        first = []
        for a in range(na):
            first.append(copy(a, 0, me, sibling, src=ins[a]))
            first += [copy(a, 1 + j, me, (*chip, c), src=ins[a]) for j, chip in enumerate(chips)]
        for cp in first:
            cp.start()
        passed = []
        for j, chip in enumerate(chips):
            for a in range(na):
                copy(a, 1 + j, (*chip, c), me).wait_recv()
                fwd = copy(a, 4 + j, (*chip, c), sibling)
                fwd.start()
                passed.append(fwd)
        for a in range(na):
            copy(a, 0, sibling, me).wait_recv()
            for j, chip in enumerate(chips):
                copy(a, 4 + j, (*chip, 1 - c), me).wait_recv()
        for cp in first + passed:
            cp.wait_send()
        for cp in mine:
            cp.wait()

    return pl.pallas_call(
        body, name="weights_all_gather",
        in_specs=[ANY] * na, out_specs=[ANY] * na,
        out_shape=[jax.ShapeDtypeStruct((N_DEV,) + s.shape, s.dtype) for s in shards],
        scratch_shapes=[pltpu.SemaphoreType.DMA((na, 7)), pltpu.SemaphoreType.DMA((na, 7)), pltpu.SemaphoreType.DMA((na,))],
    )(*shards)


def _exchange_call(sends):
    na = len(sends)

    def body(*refs):
        ins, outs = refs[:na], refs[na:2 * na]
        send_sems, recv_sems, local_sems = refs[2 * na:]
        x, y, c = _coords()
        me = 4 * x + 2 * y + c
        copies = []
        for a in range(na):
            local = pltpu.make_async_copy(ins[a].at[me], outs[a].at[me], local_sems.at[a])
            local.start()
            copies.append(local)
        remote = []
        for a in range(na):
            for k in range(1, N_DEV):
                px, py, pc = x ^ (k >> 2), y ^ ((k >> 1) & 1), c ^ (k & 1)
                cp = pltpu.make_async_remote_copy(
                    src_ref=ins[a].at[4 * px + 2 * py + pc], dst_ref=outs[a].at[me],
                    send_sem=send_sems.at[a, k - 1], recv_sem=recv_sems.at[a, k - 1],
                    device_id=(px, py, pc), device_id_type=MESH)
                cp.start()
                remote.append(cp)
        for cp in remote:
            cp.wait_recv()
        for cp in remote:
            cp.wait_send()
        for cp in copies:
            cp.wait()

    return pl.pallas_call(
        body, name="grads_exchange",
        in_specs=[ANY] * na, out_specs=[ANY] * na,
        out_shape=[jax.ShapeDtypeStruct(s.shape, s.dtype) for s in sends],
        scratch_shapes=[pltpu.SemaphoreType.DMA((na, 7)), pltpu.SemaphoreType.DMA((na, 7)), pltpu.SemaphoreType.DMA((na,))],
    )(*sends)


def _small_all_reduce_call(buf):
    rows = buf.shape[0]

    def body(in_ref, out_ref, pair_ref, slots_ref, send_sems, recv_sems):
        x, y, c = _coords()
        chip = 2 * x + y
        sib = pltpu.make_async_remote_copy(src_ref=in_ref, dst_ref=pair_ref, send_sem=send_sems.at[0], recv_sem=recv_sems.at[0],
                                           device_id=(x, y, 1 - c), device_id_type=MESH)
        sib.start()
        sib.wait_recv()
        slots_ref[chip] = in_ref[...] + pair_ref[...]
        sends = []
        for k in range(1, 4):
            px, py = x ^ (k >> 1), y ^ (k & 1)
            cp = pltpu.make_async_remote_copy(src_ref=slots_ref.at[chip], dst_ref=slots_ref.at[chip],
                                              send_sem=send_sems.at[k], recv_sem=recv_sems.at[k],
                                              device_id=(px, py, c), device_id_type=MESH)
            cp.start()
            sends.append(cp)
        for cp in sends:
            cp.wait_recv()
        out_ref[...] = (slots_ref[0] + slots_ref[1]) + (slots_ref[2] + slots_ref[3])
        sib.wait_send()
        for cp in sends:
            cp.wait_send()

    vm = pl.BlockSpec(memory_space=pltpu.VMEM)
    return pl.pallas_call(
        body, name="small_grads_all_reduce",
        in_specs=[vm], out_specs=vm,
        out_shape=jax.ShapeDtypeStruct((rows, 128), F32),
        scratch_shapes=[pltpu.VMEM((rows, 128), F32), pltpu.VMEM((4, rows, 128), F32),
                        pltpu.SemaphoreType.DMA((4,)), pltpu.SemaphoreType.DMA((4,))],
        compiler_params=pltpu.CompilerParams(vmem_limit_bytes=V7X_VMEM_LIMIT),
    )(buf)


def _adamw(w, g, m, v):
    m = ADAM_B1 * m + (1.0 - ADAM_B1) * g
    v = ADAM_B2 * v + (1.0 - ADAM_B2) * (g * g)
    m_hat = m / (1.0 - ADAM_B1 ** ADAM_STEP)
    v_hat = v / (1.0 - ADAM_B2 ** ADAM_STEP)
    delta = -ADAM_LR * (m_hat / (jnp.sqrt(v_hat) + ADAM_EPS) + ADAM_WD * w)
    return delta, m, v


def _reduce_adamw_call(recv, w, m, v, name):
    _, nl, r, c = recv.shape
    rb = min(r, 256)

    def body(recv_ref, w_ref, m_ref, v_ref, g_ref, d_ref, nm_ref, nv_ref):
        g = recv_ref[0, 0].astype(F32)
        for j in range(1, N_DEV):
            g = g + recv_ref[j, 0].astype(F32)
        delta, nm, nv = _adamw(w_ref[0], g, m_ref[0], v_ref[0])
        g_ref[0] = g
        d_ref[0] = delta
        nm_ref[0] = nm
        nv_ref[0] = nv

    blk = pl.BlockSpec((1, rb, c), lambda l, i: (l, i, 0))
    shp = jax.ShapeDtypeStruct((nl, r, c), F32)
    return pl.pallas_call(
        body, name=name, grid=(nl, r // rb),
        in_specs=[pl.BlockSpec((N_DEV, 1, rb, c), lambda l, i: (0, l, i, 0)), blk, blk, blk],
        out_specs=[blk, blk, blk, blk], out_shape=[shp, shp, shp, shp],
        compiler_params=pltpu.CompilerParams(dimension_semantics=("arbitrary", "arbitrary"), vmem_limit_bytes=V7X_VMEM_LIMIT),
    )(recv, w, m, v)


def _small_adamw_call(w, g, m, v):
    def body(w_ref, g_ref, m_ref, v_ref, d_ref, nm_ref, nv_ref):
        delta, nm, nv = _adamw(w_ref[...], g_ref[...], m_ref[...], v_ref[...])
        d_ref[...] = delta
        nm_ref[...] = nm
        nv_ref[...] = nv

    shp = jax.ShapeDtypeStruct(w.shape, F32)
    vm = pl.BlockSpec(memory_space=pltpu.VMEM)
    return pl.pallas_call(body, name="small_adamw", in_specs=[vm] * 4, out_specs=[vm] * 3, out_shape=[shp, shp, shp],
                          compiler_params=pltpu.CompilerParams(vmem_limit_bytes=V7X_VMEM_LIMIT))(w, g, m, v)


def _gathered_cols(g):
    n, nl, r, c = g.shape
    return jnp.transpose(g, (1, 2, 0, 3)).reshape(nl, r, n * c)


def _gathered_rows(g):
    n, nl, r, c = g.shape
    return jnp.transpose(g, (1, 0, 2, 3)).reshape(nl, n * r, c)


def _split_cols(full):
    nl, r, c8 = full.shape
    return jnp.transpose(full.reshape(nl, r, N_DEV, c8 // N_DEV), (2, 0, 1, 3)).astype(BF16)


def _split_rows(full):
    nl, r8, c = full.shape
    return jnp.transpose(full.reshape(nl, N_DEV, r8 // N_DEV, c), (1, 0, 2, 3)).astype(BF16)


_SMALL_ORDER = ("norm_mix_g", "sgu_w", "sgu_b", "sgu_ln_g", "sgu_ln_b", "conv_w", "pool_w", "pool_scale",
                "norm_ff_g", "norm_ple_g", "final_g")


def _pack_small(d):
    pieces, layout = [], []
    for k in _SMALL_ORDER:
        flat = d[k].reshape(-1)
        n = flat.shape[0]
        pad = (-n) % 128
        pieces.append(jnp.pad(flat, (0, pad)))
        layout.append((k, d[k].shape, n, n + pad))
    flat = jnp.concatenate(pieces)
    pad = (-flat.shape[0]) % 1024
    return jnp.pad(flat, (0, pad)).reshape(-1, 128), layout


def _unpack_small(buf, layout):
    flat = buf.reshape(-1)
    out, off = {}, 0
    for k, shape, n, padded in layout:
        out[k] = flat[off:off + n].reshape(shape)
        off += padded
    return out


def kernel(x, p, norm_mix_g, w_in, sgu_w, sgu_b, sgu_ln_g, sgu_ln_b, conv_w, pool_w, pool_scale, w_out, norm_ff_g, w_ff1, w_ff2, norm_ple_g, w_ple_gate, w_ple_proj, final_g, loss_target, m_norm_mix_g, m_w_in, m_sgu_w, m_sgu_b, m_sgu_ln_g, m_sgu_ln_b, m_conv_w, m_pool_w, m_pool_scale, m_w_out, m_norm_ff_g, m_w_ff1, m_w_ff2, m_norm_ple_g, m_w_ple_gate, m_w_ple_proj, m_final_g, v_norm_mix_g, v_w_in, v_sgu_w, v_sgu_b, v_sgu_ln_g, v_sgu_ln_b, v_conv_w, v_pool_w, v_pool_scale, v_w_out, v_norm_ff_g, v_w_ff1, v_w_ff2, v_norm_ple_g, v_w_ple_gate, v_w_ple_proj, v_final_g):
    t = x.shape[1]
    xc, yc_, cc = _coords()
    me = 4 * xc + 2 * yc_ + cc
    tm = lambda want: min(want, t)

    conv_pad = jnp.zeros((16, 128), F32).at[0:DEPTH * 3, 0:D_B // N_DEV].set(conv_w.reshape(DEPTH * 3, D_B // N_DEV))
    gathered = _all_gather_call([w_in.astype(BF16), w_out.astype(BF16), w_ff1.astype(BF16), w_ff2.astype(BF16),
                                 w_ple_gate.astype(BF16), w_ple_proj.astype(BF16), conv_pad])
    win_f = _gathered_cols(gathered[0])
    wout_f = _gathered_rows(gathered[1])
    w1_f = _gathered_cols(gathered[2])
    w2_f = _gathered_rows(gathered[3])
    wg_f = _gathered_rows(gathered[4])
    wp_f = _gathered_cols(gathered[5])
    conv_f = jnp.transpose(gathered[6][:, 0:DEPTH * 3, 0:D_B // N_DEV].reshape(N_DEV, DEPTH, 3, D_B // N_DEV),
                           (1, 2, 0, 3)).reshape(DEPTH, 3, D_B)

    idx = jnp.arange(D_A)
    pmat = ((idx[:, None] // 64) == (idx[None, :] // 64)).astype(BF16) * (1.0 / 64.0)
    pmat = pmat.astype(BF16)
    tril = jnp.tril(jnp.ones((CHUNK, CHUNK), F32))
    trilcat = jnp.concatenate([tril, tril], axis=1)
    headsel = ((idx[:, None] // 64) == jnp.arange(CHUNK)[None, :]).astype(BF16)
    row = lambda a: a.reshape(1, -1)

    def mix_params(l, tile):
        wm = sgu_w[l] * tril[None]
        wcat = jnp.stack([jnp.concatenate([wm[2 * j], wm[2 * j + 1]], axis=1) for j in range(3)]).astype(BF16)
        wtcat = jnp.stack([jnp.concatenate([wm[2 * j].T, wm[2 * j + 1].T], axis=1) for j in range(3)]).astype(BF16)
        bmat = jnp.tile(jnp.repeat(sgu_b[l].T, 64, axis=1), (tile // CHUNK, 1))
        bd = jnp.zeros((D_C, D_C), F32)
        for gi in range(4):
            bd = bd.at[gi * 64:(gi + 1) * 64, gi * 64:(gi + 1) * 64].set(pool_w[l, gi])
        mp = dict(pmat=pmat, ln_g=row(sgu_ln_g[l]), ln_b=row(sgu_ln_b[l]), wcat=wcat, bmat=bmat, conv_w=conv_f[l],
                  bd=bd.astype(BF16), pool_scale=row(pool_scale[l]))
        return mp, wtcat

    xs = x.reshape(t, D)
    saved = []
    for l in range(DEPTH):
        mp, _ = mix_params(l, tm(TM_MIX_FWD))
        x1, proj = _mix_fwd_call(xs, row(norm_mix_g[l]), win_f[l], wout_f[l], mp, tm(TM_MIX_FWD), l)
        x2, x3, r, gate = _ffn_fwd_call(x1, p[l, 0], row(norm_ff_g[l]), row(norm_ple_g[l]), w1_f[l], w2_f[l], wg_f[l], wp_f[l],
                                        tm(TM_FFN_FWD), l)
        saved.append((xs, proj, x1, r, x2, gate))
        xs = x3

    sq, dx, dfinal = _loss_call(xs, loss_target.reshape(t, D), row(final_g), tm(TM_LOSS))
    loss = lax.psum(jnp.sum(sq) * (0.5 / D), ("x", "y", "c"))

    big = {k: [None] * DEPTH for k in ("w_in", "w_out", "w_ff1", "w_ff2", "w_ple_gate", "w_ple_proj")}
    small = {k: [None] * DEPTH for k in _SMALL_ORDER if k != "final_g"}
    for l in reversed(range(DEPTH)):
        x0, proj, x1, r, x2, gate = saved[l]
        dx2, dgple, dwg, dwp = _ple_bwd_call(dx, x2, gate, p[l, 0], row(norm_ple_g[l]), wg_f[l], wp_f[l], tm(TM_PLE_BWD), l)
        da, dw2 = _ffn_bwd_hidden_call(dx2, r, w2_f[l], tm(TM_FFN_BWD), l)
        dx1, dgff, dw1 = _ffn_bwd_input_call(da, x1, dx2, row(norm_ff_g[l]), w1_f[l], tm(TM_FFN_BWD), l)
        mp, wtcat = mix_params(l, tm(TM_MIX_BWD))
        (dx, dgmix, dwcat, dsb, dlng, dlnb, dconv, dbd, dscale, dwin, dwout) = _mix_bwd_call(
            dx1, x0, proj, row(norm_mix_g[l]), win_f[l], wout_f[l], mp, wtcat, trilcat, headsel, tm(TM_MIX_BWD), l)
        big["w_in"][l], big["w_out"][l], big["w_ff1"][l], big["w_ff2"][l] = dwin, dwout, dw1, dw2
        big["w_ple_gate"][l], big["w_ple_proj"][l] = dwg, dwp
        small["norm_mix_g"][l] = dgmix[0]
        small["sgu_w"][l] = jnp.stack([dwcat[h // 2][:, (h % 2) * CHUNK:(h % 2 + 1) * CHUNK] for h in range(6)])
        small["sgu_b"][l] = dsb[:, 0:6].T
        small["sgu_ln_g"][l], small["sgu_ln_b"][l] = dlng[0], dlnb[0]
        small["conv_w"][l] = dconv
        small["pool_w"][l] = jnp.stack([dbd[gi * 64:(gi + 1) * 64, gi * 64:(gi + 1) * 64] for gi in range(4)])
        small["pool_scale"][l] = dscale[0]
        small["norm_ff_g"][l], small["norm_ple_g"][l] = dgff[0], dgple[0]
    grad_x = dx.reshape(1, t, D)

    sends = [_split_cols(jnp.stack(big["w_in"])), _split_rows(jnp.stack(big["w_out"])), _split_cols(jnp.stack(big["w_ff1"])),
             _split_rows(jnp.stack(big["w_ff2"])), _split_rows(jnp.stack(big["w_ple_gate"])), _split_cols(jnp.stack(big["w_ple_proj"]))]
    recvs = _exchange_call(sends)
    res = {}
    for name, recv, w_, m_, v_ in (("w_in", recvs[0], w_in, m_w_in, v_w_in), ("w_out", recvs[1], w_out, m_w_out, v_w_out),
                                   ("w_ff1", recvs[2], w_ff1, m_w_ff1, v_w_ff1), ("w_ff2", recvs[3], w_ff2, m_w_ff2, v_w_ff2),
                                   ("w_ple_gate", recvs[4], w_ple_gate, m_w_ple_gate, v_w_ple_gate),
                                   ("w_ple_proj", recvs[5], w_ple_proj, m_w_ple_proj, v_w_ple_proj)):
        res[name] = _reduce_adamw_call(recv, w_, m_, v_, "reduce_adamw_" + name)

    small_g = {k: jnp.stack(vs) for k, vs in small.items()}
    small_g["final_g"] = dfinal[0]
    gbuf, layout = _pack_small(small_g)
    gsum = _small_all_reduce_call(gbuf)
    conv_cols = lambda a: lax.dynamic_slice_in_dim(a, me * (D_B // N_DEV), D_B // N_DEV, axis=2)
    pad_conv = lambda a: jnp.zeros((DEPTH, 3, D_B), F32).at[:, :, 0:D_B // N_DEV].set(a)
    small_w = dict(norm_mix_g=norm_mix_g, sgu_w=sgu_w, sgu_b=sgu_b, sgu_ln_g=sgu_ln_g, sgu_ln_b=sgu_ln_b, conv_w=pad_conv(conv_w),
                   pool_w=pool_w, pool_scale=pool_scale, norm_ff_g=norm_ff_g, norm_ple_g=norm_ple_g, final_g=final_g)
    small_m = dict(norm_mix_g=m_norm_mix_g, sgu_w=m_sgu_w, sgu_b=m_sgu_b, sgu_ln_g=m_sgu_ln_g, sgu_ln_b=m_sgu_ln_b,
                   conv_w=pad_conv(m_conv_w), pool_w=m_pool_w, pool_scale=m_pool_scale, norm_ff_g=m_norm_ff_g,
                   norm_ple_g=m_norm_ple_g, final_g=m_final_g)
    small_v = dict(norm_mix_g=v_norm_mix_g, sgu_w=v_sgu_w, sgu_b=v_sgu_b, sgu_ln_g=v_sgu_ln_g, sgu_ln_b=v_sgu_ln_b,
                   conv_w=pad_conv(v_conv_w), pool_w=v_pool_w, pool_scale=v_pool_scale,
                   norm_ff_g=v_norm_ff_g, norm_ple_g=v_norm_ple_g, final_g=v_final_g)
    gs = _unpack_small(gsum, layout)
    gs_local = dict(gs)
    gs_local["conv_w"] = pad_conv(conv_cols(gs["conv_w"]))
    g_loc, _ = _pack_small(gs_local)
    wbuf, _ = _pack_small(small_w)
    mbuf, _ = _pack_small(small_m)
    vbuf, _ = _pack_small(small_v)
    dbuf, nmbuf, nvbuf = _small_adamw_call(wbuf, g_loc, mbuf, vbuf)
    sd, sm, sv = _unpack_small(dbuf, layout), _unpack_small(nmbuf, layout), _unpack_small(nvbuf, layout)
    unconv = lambda a: a[:, :, 0:D_B // N_DEV]
    for dct in (gs_local, sd, sm, sv):
        dct["conv_w"] = unconv(dct["conv_w"])

    order = ["norm_mix_g", "w_in", "sgu_w", "sgu_b", "sgu_ln_g", "sgu_ln_b", "conv_w", "pool_w", "pool_scale", "w_out",
             "norm_ff_g", "w_ff1", "w_ff2", "norm_ple_g", "w_ple_gate", "w_ple_proj", "final_g"]
    outs = [loss, grad_x]
    for which in range(4):
        for k in order:
            if k in res:
                outs.append(res[k][which])
            else:
                outs.append((gs_local, sd, sm, sv)[which][k])
    return tuple(outs)
```

```python
import functools
import math

import jax
import jax.numpy as jnp
from jax import lax
from jax.experimental import pallas as pl
from jax.experimental.pallas import tpu as pltpu

F32 = jnp.float32
BF16 = jnp.bfloat16

D = 1024
D_IN = 2176
D_A = 384
D_B = 384
D_C = 256
D_FF = 4096
D_PLE = 256
DEPTH = 4
CHUNK = 128
HALO = 16
FF_BLK = 1024
N_DEV = 8
RMS_EPS = 1e-6
LN_EPS = 1e-5
ADAM_LR = 0.001
ADAM_B1 = 0.9
ADAM_B2 = 0.999
ADAM_EPS = 1e-08
ADAM_WD = 0.01
ADAM_STEP = 10

TM_MIX_FWD = 512
TM_FFN_FWD = 512
TM_LOSS = 512
TM_PLE_BWD = 512
TM_FFN_BWD = 256
TM_MIX_BWD = 256
V7X_VMEM_LIMIT = 56 * 1024 * 1024

ANY = pl.BlockSpec(memory_space=pl.ANY)
MESH = pl.DeviceIdType.MESH


def _params(vmem=V7X_VMEM_LIMIT):
    return pltpu.CompilerParams(dimension_semantics=("arbitrary",), vmem_limit_bytes=vmem)


def _full(shape):
    nd = len(shape)
    return pl.BlockSpec(shape, lambda i: (0,) * nd)


def _rows(tm, cols):
    return pl.BlockSpec((tm, cols), lambda i: (i, 0))


def _mm(a, b):
    return jnp.dot(a, b, preferred_element_type=F32)


def _mm_nt(a, b):
    return lax.dot_general(a, b, (((1,), (1,)), ((), ())), preferred_element_type=F32)


def _mm_tn(a, b):
    return lax.dot_general(a, b, (((0,), (0,)), ((), ())), preferred_element_type=F32)


def _erf(x):
    ax = jnp.abs(x)
    t = 1.0 / (1.0 + 0.3275911 * ax)
    poly = t * (0.254829592 + t * (-0.284496736 + t * (1.421413741 + t * (-1.453152027 + t * 1.061405429))))
    y = 1.0 - poly * jnp.exp(-ax * ax)
    return jnp.where(x < 0, -y, y)


def _gelu_and_grad(x):
    cdf = 0.5 * (1.0 + _erf(x * (1.0 / math.sqrt(2.0))))
    pdf = jnp.exp(-0.5 * x * x) * (1.0 / math.sqrt(2.0 * math.pi))
    return x * cdf, cdf + x * pdf


def _rms(x, g):
    rstd = lax.rsqrt(jnp.mean(x * x, axis=-1, keepdims=True) + RMS_EPS)
    xhat = x * rstd
    return xhat * g, xhat, rstd


def _rms_bwd(dy, g, xhat, rstd):
    dg = jnp.sum(dy * xhat, axis=0, keepdims=True)
    dxh = dy * g
    dx = rstd * (dxh - xhat * jnp.mean(dxh * xhat, axis=-1, keepdims=True))
    return dx, dg


def _shift_down(ext, k):
    return pltpu.roll(ext, k, 0)[HALO:, :]


def _shift_up(ext, k):
    n = ext.shape[0]
    return pltpu.roll(ext, n - k, 0)[: n - HALO, :]


def _pool_select(s2, s4, s8, s16):
    lane = lax.broadcasted_iota(jnp.int32, s2.shape, 1)
    return jnp.where(lane < 64, s2, jnp.where(lane < 128, s4, jnp.where(lane < 192, s8, s16)))


def _pool_inv_count(tile_start, tm):
    pos = lax.broadcasted_iota(jnp.int32, (tm, D_C), 0) + tile_start + 1
    lane = lax.broadcasted_iota(jnp.int32, (tm, D_C), 1)
    win = jnp.where(lane < 64, 2, jnp.where(lane < 128, 4, jnp.where(lane < 192, 8, 16)))
    return 1.0 / jnp.minimum(pos, win).astype(F32)


def _head_halves(a):
    lane = lax.broadcasted_iota(jnp.int32, a.shape, 1)
    even = (lane & 64) == 0
    return jnp.where(even, a, 0.0).astype(BF16), jnp.where(even, 0.0, a).astype(BF16)


def _head_stack(lo, hi, j, nch):
    return jnp.concatenate(
        [jnp.concatenate([lo[c * CHUNK:(c + 1) * CHUNK, j * 128:(j + 1) * 128], hi[c * CHUNK:(c + 1) * CHUNK, j * 128:(j + 1) * 128]], axis=0)
         for c in range(nch)], axis=1)


def _chunks_to_lanes(a, j, nch):
    return jnp.concatenate([a[c * CHUNK:(c + 1) * CHUNK, j * 128:(j + 1) * 128] for c in range(nch)], axis=1)


def _lanes_to_chunks(o, nch):
    return jnp.concatenate([o[:, c * CHUNK:(c + 1) * CHUNK] for c in range(nch)], axis=0)


def _mixers_fwd(pf, halo_hc, halo_zc, tile_start, prm):
    tm = pf.shape[0]
    nch = tm // CHUNK
    u, v = pf[:, 0:D_A], pf[:, D_A:2 * D_A]
    zb, gb, gc = pf[:, 768:1152], pf[:, 1152:1536], pf[:, 1536:1920]
    zc = pf[:, 1920:2176]
    r = {}
    gu, r["dgelu_u"] = _gelu_and_grad(u)
    gv, r["dgelu_v"] = _gelu_and_grad(v)
    pmat = prm["pmat"][...]
    mu = _mm(gv.astype(BF16), pmat)
    dv = gv - mu
    var = _mm((dv * dv).astype(BF16), pmat)
    rstd = lax.rsqrt(var + LN_EPS)
    xh = dv * rstd
    vlo, vhi = _head_halves(xh * prm["ln_g"][...] + prm["ln_b"][...])
    cols, v2s = [], []
    for j in range(3):
        v2 = _head_stack(vlo, vhi, j, nch)
        v2s.append(v2)
        cols.append(_lanes_to_chunks(_mm(prm["wcat"][j], v2), nch))
    mixed = jnp.concatenate(cols, axis=1) + prm["bmat"][...]
    ya = gu * mixed
    r.update(gu=gu, mixed=mixed, v2s=v2s, xh=xh, ln_rstd=rstd)
    w0, w1, w2 = prm["conv_w"][0:1, :], prm["conv_w"][1:2, :], prm["conv_w"][2:3, :]
    hc = gc * zb
    ext = jnp.concatenate([halo_hc, hc], axis=0)
    h1, h2 = _shift_down(ext, 1), _shift_down(ext, 2)
    yc = w2 * hc + w1 * h1 + w0 * h2
    yb = gb * yc
    r.update(hc=hc, h1=h1, h2=h2, yc=yc, zb=zb, gb=gb, gc=gc, w0=w0, w1=w1, w2=w2)
    ext = jnp.concatenate([halo_zc, zc], axis=0)
    s2 = ext + pltpu.roll(ext, 1, 0)
    s4 = s2 + pltpu.roll(s2, 2, 0)
    s8 = s4 + pltpu.roll(s4, 4, 0)
    s16 = s8 + pltpu.roll(s8, 8, 0)
    inv = _pool_inv_count(tile_start, tm)
    pooled = _pool_select(s2, s4, s8, s16)[HALO:, :] * inv - zc
    pooledb = pooled.astype(BF16)
    pm = _mm(pooledb, prm["bd"][...])
    scale = prm["pool_scale"][...]
    ycm = pm * scale
    r.update(inv=inv, pooledb=pooledb, pm=pm, scale=scale, zc=zc)
    r["ycat"] = jnp.concatenate([ya, yb, ycm], axis=1)
    return r


_MIX_PARAM_NAMES = ("pmat", "ln_g", "ln_b", "wcat", "bmat", "conv_w", "bd", "pool_scale")


def _mix_param_specs(tm):
    return [_full((D_A, D_A)), _full((1, D_A)), _full((1, D_A)), _full((3, CHUNK, 2 * CHUNK)), _full((tm, D_A)),
            _full((3, D_B)), _full((D_C, D_C)), _full((1, D_C))]


def _mix_fwd_call(x, g_mix, w_in, w_out, mp, tm, layer):
    t = x.shape[0]
    nt = t // tm

    def body(x_ref, g_ref, pmat, ln_g, ln_b, wcat, bmat, conv_w, bd, pool_scale, win_hbm, wout_hbm,
             x1_ref, proj_ref, win_s, wout_s, halo_hc, halo_zc):
        i = pl.program_id(0)

        @pl.when(i == 0)
        def _():
            pltpu.sync_copy(win_hbm, win_s)
            pltpu.sync_copy(wout_hbm, wout_s)
            halo_hc[...] = jnp.zeros_like(halo_hc)
            halo_zc[...] = jnp.zeros_like(halo_zc)

        prm = dict(pmat=pmat, ln_g=ln_g, ln_b=ln_b, wcat=wcat, bmat=bmat, conv_w=conv_w, bd=bd, pool_scale=pool_scale)
        xv = x_ref[...]
        h, _, _ = _rms(xv, g_ref[...])
        pf = _mm(h.astype(BF16), win_s[...])
        proj_ref[...] = pf.astype(BF16)
        r = _mixers_fwd(pf, halo_hc[...], halo_zc[...], i * tm, prm)
        halo_hc[...] = r["hc"][tm - HALO:, :]
        halo_zc[...] = r["zc"][tm - HALO:, :]
        x1_ref[...] = xv + _mm(r["ycat"].astype(BF16), wout_s[...])

    return pl.pallas_call(
        body, name=f"mix_fwd_{layer}", grid=(nt,),
        in_specs=[_rows(tm, D), _full((1, D))] + _mix_param_specs(tm) + [ANY, ANY],
        out_specs=[_rows(tm, D), _rows(tm, D_IN)],
        out_shape=[jax.ShapeDtypeStruct((t, D), F32), jax.ShapeDtypeStruct((t, D_IN), BF16)],
        scratch_shapes=[pltpu.VMEM((D, D_IN), BF16), pltpu.VMEM((D, D), BF16),
                        pltpu.VMEM((HALO, D_B), F32), pltpu.VMEM((HALO, D_C), F32)],
        compiler_params=_params(),
    )(x, g_mix, *[mp[k] for k in _MIX_PARAM_NAMES], w_in, w_out)


def _ffn_fwd_call(x1, p, g_ff, g_ple, w1, w2, wg, wp, tm, layer):
    t = x1.shape[0]
    nt = t // tm

    def body(x1_ref, p_ref, gff_ref, gple_ref, w1_hbm, w2_hbm, wg_hbm, wp_hbm,
             x2_ref, x3_ref, r_ref, gate_ref, w1_s, w2_s, wg_s, wp_s):
        i = pl.program_id(0)

        @pl.when(i == 0)
        def _():
            pltpu.sync_copy(w1_hbm, w1_s)
            pltpu.sync_copy(w2_hbm, w2_s)
            pltpu.sync_copy(wg_hbm, wg_s)
            pltpu.sync_copy(wp_hbm, wp_s)

        x1v = x1_ref[...]
        h2, _, _ = _rms(x1v, gff_ref[...])
        h2b = h2.astype(BF16)
        acc = x1v
        for j in range(D_FF // FF_BLK):
            blk = slice(j * FF_BLK, (j + 1) * FF_BLK)
            rj = jnp.maximum(_mm(h2b, w1_s[:, blk]), 0.0)
            r_ref[:, blk] = rj.astype(BF16)
            acc = acc + _mm((rj * rj).astype(BF16), w2_s[blk, :])
        x2_ref[...] = acc
        n3, _, _ = _rms(acc, gple_ref[...])
        gate = jax.nn.sigmoid(_mm(n3.astype(BF16), wg_s[...]))
        gate_ref[...] = gate.astype(BF16)
        pp = _mm(p_ref[...].astype(BF16), wp_s[...])
        x3_ref[...] = acc + pp * gate

    return pl.pallas_call(
        body, name=f"ffn_fwd_{layer}", grid=(nt,),
        in_specs=[_rows(tm, D), _rows(tm, D_PLE), _full((1, D)), _full((1, D)), ANY, ANY, ANY, ANY],
        out_specs=[_rows(tm, D), _rows(tm, D), _rows(tm, D_FF), _rows(tm, D)],
        out_shape=[jax.ShapeDtypeStruct((t, D), F32), jax.ShapeDtypeStruct((t, D), F32),
                   jax.ShapeDtypeStruct((t, D_FF), BF16), jax.ShapeDtypeStruct((t, D), BF16)],
        scratch_shapes=[pltpu.VMEM((D, D_FF), BF16), pltpu.VMEM((D_FF, D), BF16),
                        pltpu.VMEM((D, D), BF16), pltpu.VMEM((D_PLE, D), BF16)],
        compiler_params=_params(),
    )(x1, p, g_ff, g_ple, w1, w2, wg, wp)


def _loss_call(xl, target, final_g, tm):
    t = xl.shape[0]
    nt = t // tm

    def body(x_ref, t_ref, g_ref, sq_ref, dx_ref, dg_ref):
        i = pl.program_id(0)

        @pl.when(i == 0)
        def _():
            sq_ref[...] = jnp.zeros_like(sq_ref)
            dg_ref[...] = jnp.zeros_like(dg_ref)

        g = g_ref[...]
        y, xhat, rstd = _rms(x_ref[...], g)
        err = y - t_ref[...]
        sq_ref[...] += jnp.sum(err * err, axis=0, keepdims=True)
        dx, dg = _rms_bwd(err * (1.0 / D), g, xhat, rstd)
        dx_ref[...] = dx
        dg_ref[...] += dg

    return pl.pallas_call(
        body, name="loss_head", grid=(nt,),
        in_specs=[_rows(tm, D), _rows(tm, D), _full((1, D))],
        out_specs=[_full((1, D)), _rows(tm, D), _full((1, D))],
        out_shape=[jax.ShapeDtypeStruct((1, D), F32), jax.ShapeDtypeStruct((t, D), F32), jax.ShapeDtypeStruct((1, D), F32)],
        compiler_params=_params(),
    )(xl, target, final_g)


def _ple_bwd_call(dx3, x2, gate, p, g_ple, wg, wp, tm, layer):
    t = dx3.shape[0]
    nt = t // tm

    def body(dx3_ref, x2_ref, gate_ref, p_ref, g_ref, wg_hbm, wp_hbm,
             dx2_ref, dg_ref, dwg_hbm, dwp_hbm, wg_s, wp_s, dwg_acc, dwp_acc):
        i = pl.program_id(0)

        @pl.when(i == 0)
        def _():
            pltpu.sync_copy(wg_hbm, wg_s)
            pltpu.sync_copy(wp_hbm, wp_s)
            dwg_acc[...] = jnp.zeros_like(dwg_acc)
            dwp_acc[...] = jnp.zeros_like(dwp_acc)
            dg_ref[...] = jnp.zeros_like(dg_ref)

        g = g_ref[...]
        dx3v = dx3_ref[...]
        gatev = gate_ref[...].astype(F32)
        pb = p_ref[...].astype(BF16)
        pp = _mm(pb, wp_s[...])
        dwp_acc[...] += _mm_tn(pb, (dx3v * gatev).astype(BF16))
        dgpre = (dx3v * pp * gatev * (1.0 - gatev)).astype(BF16)
        n3, xhat, rstd = _rms(x2_ref[...], g)
        dwg_acc[...] += _mm_tn(n3.astype(BF16), dgpre)
        dn3 = _mm_nt(dgpre, wg_s[...])
        dx, dg = _rms_bwd(dn3, g, xhat, rstd)
        dx2_ref[...] = dx3v + dx
        dg_ref[...] += dg

        @pl.when(i == nt - 1)
        def _():
            pltpu.sync_copy(dwg_acc, dwg_hbm)
            pltpu.sync_copy(dwp_acc, dwp_hbm)

    return pl.pallas_call(
        body, name=f"ple_bwd_{layer}", grid=(nt,),
        in_specs=[_rows(tm, D), _rows(tm, D), _rows(tm, D), _rows(tm, D_PLE), _full((1, D)), ANY, ANY],
        out_specs=[_rows(tm, D), _full((1, D)), ANY, ANY],
        out_shape=[jax.ShapeDtypeStruct((t, D), F32), jax.ShapeDtypeStruct((1, D), F32),
                   jax.ShapeDtypeStruct((D, D), F32), jax.ShapeDtypeStruct((D_PLE, D), F32)],
        scratch_shapes=[pltpu.VMEM((D, D), BF16), pltpu.VMEM((D_PLE, D), BF16),
                        pltpu.VMEM((D, D), F32), pltpu.VMEM((D_PLE, D), F32)],
        compiler_params=_params(),
    )(dx3, x2, gate, p, g_ple, wg, wp)


def _ffn_bwd_hidden_call(dx2, r, w2, tm, layer):
    t = dx2.shape[0]
    nt = t // tm

    def body(dx2_ref, r_ref, w2_hbm, da_ref, dw2_hbm, w2_s, dw2_acc):
        i = pl.program_id(0)

        @pl.when(i == 0)
        def _():
            pltpu.sync_copy(w2_hbm, w2_s)
            dw2_acc[...] = jnp.zeros_like(dw2_acc)

        dxb = dx2_ref[...].astype(BF16)
        for j in range(D_FF // FF_BLK):
            blk = slice(j * FF_BLK, (j + 1) * FF_BLK)
            rj = r_ref[:, blk].astype(F32)
            ds = _mm_nt(dxb, w2_s[blk, :])
            da_ref[:, blk] = (2.0 * rj * ds).astype(BF16)
            dw2_acc[blk, :] += _mm_tn((rj * rj).astype(BF16), dxb)

        @pl.when(i == nt - 1)
        def _():
            pltpu.sync_copy(dw2_acc, dw2_hbm)

    return pl.pallas_call(
        body, name=f"ffn_bwd_hidden_{layer}", grid=(nt,),
        in_specs=[_rows(tm, D), _rows(tm, D_FF), ANY],
        out_specs=[_rows(tm, D_FF), ANY],
        out_shape=[jax.ShapeDtypeStruct((t, D_FF), BF16), jax.ShapeDtypeStruct((D_FF, D), F32)],
        scratch_shapes=[pltpu.VMEM((D_FF, D), BF16), pltpu.VMEM((D_FF, D), F32)],
        compiler_params=_params(),
    )(dx2, r, w2)


def _ffn_bwd_input_call(da, x1, dx2, g_ff, w1, tm, layer):
    t = dx2.shape[0]
    nt = t // tm

    def body(da_ref, x1_ref, dx2_ref, g_ref, w1_hbm, dx1_ref, dg_ref, dw1_hbm, w1_s, dw1_acc):
        i = pl.program_id(0)

        @pl.when(i == 0)
        def _():
            pltpu.sync_copy(w1_hbm, w1_s)
            dw1_acc[...] = jnp.zeros_like(dw1_acc)
            dg_ref[...] = jnp.zeros_like(dg_ref)

        g = g_ref[...]
        h2, xhat, rstd = _rms(x1_ref[...], g)
        h2b = h2.astype(BF16)
        dh2 = jnp.zeros((tm, D), F32)
        for j in range(D_FF // FF_BLK):
            blk = slice(j * FF_BLK, (j + 1) * FF_BLK)
            daj = da_ref[:, blk]
            dh2 = dh2 + _mm_nt(daj, w1_s[:, blk])
            dw1_acc[:, blk] += _mm_tn(h2b, daj)
        dx, dg = _rms_bwd(dh2, g, xhat, rstd)
        dx1_ref[...] = dx2_ref[...] + dx
        dg_ref[...] += dg

        @pl.when(i == nt - 1)
        def _():
            pltpu.sync_copy(dw1_acc, dw1_hbm)

    return pl.pallas_call(
        body, name=f"ffn_bwd_input_{layer}", grid=(nt,),
        in_specs=[_rows(tm, D_FF), _rows(tm, D), _rows(tm, D), _full((1, D)), ANY],
        out_specs=[_rows(tm, D), _full((1, D)), ANY],
        out_shape=[jax.ShapeDtypeStruct((t, D), F32), jax.ShapeDtypeStruct((1, D), F32), jax.ShapeDtypeStruct((D, D_FF), F32)],
        scratch_shapes=[pltpu.VMEM((D, D_FF), BF16), pltpu.VMEM((D, D_FF), F32)],
        compiler_params=_params(),
    )(da, x1, dx2, g_ff, w1)


def _mix_bwd_call(dx1, x, proj, g_mix, w_in, w_out, mp, wtcat, trilcat, headsel, tm, layer):
    t = dx1.shape[0]
    nt = t // tm
    nch = tm // CHUNK
    hb = tm // HALO

    def rev(i):
        return nt - 1 - i

    def body(dx1_ref, x_ref, proj_ref, halo_ref, g_ref, pmat, ln_g, ln_b, wcat, bmat, conv_w, bd, pool_scale,
             wtcat_ref, tril_ref, sel_ref, win_hbm, wout_hbm,
             dx_ref, dg_ref, dwcat_ref, dsb_ref, dlng_ref, dlnb_ref, dconv_ref, dbd_ref, dscale_ref, dwin_hbm, dwout_hbm,
             win_s, wout_s, dwin_acc, dwout_acc, dbm_acc, carry_yc, carry_q, dvn_s):
        i = pl.program_id(0)
        ri = nt - 1 - i

        @pl.when(i == 0)
        def _():
            pltpu.sync_copy(win_hbm, win_s)
            pltpu.sync_copy(wout_hbm, wout_s)
            for ref in (dwin_acc, dwout_acc, dbm_acc, carry_yc, carry_q, dg_ref, dwcat_ref, dlng_ref, dlnb_ref,
                        dconv_ref, dbd_ref, dscale_ref):
                ref[...] = jnp.zeros_like(ref)

        prm = dict(pmat=pmat, ln_g=ln_g, ln_b=ln_b, wcat=wcat, bmat=bmat, conv_w=conv_w, bd=bd, pool_scale=pool_scale)
        g = g_ref[...]
        h, xhat, rstd = _rms(x_ref[...], g)
        hb16 = h.astype(BF16)
        dx1v = dx1_ref[...]
        dx1b = dx1v.astype(BF16)
        pf = proj_ref[...].astype(F32)
        ph = halo_ref[...].astype(F32) * (ri > 0).astype(F32)
        r = _mixers_fwd(pf, ph[:, 1536:1920] * ph[:, 768:1152], ph[:, 1920:2176], ri * tm, prm)

        dwout_acc[...] += _mm_tn(r["ycat"].astype(BF16), dx1b)
        dycat = _mm_nt(dx1b, wout_s[...])
        dya, dyb, dyc = dycat[:, 0:D_A], dycat[:, D_A:D_A + D_B], dycat[:, D_A + D_B:D]

        dgu = dya * r["mixed"]
        dmix = dya * r["gu"]
        dmix_b = dmix.astype(BF16)
        dlo, dhi = _head_halves(dmix)
        dbm = dmix[0:CHUNK, :]
        for c in range(1, nch):
            dbm = dbm + dmix[c * CHUNK:(c + 1) * CHUNK, :]
        dbm_acc[...] += dbm
        for j in range(3):
            dm_all = _chunks_to_lanes(dmix_b, j, nch)
            dwcat_ref[j] += _mm_nt(dm_all, r["v2s"][j])
            dm2 = _head_stack(dlo, dhi, j, nch)
            dvn_s[:, j * 128:(j + 1) * 128] = _lanes_to_chunks(_mm(wtcat_ref[j], dm2), nch)
        dvn = dvn_s[...]
        xh = r["xh"]
        dlng_ref[...] += jnp.sum(dvn * xh, axis=0, keepdims=True)
        dlnb_ref[...] += jnp.sum(dvn, axis=0, keepdims=True)
        dxh = dvn * ln_g[...]
        pm_ = pmat[...]
        m1 = _mm(dxh.astype(BF16), pm_)
        m2 = _mm((dxh * xh).astype(BF16), pm_)
        dgv = r["ln_rstd"] * (dxh - m1 - xh * m2)
        du = dgu * r["dgelu_u"]
        dv = dgv * r["dgelu_v"]

        dgb = dyb * r["yc"]
        dyc2 = dyb * r["gb"]
        dconv_ref[0:1, :] += jnp.sum(dyc2 * r["h2"], axis=0, keepdims=True)
        dconv_ref[1:2, :] += jnp.sum(dyc2 * r["h1"], axis=0, keepdims=True)
        dconv_ref[2:3, :] += jnp.sum(dyc2 * r["hc"], axis=0, keepdims=True)
        ext = jnp.concatenate([dyc2, carry_yc[...]], axis=0)
        dhc = r["w2"] * dyc2 + r["w1"] * _shift_up(ext, 1) + r["w0"] * _shift_up(ext, 2)
        carry_yc[...] = dyc2[0:HALO, :]
        dgc = dhc * r["zb"]
        dzb = dhc * r["gc"]

        dscale_ref[...] += jnp.sum(dyc * r["pm"], axis=0, keepdims=True)
        dpm = (dyc * r["scale"]).astype(BF16)
        dbd_ref[...] += _mm_tn(r["pooledb"], dpm)
        dpooled = _mm_nt(dpm, bd[...])
        q = dpooled * r["inv"]
        ext = jnp.concatenate([q, carry_q[...]], axis=0)
        n = tm + HALO
        r2 = ext + pltpu.roll(ext, n - 1, 0)
        r4 = r2 + pltpu.roll(r2, n - 2, 0)
        r8 = r4 + pltpu.roll(r4, n - 4, 0)
        r16 = r8 + pltpu.roll(r8, n - 8, 0)
        dzc = _pool_select(r2, r4, r8, r16)[0:tm, :] - dpooled
        carry_q[...] = q[0:HALO, :]

        dproj = jnp.concatenate([du, dv, dzb, dgb, dgc, dzc], axis=1).astype(BF16)
        dwin_acc[...] += _mm_tn(hb16, dproj)
        dh = _mm_nt(dproj, win_s[...])
        dx, dg = _rms_bwd(dh, g, xhat, rstd)
        dx_ref[...] = dx1v + dx
        dg_ref[...] += dg

        @pl.when(i == nt - 1)
        def _():
            pltpu.sync_copy(dwin_acc, dwin_hbm)
            pltpu.sync_copy(dwout_acc, dwout_hbm)
            for j in range(3):
                dwcat_ref[j] = dwcat_ref[j] * tril_ref[...]
            acc = dbm_acc[...]
            hi = acc.astype(BF16)
            lo = (acc - hi.astype(F32)).astype(BF16)
            dsb_ref[...] = _mm(hi, sel_ref[...]) + _mm(lo, sel_ref[...])

    return pl.pallas_call(
        body, name=f"mix_bwd_{layer}", grid=(nt,),
        in_specs=[pl.BlockSpec((tm, D), lambda i: (rev(i), 0)), pl.BlockSpec((tm, D), lambda i: (rev(i), 0)),
                  pl.BlockSpec((tm, D_IN), lambda i: (rev(i), 0)),
                  pl.BlockSpec((HALO, D_IN), lambda i: (jnp.maximum(rev(i) * hb - 1, 0), 0)),
                  _full((1, D))] + _mix_param_specs(tm)
                 + [_full((3, CHUNK, 2 * CHUNK)), _full((CHUNK, 2 * CHUNK)), _full((D_A, CHUNK)), ANY, ANY],
        out_specs=[pl.BlockSpec((tm, D), lambda i: (rev(i), 0)), _full((1, D)), _full((3, CHUNK, 2 * CHUNK)),
                   _full((CHUNK, CHUNK)), _full((1, D_A)), _full((1, D_A)), _full((3, D_B)), _full((D_C, D_C)),
                   _full((1, D_C)), ANY, ANY],
        out_shape=[jax.ShapeDtypeStruct((t, D), F32), jax.ShapeDtypeStruct((1, D), F32),
                   jax.ShapeDtypeStruct((3, CHUNK, 2 * CHUNK), F32), jax.ShapeDtypeStruct((CHUNK, CHUNK), F32),
                   jax.ShapeDtypeStruct((1, D_A), F32), jax.ShapeDtypeStruct((1, D_A), F32),
                   jax.ShapeDtypeStruct((3, D_B), F32), jax.ShapeDtypeStruct((D_C, D_C), F32),
                   jax.ShapeDtypeStruct((1, D_C), F32), jax.ShapeDtypeStruct((D, D_IN), F32),
                   jax.ShapeDtypeStruct((D, D), F32)],
        scratch_shapes=[pltpu.VMEM((D, D_IN), BF16), pltpu.VMEM((D, D), BF16),
                        pltpu.VMEM((D, D_IN), F32), pltpu.VMEM((D, D), F32), pltpu.VMEM((CHUNK, D_A), F32),
                        pltpu.VMEM((HALO, D_B), F32), pltpu.VMEM((HALO, D_C), F32), pltpu.VMEM((tm, D_A), F32)],
        compiler_params=_params(),
    )(dx1, x, proj, proj, g_mix, *[mp[k] for k in _MIX_PARAM_NAMES], wtcat, trilcat, headsel, w_in, w_out)


def _coords():
    return lax.axis_index("x"), lax.axis_index("y"), lax.axis_index("c")


HBM = pl.BlockSpec(memory_space=pltpu.HBM)
SEM = pl.BlockSpec(memory_space=pltpu.SEMAPHORE)
EFFECT = pltpu.SideEffectType.DATAFLOW_SIDE_EFFECTING


def _peer(k):
    x, y, c = _coords()
    px, py, pc = x ^ (k >> 2), y ^ ((k >> 1) & 1), c ^ (k & 1)
    return (px, py, pc), 4 * px + 2 * py + pc


def _landing_shape(src, mode):
    if mode == "block":
        return (N_DEV,) + src.shape
    if mode == "slot":
        return src.shape
    if mode == "cols_in":
        return (src.shape[0], N_DEV * src.shape[1])
    return (N_DEV, src.shape[0], src.shape[1] // N_DEV)


def _exchange_copy(src, land, mode, send_sem, recv_sem, ai, k, starting):
    x, y, c = _coords()
    peer, pidx = _peer(k)
    far = 4 * x + 2 * y + c if starting else pidx
    if mode == "block":
        s, d = src, land.at[far]
    elif mode == "slot":
        s, d = src.at[pidx], land.at[far]
    elif mode == "cols_in":
        cw = src.shape[1]
        s, d = src, land.at[:, pl.ds(pl.multiple_of(far * cw, 128), cw)]
    else:
        cw = land.shape[2]
        s, d = src.at[:, pl.ds(pl.multiple_of(pidx * cw, 128), cw)], land.at[far]
    i = ai * (N_DEV - 1) + k - 1
    return pltpu.make_async_remote_copy(src_ref=s, dst_ref=d, send_sem=send_sem.at[i], recv_sem=recv_sem.at[i],
                                        device_id=peer, device_id_type=MESH)


def _exchange_start(srcs, modes, groups, name):
    n, ng = len(srcs), len(groups)
    lands = [lax.empty(_landing_shape(s, m), s.dtype) for s, m in zip(srcs, modes)]

    def body(*refs):
        ins, land_refs = refs[:n], refs[n:2 * n]
        sems = refs[2 * n:2 * n + 2 * ng]
        token = refs[-1]
        for g, idxs in enumerate(groups):
            for ai, a in enumerate(idxs):
                for k in range(1, N_DEV):
                    _exchange_copy(ins[a], land_refs[a], modes[a], sems[2 * g], sems[2 * g + 1], ai, k, True).start()
        token[...] = jnp.zeros_like(token)

    sem_shapes = []
    for idxs in groups:
        sem_shapes += [pltpu.SemaphoreType.DMA((len(idxs) * (N_DEV - 1),))] * 2
    out = pl.pallas_call(
        body, name=name,
        out_shape=tuple(sem_shapes) + tuple(pltpu.HBM(s.shape, s.dtype) for s in srcs)
        + tuple(pltpu.HBM(l.shape, l.dtype) for l in lands) + (jax.ShapeDtypeStruct((8, 128), F32),),
        in_specs=[HBM] * (2 * n),
        out_specs=tuple([SEM] * (2 * ng) + [HBM] * (2 * n) + [pl.BlockSpec(memory_space=pltpu.VMEM)]),
        input_output_aliases={i: 2 * ng + i for i in range(2 * n)},
        compiler_params=pltpu.CompilerParams(has_side_effects=EFFECT),
    )(*[pltpu.with_memory_space_constraint(s, pltpu.HBM) for s in srcs],
      *[pltpu.with_memory_space_constraint(l, pltpu.HBM) for l in lands])
    sems = [(out[2 * g], out[2 * g + 1]) for g in range(ng)]
    return sems, list(out[2 * ng:2 * ng + n]), list(out[2 * ng + n:2 * ng + 2 * n]), out[-1]


def _exchange_wait(sems, srcs, lands, modes, groups, after, name):
    n, ng = len(srcs), len(groups)

    def body(*refs):
        ins, land_refs = refs[:n], refs[n:2 * n]
        sem_refs = refs[2 * n:2 * n + 2 * ng]
        for g, idxs in enumerate(groups):
            for ai, a in enumerate(idxs):
                for k in range(1, N_DEV):
                    cp = _exchange_copy(ins[a], land_refs[a], modes[a], sem_refs[2 * g], sem_refs[2 * g + 1], ai, k, False)
                    cp.wait_send()
                    cp.wait_recv()

    flat_sems = [s for pair in sems for s in pair]
    out = pl.pallas_call(
        body, name=name,
        out_shape=tuple(pltpu.HBM(s.shape, s.dtype) for s in srcs) + tuple(pltpu.HBM(l.shape, l.dtype) for l in lands),
        in_specs=[HBM] * (2 * n) + [SEM] * (2 * ng) + [ANY],
        out_specs=tuple([HBM] * (2 * n)),
        input_output_aliases={i: i for i in range(2 * n)},
        compiler_params=pltpu.CompilerParams(has_side_effects=EFFECT),
    )(*srcs, *lands, *flat_sems, after)
    return list(out[n:])


def _small_all_reduce_call(buf):
    rows = buf.shape[0]

    def body(in_ref, out_ref, pair_ref, slots_ref, send_sems, recv_sems):
        x, y, c = _coords()
        chip = 2 * x + y
        sib = pltpu.make_async_remote_copy(src_ref=in_ref, dst_ref=pair_ref, send_sem=send_sems.at[0], recv_sem=recv_sems.at[0],
                                           device_id=(x, y, 1 - c), device_id_type=MESH)
        sib.start()
        sib.wait_recv()
        slots_ref[chip] = in_ref[...] + pair_ref[...]
        sends = []
        for k in range(1, 4):
            px, py = x ^ (k >> 1), y ^ (k & 1)
            cp = pltpu.make_async_remote_copy(src_ref=slots_ref.at[chip], dst_ref=slots_ref.at[chip],
                                              send_sem=send_sems.at[k], recv_sem=recv_sems.at[k],
                                              device_id=(px, py, c), device_id_type=MESH)
            cp.start()
            sends.append(cp)
        for cp in sends:
            cp.wait_recv()
        out_ref[...] = (slots_ref[0] + slots_ref[1]) + (slots_ref[2] + slots_ref[3])
        sib.wait_send()
        for cp in sends:
            cp.wait_send()

    vm = pl.BlockSpec(memory_space=pltpu.VMEM)
    return pl.pallas_call(
        body, name="small_grads_all_reduce",
        in_specs=[vm], out_specs=vm,
        out_shape=jax.ShapeDtypeStruct((rows, 128), F32),
        scratch_shapes=[pltpu.VMEM((rows, 128), F32), pltpu.VMEM((4, rows, 128), F32),
                        pltpu.SemaphoreType.DMA((4,)), pltpu.SemaphoreType.DMA((4,))],
        compiler_params=pltpu.CompilerParams(vmem_limit_bytes=V7X_VMEM_LIMIT),
    )(buf)


def _adamw(w, g, m, v):
    m = ADAM_B1 * m + (1.0 - ADAM_B1) * g
    v = ADAM_B2 * v + (1.0 - ADAM_B2) * (g * g)
    m_hat = m / (1.0 - ADAM_B1 ** ADAM_STEP)
    v_hat = v / (1.0 - ADAM_B2 ** ADAM_STEP)
    delta = -ADAM_LR * (m_hat / (jnp.sqrt(v_hat) + ADAM_EPS) + ADAM_WD * w)
    return delta, m, v


def _reduce_adamw_call(recvs, w, m, v, name):
    nl = len(recvs)
    _, r, c = recvs[0].shape
    rb = min(r, 256)
    nb = r // rb

    def body(*refs):
        recv_refs = refs[:nl]
        w_ref, m_ref, v_ref, g_ref, d_ref, nm_ref, nv_ref = refs[nl:]
        for l in range(nl):
            @pl.when(pl.program_id(0) == l)
            def _(l=l):
                g = recv_refs[l][0].astype(F32)
                for j in range(1, N_DEV):
                    g = g + recv_refs[l][j].astype(F32)
                delta, nm, nv = _adamw(w_ref[0], g, m_ref[0], v_ref[0])
                g_ref[0] = g
                d_ref[0] = delta
                nm_ref[0] = nm
                nv_ref[0] = nv

    def recv_spec(l):
        return pl.BlockSpec((N_DEV, rb, c), lambda lg, i: (0, jnp.where(lg == l, i, jnp.where(lg < l, 0, nb - 1)), 0))

    blk = pl.BlockSpec((1, rb, c), lambda lg, i: (lg, i, 0))
    shp = jax.ShapeDtypeStruct((nl, r, c), F32)
    return pl.pallas_call(
        body, name=name, grid=(nl, nb),
        in_specs=[recv_spec(l) for l in range(nl)] + [blk, blk, blk],
        out_specs=[blk, blk, blk, blk], out_shape=[shp, shp, shp, shp],
        compiler_params=pltpu.CompilerParams(dimension_semantics=("arbitrary", "arbitrary"), vmem_limit_bytes=V7X_VMEM_LIMIT),
    )(*recvs, w, m, v)


def _small_adamw_call(w, g, m, v):
    def body(w_ref, g_ref, m_ref, v_ref, d_ref, nm_ref, nv_ref):
        delta, nm, nv = _adamw(w_ref[...], g_ref[...], m_ref[...], v_ref[...])
        d_ref[...] = delta
        nm_ref[...] = nm
        nv_ref[...] = nv

    shp = jax.ShapeDtypeStruct(w.shape, F32)
    vm = pl.BlockSpec(memory_space=pltpu.VMEM)
    return pl.pallas_call(body, name="small_adamw", in_specs=[vm] * 4, out_specs=[vm] * 3, out_shape=[shp, shp, shp],
                          compiler_params=pltpu.CompilerParams(vmem_limit_bytes=V7X_VMEM_LIMIT))(w, g, m, v)


def _own_slot(landed, own, me):
    return lax.dynamic_update_slice_in_dim(landed, own[None], me, 0)


def _own_cols(landed, own, me):
    return lax.dynamic_update_slice_in_dim(landed, own, me * own.shape[1], 1)


_GATHER_MODE = dict(w_in="block", w_out="block", w_ff1="cols_in", w_ff2="block", w_ple_gate="block", w_ple_proj="cols_in")
_SCATTER_MODE = dict(w_in="slot", w_out="slot", w_ff1="cols_out", w_ff2="slot", w_ple_gate="slot", w_ple_proj="cols_out")
_GROUP_A = ("w_in", "w_out")
_GROUP_B = ("w_ff1", "w_ff2", "w_ple_gate", "w_ple_proj")


def _gathered_full(k, landed, own, me):
    if _GATHER_MODE[k] == "cols_in":
        return _own_cols(landed, own, me)
    got = _own_slot(landed, own, me)
    n, r, c = got.shape
    if k == "w_in":
        return jnp.transpose(got, (1, 0, 2)).reshape(r, n * c)
    return got.reshape(n * r, c)


def _grad_send(k, g):
    if k == "w_in":
        r, c8 = g.shape
        return jnp.transpose(g.reshape(r, N_DEV, c8 // N_DEV), (1, 0, 2)).astype(BF16)
    if _SCATTER_MODE[k] == "cols_out":
        return g.astype(BF16)
    r8, c = g.shape
    return g.reshape(N_DEV, r8 // N_DEV, c).astype(BF16)


def _grad_own(k, send, me):
    if _SCATTER_MODE[k] == "cols_out":
        cw = send.shape[1] // N_DEV
        return lax.dynamic_slice_in_dim(send, me * cw, cw, 1)
    return lax.dynamic_index_in_dim(send, me, 0, keepdims=False)


_SMALL_ORDER = ("norm_mix_g", "sgu_w", "sgu_b", "sgu_ln_g", "sgu_ln_b", "conv_w", "pool_w", "pool_scale",
                "norm_ff_g", "norm_ple_g", "final_g")


def _pack_small(d):
    pieces, layout = [], []
    for k in _SMALL_ORDER:
        flat = d[k].reshape(-1)
        n = flat.shape[0]
        pad = (-n) % 128
        pieces.append(jnp.pad(flat, (0, pad)))
        layout.append((k, d[k].shape, n, n + pad))
    flat = jnp.concatenate(pieces)
    pad = (-flat.shape[0]) % 1024
    return jnp.pad(flat, (0, pad)).reshape(-1, 128), layout


def _unpack_small(buf, layout):
    flat = buf.reshape(-1)
    out, off = {}, 0
    for k, shape, n, padded in layout:
        out[k] = flat[off:off + n].reshape(shape)
        off += padded
    return out


def kernel(x, p, norm_mix_g, w_in, sgu_w, sgu_b, sgu_ln_g, sgu_ln_b, conv_w, pool_w, pool_scale, w_out, norm_ff_g, w_ff1, w_ff2, norm_ple_g, w_ple_gate, w_ple_proj, final_g, loss_target, m_norm_mix_g, m_w_in, m_sgu_w, m_sgu_b, m_sgu_ln_g, m_sgu_ln_b, m_conv_w, m_pool_w, m_pool_scale, m_w_out, m_norm_ff_g, m_w_ff1, m_w_ff2, m_norm_ple_g, m_w_ple_gate, m_w_ple_proj, m_final_g, v_norm_mix_g, v_w_in, v_sgu_w, v_sgu_b, v_sgu_ln_g, v_sgu_ln_b, v_conv_w, v_pool_w, v_pool_scale, v_w_out, v_norm_ff_g, v_w_ff1, v_w_ff2, v_norm_ple_g, v_w_ple_gate, v_w_ple_proj, v_final_g):
    t = x.shape[1]
    xc, yc_, cc = _coords()
    me = 4 * xc + 2 * yc_ + cc
    tm = lambda want: min(want, t)

    shard = dict(w_in=w_in, w_out=w_out, w_ff1=w_ff1, w_ff2=w_ff2, w_ple_gate=w_ple_gate, w_ple_proj=w_ple_proj)
    conv_pad = jnp.zeros((16, 128), F32).at[0:DEPTH * 3, 0:D_B // N_DEV].set(conv_w.reshape(DEPTH * 3, D_B // N_DEV))
    srcs, groups = [], []
    for l in range(DEPTH):
        for names in (_GROUP_A, _GROUP_B):
            groups.append(list(range(len(srcs), len(srcs) + len(names))))
            srcs += [shard[k][l].astype(BF16) for k in names]
            if l == 0 and names is _GROUP_A:
                groups[-1].append(len(srcs))
                srcs.append(conv_pad)
    modes = []
    for idxs, names in zip(groups, (_GROUP_A, _GROUP_B) * DEPTH):
        modes += [_GATHER_MODE[k] for k in names] + ["block"] * (len(idxs) - len(names))
    ag_sems, ag_srcs, ag_lands, _ = _exchange_start(srcs, modes, groups, "weights_gather_start")

    def gathered(l, which, after):
        idxs = groups[2 * l + which]
        landed = _exchange_wait([ag_sems[2 * l + which]], [ag_srcs[i] for i in idxs], [ag_lands[i] for i in idxs],
                                [modes[i] for i in idxs], [list(range(len(idxs)))], after, f"weights_gather_wait_{l}_{which}")
        full = {}
        for k, i, got in zip((_GROUP_A, _GROUP_B)[which], idxs, landed):
            full[k] = _gathered_full(k, got, ag_srcs[i], me)
        if l == 0 and which == 0:
            got = _own_slot(landed[-1], ag_srcs[idxs[-1]], me)
            full["conv_w"] = jnp.transpose(got[:, 0:DEPTH * 3, 0:D_B // N_DEV].reshape(N_DEV, DEPTH, 3, D_B // N_DEV),
                                           (1, 2, 0, 3)).reshape(DEPTH, 3, D_B)
        return full

    idx = jnp.arange(D_A)
    pmat = ((idx[:, None] // 64) == (idx[None, :] // 64)).astype(BF16) * (1.0 / 64.0)
    pmat = pmat.astype(BF16)
    tril = jnp.tril(jnp.ones((CHUNK, CHUNK), F32))
    trilcat = jnp.concatenate([tril, tril], axis=1)
    headsel = ((idx[:, None] // 64) == jnp.arange(CHUNK)[None, :]).astype(BF16)
    row = lambda a: a.reshape(1, -1)

    def mix_params(l, tile):
        wm = sgu_w[l] * tril[None]
        wcat = jnp.stack([jnp.concatenate([wm[2 * j], wm[2 * j + 1]], axis=1) for j in range(3)]).astype(BF16)
        wtcat = jnp.stack([jnp.concatenate([wm[2 * j].T, wm[2 * j + 1].T], axis=1) for j in range(3)]).astype(BF16)
        bmat = jnp.tile(jnp.repeat(sgu_b[l].T, 64, axis=1), (tile // CHUNK, 1))
        bd = jnp.zeros((D_C, D_C), F32)
        for gi in range(4):
            bd = bd.at[gi * 64:(gi + 1) * 64, gi * 64:(gi + 1) * 64].set(pool_w[l, gi])
        mp = dict(pmat=pmat, ln_g=row(sgu_ln_g[l]), ln_b=row(sgu_ln_b[l]), wcat=wcat, bmat=bmat, conv_w=conv_full[l],
                  bd=bd.astype(BF16), pool_scale=row(pool_scale[l]))
        return mp, wtcat

    xs = x.reshape(t, D)
    saved, full_w = [], []
    conv_full = None
    for l in range(DEPTH):
        wa = gathered(l, 0, xs)
        if l == 0:
            conv_full = wa["conv_w"]
        mp, _ = mix_params(l, tm(TM_MIX_FWD))
        x1, proj = _mix_fwd_call(xs, row(norm_mix_g[l]), wa["w_in"], wa["w_out"], mp, tm(TM_MIX_FWD), l)
        wb = gathered(l, 1, x1)
        x2, x3, r, gate = _ffn_fwd_call(x1, p[l, 0], row(norm_ff_g[l]), row(norm_ple_g[l]), wb["w_ff1"], wb["w_ff2"],
                                        wb["w_ple_gate"], wb["w_ple_proj"], tm(TM_FFN_FWD), l)
        saved.append((xs, proj, x1, r, x2, gate))
        full_w.append({**wa, **wb})
        xs = x3

    sq, dx, dfinal = _loss_call(xs, loss_target.reshape(t, D), row(final_g), tm(TM_LOSS))
    loss = lax.psum(jnp.sum(sq) * (0.5 / D), ("x", "y", "c"))

    small = {k: [None] * DEPTH for k in _SMALL_ORDER if k != "final_g"}
    ex = {}
    token = None

    def after_start(g):
        return g if token is None else g + token[0:1, 0:1]

    def start_exchange(l, which, grads):
        names = (_GROUP_A, _GROUP_B)[which]
        sends = [_grad_send(k, grads[k]) for k in names]
        sems, srcs_thru, lands, tok = _exchange_start(sends, [_SCATTER_MODE[k] for k in names], [list(range(len(names)))],
                                                      f"grads_exchange_start_{l}_{which}")
        ex[(l, which)] = (sems[0], srcs_thru, lands)
        return tok

    for l in reversed(range(DEPTH)):
        x0, proj, x1, r, x2, gate = saved[l]
        fw = full_w[l]
        dx2, dgple, dwg, dwp = _ple_bwd_call(dx, x2, gate, p[l, 0], after_start(row(norm_ple_g[l])), fw["w_ple_gate"], fw["w_ple_proj"],
                                             tm(TM_PLE_BWD), l)
        da, dw2 = _ffn_bwd_hidden_call(dx2, r, fw["w_ff2"], tm(TM_FFN_BWD), l)
        dx1, dgff, dw1 = _ffn_bwd_input_call(da, x1, dx2, row(norm_ff_g[l]), fw["w_ff1"], tm(TM_FFN_BWD), l)
        token = start_exchange(l, 1, dict(w_ff1=dw1, w_ff2=dw2, w_ple_gate=dwg, w_ple_proj=dwp))
        mp, wtcat = mix_params(l, tm(TM_MIX_BWD))
        (dx, dgmix, dwcat, dsb, dlng, dlnb, dconv, dbd, dscale, dwin, dwout) = _mix_bwd_call(
            dx1, x0, proj, after_start(row(norm_mix_g[l])), fw["w_in"], fw["w_out"], mp, wtcat, trilcat, headsel, tm(TM_MIX_BWD), l)
        token = start_exchange(l, 0, dict(w_in=dwin, w_out=dwout))
        small["norm_mix_g"][l] = dgmix[0]
        small["sgu_w"][l] = jnp.stack([dwcat[h // 2][:, (h % 2) * CHUNK:(h % 2 + 1) * CHUNK] for h in range(6)])
        small["sgu_b"][l] = dsb[:, 0:6].T
        small["sgu_ln_g"][l], small["sgu_ln_b"][l] = dlng[0], dlnb[0]
        small["conv_w"][l] = dconv
        small["pool_w"][l] = jnp.stack([dbd[gi * 64:(gi + 1) * 64, gi * 64:(gi + 1) * 64] for gi in range(4)])
        small["pool_scale"][l] = dscale[0]
        small["norm_ff_g"][l], small["norm_ple_g"][l] = dgff[0], dgple[0]
    grad_x = dx.reshape(1, t, D)

    state = dict(w_in=(w_in, m_w_in, v_w_in), w_out=(w_out, m_w_out, v_w_out), w_ff1=(w_ff1, m_w_ff1, v_w_ff1),
                 w_ff2=(w_ff2, m_w_ff2, v_w_ff2), w_ple_gate=(w_ple_gate, m_w_ple_gate, v_w_ple_gate),
                 w_ple_proj=(w_ple_proj, m_w_ple_proj, v_w_ple_proj))
    res = {}
    after = dx
    for which in (1, 0):
        names = (_GROUP_A, _GROUP_B)[which]
        n = len(names)
        sems = [ex[(l, which)][0] for l in range(DEPTH)]
        srcs_thru = [s_ for l in range(DEPTH) for s_ in ex[(l, which)][1]]
        lands = [a_ for l in range(DEPTH) for a_ in ex[(l, which)][2]]
        landed = _exchange_wait(sems, srcs_thru, lands, [_SCATTER_MODE[k] for k in names] * DEPTH,
                                [list(range(l * n, (l + 1) * n)) for l in range(DEPTH)], after, f"grads_exchange_wait_{which}")
        for i, k in enumerate(names):
            recvs = [_own_slot(landed[l * n + i], _grad_own(k, srcs_thru[l * n + i], me), me) for l in range(DEPTH)]
            res[k] = _reduce_adamw_call(recvs, *state[k], "reduce_adamw_" + k)
        after = res[names[-1]][0]

    small_g = {k: jnp.stack(vs) for k, vs in small.items()}
    small_g["final_g"] = dfinal[0]
    gbuf, layout = _pack_small(small_g)
    gsum = _small_all_reduce_call(gbuf)
    conv_cols = lambda a: lax.dynamic_slice_in_dim(a, me * (D_B // N_DEV), D_B // N_DEV, axis=2)
    pad_conv = lambda a: jnp.zeros((DEPTH, 3, D_B), F32).at[:, :, 0:D_B // N_DEV].set(a)
    small_w = dict(norm_mix_g=norm_mix_g, sgu_w=sgu_w, sgu_b=sgu_b, sgu_ln_g=sgu_ln_g, sgu_ln_b=sgu_ln_b, conv_w=pad_conv(conv_w),
                   pool_w=pool_w, pool_scale=pool_scale, norm_ff_g=norm_ff_g, norm_ple_g=norm_ple_g, final_g=final_g)
    small_m = dict(norm_mix_g=m_norm_mix_g, sgu_w=m_sgu_w, sgu_b=m_sgu_b, sgu_ln_g=m_sgu_ln_g, sgu_ln_b=m_sgu_ln_b,
                   conv_w=pad_conv(m_conv_w), pool_w=m_pool_w, pool_scale=m_pool_scale, norm_ff_g=m_norm_ff_g,
                   norm_ple_g=m_norm_ple_g, final_g=m_final_g)
    small_v = dict(norm_mix_g=v_norm_mix_g, sgu_w=v_sgu_w, sgu_b=v_sgu_b, sgu_ln_g=v_sgu_ln_g, sgu_ln_b=v_sgu_ln_b,
                   conv_w=pad_conv(v_conv_w), pool_w=v_pool_w, pool_scale=v_pool_scale,
                   norm_ff_g=v_norm_ff_g, norm_ple_g=v_norm_ple_g, final_g=v_final_g)
    gs = _unpack_small(gsum, layout)
    gs_local = dict(gs)
    gs_local["conv_w"] = pad_conv(conv_cols(gs["conv_w"]))
    g_loc, _ = _pack_small(gs_local)
    wbuf, _ = _pack_small(small_w)
    mbuf, _ = _pack_small(small_m)
    vbuf, _ = _pack_small(small_v)
    dbuf, nmbuf, nvbuf = _small_adamw_call(wbuf, g_loc, mbuf, vbuf)
    sd, sm, sv = _unpack_small(dbuf, layout), _unpack_small(nmbuf, layout), _unpack_small(nvbuf, layout)
    unconv = lambda a: a[:, :, 0:D_B // N_DEV]
    for dct in (gs_local, sd, sm, sv):
        dct["conv_w"] = unconv(dct["conv_w"])

    order = ["norm_mix_g", "w_in", "sgu_w", "sgu_b", "sgu_ln_g", "sgu_ln_b", "conv_w", "pool_w", "pool_scale", "w_out",
             "norm_ff_g", "w_ff1", "w_ff2", "norm_ple_g", "w_ple_gate", "w_ple_proj", "final_g"]
    outs = [loss, grad_x]
    for which in range(4):
        for k in order:
            if k in res:
                outs.append(res[k][which])
            else:
                outs.append((gs_local, sd, sm, sv)[which][k])
    return tuple(outs)
```

```python
import functools
import math

import jax
import jax.numpy as jnp
from jax import lax
from jax.experimental import pallas as pl
from jax.experimental.pallas import tpu as pltpu

F32 = jnp.float32
BF16 = jnp.bfloat16

D = 1024
D_IN = 2176
D_A = 384
D_B = 384
D_C = 256
D_FF = 4096
D_PLE = 256
DEPTH = 4
CHUNK = 128
HALO = 16
FF_BLK = 1024
N_DEV = 8
RMS_EPS = 1e-6
LN_EPS = 1e-5
ADAM_LR = 0.001
ADAM_B1 = 0.9
ADAM_B2 = 0.999
ADAM_EPS = 1e-08
ADAM_WD = 0.01
ADAM_STEP = 10

TM_MIX_FWD = 1024
TM_FFN_FWD = 512
TM_LOSS = 512
TM_PLE_BWD = 1024
TM_FFN_BWD = 512
TM_MIX_BWD = 512
V7X_VMEM_LIMIT = 60000 * 1024

ANY = pl.BlockSpec(memory_space=pl.ANY)
HBM = pl.BlockSpec(memory_space=pltpu.HBM)
SEM = pl.BlockSpec(memory_space=pltpu.SEMAPHORE)
MESH = pl.DeviceIdType.MESH


def _params(vmem=V7X_VMEM_LIMIT):
    return pltpu.CompilerParams(dimension_semantics=("arbitrary",), vmem_limit_bytes=vmem)


def _in_hbm(a):
    return pltpu.with_memory_space_constraint(a, pltpu.HBM)


def _full(shape):
    nd = len(shape)
    return pl.BlockSpec(shape, lambda i: (0,) * nd)


def _rows(tm, cols):
    return pl.BlockSpec((tm, cols), lambda i: (i, 0))


def _mm(a, b):
    return jnp.dot(a, b, preferred_element_type=F32)


def _mm_nt(a, b):
    return lax.dot_general(a, b, (((1,), (1,)), ((), ())), preferred_element_type=F32)


def _mm_tn(a, b):
    return lax.dot_general(a, b, (((0,), (0,)), ((), ())), preferred_element_type=F32)


def _erf(x):
    ax = jnp.abs(x)
    t = 1.0 / (1.0 + 0.3275911 * ax)
    poly = t * (0.254829592 + t * (-0.284496736 + t * (1.421413741 + t * (-1.453152027 + t * 1.061405429))))
    y = 1.0 - poly * jnp.exp(-ax * ax)
    return jnp.where(x < 0, -y, y)


def _gelu_and_grad(x):
    cdf = 0.5 * (1.0 + _erf(x * (1.0 / math.sqrt(2.0))))
    pdf = jnp.exp(-0.5 * x * x) * (1.0 / math.sqrt(2.0 * math.pi))
    return x * cdf, cdf + x * pdf


def _rms(x, g):
    rstd = lax.rsqrt(jnp.mean(x * x, axis=-1, keepdims=True) + RMS_EPS)
    xhat = x * rstd
    return xhat * g, xhat, rstd


def _rms_bwd(dy, g, xhat, rstd):
    dg = jnp.sum(dy * xhat, axis=0, keepdims=True)
    dxh = dy * g
    dx = rstd * (dxh - xhat * jnp.mean(dxh * xhat, axis=-1, keepdims=True))
    return dx, dg


def _shift_down(ext, k):
    return pltpu.roll(ext, k, 0)[HALO:, :]


def _shift_up(ext, k):
    n = ext.shape[0]
    return pltpu.roll(ext, n - k, 0)[: n - HALO, :]


def _pool_select(s2, s4, s8, s16):
    lane = lax.broadcasted_iota(jnp.int32, s2.shape, 1)
    return jnp.where(lane < 64, s2, jnp.where(lane < 128, s4, jnp.where(lane < 192, s8, s16)))


def _pool_inv_count(tile_start, tm):
    pos = lax.broadcasted_iota(jnp.int32, (tm, D_C), 0) + tile_start + 1
    lane = lax.broadcasted_iota(jnp.int32, (tm, D_C), 1)
    win = jnp.where(lane < 64, 2, jnp.where(lane < 128, 4, jnp.where(lane < 192, 8, 16)))
    return 1.0 / jnp.minimum(pos, win).astype(F32)


def _head_halves(a):
    lane = lax.broadcasted_iota(jnp.int32, a.shape, 1)
    even = (lane & 64) == 0
    return jnp.where(even, a, 0.0).astype(BF16), jnp.where(even, 0.0, a).astype(BF16)


def _head_stack(lo, hi, j, nch):
    return jnp.concatenate(
        [jnp.concatenate([lo[c * CHUNK:(c + 1) * CHUNK, j * 128:(j + 1) * 128], hi[c * CHUNK:(c + 1) * CHUNK, j * 128:(j + 1) * 128]], axis=0)
         for c in range(nch)], axis=1)


def _chunks_to_lanes(a, j, nch):
    return jnp.concatenate([a[c * CHUNK:(c + 1) * CHUNK, j * 128:(j + 1) * 128] for c in range(nch)], axis=1)


def _lanes_to_chunks(o, nch):
    return jnp.concatenate([o[:, c * CHUNK:(c + 1) * CHUNK] for c in range(nch)], axis=0)


def _loads(pairs, sem):
    return [pltpu.make_async_copy(src, dst, sem.at[n]) for n, (src, dst) in enumerate(pairs)]


def _stage_bf16(acc, stage):
    rows = acc.shape[0]
    strip = min(rows, 128)

    @pl.loop(0, rows // strip)
    def _(n):
        sl = pl.ds(pl.multiple_of(n * strip, strip), strip)
        stage[sl, :] = acc[sl, :].astype(BF16)


def _on_first_step(fn):
    pl.when(pl.program_id(0) == 0)(fn)


def _mixers_fwd(pf, halo_hc, halo_zc, tile_start, prm):
    tm = pf.shape[0]
    nch = tm // CHUNK
    u, v = pf[:, 0:D_A], pf[:, D_A:2 * D_A]
    zb, gb, gc = pf[:, 768:1152], pf[:, 1152:1536], pf[:, 1536:1920]
    zc = pf[:, 1920:2176]
    r = {}
    gu, r["dgelu_u"] = _gelu_and_grad(u)
    gv, r["dgelu_v"] = _gelu_and_grad(v)
    pmat = prm["pmat"][...]
    mu = _mm(gv.astype(BF16), pmat)
    dv = gv - mu
    var = _mm((dv * dv).astype(BF16), pmat)
    rstd = lax.rsqrt(var + LN_EPS)
    xh = dv * rstd
    vlo, vhi = _head_halves(xh * prm["ln_g"][...] + prm["ln_b"][...])
    cols, v2s = [], []
    for j in range(3):
        v2 = _head_stack(vlo, vhi, j, nch)
        v2s.append(v2)
        cols.append(_lanes_to_chunks(_mm(prm["wcat"][j], v2), nch))
    mixed = jnp.concatenate(cols, axis=1) + prm["bmat"][...]
    ya = gu * mixed
    r.update(gu=gu, mixed=mixed, v2s=v2s, xh=xh, ln_rstd=rstd)
    w0, w1, w2 = prm["conv_w"][0:1, :], prm["conv_w"][1:2, :], prm["conv_w"][2:3, :]
    hc = gc * zb
    ext = jnp.concatenate([halo_hc, hc], axis=0)
    h1, h2 = _shift_down(ext, 1), _shift_down(ext, 2)
    yc = w2 * hc + w1 * h1 + w0 * h2
    yb = gb * yc
    r.update(hc=hc, h1=h1, h2=h2, yc=yc, zb=zb, gb=gb, gc=gc, w0=w0, w1=w1, w2=w2)
    ext = jnp.concatenate([halo_zc, zc], axis=0)
    s2 = ext + pltpu.roll(ext, 1, 0)
    s4 = s2 + pltpu.roll(s2, 2, 0)
    s8 = s4 + pltpu.roll(s4, 4, 0)
    s16 = s8 + pltpu.roll(s8, 8, 0)
    inv = _pool_inv_count(tile_start, tm)
    pooled = _pool_select(s2, s4, s8, s16)[HALO:, :] * inv - zc
    pooledb = pooled.astype(BF16)
    pm = _mm(pooledb, prm["bd"][...])
    scale = prm["pool_scale"][...]
    ycm = pm * scale
    r.update(inv=inv, pooledb=pooledb, pm=pm, scale=scale, zc=zc)
    r["ycat"] = jnp.concatenate([ya, yb, ycm], axis=1)
    return r


_MIX_PARAM_NAMES = ("pmat", "ln_g", "ln_b", "wcat", "bmat", "conv_w", "bd", "pool_scale")


def _mix_param_specs(tm):
    return [_full((D_A, D_A)), _full((1, D_A)), _full((1, D_A)), _full((3, CHUNK, 2 * CHUNK)), _full((tm, D_A)),
            _full((3, D_B)), _full((D_C, D_C)), _full((1, D_C))]


def _mix_fwd_call(x, g_mix, w_in, w_out, mp, tm, layer):
    t = x.shape[0]
    nt = t // tm

    def body(x_ref, g_ref, pmat, ln_g, ln_b, wcat, bmat, conv_w, bd, pool_scale, win_hbm, wout_hbm,
             x1_ref, proj_ref, win_s, wout_s, halo_hc, halo_zc, load_sem):
        i = pl.program_id(0)
        loads = _loads([(win_hbm, win_s), (wout_hbm, wout_s)], load_sem)

        @pl.when(i == 0)
        def _():
            for cp in loads:
                cp.start()
            halo_hc[...] = jnp.zeros_like(halo_hc)
            halo_zc[...] = jnp.zeros_like(halo_zc)

        prm = dict(pmat=pmat, ln_g=ln_g, ln_b=ln_b, wcat=wcat, bmat=bmat, conv_w=conv_w, bd=bd, pool_scale=pool_scale)
        xv = x_ref[...]
        h, _, _ = _rms(xv, g_ref[...])
        _on_first_step(loads[0].wait)
        pf = _mm(h.astype(BF16), win_s[...])
        proj_ref[...] = pf.astype(BF16)
        r = _mixers_fwd(pf, halo_hc[...], halo_zc[...], i * tm, prm)
        halo_hc[...] = r["hc"][tm - HALO:, :]
        halo_zc[...] = r["zc"][tm - HALO:, :]
        _on_first_step(loads[1].wait)
        x1_ref[...] = xv + _mm(r["ycat"].astype(BF16), wout_s[...])

    return pl.pallas_call(
        body, name=f"mix_fwd_{layer}", grid=(nt,),
        in_specs=[_rows(tm, D), _full((1, D))] + _mix_param_specs(tm) + [HBM, HBM],
        out_specs=[_rows(tm, D), _rows(tm, D_IN)],
        out_shape=[jax.ShapeDtypeStruct((t, D), F32), jax.ShapeDtypeStruct((t, D_IN), BF16)],
        scratch_shapes=[pltpu.VMEM((D, D_IN), BF16), pltpu.VMEM((D, D), BF16),
                        pltpu.VMEM((HALO, D_B), F32), pltpu.VMEM((HALO, D_C), F32), pltpu.SemaphoreType.DMA((2,))],
        compiler_params=_params(),
    )(x, g_mix, *[mp[k] for k in _MIX_PARAM_NAMES], _in_hbm(w_in), _in_hbm(w_out))


def _ffn_fwd_call(x1, p, g_ff, g_ple, w1, w2, wg, wp, tm, layer):
    t = x1.shape[0]
    nt = t // tm

    def body(x1_ref, p_ref, gff_ref, gple_ref, w1_hbm, w2_hbm, wg_hbm, wp_hbm,
             x2_ref, x3_ref, r_ref, gate_ref, w1_s, w2_s, wg_s, wp_s, load_sem):
        i = pl.program_id(0)
        loads = _loads([(w1_hbm, w1_s), (w2_hbm, w2_s), (wg_hbm, wg_s), (wp_hbm, wp_s)], load_sem)

        @pl.when(i == 0)
        def _():
            for cp in loads:
                cp.start()

        x1v = x1_ref[...]
        h2, _, _ = _rms(x1v, gff_ref[...])
        h2b = h2.astype(BF16)
        acc = x1v
        _on_first_step(loads[0].wait)
        _on_first_step(loads[1].wait)
        for j in range(D_FF // FF_BLK):
            blk = slice(j * FF_BLK, (j + 1) * FF_BLK)
            rj = jnp.maximum(_mm(h2b, w1_s[:, blk]), 0.0)
            r_ref[:, blk] = rj.astype(BF16)
            acc = acc + _mm((rj * rj).astype(BF16), w2_s[blk, :])
        x2_ref[...] = acc
        n3, _, _ = _rms(acc, gple_ref[...])
        _on_first_step(loads[2].wait)
        gate = jax.nn.sigmoid(_mm(n3.astype(BF16), wg_s[...]))
        gate_ref[...] = gate.astype(BF16)
        _on_first_step(loads[3].wait)
        pp = _mm(p_ref[...].astype(BF16), wp_s[...])
        x3_ref[...] = acc + pp * gate

    return pl.pallas_call(
        body, name=f"ffn_fwd_{layer}", grid=(nt,),
        in_specs=[_rows(tm, D), _rows(tm, D_PLE), _full((1, D)), _full((1, D)), HBM, HBM, HBM, HBM],
        out_specs=[_rows(tm, D), _rows(tm, D), _rows(tm, D_FF), _rows(tm, D)],
        out_shape=[jax.ShapeDtypeStruct((t, D), F32), jax.ShapeDtypeStruct((t, D), F32),
                   jax.ShapeDtypeStruct((t, D_FF), BF16), jax.ShapeDtypeStruct((t, D), BF16)],
        scratch_shapes=[pltpu.VMEM((D, D_FF), BF16), pltpu.VMEM((D_FF, D), BF16),
                        pltpu.VMEM((D, D), BF16), pltpu.VMEM((D_PLE, D), BF16), pltpu.SemaphoreType.DMA((4,))],
        compiler_params=_params(),
    )(x1, p, g_ff, g_ple, _in_hbm(w1), _in_hbm(w2), _in_hbm(wg), _in_hbm(wp))


def _loss_call(xl, target, final_g, tm):
    t = xl.shape[0]
    nt = t // tm

    def body(x_ref, t_ref, g_ref, sq_ref, dx_ref, dg_ref):
        i = pl.program_id(0)

        @pl.when(i == 0)
        def _():
            sq_ref[...] = jnp.zeros_like(sq_ref)
            dg_ref[...] = jnp.zeros_like(dg_ref)

        g = g_ref[...]
        y, xhat, rstd = _rms(x_ref[...], g)
        err = y - t_ref[...]
        sq_ref[...] += jnp.sum(err * err, axis=0, keepdims=True)
        dx, dg = _rms_bwd(err * (1.0 / D), g, xhat, rstd)
        dx_ref[...] = dx
        dg_ref[...] += dg

    return pl.pallas_call(
        body, name="loss_head", grid=(nt,),
        in_specs=[_rows(tm, D), _rows(tm, D), _full((1, D))],
        out_specs=[_full((1, D)), _rows(tm, D), _full((1, D))],
        out_shape=[jax.ShapeDtypeStruct((1, D), F32), jax.ShapeDtypeStruct((t, D), F32), jax.ShapeDtypeStruct((1, D), F32)],
        compiler_params=_params(),
    )(xl, target, final_g)


def _ple_bwd_call(dx3, x2, gate, p, g_ple, wg, wp, tm, layer):
    t = dx3.shape[0]
    nt = t // tm

    def body(dx3_ref, x2_ref, gate_ref, p_ref, g_ref, wg_hbm, wp_hbm,
             dx2_ref, dg_ref, dwg_hbm, dwp_hbm, wg_s, wp_s, dwg_acc, dwp_acc, load_sem):
        i = pl.program_id(0)
        loads = _loads([(wp_hbm, wp_s), (wg_hbm, wg_s)], load_sem)

        @pl.when(i == 0)
        def _():
            for cp in loads:
                cp.start()
            dwg_acc[...] = jnp.zeros_like(dwg_acc)
            dwp_acc[...] = jnp.zeros_like(dwp_acc)
            dg_ref[...] = jnp.zeros_like(dg_ref)

        g = g_ref[...]
        dx3v = dx3_ref[...]
        gatev = gate_ref[...].astype(F32)
        pb = p_ref[...].astype(BF16)
        _on_first_step(loads[0].wait)
        pp = _mm(pb, wp_s[...])
        dwp_acc[...] += _mm_tn(pb, (dx3v * gatev).astype(BF16))
        dgpre = (dx3v * pp * gatev * (1.0 - gatev)).astype(BF16)
        n3, xhat, rstd = _rms(x2_ref[...], g)
        dwg_acc[...] += _mm_tn(n3.astype(BF16), dgpre)
        _on_first_step(loads[1].wait)
        dn3 = _mm_nt(dgpre, wg_s[...])
        dx, dg = _rms_bwd(dn3, g, xhat, rstd)
        dx2_ref[...] = dx3v + dx
        dg_ref[...] += dg

        @pl.when(i == nt - 1)
        def _():
            _stage_bf16(dwg_acc, wg_s)
            _stage_bf16(dwp_acc, wp_s)
            pltpu.sync_copy(wg_s, dwg_hbm)
            pltpu.sync_copy(wp_s, dwp_hbm)

    return pl.pallas_call(
        body, name=f"ple_bwd_{layer}", grid=(nt,),
        in_specs=[_rows(tm, D), _rows(tm, D), _rows(tm, D), _rows(tm, D_PLE), _full((1, D)), HBM, HBM],
        out_specs=[_rows(tm, D), _full((1, D)), HBM, HBM],
        out_shape=[jax.ShapeDtypeStruct((t, D), F32), jax.ShapeDtypeStruct((1, D), F32),
                   pltpu.HBM((D, D), BF16), pltpu.HBM((D_PLE, D), BF16)],
        scratch_shapes=[pltpu.VMEM((D, D), BF16), pltpu.VMEM((D_PLE, D), BF16),
                        pltpu.VMEM((D, D), F32), pltpu.VMEM((D_PLE, D), F32), pltpu.SemaphoreType.DMA((2,))],
        compiler_params=_params(),
    )(dx3, x2, gate, p, g_ple, _in_hbm(wg), _in_hbm(wp))


def _ffn_bwd_hidden_call(dx2, r, w2, tm, layer):
    t = dx2.shape[0]
    nt = t // tm

    def body(dx2_ref, r_ref, w2_hbm, da_ref, dw2_hbm, w2_s, dw2_acc, load_sem):
        i = pl.program_id(0)
        loads = _loads([(w2_hbm, w2_s)], load_sem)

        @pl.when(i == 0)
        def _():
            loads[0].start()
            dw2_acc[...] = jnp.zeros_like(dw2_acc)

        dxb = dx2_ref[...].astype(BF16)
        _on_first_step(loads[0].wait)
        for j in range(D_FF // FF_BLK):
            blk = slice(j * FF_BLK, (j + 1) * FF_BLK)
            rj = r_ref[:, blk].astype(F32)
            ds = _mm_nt(dxb, w2_s[blk, :])
            da_ref[:, blk] = (2.0 * rj * ds).astype(BF16)
            dw2_acc[blk, :] += _mm_tn((rj * rj).astype(BF16), dxb)

        @pl.when(i == nt - 1)
        def _():
            _stage_bf16(dw2_acc, w2_s)
            pltpu.sync_copy(w2_s, dw2_hbm)

    return pl.pallas_call(
        body, name=f"ffn_bwd_hidden_{layer}", grid=(nt,),
        in_specs=[_rows(tm, D), _rows(tm, D_FF), HBM],
        out_specs=[_rows(tm, D_FF), HBM],
        out_shape=[jax.ShapeDtypeStruct((t, D_FF), BF16), pltpu.HBM((D_FF, D), BF16)],
        scratch_shapes=[pltpu.VMEM((D_FF, D), BF16), pltpu.VMEM((D_FF, D), F32), pltpu.SemaphoreType.DMA((1,))],
        compiler_params=_params(),
    )(dx2, r, _in_hbm(w2))


def _ffn_bwd_input_call(da, x1, dx2, g_ff, w1, tm, layer):
    t = dx2.shape[0]
    nt = t // tm

    def body(da_ref, x1_ref, dx2_ref, g_ref, w1_hbm, dx1_ref, dg_ref, dw1_hbm, w1_s, dw1_acc, load_sem):
        i = pl.program_id(0)
        loads = _loads([(w1_hbm, w1_s)], load_sem)

        @pl.when(i == 0)
        def _():
            loads[0].start()
            dw1_acc[...] = jnp.zeros_like(dw1_acc)
            dg_ref[...] = jnp.zeros_like(dg_ref)

        g = g_ref[...]
        h2, xhat, rstd = _rms(x1_ref[...], g)
        h2b = h2.astype(BF16)
        _on_first_step(loads[0].wait)
        dh2 = jnp.zeros((tm, D), F32)
        for j in range(D_FF // FF_BLK):
            blk = slice(j * FF_BLK, (j + 1) * FF_BLK)
            daj = da_ref[:, blk]
            dh2 = dh2 + _mm_nt(daj, w1_s[:, blk])
            dw1_acc[:, blk] += _mm_tn(h2b, daj)
        dx, dg = _rms_bwd(dh2, g, xhat, rstd)
        dx1_ref[...] = dx2_ref[...] + dx
        dg_ref[...] += dg

        @pl.when(i == nt - 1)
        def _():
            _stage_bf16(dw1_acc, w1_s)
            pltpu.sync_copy(w1_s, dw1_hbm)

    return pl.pallas_call(
        body, name=f"ffn_bwd_input_{layer}", grid=(nt,),
        in_specs=[_rows(tm, D_FF), _rows(tm, D), _rows(tm, D), _full((1, D)), HBM],
        out_specs=[_rows(tm, D), _full((1, D)), HBM],
        out_shape=[jax.ShapeDtypeStruct((t, D), F32), jax.ShapeDtypeStruct((1, D), F32), pltpu.HBM((D, D_FF), BF16)],
        scratch_shapes=[pltpu.VMEM((D, D_FF), BF16), pltpu.VMEM((D, D_FF), F32), pltpu.SemaphoreType.DMA((1,))],
        compiler_params=_params(),
    )(da, x1, dx2, g_ff, _in_hbm(w1))


def _mix_bwd_call(dx1, x, proj, g_mix, w_in, w_out, mp, wtcat, trilcat, headsel, tm, layer):
    t = dx1.shape[0]
    nt = t // tm
    nch = tm // CHUNK
    hb = tm // HALO

    def rev(i):
        return nt - 1 - i

    def body(dx1_ref, x_ref, proj_ref, halo_ref, g_ref, pmat, ln_g, ln_b, wcat, bmat, conv_w, bd, pool_scale,
             wtcat_ref, tril_ref, sel_ref, win_hbm, wout_hbm,
             dx_ref, dg_ref, dwcat_ref, dsb_ref, dlng_ref, dlnb_ref, dconv_ref, dbd_ref, dscale_ref, dwin_hbm, dwout_hbm,
             win_s, wout_s, dwin_acc, dwout_acc, dbm_acc, carry_yc, carry_q, dvn_s, load_sem):
        i = pl.program_id(0)
        ri = nt - 1 - i
        loads = _loads([(wout_hbm, wout_s), (win_hbm, win_s)], load_sem)

        @pl.when(i == 0)
        def _():
            for cp in loads:
                cp.start()
            for ref in (dwin_acc, dwout_acc, dbm_acc, carry_yc, carry_q, dg_ref, dwcat_ref, dlng_ref, dlnb_ref,
                        dconv_ref, dbd_ref, dscale_ref):
                ref[...] = jnp.zeros_like(ref)

        prm = dict(pmat=pmat, ln_g=ln_g, ln_b=ln_b, wcat=wcat, bmat=bmat, conv_w=conv_w, bd=bd, pool_scale=pool_scale)
        g = g_ref[...]
        h, xhat, rstd = _rms(x_ref[...], g)
        hb16 = h.astype(BF16)
        dx1v = dx1_ref[...]
        dx1b = dx1v.astype(BF16)
        pf = proj_ref[...].astype(F32)
        ph = halo_ref[...].astype(F32) * (ri > 0).astype(F32)
        r = _mixers_fwd(pf, ph[:, 1536:1920] * ph[:, 768:1152], ph[:, 1920:2176], ri * tm, prm)

        dwout_acc[...] += _mm_tn(r["ycat"].astype(BF16), dx1b)
        _on_first_step(loads[0].wait)
        dycat = _mm_nt(dx1b, wout_s[...])
        dya, dyb, dyc = dycat[:, 0:D_A], dycat[:, D_A:D_A + D_B], dycat[:, D_A + D_B:D]

        dgu = dya * r["mixed"]
        dmix = dya * r["gu"]
        dmix_b = dmix.astype(BF16)
        dlo, dhi = _head_halves(dmix)
        dbm = dmix[0:CHUNK, :]
        for c in range(1, nch):
            dbm = dbm + dmix[c * CHUNK:(c + 1) * CHUNK, :]
        dbm_acc[...] += dbm
        for j in range(3):
            dm_all = _chunks_to_lanes(dmix_b, j, nch)
            dwcat_ref[j] += _mm_nt(dm_all, r["v2s"][j])
            dm2 = _head_stack(dlo, dhi, j, nch)
            dvn_s[:, j * 128:(j + 1) * 128] = _lanes_to_chunks(_mm(wtcat_ref[j], dm2), nch)
        dvn = dvn_s[...]
        xh = r["xh"]
        dlng_ref[...] += jnp.sum(dvn * xh, axis=0, keepdims=True)
        dlnb_ref[...] += jnp.sum(dvn, axis=0, keepdims=True)
        dxh = dvn * ln_g[...]
        pm_ = pmat[...]
        m1 = _mm(dxh.astype(BF16), pm_)
        m2 = _mm((dxh * xh).astype(BF16), pm_)
        dgv = r["ln_rstd"] * (dxh - m1 - xh * m2)
        du = dgu * r["dgelu_u"]
        dv = dgv * r["dgelu_v"]

        dgb = dyb * r["yc"]
        dyc2 = dyb * r["gb"]
        dconv_ref[0:1, :] += jnp.sum(dyc2 * r["h2"], axis=0, keepdims=True)
        dconv_ref[1:2, :] += jnp.sum(dyc2 * r["h1"], axis=0, keepdims=True)
        dconv_ref[2:3, :] += jnp.sum(dyc2 * r["hc"], axis=0, keepdims=True)
        ext = jnp.concatenate([dyc2, carry_yc[...]], axis=0)
        dhc = r["w2"] * dyc2 + r["w1"] * _shift_up(ext, 1) + r["w0"] * _shift_up(ext, 2)
        carry_yc[...] = dyc2[0:HALO, :]
        dgc = dhc * r["zb"]
        dzb = dhc * r["gc"]

        dscale_ref[...] += jnp.sum(dyc * r["pm"], axis=0, keepdims=True)
        dpm = (dyc * r["scale"]).astype(BF16)
        dbd_ref[...] += _mm_tn(r["pooledb"], dpm)
        dpooled = _mm_nt(dpm, bd[...])
        q = dpooled * r["inv"]
        ext = jnp.concatenate([q, carry_q[...]], axis=0)
        n = tm + HALO
        r2 = ext + pltpu.roll(ext, n - 1, 0)
        r4 = r2 + pltpu.roll(r2, n - 2, 0)
        r8 = r4 + pltpu.roll(r4, n - 4, 0)
        r16 = r8 + pltpu.roll(r8, n - 8, 0)
        dzc = _pool_select(r2, r4, r8, r16)[0:tm, :] - dpooled
        carry_q[...] = q[0:HALO, :]

        dproj = jnp.concatenate([du, dv, dzb, dgb, dgc, dzc], axis=1).astype(BF16)
        dwin_acc[...] += _mm_tn(hb16, dproj)
        _on_first_step(loads[1].wait)
        dh = _mm_nt(dproj, win_s[...])
        dx, dg = _rms_bwd(dh, g, xhat, rstd)
        dx_ref[...] = dx1v + dx
        dg_ref[...] += dg

        @pl.when(i == nt - 1)
        def _():
            pltpu.sync_copy(dwin_acc, dwin_hbm)
            _stage_bf16(dwout_acc, wout_s)
            pltpu.sync_copy(wout_s, dwout_hbm)
            for j in range(3):
                dwcat_ref[j] = dwcat_ref[j] * tril_ref[...]
            acc = dbm_acc[...]
            hi = acc.astype(BF16)
            lo = (acc - hi.astype(F32)).astype(BF16)
            dsb_ref[...] = _mm(hi, sel_ref[...]) + _mm(lo, sel_ref[...])

    return pl.pallas_call(
        body, name=f"mix_bwd_{layer}", grid=(nt,),
        in_specs=[pl.BlockSpec((tm, D), lambda i: (rev(i), 0)), pl.BlockSpec((tm, D), lambda i: (rev(i), 0)),
                  pl.BlockSpec((tm, D_IN), lambda i: (rev(i), 0)),
                  pl.BlockSpec((HALO, D_IN), lambda i: (jnp.maximum(rev(i) * hb - 1, 0), 0)),
                  _full((1, D))] + _mix_param_specs(tm)
                 + [_full((3, CHUNK, 2 * CHUNK)), _full((CHUNK, 2 * CHUNK)), _full((D_A, CHUNK)), HBM, HBM],
        out_specs=[pl.BlockSpec((tm, D), lambda i: (rev(i), 0)), _full((1, D)), _full((3, CHUNK, 2 * CHUNK)),
                   _full((CHUNK, CHUNK)), _full((1, D_A)), _full((1, D_A)), _full((3, D_B)), _full((D_C, D_C)),
                   _full((1, D_C)), HBM, HBM],
        out_shape=[jax.ShapeDtypeStruct((t, D), F32), jax.ShapeDtypeStruct((1, D), F32),
                   jax.ShapeDtypeStruct((3, CHUNK, 2 * CHUNK), F32), jax.ShapeDtypeStruct((CHUNK, CHUNK), F32),
                   jax.ShapeDtypeStruct((1, D_A), F32), jax.ShapeDtypeStruct((1, D_A), F32),
                   jax.ShapeDtypeStruct((3, D_B), F32), jax.ShapeDtypeStruct((D_C, D_C), F32),
                   jax.ShapeDtypeStruct((1, D_C), F32), pltpu.HBM((D, D_IN), F32), pltpu.HBM((D, D), BF16)],
        scratch_shapes=[pltpu.VMEM((D, D_IN), BF16), pltpu.VMEM((D, D), BF16),
                        pltpu.VMEM((D, D_IN), F32), pltpu.VMEM((D, D), F32), pltpu.VMEM((CHUNK, D_A), F32),
                        pltpu.VMEM((HALO, D_B), F32), pltpu.VMEM((HALO, D_C), F32), pltpu.VMEM((tm, D_A), F32),
                        pltpu.SemaphoreType.DMA((2,))],
        compiler_params=_params(),
    )(dx1, x, proj, proj, g_mix, *[mp[k] for k in _MIX_PARAM_NAMES], wtcat, trilcat, headsel, _in_hbm(w_in), _in_hbm(w_out))


def _coords():
    return lax.axis_index("x"), lax.axis_index("y"), lax.axis_index("c")


EFFECT = pltpu.SideEffectType.DATAFLOW_SIDE_EFFECTING


def _peer(k):
    x, y, c = _coords()
    px, py, pc = x ^ (k >> 2), y ^ ((k >> 1) & 1), c ^ (k & 1)
    return (px, py, pc), 4 * px + 2 * py + pc


def _landing_shape(shape, mode):
    if mode == "block":
        return (N_DEV,) + shape
    if mode == "slot":
        return shape
    if mode == "cols_in":
        return (shape[0], N_DEV * shape[1])
    return (N_DEV, shape[0], shape[1] // N_DEV)


def _pieces(src, land, mode, src_idx, land_idx):
    if mode == "block":
        return src, land.at[land_idx]
    if mode == "slot":
        return src.at[src_idx], land.at[land_idx]
    if mode == "cols_in":
        cw = src.shape[1]
        return src, land.at[:, pl.ds(pl.multiple_of(land_idx * cw, 128), cw)]
    cw = land.shape[2]
    return src.at[:, pl.ds(pl.multiple_of(src_idx * cw, 128), cw)], land.at[land_idx]


def _exchange_copy(src, land, mode, send_sem, recv_sem, ai, k, starting):
    x, y, c = _coords()
    peer, pidx = _peer(k)
    s, d = _pieces(src, land, mode, pidx, 4 * x + 2 * y + c if starting else pidx)
    i = ai * (N_DEV - 1) + k - 1
    return pltpu.make_async_remote_copy(src_ref=s, dst_ref=d, send_sem=send_sem.at[i], recv_sem=recv_sem.at[i],
                                        device_id=peer, device_id_type=MESH)


def _own_copy(src, land, mode, local_sem, ai):
    x, y, c = _coords()
    me = 4 * x + 2 * y + c
    s, d = _pieces(src, land, mode, me, me)
    return pltpu.make_async_copy(s, d, local_sem.at[ai])


def _item_src(ins, item):
    a, sub = item
    return ins[a] if sub is None else ins[a].at[sub]


def _exchange_start(srcs, items, modes, groups, name):
    n, ni, ng = len(srcs), len(items), len(groups)
    shapes = [srcs[a].shape if sub is None else srcs[a].shape[1:] for a, sub in items]
    land_shapes = [pltpu.HBM(_landing_shape(sh, m), srcs[a].dtype) for sh, m, (a, _) in zip(shapes, modes, items)]

    def body(*refs):
        ins = refs[:n]
        sems = refs[n:n + 3 * ng]
        land_refs = refs[n + 3 * ng:n + 3 * ng + ni]
        token = refs[-1]
        for g, idxs in enumerate(groups):
            for ai, it in enumerate(idxs):
                src = _item_src(ins, items[it])
                _own_copy(src, land_refs[it], modes[it], sems[3 * g + 2], ai).start()
                for k in range(1, N_DEV):
                    _exchange_copy(src, land_refs[it], modes[it], sems[3 * g], sems[3 * g + 1], ai, k, True).start()
        token[...] = jnp.zeros_like(token)

    sem_shapes = []
    for idxs in groups:
        sem_shapes += [pltpu.SemaphoreType.DMA((len(idxs) * (N_DEV - 1),))] * 2 + [pltpu.SemaphoreType.DMA((len(idxs),))]
    out = pl.pallas_call(
        body, name=name,
        out_shape=tuple(sem_shapes) + tuple(land_shapes) + (jax.ShapeDtypeStruct((8, 128), F32),),
        in_specs=[HBM] * n,
        out_specs=tuple([SEM] * (3 * ng) + [HBM] * ni + [pl.BlockSpec(memory_space=pltpu.VMEM)]),
        compiler_params=pltpu.CompilerParams(has_side_effects=EFFECT),
    )(*[pltpu.with_memory_space_constraint(s, pltpu.HBM) for s in srcs])
    sems = [tuple(out[3 * g:3 * g + 3]) for g in range(ng)]
    return sems, list(out[3 * ng:3 * ng + ni]), out[-1]


def _exchange_wait(sems, srcs, items, lands, modes, groups, after, name):
    n, ni, ng = len(srcs), len(items), len(groups)

    def body(*refs):
        ins, land_refs = refs[:n], refs[n:n + ni]
        sem_refs = refs[n + ni:n + ni + 3 * ng]
        for g, idxs in enumerate(groups):
            for ai, it in enumerate(idxs):
                src = _item_src(ins, items[it])
                _own_copy(src, land_refs[it], modes[it], sem_refs[3 * g + 2], ai).wait()
                for k in range(1, N_DEV):
                    cp = _exchange_copy(src, land_refs[it], modes[it], sem_refs[3 * g], sem_refs[3 * g + 1], ai, k, False)
                    cp.wait_send()
                    cp.wait_recv()

    flat_sems = [s for trio in sems for s in trio]
    out = pl.pallas_call(
        body, name=name,
        out_shape=tuple(pltpu.HBM(l.shape, l.dtype) for l in lands),
        in_specs=[HBM] * (n + ni) + [SEM] * (3 * ng) + [ANY],
        out_specs=tuple([HBM] * ni),
        input_output_aliases={n + i: i for i in range(ni)},
        compiler_params=pltpu.CompilerParams(has_side_effects=EFFECT),
    )(*srcs, *lands, *flat_sems, after)
    return list(out)


def _small_all_reduce_call(buf):
    rows = buf.shape[0]

    def body(in_ref, out_ref, pair_ref, slots_ref, send_sems, recv_sems):
        x, y, c = _coords()
        chip = 2 * x + y
        sib = pltpu.make_async_remote_copy(src_ref=in_ref, dst_ref=pair_ref, send_sem=send_sems.at[0], recv_sem=recv_sems.at[0],
                                           device_id=(x, y, 1 - c), device_id_type=MESH)
        sib.start()
        sib.wait_recv()
        slots_ref[chip] = in_ref[...] + pair_ref[...]
        sends = []
        for k in range(1, 4):
            px, py = x ^ (k >> 1), y ^ (k & 1)
            cp = pltpu.make_async_remote_copy(src_ref=slots_ref.at[chip], dst_ref=slots_ref.at[chip],
                                              send_sem=send_sems.at[k], recv_sem=recv_sems.at[k],
                                              device_id=(px, py, c), device_id_type=MESH)
            cp.start()
            sends.append(cp)
        for cp in sends:
            cp.wait_recv()
        out_ref[...] = (slots_ref[0] + slots_ref[1]) + (slots_ref[2] + slots_ref[3])
        sib.wait_send()
        for cp in sends:
            cp.wait_send()

    vm = pl.BlockSpec(memory_space=pltpu.VMEM)
    return pl.pallas_call(
        body, name="small_grads_all_reduce",
        in_specs=[vm], out_specs=vm,
        out_shape=jax.ShapeDtypeStruct((rows, 128), F32),
        scratch_shapes=[pltpu.VMEM((rows, 128), F32), pltpu.VMEM((4, rows, 128), F32),
                        pltpu.SemaphoreType.DMA((4,)), pltpu.SemaphoreType.DMA((4,))],
        compiler_params=pltpu.CompilerParams(vmem_limit_bytes=V7X_VMEM_LIMIT),
    )(buf)


def _adamw(w, g, m, v):
    m = ADAM_B1 * m + (1.0 - ADAM_B1) * g
    v = ADAM_B2 * v + (1.0 - ADAM_B2) * (g * g)
    m_hat = m / (1.0 - ADAM_B1 ** ADAM_STEP)
    v_hat = v / (1.0 - ADAM_B2 ** ADAM_STEP)
    delta = -ADAM_LR * (m_hat / (jnp.sqrt(v_hat) + ADAM_EPS) + ADAM_WD * w)
    return delta, m, v


def _reduce_adamw_call(recvs, w, m, v, name):
    nl = len(recvs)
    _, r, c = recvs[0].shape
    rb = min(r, 256)
    nb = r // rb

    def body(*refs):
        recv_refs = refs[:nl]
        w_ref, m_ref, v_ref, g_ref, d_ref, nm_ref, nv_ref = refs[nl:]
        for l in range(nl):
            @pl.when(pl.program_id(0) == l)
            def _(l=l):
                g = recv_refs[l][0].astype(F32)
                for j in range(1, N_DEV):
                    g = g + recv_refs[l][j].astype(F32)
                delta, nm, nv = _adamw(w_ref[0], g, m_ref[0], v_ref[0])
                g_ref[0] = g
                d_ref[0] = delta
                nm_ref[0] = nm
                nv_ref[0] = nv

    def recv_spec(l):
        return pl.BlockSpec((N_DEV, rb, c), lambda lg, i: (0, jnp.where(lg == l, i, jnp.where(lg < l, 0, nb - 1)), 0))

    blk = pl.BlockSpec((1, rb, c), lambda lg, i: (lg, i, 0))
    shp = jax.ShapeDtypeStruct((nl, r, c), F32)
    return pl.pallas_call(
        body, name=name, grid=(nl, nb),
        in_specs=[recv_spec(l) for l in range(nl)] + [blk, blk, blk],
        out_specs=[blk, blk, blk, blk], out_shape=[shp, shp, shp, shp],
        compiler_params=pltpu.CompilerParams(dimension_semantics=("arbitrary", "arbitrary"), vmem_limit_bytes=V7X_VMEM_LIMIT),
    )(*recvs, w, m, v)


def _small_adamw_call(w, g, m, v):
    def body(w_ref, g_ref, m_ref, v_ref, d_ref, nm_ref, nv_ref):
        delta, nm, nv = _adamw(w_ref[...], g_ref[...], m_ref[...], v_ref[...])
        d_ref[...] = delta
        nm_ref[...] = nm
        nv_ref[...] = nv

    shp = jax.ShapeDtypeStruct(w.shape, F32)
    vm = pl.BlockSpec(memory_space=pltpu.VMEM)
    return pl.pallas_call(body, name="small_adamw", in_specs=[vm] * 4, out_specs=[vm] * 3, out_shape=[shp, shp, shp],
                          compiler_params=pltpu.CompilerParams(vmem_limit_bytes=V7X_VMEM_LIMIT))(w, g, m, v)


_GATHER_MODE = dict(w_in="block", w_out="block", w_ff1="cols_in", w_ff2="block", w_ple_gate="block", w_ple_proj="cols_in")
_SCATTER_MODE = dict(w_in="slot", w_out="slot", w_ff1="cols_out", w_ff2="slot", w_ple_gate="slot", w_ple_proj="cols_out")
_GROUP_A = ("w_in", "w_out")
_GROUP_B = ("w_ff1", "w_ff2", "w_ple_gate", "w_ple_proj")


def _gathered_full(k, landed):
    if _GATHER_MODE[k] == "cols_in":
        return landed
    n, r, c = landed.shape
    if k == "w_in":
        return jnp.transpose(landed, (1, 0, 2)).reshape(r, n * c)
    return landed.reshape(n * r, c)


def _grad_send(k, g):
    if k == "w_in":
        r, c8 = g.shape
        return jnp.transpose(g.reshape(r, N_DEV, c8 // N_DEV), (1, 0, 2)).astype(BF16)
    if _SCATTER_MODE[k] == "cols_out":
        return g
    r8, c = g.shape
    return g.reshape(N_DEV, r8 // N_DEV, c)


_SMALL_ORDER = ("norm_mix_g", "sgu_w", "sgu_b", "sgu_ln_g", "sgu_ln_b", "conv_w", "pool_w", "pool_scale",
                "norm_ff_g", "norm_ple_g", "final_g")


def _pack_small(d):
    pieces, layout = [], []
    for k in _SMALL_ORDER:
        flat = d[k].reshape(-1)
        n = flat.shape[0]
        pad = (-n) % 128
        pieces.append(jnp.pad(flat, (0, pad)))
        layout.append((k, d[k].shape, n, n + pad))
    flat = jnp.concatenate(pieces)
    pad = (-flat.shape[0]) % 1024
    return jnp.pad(flat, (0, pad)).reshape(-1, 128), layout


def _unpack_small(buf, layout):
    flat = buf.reshape(-1)
    out, off = {}, 0
    for k, shape, n, padded in layout:
        out[k] = flat[off:off + n].reshape(shape)
        off += padded
    return out


def kernel(x, p, norm_mix_g, w_in, sgu_w, sgu_b, sgu_ln_g, sgu_ln_b, conv_w, pool_w, pool_scale, w_out, norm_ff_g, w_ff1, w_ff2, norm_ple_g, w_ple_gate, w_ple_proj, final_g, loss_target, m_norm_mix_g, m_w_in, m_sgu_w, m_sgu_b, m_sgu_ln_g, m_sgu_ln_b, m_conv_w, m_pool_w, m_pool_scale, m_w_out, m_norm_ff_g, m_w_ff1, m_w_ff2, m_norm_ple_g, m_w_ple_gate, m_w_ple_proj, m_final_g, v_norm_mix_g, v_w_in, v_sgu_w, v_sgu_b, v_sgu_ln_g, v_sgu_ln_b, v_conv_w, v_pool_w, v_pool_scale, v_w_out, v_norm_ff_g, v_w_ff1, v_w_ff2, v_norm_ple_g, v_w_ple_gate, v_w_ple_proj, v_final_g):
    t = x.shape[1]
    xc, yc_, cc = _coords()
    me = 4 * xc + 2 * yc_ + cc
    tm = lambda want: min(want, t)

    shard_names = _GROUP_A + _GROUP_B
    shard = dict(w_in=w_in, w_out=w_out, w_ff1=w_ff1, w_ff2=w_ff2, w_ple_gate=w_ple_gate, w_ple_proj=w_ple_proj)
    conv_pad = jnp.zeros((16, 128), F32).at[0:DEPTH * 3, 0:D_B // N_DEV].set(conv_w.reshape(DEPTH * 3, D_B // N_DEV))
    ag_srcs = [shard[k].astype(BF16) for k in shard_names] + [conv_pad]
    ag_items, ag_modes, ag_groups = [], [], []
    for l in range(DEPTH):
        for names in (_GROUP_A, _GROUP_B):
            ag_groups.append(list(range(len(ag_items), len(ag_items) + len(names))))
            ag_items += [(shard_names.index(k), l) for k in names]
            ag_modes += [_GATHER_MODE[k] for k in names]
            if l == 0 and names is _GROUP_A:
                ag_groups[-1].append(len(ag_items))
                ag_items.append((len(shard_names), None))
                ag_modes.append("block")
    ag_sems, ag_lands, _ = _exchange_start(ag_srcs, ag_items, ag_modes, ag_groups, "weights_gather_start")

    def gathered(l, which, after):
        idxs = ag_groups[2 * l + which]
        landed = _exchange_wait([ag_sems[2 * l + which]], ag_srcs, [ag_items[i] for i in idxs], [ag_lands[i] for i in idxs],
                                [ag_modes[i] for i in idxs], [list(range(len(idxs)))], after, f"weights_gather_wait_{l}_{which}")
        full = {k: _gathered_full(k, got) for k, got in zip((_GROUP_A, _GROUP_B)[which], landed)}
        if l == 0 and which == 0:
            full["conv_w"] = jnp.transpose(landed[-1][:, 0:DEPTH * 3, 0:D_B // N_DEV].reshape(N_DEV, DEPTH, 3, D_B // N_DEV),
                                           (1, 2, 0, 3)).reshape(DEPTH, 3, D_B)
        return full

    idx = jnp.arange(D_A)
    pmat = ((idx[:, None] // 64) == (idx[None, :] // 64)).astype(BF16) * (1.0 / 64.0)
    pmat = pmat.astype(BF16)
    tril = jnp.tril(jnp.ones((CHUNK, CHUNK), F32))
    trilcat = jnp.concatenate([tril, tril], axis=1)
    headsel = ((idx[:, None] // 64) == jnp.arange(CHUNK)[None, :]).astype(BF16)
    row = lambda a: a.reshape(1, -1)

    def mix_params(l, tile):
        wm = sgu_w[l] * tril[None]
        wcat = jnp.stack([jnp.concatenate([wm[2 * j], wm[2 * j + 1]], axis=1) for j in range(3)]).astype(BF16)
        wtcat = jnp.stack([jnp.concatenate([wm[2 * j].T, wm[2 * j + 1].T], axis=1) for j in range(3)]).astype(BF16)
        bmat = jnp.tile(jnp.repeat(sgu_b[l].T, 64, axis=1), (tile // CHUNK, 1))
        bd = jnp.zeros((D_C, D_C), F32)
        for gi in range(4):
            bd = bd.at[gi * 64:(gi + 1) * 64, gi * 64:(gi + 1) * 64].set(pool_w[l, gi])
        mp = dict(pmat=pmat, ln_g=row(sgu_ln_g[l]), ln_b=row(sgu_ln_b[l]), wcat=wcat, bmat=bmat, conv_w=conv_full[l],
                  bd=bd.astype(BF16), pool_scale=row(pool_scale[l]))
        return mp, wtcat

    xs = x.reshape(t, D)
    saved, full_w = [], []
    conv_full = None
    for l in range(DEPTH):
        wa = gathered(l, 0, xs)
        if l == 0:
            conv_full = wa["conv_w"]
        mp, _ = mix_params(l, tm(TM_MIX_FWD))
        x1, proj = _mix_fwd_call(xs, row(norm_mix_g[l]), wa["w_in"], wa["w_out"], mp, tm(TM_MIX_FWD), l)
        wb = gathered(l, 1, x1)
        x2, x3, r, gate = _ffn_fwd_call(x1, p[l, 0], row(norm_ff_g[l]), row(norm_ple_g[l]), wb["w_ff1"], wb["w_ff2"],
                                        wb["w_ple_gate"], wb["w_ple_proj"], tm(TM_FFN_FWD), l)
        saved.append((xs, proj, x1, r, x2, gate))
        full_w.append({**wa, **wb})
        xs = x3

    sq, dx, dfinal = _loss_call(xs, loss_target.reshape(t, D), row(final_g), tm(TM_LOSS))
    loss = lax.psum(jnp.sum(sq) * (0.5 / D), ("x", "y", "c"))

    small = {k: [None] * DEPTH for k in _SMALL_ORDER if k != "final_g"}
    ex = {}
    token = None

    def after_start(g):
        return g if token is None else g + token[0:1, 0:1]

    def start_exchange(l, which, grads):
        names = (_GROUP_A, _GROUP_B)[which]
        sends = [_grad_send(k, grads[k]) for k in names]
        sems, lands, tok = _exchange_start(sends, [(i, None) for i in range(len(names))], [_SCATTER_MODE[k] for k in names],
                                           [list(range(len(names)))], f"grads_exchange_start_{l}_{which}")
        ex[(l, which)] = (sems[0], sends, lands)
        return tok

    for l in reversed(range(DEPTH)):
        x0, proj, x1, r, x2, gate = saved[l]
        fw = full_w[l]
        dx2, dgple, dwg, dwp = _ple_bwd_call(dx, x2, gate, p[l, 0], after_start(row(norm_ple_g[l])), fw["w_ple_gate"], fw["w_ple_proj"],
                                             tm(TM_PLE_BWD), l)
        da, dw2 = _ffn_bwd_hidden_call(dx2, r, fw["w_ff2"], tm(TM_FFN_BWD), l)
        dx1, dgff, dw1 = _ffn_bwd_input_call(da, x1, dx2, row(norm_ff_g[l]), fw["w_ff1"], tm(TM_FFN_BWD), l)
        token = start_exchange(l, 1, dict(w_ff1=dw1, w_ff2=dw2, w_ple_gate=dwg, w_ple_proj=dwp))
        mp, wtcat = mix_params(l, tm(TM_MIX_BWD))
        (dx, dgmix, dwcat, dsb, dlng, dlnb, dconv, dbd, dscale, dwin, dwout) = _mix_bwd_call(
            dx1, x0, proj, after_start(row(norm_mix_g[l])), fw["w_in"], fw["w_out"], mp, wtcat, trilcat, headsel, tm(TM_MIX_BWD), l)
        token = start_exchange(l, 0, dict(w_in=dwin, w_out=dwout))
        small["norm_mix_g"][l] = dgmix[0]
        small["sgu_w"][l] = jnp.stack([dwcat[h // 2][:, (h % 2) * CHUNK:(h % 2 + 1) * CHUNK] for h in range(6)])
        small["sgu_b"][l] = dsb[:, 0:6].T
        small["sgu_ln_g"][l], small["sgu_ln_b"][l] = dlng[0], dlnb[0]
        small["conv_w"][l] = dconv
        small["pool_w"][l] = jnp.stack([dbd[gi * 64:(gi + 1) * 64, gi * 64:(gi + 1) * 64] for gi in range(4)])
        small["pool_scale"][l] = dscale[0]
        small["norm_ff_g"][l], small["norm_ple_g"][l] = dgff[0], dgple[0]
    grad_x = dx.reshape(1, t, D)

    state = dict(w_in=(w_in, m_w_in, v_w_in), w_out=(w_out, m_w_out, v_w_out), w_ff1=(w_ff1, m_w_ff1, v_w_ff1),
                 w_ff2=(w_ff2, m_w_ff2, v_w_ff2), w_ple_gate=(w_ple_gate, m_w_ple_gate, v_w_ple_gate),
                 w_ple_proj=(w_ple_proj, m_w_ple_proj, v_w_ple_proj))
    res = {}
    after = dx
    for which in (1, 0):
        names = (_GROUP_A, _GROUP_B)[which]
        n = len(names)
        sems = [ex[(l, which)][0] for l in range(DEPTH)]
        sends = [s_ for l in range(DEPTH) for s_ in ex[(l, which)][1]]
        lands = [a_ for l in range(DEPTH) for a_ in ex[(l, which)][2]]
        landed = _exchange_wait(sems, sends, [(i, None) for i in range(DEPTH * n)], lands, [_SCATTER_MODE[k] for k in names] * DEPTH,
                                [list(range(l * n, (l + 1) * n)) for l in range(DEPTH)], after, f"grads_exchange_wait_{which}")
        for i, k in enumerate(names):
            res[k] = _reduce_adamw_call([landed[l * n + i] for l in range(DEPTH)], *state[k], "reduce_adamw_" + k)
        after = res[names[-1]][0]

    small_g = {k: jnp.stack(vs) for k, vs in small.items()}
    small_g["final_g"] = dfinal[0]
    gbuf, layout = _pack_small(small_g)
    gsum = _small_all_reduce_call(gbuf)
    conv_cols = lambda a: lax.dynamic_slice_in_dim(a, me * (D_B // N_DEV), D_B // N_DEV, axis=2)
    pad_conv = lambda a: jnp.zeros((DEPTH, 3, D_B), F32).at[:, :, 0:D_B // N_DEV].set(a)
    small_w = dict(norm_mix_g=norm_mix_g, sgu_w=sgu_w, sgu_b=sgu_b, sgu_ln_g=sgu_ln_g, sgu_ln_b=sgu_ln_b, conv_w=pad_conv(conv_w),
                   pool_w=pool_w, pool_scale=pool_scale, norm_ff_g=norm_ff_g, norm_ple_g=norm_ple_g, final_g=final_g)
    small_m = dict(norm_mix_g=m_norm_mix_g, sgu_w=m_sgu_w, sgu_b=m_sgu_b, sgu_ln_g=m_sgu_ln_g, sgu_ln_b=m_sgu_ln_b,
                   conv_w=pad_conv(m_conv_w), pool_w=m_pool_w, pool_scale=m_pool_scale, norm_ff_g=m_norm_ff_g,
                   norm_ple_g=m_norm_ple_g, final_g=m_final_g)
    small_v = dict(norm_mix_g=v_norm_mix_g, sgu_w=v_sgu_w, sgu_b=v_sgu_b, sgu_ln_g=v_sgu_ln_g, sgu_ln_b=v_sgu_ln_b,
                   conv_w=pad_conv(v_conv_w), pool_w=v_pool_w, pool_scale=v_pool_scale,
                   norm_ff_g=v_norm_ff_g, norm_ple_g=v_norm_ple_g, final_g=v_final_g)
    gs = _unpack_small(gsum, layout)
    gs_local = dict(gs)
    gs_local["conv_w"] = pad_conv(conv_cols(gs["conv_w"]))
    g_loc, _ = _pack_small(gs_local)
    wbuf, _ = _pack_small(small_w)
    mbuf, _ = _pack_small(small_m)
    vbuf, _ = _pack_small(small_v)
    dbuf, nmbuf, nvbuf = _small_adamw_call(wbuf, g_loc, mbuf, vbuf)
    sd, sm, sv = _unpack_small(dbuf, layout), _unpack_small(nmbuf, layout), _unpack_small(nvbuf, layout)
    unconv = lambda a: a[:, :, 0:D_B // N_DEV]
    for dct in (gs_local, sd, sm, sv):
        dct["conv_w"] = unconv(dct["conv_w"])

    order = ["norm_mix_g", "w_in", "sgu_w", "sgu_b", "sgu_ln_g", "sgu_ln_b", "conv_w", "pool_w", "pool_scale", "w_out",
             "norm_ff_g", "w_ff1", "w_ff2", "norm_ple_g", "w_ple_gate", "w_ple_proj", "final_g"]
    outs = [loss, grad_x]
    for which in range(4):
        for k in order:
            if k in res:
                outs.append(res[k][which])
            else:
                outs.append((gs_local, sd, sm, sv)[which][k])
    return tuple(outs)
```

```python
import functools
import math

import jax
import jax.numpy as jnp
from jax import lax
from jax.experimental import pallas as pl
from jax.experimental.pallas import tpu as pltpu

F32 = jnp.float32
BF16 = jnp.bfloat16

D = 1024
D_IN = 2176
D_A = 384
D_B = 384
D_C = 256
D_FF = 4096
D_PLE = 256
DEPTH = 4
CHUNK = 128
HALO = 16
FF_BLK = 1024
N_DEV = 8
RMS_EPS = 1e-6
LN_EPS = 1e-5
ADAM_LR = 0.001
ADAM_B1 = 0.9
ADAM_B2 = 0.999
ADAM_EPS = 1e-08
ADAM_WD = 0.01
ADAM_STEP = 10

TM_MIX_FWD = 1024
TM_FFN_FWD = 512
TM_LOSS = 512
TM_PLE_BWD = 1024
TM_FFN_BWD = 512
TM_MIX_BWD = 512
V7X_VMEM_LIMIT = 60000 * 1024

ANY = pl.BlockSpec(memory_space=pl.ANY)
HBM = pl.BlockSpec(memory_space=pltpu.HBM)
SEM = pl.BlockSpec(memory_space=pltpu.SEMAPHORE)
MESH = pl.DeviceIdType.MESH


def _params(vmem=V7X_VMEM_LIMIT):
    return pltpu.CompilerParams(dimension_semantics=("arbitrary",), vmem_limit_bytes=vmem)


def _in_hbm(a):
    return pltpu.with_memory_space_constraint(a, pltpu.HBM)


def _full(shape):
    nd = len(shape)
    return pl.BlockSpec(shape, lambda i: (0,) * nd)


def _rows(tm, cols):
    return pl.BlockSpec((tm, cols), lambda i: (i, 0))


def _mm(a, b):
    return jnp.dot(a, b, preferred_element_type=F32)


def _mm_nt(a, b):
    return lax.dot_general(a, b, (((1,), (1,)), ((), ())), preferred_element_type=F32)


def _mm_tn(a, b):
    return lax.dot_general(a, b, (((0,), (0,)), ((), ())), preferred_element_type=F32)


def _erf(x):
    ax = jnp.abs(x)
    t = 1.0 / (1.0 + 0.3275911 * ax)
    poly = t * (0.254829592 + t * (-0.284496736 + t * (1.421413741 + t * (-1.453152027 + t * 1.061405429))))
    y = 1.0 - poly * jnp.exp(-ax * ax)
    return jnp.where(x < 0, -y, y)


def _gelu_and_grad(x):
    cdf = 0.5 * (1.0 + _erf(x * (1.0 / math.sqrt(2.0))))
    pdf = jnp.exp(-0.5 * x * x) * (1.0 / math.sqrt(2.0 * math.pi))
    return x * cdf, cdf + x * pdf


def _rms(x, g):
    rstd = lax.rsqrt(jnp.mean(x * x, axis=-1, keepdims=True) + RMS_EPS)
    xhat = x * rstd
    return xhat * g, xhat, rstd


def _rms_bwd(dy, g, xhat, rstd):
    dg = jnp.sum(dy * xhat, axis=0, keepdims=True)
    dxh = dy * g
    dx = rstd * (dxh - xhat * jnp.mean(dxh * xhat, axis=-1, keepdims=True))
    return dx, dg


def _shift_down(ext, k):
    return pltpu.roll(ext, k, 0)[HALO:, :]


def _shift_up(ext, k):
    n = ext.shape[0]
    return pltpu.roll(ext, n - k, 0)[: n - HALO, :]


def _pool_select(s2, s4, s8, s16):
    lane = lax.broadcasted_iota(jnp.int32, s2.shape, 1)
    return jnp.where(lane < 64, s2, jnp.where(lane < 128, s4, jnp.where(lane < 192, s8, s16)))


def _pool_inv_count(tile_start, tm):
    pos = lax.broadcasted_iota(jnp.int32, (tm, D_C), 0) + tile_start + 1
    lane = lax.broadcasted_iota(jnp.int32, (tm, D_C), 1)
    win = jnp.where(lane < 64, 2, jnp.where(lane < 128, 4, jnp.where(lane < 192, 8, 16)))
    return 1.0 / jnp.minimum(pos, win).astype(F32)


def _head_halves(a):
    lane = lax.broadcasted_iota(jnp.int32, a.shape, 1)
    even = (lane & 64) == 0
    return jnp.where(even, a, 0.0).astype(BF16), jnp.where(even, 0.0, a).astype(BF16)


def _head_stack(lo, hi, j, nch):
    return jnp.concatenate(
        [jnp.concatenate([lo[c * CHUNK:(c + 1) * CHUNK, j * 128:(j + 1) * 128], hi[c * CHUNK:(c + 1) * CHUNK, j * 128:(j + 1) * 128]], axis=0)
         for c in range(nch)], axis=1)


def _chunks_to_lanes(a, j, nch):
    return jnp.concatenate([a[c * CHUNK:(c + 1) * CHUNK, j * 128:(j + 1) * 128] for c in range(nch)], axis=1)


def _lanes_to_chunks(o, nch):
    return jnp.concatenate([o[:, c * CHUNK:(c + 1) * CHUNK] for c in range(nch)], axis=0)


def _loads(pairs, sem):
    return [pltpu.make_async_copy(src, dst, sem.at[n]) for n, (src, dst) in enumerate(pairs)]


def _load_all(loads):
    for cp in loads:
        cp.start()
    for cp in loads:
        cp.wait()


def _stage_bf16(acc, stage):
    rows = acc.shape[0]
    strip = min(rows, 128)

    @pl.loop(0, rows // strip)
    def _(n):
        sl = pl.ds(pl.multiple_of(n * strip, strip), strip)
        stage[sl, :] = acc[sl, :].astype(BF16)


def _mixers_fwd(pf, halo_hc, halo_zc, tile_start, prm):
    tm = pf.shape[0]
    nch = tm // CHUNK
    u, v = pf[:, 0:D_A], pf[:, D_A:2 * D_A]
    zb, gb, gc = pf[:, 768:1152], pf[:, 1152:1536], pf[:, 1536:1920]
    zc = pf[:, 1920:2176]
    r = {}
    gu, r["dgelu_u"] = _gelu_and_grad(u)
    gv, r["dgelu_v"] = _gelu_and_grad(v)
    pmat = prm["pmat"][...]
    mu = _mm(gv.astype(BF16), pmat)
    dv = gv - mu
    var = _mm((dv * dv).astype(BF16), pmat)
    rstd = lax.rsqrt(var + LN_EPS)
    xh = dv * rstd
    vlo, vhi = _head_halves(xh * prm["ln_g"][...] + prm["ln_b"][...])
    cols, v2s = [], []
    for j in range(3):
        v2 = _head_stack(vlo, vhi, j, nch)
        v2s.append(v2)
        cols.append(_lanes_to_chunks(_mm(prm["wcat"][j], v2), nch))
    mixed = jnp.concatenate(cols, axis=1) + prm["bmat"][...]
    ya = gu * mixed
    r.update(gu=gu, mixed=mixed, v2s=v2s, xh=xh, ln_rstd=rstd)
    w0, w1, w2 = prm["conv_w"][0:1, :], prm["conv_w"][1:2, :], prm["conv_w"][2:3, :]
    hc = gc * zb
    ext = jnp.concatenate([halo_hc, hc], axis=0)
    h1, h2 = _shift_down(ext, 1), _shift_down(ext, 2)
    yc = w2 * hc + w1 * h1 + w0 * h2
    yb = gb * yc
    r.update(hc=hc, h1=h1, h2=h2, yc=yc, zb=zb, gb=gb, gc=gc, w0=w0, w1=w1, w2=w2)
    ext = jnp.concatenate([halo_zc, zc], axis=0)
    s2 = ext + pltpu.roll(ext, 1, 0)
    s4 = s2 + pltpu.roll(s2, 2, 0)
    s8 = s4 + pltpu.roll(s4, 4, 0)
    s16 = s8 + pltpu.roll(s8, 8, 0)
    inv = _pool_inv_count(tile_start, tm)
    pooled = _pool_select(s2, s4, s8, s16)[HALO:, :] * inv - zc
    pooledb = pooled.astype(BF16)
    pm = _mm(pooledb, prm["bd"][...])
    scale = prm["pool_scale"][...]
    ycm = pm * scale
    r.update(inv=inv, pooledb=pooledb, pm=pm, scale=scale, zc=zc)
    r["ycat"] = jnp.concatenate([ya, yb, ycm], axis=1)
    return r


_MIX_PARAM_NAMES = ("pmat", "ln_g", "ln_b", "wcat", "bmat", "conv_w", "bd", "pool_scale")


def _mix_param_specs(tm):
    return [_full((D_A, D_A)), _full((1, D_A)), _full((1, D_A)), _full((3, CHUNK, 2 * CHUNK)), _full((tm, D_A)),
            _full((3, D_B)), _full((D_C, D_C)), _full((1, D_C))]


def _mix_fwd_call(x, g_mix, w_in, w_out, mp, tm, layer):
    t = x.shape[0]
    nt = t // tm

    def body(x_ref, g_ref, pmat, ln_g, ln_b, wcat, bmat, conv_w, bd, pool_scale, win_hbm, wout_hbm,
             x1_ref, proj_ref, win_s, wout_s, halo_hc, halo_zc, load_sem):
        i = pl.program_id(0)
        loads = _loads([(win_hbm, win_s), (wout_hbm, wout_s)], load_sem)

        @pl.when(i == 0)
        def _():
            _load_all(loads)
            halo_hc[...] = jnp.zeros_like(halo_hc)
            halo_zc[...] = jnp.zeros_like(halo_zc)

        prm = dict(pmat=pmat, ln_g=ln_g, ln_b=ln_b, wcat=wcat, bmat=bmat, conv_w=conv_w, bd=bd, pool_scale=pool_scale)
        xv = x_ref[...]
        h, _, _ = _rms(xv, g_ref[...])
        pf = _mm(h.astype(BF16), win_s[...])
        proj_ref[...] = pf.astype(BF16)
        r = _mixers_fwd(pf, halo_hc[...], halo_zc[...], i * tm, prm)
        halo_hc[...] = r["hc"][tm - HALO:, :]
        halo_zc[...] = r["zc"][tm - HALO:, :]
        x1_ref[...] = xv + _mm(r["ycat"].astype(BF16), wout_s[...])

    return pl.pallas_call(
        body, name=f"mix_fwd_{layer}", grid=(nt,),
        in_specs=[_rows(tm, D), _full((1, D))] + _mix_param_specs(tm) + [HBM, HBM],
        out_specs=[_rows(tm, D), _rows(tm, D_IN)],
        out_shape=[jax.ShapeDtypeStruct((t, D), F32), jax.ShapeDtypeStruct((t, D_IN), BF16)],
        scratch_shapes=[pltpu.VMEM((D, D_IN), BF16), pltpu.VMEM((D, D), BF16),
                        pltpu.VMEM((HALO, D_B), F32), pltpu.VMEM((HALO, D_C), F32), pltpu.SemaphoreType.DMA((2,))],
        compiler_params=_params(),
    )(x, g_mix, *[mp[k] for k in _MIX_PARAM_NAMES], _in_hbm(w_in), _in_hbm(w_out))


def _ffn_fwd_call(x1, p, g_ff, g_ple, w1, w2, wg, wp, tm, layer):
    t = x1.shape[0]
    nt = t // tm

    def body(x1_ref, p_ref, gff_ref, gple_ref, w1_hbm, w2_hbm, wg_hbm, wp_hbm,
             x2_ref, x3_ref, r_ref, gate_ref, w1_s, w2_s, wg_s, wp_s, load_sem):
        i = pl.program_id(0)
        loads = _loads([(w1_hbm, w1_s), (w2_hbm, w2_s), (wg_hbm, wg_s), (wp_hbm, wp_s)], load_sem)

        @pl.when(i == 0)
        def _():
            _load_all(loads)

        x1v = x1_ref[...]
        h2, _, _ = _rms(x1v, gff_ref[...])
        h2b = h2.astype(BF16)
        acc = x1v
        for j in range(D_FF // FF_BLK):
            blk = slice(j * FF_BLK, (j + 1) * FF_BLK)
            rj = jnp.maximum(_mm(h2b, w1_s[:, blk]), 0.0)
            r_ref[:, blk] = rj.astype(BF16)
            acc = acc + _mm((rj * rj).astype(BF16), w2_s[blk, :])
        x2_ref[...] = acc
        n3, _, _ = _rms(acc, gple_ref[...])
        gate = jax.nn.sigmoid(_mm(n3.astype(BF16), wg_s[...]))
        gate_ref[...] = gate.astype(BF16)
        pp = _mm(p_ref[...].astype(BF16), wp_s[...])
        x3_ref[...] = acc + pp * gate

    return pl.pallas_call(
        body, name=f"ffn_fwd_{layer}", grid=(nt,),
        in_specs=[_rows(tm, D), _rows(tm, D_PLE), _full((1, D)), _full((1, D)), HBM, HBM, HBM, HBM],
        out_specs=[_rows(tm, D), _rows(tm, D), _rows(tm, D_FF), _rows(tm, D)],
        out_shape=[jax.ShapeDtypeStruct((t, D), F32), jax.ShapeDtypeStruct((t, D), F32),
                   jax.ShapeDtypeStruct((t, D_FF), BF16), jax.ShapeDtypeStruct((t, D), BF16)],
        scratch_shapes=[pltpu.VMEM((D, D_FF), BF16), pltpu.VMEM((D_FF, D), BF16),
                        pltpu.VMEM((D, D), BF16), pltpu.VMEM((D_PLE, D), BF16), pltpu.SemaphoreType.DMA((4,))],
        compiler_params=_params(),
    )(x1, p, g_ff, g_ple, _in_hbm(w1), _in_hbm(w2), _in_hbm(wg), _in_hbm(wp))


def _loss_call(xl, target, final_g, tm):
    t = xl.shape[0]
    nt = t // tm

    def body(x_ref, t_ref, g_ref, sq_ref, dx_ref, dg_ref):
        i = pl.program_id(0)

        @pl.when(i == 0)
        def _():
            sq_ref[...] = jnp.zeros_like(sq_ref)
            dg_ref[...] = jnp.zeros_like(dg_ref)

        g = g_ref[...]
        y, xhat, rstd = _rms(x_ref[...], g)
        err = y - t_ref[...]
        sq_ref[...] += jnp.sum(err * err, axis=0, keepdims=True)
        dx, dg = _rms_bwd(err * (1.0 / D), g, xhat, rstd)
        dx_ref[...] = dx
        dg_ref[...] += dg

    return pl.pallas_call(
        body, name="loss_head", grid=(nt,),
        in_specs=[_rows(tm, D), _rows(tm, D), _full((1, D))],
        out_specs=[_full((1, D)), _rows(tm, D), _full((1, D))],
        out_shape=[jax.ShapeDtypeStruct((1, D), F32), jax.ShapeDtypeStruct((t, D), F32), jax.ShapeDtypeStruct((1, D), F32)],
        compiler_params=_params(),
    )(xl, target, final_g)


def _ple_bwd_call(dx3, x2, gate, p, g_ple, wg, wp, tm, layer):
    t = dx3.shape[0]
    nt = t // tm

    def body(dx3_ref, x2_ref, gate_ref, p_ref, g_ref, wg_hbm, wp_hbm,
             dx2_ref, dg_ref, dwg_hbm, dwp_hbm, wg_s, wp_s, dwg_acc, dwp_acc, load_sem):
        i = pl.program_id(0)
        loads = _loads([(wp_hbm, wp_s), (wg_hbm, wg_s)], load_sem)

        @pl.when(i == 0)
        def _():
            _load_all(loads)
            dwg_acc[...] = jnp.zeros_like(dwg_acc)
            dwp_acc[...] = jnp.zeros_like(dwp_acc)
            dg_ref[...] = jnp.zeros_like(dg_ref)

        g = g_ref[...]
        dx3v = dx3_ref[...]
        gatev = gate_ref[...].astype(F32)
        pb = p_ref[...].astype(BF16)
        pp = _mm(pb, wp_s[...])
        dwp_acc[...] += _mm_tn(pb, (dx3v * gatev).astype(BF16))
        dgpre = (dx3v * pp * gatev * (1.0 - gatev)).astype(BF16)
        n3, xhat, rstd = _rms(x2_ref[...], g)
        dwg_acc[...] += _mm_tn(n3.astype(BF16), dgpre)
        dn3 = _mm_nt(dgpre, wg_s[...])
        dx, dg = _rms_bwd(dn3, g, xhat, rstd)
        dx2_ref[...] = dx3v + dx
        dg_ref[...] += dg

        @pl.when(i == nt - 1)
        def _():
            _stage_bf16(dwg_acc, wg_s)
            _stage_bf16(dwp_acc, wp_s)
            pltpu.sync_copy(wg_s, dwg_hbm)
            pltpu.sync_copy(wp_s, dwp_hbm)

    return pl.pallas_call(
        body, name=f"ple_bwd_{layer}", grid=(nt,),
        in_specs=[_rows(tm, D), _rows(tm, D), _rows(tm, D), _rows(tm, D_PLE), _full((1, D)), HBM, HBM],
        out_specs=[_rows(tm, D), _full((1, D)), HBM, HBM],
        out_shape=[jax.ShapeDtypeStruct((t, D), F32), jax.ShapeDtypeStruct((1, D), F32),
                   pltpu.HBM((D, D), BF16), pltpu.HBM((D_PLE, D), BF16)],
        scratch_shapes=[pltpu.VMEM((D, D), BF16), pltpu.VMEM((D_PLE, D), BF16),
                        pltpu.VMEM((D, D), F32), pltpu.VMEM((D_PLE, D), F32), pltpu.SemaphoreType.DMA((2,))],
        compiler_params=_params(),
    )(dx3, x2, gate, p, g_ple, _in_hbm(wg), _in_hbm(wp))


def _ffn_bwd_hidden_call(dx2, r, w2, tm, layer):
    t = dx2.shape[0]
    nt = t // tm

    def body(dx2_ref, r_ref, w2_hbm, da_ref, dw2_hbm, w2_s, dw2_acc, load_sem):
        i = pl.program_id(0)
        loads = _loads([(w2_hbm, w2_s)], load_sem)

        @pl.when(i == 0)
        def _():
            _load_all(loads)
            dw2_acc[...] = jnp.zeros_like(dw2_acc)

        dxb = dx2_ref[...].astype(BF16)
        for j in range(D_FF // FF_BLK):
            blk = slice(j * FF_BLK, (j + 1) * FF_BLK)
            rj = r_ref[:, blk].astype(F32)
            ds = _mm_nt(dxb, w2_s[blk, :])
            da_ref[:, blk] = (2.0 * rj * ds).astype(BF16)
            dw2_acc[blk, :] += _mm_tn((rj * rj).astype(BF16), dxb)

        @pl.when(i == nt - 1)
        def _():
            _stage_bf16(dw2_acc, w2_s)
            pltpu.sync_copy(w2_s, dw2_hbm)

    return pl.pallas_call(
        body, name=f"ffn_bwd_hidden_{layer}", grid=(nt,),
        in_specs=[_rows(tm, D), _rows(tm, D_FF), HBM],
        out_specs=[_rows(tm, D_FF), HBM],
        out_shape=[jax.ShapeDtypeStruct((t, D_FF), BF16), pltpu.HBM((D_FF, D), BF16)],
        scratch_shapes=[pltpu.VMEM((D_FF, D), BF16), pltpu.VMEM((D_FF, D), F32), pltpu.SemaphoreType.DMA((1,))],
        compiler_params=_params(),
    )(dx2, r, _in_hbm(w2))


def _ffn_bwd_input_call(da, x1, dx2, g_ff, w1, tm, layer):
    t = dx2.shape[0]
    nt = t // tm

    def body(da_ref, x1_ref, dx2_ref, g_ref, w1_hbm, dx1_ref, dg_ref, dw1_hbm, w1_s, dw1_acc, load_sem):
        i = pl.program_id(0)
        loads = _loads([(w1_hbm, w1_s)], load_sem)

        @pl.when(i == 0)
        def _():
            _load_all(loads)
            dw1_acc[...] = jnp.zeros_like(dw1_acc)
            dg_ref[...] = jnp.zeros_like(dg_ref)

        g = g_ref[...]
        h2, xhat, rstd = _rms(x1_ref[...], g)
        h2b = h2.astype(BF16)
        dh2 = jnp.zeros((tm, D), F32)
        for j in range(D_FF // FF_BLK):
            blk = slice(j * FF_BLK, (j + 1) * FF_BLK)
            daj = da_ref[:, blk]
            dh2 = dh2 + _mm_nt(daj, w1_s[:, blk])
            dw1_acc[:, blk] += _mm_tn(h2b, daj)
        dx, dg = _rms_bwd(dh2, g, xhat, rstd)
        dx1_ref[...] = dx2_ref[...] + dx
        dg_ref[...] += dg

        @pl.when(i == nt - 1)
        def _():
            _stage_bf16(dw1_acc, w1_s)
            pltpu.sync_copy(w1_s, dw1_hbm)

    return pl.pallas_call(
        body, name=f"ffn_bwd_input_{layer}", grid=(nt,),
        in_specs=[_rows(tm, D_FF), _rows(tm, D), _rows(tm, D), _full((1, D)), HBM],
        out_specs=[_rows(tm, D), _full((1, D)), HBM],
        out_shape=[jax.ShapeDtypeStruct((t, D), F32), jax.ShapeDtypeStruct((1, D), F32), pltpu.HBM((D, D_FF), BF16)],
        scratch_shapes=[pltpu.VMEM((D, D_FF), BF16), pltpu.VMEM((D, D_FF), F32), pltpu.SemaphoreType.DMA((1,))],
        compiler_params=_params(),
    )(da, x1, dx2, g_ff, _in_hbm(w1))


def _mix_bwd_call(dx1, x, proj, g_mix, w_in, w_out, mp, wtcat, trilcat, headsel, tm, layer):
    t = dx1.shape[0]
    nt = t // tm
    nch = tm // CHUNK
    hb = tm // HALO

    def rev(i):
        return nt - 1 - i

    def body(dx1_ref, x_ref, proj_ref, halo_ref, g_ref, pmat, ln_g, ln_b, wcat, bmat, conv_w, bd, pool_scale,
             wtcat_ref, tril_ref, sel_ref, win_hbm, wout_hbm,
             dx_ref, dg_ref, dwcat_ref, dsb_ref, dlng_ref, dlnb_ref, dconv_ref, dbd_ref, dscale_ref, dwin_hbm, dwout_hbm,
             win_s, wout_s, dwin_acc, dwout_acc, dbm_acc, carry_yc, carry_q, dvn_s, load_sem):
        i = pl.program_id(0)
        ri = nt - 1 - i
        loads = _loads([(wout_hbm, wout_s), (win_hbm, win_s)], load_sem)

        @pl.when(i == 0)
        def _():
            _load_all(loads)
            for ref in (dwin_acc, dwout_acc, dbm_acc, carry_yc, carry_q, dg_ref, dwcat_ref, dlng_ref, dlnb_ref,
                        dconv_ref, dbd_ref, dscale_ref):
                ref[...] = jnp.zeros_like(ref)

        prm = dict(pmat=pmat, ln_g=ln_g, ln_b=ln_b, wcat=wcat, bmat=bmat, conv_w=conv_w, bd=bd, pool_scale=pool_scale)
        g = g_ref[...]
        h, xhat, rstd = _rms(x_ref[...], g)
        hb16 = h.astype(BF16)
        dx1v = dx1_ref[...]
        dx1b = dx1v.astype(BF16)
        pf = proj_ref[...].astype(F32)
        ph = halo_ref[...].astype(F32) * (ri > 0).astype(F32)
        r = _mixers_fwd(pf, ph[:, 1536:1920] * ph[:, 768:1152], ph[:, 1920:2176], ri * tm, prm)

        dwout_acc[...] += _mm_tn(r["ycat"].astype(BF16), dx1b)
        dycat = _mm_nt(dx1b, wout_s[...])
        dya, dyb, dyc = dycat[:, 0:D_A], dycat[:, D_A:D_A + D_B], dycat[:, D_A + D_B:D]

        dgu = dya * r["mixed"]
        dmix = dya * r["gu"]
        dmix_b = dmix.astype(BF16)
        dlo, dhi = _head_halves(dmix)
        dbm = dmix[0:CHUNK, :]
        for c in range(1, nch):
            dbm = dbm + dmix[c * CHUNK:(c + 1) * CHUNK, :]
        dbm_acc[...] += dbm
        for j in range(3):
            dm_all = _chunks_to_lanes(dmix_b, j, nch)
            dwcat_ref[j] += _mm_nt(dm_all, r["v2s"][j])
            dm2 = _head_stack(dlo, dhi, j, nch)
            dvn_s[:, j * 128:(j + 1) * 128] = _lanes_to_chunks(_mm(wtcat_ref[j], dm2), nch)
        dvn = dvn_s[...]
        xh = r["xh"]
        dlng_ref[...] += jnp.sum(dvn * xh, axis=0, keepdims=True)
        dlnb_ref[...] += jnp.sum(dvn, axis=0, keepdims=True)
        dxh = dvn * ln_g[...]
        pm_ = pmat[...]
        m1 = _mm(dxh.astype(BF16), pm_)
        m2 = _mm((dxh * xh).astype(BF16), pm_)
        dgv = r["ln_rstd"] * (dxh - m1 - xh * m2)
        du = dgu * r["dgelu_u"]
        dv = dgv * r["dgelu_v"]

        dgb = dyb * r["yc"]
        dyc2 = dyb * r["gb"]
        dconv_ref[0:1, :] += jnp.sum(dyc2 * r["h2"], axis=0, keepdims=True)
        dconv_ref[1:2, :] += jnp.sum(dyc2 * r["h1"], axis=0, keepdims=True)
        dconv_ref[2:3, :] += jnp.sum(dyc2 * r["hc"], axis=0, keepdims=True)
        ext = jnp.concatenate([dyc2, carry_yc[...]], axis=0)
        dhc = r["w2"] * dyc2 + r["w1"] * _shift_up(ext, 1) + r["w0"] * _shift_up(ext, 2)
        carry_yc[...] = dyc2[0:HALO, :]
        dgc = dhc * r["zb"]
        dzb = dhc * r["gc"]

        dscale_ref[...] += jnp.sum(dyc * r["pm"], axis=0, keepdims=True)
        dpm = (dyc * r["scale"]).astype(BF16)
        dbd_ref[...] += _mm_tn(r["pooledb"], dpm)
        dpooled = _mm_nt(dpm, bd[...])
        q = dpooled * r["inv"]
        ext = jnp.concatenate([q, carry_q[...]], axis=0)
        n = tm + HALO
        r2 = ext + pltpu.roll(ext, n - 1, 0)
        r4 = r2 + pltpu.roll(r2, n - 2, 0)
        r8 = r4 + pltpu.roll(r4, n - 4, 0)
        r16 = r8 + pltpu.roll(r8, n - 8, 0)
        dzc = _pool_select(r2, r4, r8, r16)[0:tm, :] - dpooled
        carry_q[...] = q[0:HALO, :]

        dproj = jnp.concatenate([du, dv, dzb, dgb, dgc, dzc], axis=1).astype(BF16)
        dwin_acc[...] += _mm_tn(hb16, dproj)
        dh = _mm_nt(dproj, win_s[...])
        dx, dg = _rms_bwd(dh, g, xhat, rstd)
        dx_ref[...] = dx1v + dx
        dg_ref[...] += dg

        @pl.when(i == nt - 1)
        def _():
            pltpu.sync_copy(dwin_acc, dwin_hbm)
            _stage_bf16(dwout_acc, wout_s)
            pltpu.sync_copy(wout_s, dwout_hbm)
            for j in range(3):
                dwcat_ref[j] = dwcat_ref[j] * tril_ref[...]
            acc = dbm_acc[...]
            hi = acc.astype(BF16)
            lo = (acc - hi.astype(F32)).astype(BF16)
            dsb_ref[...] = _mm(hi, sel_ref[...]) + _mm(lo, sel_ref[...])

    return pl.pallas_call(
        body, name=f"mix_bwd_{layer}", grid=(nt,),
        in_specs=[pl.BlockSpec((tm, D), lambda i: (rev(i), 0)), pl.BlockSpec((tm, D), lambda i: (rev(i), 0)),
                  pl.BlockSpec((tm, D_IN), lambda i: (rev(i), 0)),
                  pl.BlockSpec((HALO, D_IN), lambda i: (jnp.maximum(rev(i) * hb - 1, 0), 0)),
                  _full((1, D))] + _mix_param_specs(tm)
                 + [_full((3, CHUNK, 2 * CHUNK)), _full((CHUNK, 2 * CHUNK)), _full((D_A, CHUNK)), HBM, HBM],
        out_specs=[pl.BlockSpec((tm, D), lambda i: (rev(i), 0)), _full((1, D)), _full((3, CHUNK, 2 * CHUNK)),
                   _full((CHUNK, CHUNK)), _full((1, D_A)), _full((1, D_A)), _full((3, D_B)), _full((D_C, D_C)),
                   _full((1, D_C)), HBM, HBM],
        out_shape=[jax.ShapeDtypeStruct((t, D), F32), jax.ShapeDtypeStruct((1, D), F32),
                   jax.ShapeDtypeStruct((3, CHUNK, 2 * CHUNK), F32), jax.ShapeDtypeStruct((CHUNK, CHUNK), F32),
                   jax.ShapeDtypeStruct((1, D_A), F32), jax.ShapeDtypeStruct((1, D_A), F32),
                   jax.ShapeDtypeStruct((3, D_B), F32), jax.ShapeDtypeStruct((D_C, D_C), F32),
                   jax.ShapeDtypeStruct((1, D_C), F32), pltpu.HBM((D, D_IN), F32), pltpu.HBM((D, D), BF16)],
        scratch_shapes=[pltpu.VMEM((D, D_IN), BF16), pltpu.VMEM((D, D), BF16),
                        pltpu.VMEM((D, D_IN), F32), pltpu.VMEM((D, D), F32), pltpu.VMEM((CHUNK, D_A), F32),
                        pltpu.VMEM((HALO, D_B), F32), pltpu.VMEM((HALO, D_C), F32), pltpu.VMEM((tm, D_A), F32),
                        pltpu.SemaphoreType.DMA((2,))],
        compiler_params=_params(),
    )(dx1, x, proj, proj, g_mix, *[mp[k] for k in _MIX_PARAM_NAMES], wtcat, trilcat, headsel, _in_hbm(w_in), _in_hbm(w_out))


def _coords():
    return lax.axis_index("x"), lax.axis_index("y"), lax.axis_index("c")


EFFECT = pltpu.SideEffectType.DATAFLOW_SIDE_EFFECTING


def _peer(k):
    x, y, c = _coords()
    px, py, pc = x ^ (k >> 2), y ^ ((k >> 1) & 1), c ^ (k & 1)
    return (px, py, pc), 4 * px + 2 * py + pc


def _landing_shape(shape, mode):
    if mode == "block":
        return (N_DEV,) + shape
    if mode == "slot":
        return shape
    if mode == "cols_in":
        return (shape[0], N_DEV * shape[1])
    return (N_DEV, shape[0], shape[1] // N_DEV)


def _pieces(src, land, mode, src_idx, land_idx):
    if mode == "block":
        return src, land.at[land_idx]
    if mode == "slot":
        return src.at[src_idx], land.at[land_idx]
    if mode == "cols_in":
        cw = src.shape[1]
        return src, land.at[:, pl.ds(pl.multiple_of(land_idx * cw, 128), cw)]
    cw = land.shape[2]
    return src.at[:, pl.ds(pl.multiple_of(src_idx * cw, 128), cw)], land.at[land_idx]


def _exchange_copy(src, land, mode, send_sem, recv_sem, ai, k, starting):
    x, y, c = _coords()
    peer, pidx = _peer(k)
    s, d = _pieces(src, land, mode, pidx, 4 * x + 2 * y + c if starting else pidx)
    i = ai * (N_DEV - 1) + k - 1
    return pltpu.make_async_remote_copy(src_ref=s, dst_ref=d, send_sem=send_sem.at[i], recv_sem=recv_sem.at[i],
                                        device_id=peer, device_id_type=MESH)


def _own_copy(src, land, mode, local_sem, ai):
    x, y, c = _coords()
    me = 4 * x + 2 * y + c
    s, d = _pieces(src, land, mode, me, me)
    return pltpu.make_async_copy(s, d, local_sem.at[ai])


def _item_src(ins, item):
    a, sub = item
    return ins[a] if sub is None else ins[a].at[sub]


def _exchange_start(srcs, items, modes, groups, name):
    n, ni, ng = len(srcs), len(items), len(groups)
    shapes = [srcs[a].shape if sub is None else srcs[a].shape[1:] for a, sub in items]
    land_shapes = [pltpu.HBM(_landing_shape(sh, m), srcs[a].dtype) for sh, m, (a, _) in zip(shapes, modes, items)]

    def body(*refs):
        ins = refs[:n]
        sems = refs[n:n + 3 * ng]
        land_refs = refs[n + 3 * ng:n + 3 * ng + ni]
        token = refs[-1]
        for g, idxs in enumerate(groups):
            for ai, it in enumerate(idxs):
                src = _item_src(ins, items[it])
                _own_copy(src, land_refs[it], modes[it], sems[3 * g + 2], ai).start()
                for k in range(1, N_DEV):
                    _exchange_copy(src, land_refs[it], modes[it], sems[3 * g], sems[3 * g + 1], ai, k, True).start()
        token[...] = jnp.zeros_like(token)

    sem_shapes = []
    for idxs in groups:
        sem_shapes += [pltpu.SemaphoreType.DMA((len(idxs) * (N_DEV - 1),))] * 2 + [pltpu.SemaphoreType.DMA((len(idxs),))]
    out = pl.pallas_call(
        body, name=name,
        out_shape=tuple(sem_shapes) + tuple(land_shapes) + (jax.ShapeDtypeStruct((8, 128), F32),),
        in_specs=[HBM] * n,
        out_specs=tuple([SEM] * (3 * ng) + [HBM] * ni + [pl.BlockSpec(memory_space=pltpu.VMEM)]),
        compiler_params=pltpu.CompilerParams(has_side_effects=EFFECT),
    )(*[pltpu.with_memory_space_constraint(s, pltpu.HBM) for s in srcs])
    sems = [tuple(out[3 * g:3 * g + 3]) for g in range(ng)]
    return sems, list(out[3 * ng:3 * ng + ni]), out[-1]


def _exchange_wait(sems, srcs, items, lands, modes, groups, after, name):
    n, ni, ng = len(srcs), len(items), len(groups)

    def body(*refs):
        ins, land_refs = refs[:n], refs[n:n + ni]
        sem_refs = refs[n + ni:n + ni + 3 * ng]
        for g, idxs in enumerate(groups):
            for ai, it in enumerate(idxs):
                src = _item_src(ins, items[it])
                _own_copy(src, land_refs[it], modes[it], sem_refs[3 * g + 2], ai).wait()
                for k in range(1, N_DEV):
                    cp = _exchange_copy(src, land_refs[it], modes[it], sem_refs[3 * g], sem_refs[3 * g + 1], ai, k, False)
                    cp.wait_send()
                    cp.wait_recv()

    flat_sems = [s for trio in sems for s in trio]
    afters = list(after) if isinstance(after, (list, tuple)) else [after]
    out = pl.pallas_call(
        body, name=name,
        out_shape=tuple(pltpu.HBM(l.shape, l.dtype) for l in lands),
        in_specs=[HBM] * (n + ni) + [SEM] * (3 * ng) + [ANY] * len(afters),
        out_specs=tuple([HBM] * ni),
        input_output_aliases={n + i: i for i in range(ni)},
        compiler_params=pltpu.CompilerParams(has_side_effects=EFFECT),
    )(*srcs, *lands, *flat_sems, *afters)
    return list(out)


def _small_all_reduce_call(buf):
    rows = buf.shape[0]

    def body(in_ref, out_ref, pair_ref, slots_ref, send_sems, recv_sems):
        x, y, c = _coords()
        chip = 2 * x + y
        sib = pltpu.make_async_remote_copy(src_ref=in_ref, dst_ref=pair_ref, send_sem=send_sems.at[0], recv_sem=recv_sems.at[0],
                                           device_id=(x, y, 1 - c), device_id_type=MESH)
        sib.start()
        sib.wait_recv()
        slots_ref[chip] = in_ref[...] + pair_ref[...]
        sends = []
        for k in range(1, 4):
            px, py = x ^ (k >> 1), y ^ (k & 1)
            cp = pltpu.make_async_remote_copy(src_ref=slots_ref.at[chip], dst_ref=slots_ref.at[chip],
                                              send_sem=send_sems.at[k], recv_sem=recv_sems.at[k],
                                              device_id=(px, py, c), device_id_type=MESH)
            cp.start()
            sends.append(cp)
        for cp in sends:
            cp.wait_recv()
        out_ref[...] = (slots_ref[0] + slots_ref[1]) + (slots_ref[2] + slots_ref[3])
        sib.wait_send()
        for cp in sends:
            cp.wait_send()

    vm = pl.BlockSpec(memory_space=pltpu.VMEM)
    return pl.pallas_call(
        body, name="small_grads_all_reduce",
        in_specs=[vm], out_specs=vm,
        out_shape=jax.ShapeDtypeStruct((rows, 128), F32),
        scratch_shapes=[pltpu.VMEM((rows, 128), F32), pltpu.VMEM((4, rows, 128), F32),
                        pltpu.SemaphoreType.DMA((4,)), pltpu.SemaphoreType.DMA((4,))],
        compiler_params=pltpu.CompilerParams(vmem_limit_bytes=V7X_VMEM_LIMIT),
    )(buf)


def _adamw(w, g, m, v):
    m = ADAM_B1 * m + (1.0 - ADAM_B1) * g
    v = ADAM_B2 * v + (1.0 - ADAM_B2) * (g * g)
    m_hat = m / (1.0 - ADAM_B1 ** ADAM_STEP)
    v_hat = v / (1.0 - ADAM_B2 ** ADAM_STEP)
    delta = -ADAM_LR * (m_hat / (jnp.sqrt(v_hat) + ADAM_EPS) + ADAM_WD * w)
    return delta, m, v


def _reduce_adamw_call(recvs, w, m, v, name):
    nl = len(recvs)
    _, r, c = recvs[0].shape
    rb = min(r, 256)
    nb = r // rb

    def body(*refs):
        recv_refs = refs[:nl]
        w_ref, m_ref, v_ref, g_ref, d_ref, nm_ref, nv_ref = refs[nl:]
        for l in range(nl):
            @pl.when(pl.program_id(0) == l)
            def _(l=l):
                g = recv_refs[l][0].astype(F32)
                for j in range(1, N_DEV):
                    g = g + recv_refs[l][j].astype(F32)
                delta, nm, nv = _adamw(w_ref[0], g, m_ref[0], v_ref[0])
                g_ref[0] = g
                d_ref[0] = delta
                nm_ref[0] = nm
                nv_ref[0] = nv

    def recv_spec(l):
        return pl.BlockSpec((N_DEV, rb, c), lambda lg, i: (0, jnp.where(lg == l, i, jnp.where(lg < l, 0, nb - 1)), 0))

    blk = pl.BlockSpec((1, rb, c), lambda lg, i: (lg, i, 0))
    shp = jax.ShapeDtypeStruct((nl, r, c), F32)
    return pl.pallas_call(
        body, name=name, grid=(nl, nb),
        in_specs=[recv_spec(l) for l in range(nl)] + [blk, blk, blk],
        out_specs=[blk, blk, blk, blk], out_shape=[shp, shp, shp, shp],
        compiler_params=pltpu.CompilerParams(dimension_semantics=("arbitrary", "arbitrary"), vmem_limit_bytes=V7X_VMEM_LIMIT),
    )(*recvs, w, m, v)


def _small_adamw_call(w, g, m, v):
    def body(w_ref, g_ref, m_ref, v_ref, d_ref, nm_ref, nv_ref):
        delta, nm, nv = _adamw(w_ref[...], g_ref[...], m_ref[...], v_ref[...])
        d_ref[...] = delta
        nm_ref[...] = nm
        nv_ref[...] = nv

    shp = jax.ShapeDtypeStruct(w.shape, F32)
    vm = pl.BlockSpec(memory_space=pltpu.VMEM)
    return pl.pallas_call(body, name="small_adamw", in_specs=[vm] * 4, out_specs=[vm] * 3, out_shape=[shp, shp, shp],
                          compiler_params=pltpu.CompilerParams(vmem_limit_bytes=V7X_VMEM_LIMIT))(w, g, m, v)


_GATHER_MODE = dict(w_in="block", w_out="block", w_ff1="cols_in", w_ff2="block", w_ple_gate="block", w_ple_proj="cols_in")
_SCATTER_MODE = dict(w_in="slot", w_out="slot", w_ff1="cols_out", w_ff2="slot", w_ple_gate="slot", w_ple_proj="cols_out")
_GROUP_A = ("w_in", "w_out")
_GROUP_B = ("w_ff1", "w_ff2", "w_ple_gate", "w_ple_proj")


def _gathered_full(k, landed):
    if _GATHER_MODE[k] == "cols_in":
        return landed
    n, r, c = landed.shape
    if k == "w_in":
        return jnp.transpose(landed, (1, 0, 2)).reshape(r, n * c)
    return landed.reshape(n * r, c)


def _grad_send(k, g):
    if k == "w_in":
        r, c8 = g.shape
        return jnp.transpose(g.reshape(r, N_DEV, c8 // N_DEV), (1, 0, 2)).astype(BF16)
    if _SCATTER_MODE[k] == "cols_out":
        return g
    r8, c = g.shape
    return g.reshape(N_DEV, r8 // N_DEV, c)


_SMALL_ORDER = ("norm_mix_g", "sgu_w", "sgu_b", "sgu_ln_g", "sgu_ln_b", "conv_w", "pool_w", "pool_scale",
                "norm_ff_g", "norm_ple_g", "final_g")


def _pack_small(d):
    pieces, layout = [], []
    for k in _SMALL_ORDER:
        flat = d[k].reshape(-1)
        n = flat.shape[0]
        pad = (-n) % 128
        pieces.append(jnp.pad(flat, (0, pad)))
        layout.append((k, d[k].shape, n, n + pad))
    flat = jnp.concatenate(pieces)
    pad = (-flat.shape[0]) % 1024
    return jnp.pad(flat, (0, pad)).reshape(-1, 128), layout


def _unpack_small(buf, layout):
    flat = buf.reshape(-1)
    out, off = {}, 0
    for k, shape, n, padded in layout:
        out[k] = flat[off:off + n].reshape(shape)
        off += padded
    return out


def kernel(x, p, norm_mix_g, w_in, sgu_w, sgu_b, sgu_ln_g, sgu_ln_b, conv_w, pool_w, pool_scale, w_out, norm_ff_g, w_ff1, w_ff2, norm_ple_g, w_ple_gate, w_ple_proj, final_g, loss_target, m_norm_mix_g, m_w_in, m_sgu_w, m_sgu_b, m_sgu_ln_g, m_sgu_ln_b, m_conv_w, m_pool_w, m_pool_scale, m_w_out, m_norm_ff_g, m_w_ff1, m_w_ff2, m_norm_ple_g, m_w_ple_gate, m_w_ple_proj, m_final_g, v_norm_mix_g, v_w_in, v_sgu_w, v_sgu_b, v_sgu_ln_g, v_sgu_ln_b, v_conv_w, v_pool_w, v_pool_scale, v_w_out, v_norm_ff_g, v_w_ff1, v_w_ff2, v_norm_ple_g, v_w_ple_gate, v_w_ple_proj, v_final_g):
    t = x.shape[1]
    xc, yc_, cc = _coords()
    me = 4 * xc + 2 * yc_ + cc
    tm = lambda want: min(want, t)

    shard_names = _GROUP_A + _GROUP_B
    shard = dict(w_in=w_in, w_out=w_out, w_ff1=w_ff1, w_ff2=w_ff2, w_ple_gate=w_ple_gate, w_ple_proj=w_ple_proj)
    conv_pad = jnp.zeros((16, 128), F32).at[0:DEPTH * 3, 0:D_B // N_DEV].set(conv_w.reshape(DEPTH * 3, D_B // N_DEV))
    ag_srcs = [shard[k].astype(BF16) for k in shard_names] + [conv_pad]
    ag_items, ag_modes, ag_groups = [], [], []
    for l in range(DEPTH):
        for names in (_GROUP_A, _GROUP_B):
            ag_groups.append(list(range(len(ag_items), len(ag_items) + len(names))))
            ag_items += [(shard_names.index(k), l) for k in names]
            ag_modes += [_GATHER_MODE[k] for k in names]
            if l == 0 and names is _GROUP_A:
                ag_groups[-1].append(len(ag_items))
                ag_items.append((len(shard_names), None))
                ag_modes.append("block")
    ag_sems, ag_lands, _ = _exchange_start(ag_srcs, ag_items, ag_modes, ag_groups, "weights_gather_start")

    def gathered(l, which, after):
        idxs = ag_groups[2 * l + which]
        landed = _exchange_wait([ag_sems[2 * l + which]], ag_srcs, [ag_items[i] for i in idxs], [ag_lands[i] for i in idxs],
                                [ag_modes[i] for i in idxs], [list(range(len(idxs)))], after, f"weights_gather_wait_{l}_{which}")
        full = {k: _gathered_full(k, got) for k, got in zip((_GROUP_A, _GROUP_B)[which], landed)}
        if l == 0 and which == 0:
            full["conv_w"] = jnp.transpose(landed[-1][:, 0:DEPTH * 3, 0:D_B // N_DEV].reshape(N_DEV, DEPTH, 3, D_B // N_DEV),
                                           (1, 2, 0, 3)).reshape(DEPTH, 3, D_B)
        return full

    idx = jnp.arange(D_A)
    pmat = ((idx[:, None] // 64) == (idx[None, :] // 64)).astype(BF16) * (1.0 / 64.0)
    pmat = pmat.astype(BF16)
    tril = jnp.tril(jnp.ones((CHUNK, CHUNK), F32))
    trilcat = jnp.concatenate([tril, tril], axis=1)
    headsel = ((idx[:, None] // 64) == jnp.arange(CHUNK)[None, :]).astype(BF16)
    row = lambda a: a.reshape(1, -1)

    def mix_params(l, tile):
        wm = sgu_w[l] * tril[None]
        wcat = jnp.stack([jnp.concatenate([wm[2 * j], wm[2 * j + 1]], axis=1) for j in range(3)]).astype(BF16)
        wtcat = jnp.stack([jnp.concatenate([wm[2 * j].T, wm[2 * j + 1].T], axis=1) for j in range(3)]).astype(BF16)
        bmat = jnp.tile(jnp.repeat(sgu_b[l].T, 64, axis=1), (tile // CHUNK, 1))
        bd = jnp.zeros((D_C, D_C), F32)
        for gi in range(4):
            bd = bd.at[gi * 64:(gi + 1) * 64, gi * 64:(gi + 1) * 64].set(pool_w[l, gi])
        mp = dict(pmat=pmat, ln_g=row(sgu_ln_g[l]), ln_b=row(sgu_ln_b[l]), wcat=wcat, bmat=bmat, conv_w=conv_full[l],
                  bd=bd.astype(BF16), pool_scale=row(pool_scale[l]))
        return mp, wtcat

    xs = x.reshape(t, D)
    saved, full_w = [], []
    conv_full = None
    for l in range(DEPTH):
        wa = gathered(l, 0, xs)
        if l == 0:
            conv_full = wa["conv_w"]
        mp, _ = mix_params(l, tm(TM_MIX_FWD))
        x1, proj = _mix_fwd_call(xs, row(norm_mix_g[l]), wa["w_in"], wa["w_out"], mp, tm(TM_MIX_FWD), l)
        wb = gathered(l, 1, x1)
        x2, x3, r, gate = _ffn_fwd_call(x1, p[l, 0], row(norm_ff_g[l]), row(norm_ple_g[l]), wb["w_ff1"], wb["w_ff2"],
                                        wb["w_ple_gate"], wb["w_ple_proj"], tm(TM_FFN_FWD), l)
        saved.append((xs, proj, x1, r, x2, gate))
        full_w.append({**wa, **wb})
        xs = x3

    sq, dx, dfinal = _loss_call(xs, loss_target.reshape(t, D), row(final_g), tm(TM_LOSS))
    loss = lax.psum(jnp.sum(sq) * (0.5 / D), ("x", "y", "c"))

    small = {k: [None] * DEPTH for k in _SMALL_ORDER if k != "final_g"}
    ex = {}
    token = None

    def after_start(g):
        return g if token is None else g + token[0:1, 0:1]

    def start_exchange(l, which, grads):
        names = (_GROUP_A, _GROUP_B)[which]
        sends = [_grad_send(k, grads[k]) for k in names]
        sems, lands, tok = _exchange_start(sends, [(i, None) for i in range(len(names))], [_SCATTER_MODE[k] for k in names],
                                           [list(range(len(names)))], f"grads_exchange_start_{l}_{which}")
        ex[(l, which)] = (sems[0], sends, lands)
        return tok

    for l in reversed(range(DEPTH)):
        x0, proj, x1, r, x2, gate = saved[l]
        fw = full_w[l]
        dx2, dgple, dwg, dwp = _ple_bwd_call(dx, x2, gate, p[l, 0], after_start(row(norm_ple_g[l])), fw["w_ple_gate"], fw["w_ple_proj"],
                                             tm(TM_PLE_BWD), l)
        da, dw2 = _ffn_bwd_hidden_call(dx2, r, fw["w_ff2"], tm(TM_FFN_BWD), l)
        dx1, dgff, dw1 = _ffn_bwd_input_call(da, x1, dx2, row(norm_ff_g[l]), fw["w_ff1"], tm(TM_FFN_BWD), l)
        token = start_exchange(l, 1, dict(w_ff1=dw1, w_ff2=dw2, w_ple_gate=dwg, w_ple_proj=dwp))
        mp, wtcat = mix_params(l, tm(TM_MIX_BWD))
        (dx, dgmix, dwcat, dsb, dlng, dlnb, dconv, dbd, dscale, dwin, dwout) = _mix_bwd_call(
            dx1, x0, proj, after_start(row(norm_mix_g[l])), fw["w_in"], fw["w_out"], mp, wtcat, trilcat, headsel, tm(TM_MIX_BWD), l)
        token = start_exchange(l, 0, dict(w_in=dwin, w_out=dwout))
        small["norm_mix_g"][l] = dgmix[0]
        small["sgu_w"][l] = jnp.stack([dwcat[h // 2][:, (h % 2) * CHUNK:(h % 2 + 1) * CHUNK] for h in range(6)])
        small["sgu_b"][l] = dsb[:, 0:6].T
        small["sgu_ln_g"][l], small["sgu_ln_b"][l] = dlng[0], dlnb[0]
        small["conv_w"][l] = dconv
        small["pool_w"][l] = jnp.stack([dbd[gi * 64:(gi + 1) * 64, gi * 64:(gi + 1) * 64] for gi in range(4)])
        small["pool_scale"][l] = dscale[0]
        small["norm_ff_g"][l], small["norm_ple_g"][l] = dgff[0], dgple[0]
    grad_x = dx.reshape(1, t, D)

    state = dict(w_in=(w_in, m_w_in, v_w_in), w_out=(w_out, m_w_out, v_w_out), w_ff1=(w_ff1, m_w_ff1, v_w_ff1),
                 w_ff2=(w_ff2, m_w_ff2, v_w_ff2), w_ple_gate=(w_ple_gate, m_w_ple_gate, v_w_ple_gate),
                 w_ple_proj=(w_ple_proj, m_w_ple_proj, v_w_ple_proj))
    res = {}

    def finish_group(which, after):
        names = (_GROUP_A, _GROUP_B)[which]
        n = len(names)
        sems = [ex[(l, which)][0] for l in range(DEPTH)]
        sends = [s_ for l in range(DEPTH) for s_ in ex[(l, which)][1]]
        lands = [a_ for l in range(DEPTH) for a_ in ex[(l, which)][2]]
        landed = _exchange_wait(sems, sends, [(i, None) for i in range(DEPTH * n)], lands, [_SCATTER_MODE[k] for k in names] * DEPTH,
                                [list(range(l * n, (l + 1) * n)) for l in range(DEPTH)], after, f"grads_exchange_wait_{which}")
        for i, k in enumerate(names):
            res[k] = _reduce_adamw_call([landed[l * n + i] for l in range(DEPTH)], *state[k], "reduce_adamw_" + k)

    finish_group(1, dx)

    small_g = {k: jnp.stack(vs) for k, vs in small.items()}
    small_g["final_g"] = dfinal[0]
    gbuf, layout = _pack_small(small_g)
    gsum = _small_all_reduce_call(gbuf)
    conv_cols = lambda a: lax.dynamic_slice_in_dim(a, me * (D_B // N_DEV), D_B // N_DEV, axis=2)
    pad_conv = lambda a: jnp.zeros((DEPTH, 3, D_B), F32).at[:, :, 0:D_B // N_DEV].set(a)
    small_w = dict(norm_mix_g=norm_mix_g, sgu_w=sgu_w, sgu_b=sgu_b, sgu_ln_g=sgu_ln_g, sgu_ln_b=sgu_ln_b, conv_w=pad_conv(conv_w),
                   pool_w=pool_w, pool_scale=pool_scale, norm_ff_g=norm_ff_g, norm_ple_g=norm_ple_g, final_g=final_g)
    small_m = dict(norm_mix_g=m_norm_mix_g, sgu_w=m_sgu_w, sgu_b=m_sgu_b, sgu_ln_g=m_sgu_ln_g, sgu_ln_b=m_sgu_ln_b,
                   conv_w=pad_conv(m_conv_w), pool_w=m_pool_w, pool_scale=m_pool_scale, norm_ff_g=m_norm_ff_g,
                   norm_ple_g=m_norm_ple_g, final_g=m_final_g)
    small_v = dict(norm_mix_g=v_norm_mix_g, sgu_w=v_sgu_w, sgu_b=v_sgu_b, sgu_ln_g=v_sgu_ln_g, sgu_ln_b=v_sgu_ln_b,
                   conv_w=pad_conv(v_conv_w), pool_w=v_pool_w, pool_scale=v_pool_scale,
                   norm_ff_g=v_norm_ff_g, norm_ple_g=v_norm_ple_g, final_g=v_final_g)
    gs = _unpack_small(gsum, layout)
    gs_local = dict(gs)
    gs_local["conv_w"] = pad_conv(conv_cols(gs["conv_w"]))
    g_loc, _ = _pack_small(gs_local)
    wbuf, _ = _pack_small(small_w)
    mbuf, _ = _pack_small(small_m)
    vbuf, _ = _pack_small(small_v)
    dbuf, nmbuf, nvbuf = _small_adamw_call(wbuf, g_loc, mbuf, vbuf)
    sd, sm, sv = _unpack_small(dbuf, layout), _unpack_small(nmbuf, layout), _unpack_small(nvbuf, layout)
    unconv = lambda a: a[:, :, 0:D_B // N_DEV]
    for dct in (gs_local, sd, sm, sv):
        dct["conv_w"] = unconv(dct["conv_w"])

    finish_group(0, [res[_GROUP_B[-1]][0], dbuf])

    order = ["norm_mix_g", "w_in", "sgu_w", "sgu_b", "sgu_ln_g", "sgu_ln_b", "conv_w", "pool_w", "pool_scale", "w_out",
             "norm_ff_g", "w_ff1", "w_ff2", "norm_ple_g", "w_ple_gate", "w_ple_proj", "final_g"]
    outs = [loss, grad_x]
    for which in range(4):
        for k in order:
            if k in res:
                outs.append(res[k][which])
            else:
                outs.append((gs_local, sd, sm, sv)[which][k])
    return tuple(outs)
```

```python
import functools
import math

import jax
import jax.numpy as jnp
from jax import lax
from jax.experimental import pallas as pl
from jax.experimental.pallas import tpu as pltpu

F32 = jnp.float32
BF16 = jnp.bfloat16

D = 1024
D_IN = 2176
D_A = 384
D_B = 384
D_C = 256
D_FF = 4096
D_PLE = 256
DEPTH = 4
CHUNK = 128
HALO = 16
FF_BLK = 1024
N_DEV = 8
RMS_EPS = 1e-6
LN_EPS = 1e-5
ADAM_LR = 0.001
ADAM_B1 = 0.9
ADAM_B2 = 0.999
ADAM_EPS = 1e-08
ADAM_WD = 0.01
ADAM_STEP = 10

TM_MIX_FWD = 1024
TM_FFN_FWD = 512
TM_LOSS = 512
TM_PLE_BWD = 1024
TM_FFN_BWD = 512
TM_MIX_BWD = 512
RUN_MIX_FWD = 1024
RUN_MIX_BWD = 512
V7X_VMEM_LIMIT = 60000 * 1024

ANY = pl.BlockSpec(memory_space=pl.ANY)
HBM = pl.BlockSpec(memory_space=pltpu.HBM)
SEM = pl.BlockSpec(memory_space=pltpu.SEMAPHORE)
MESH = pl.DeviceIdType.MESH


def _params(vmem=V7X_VMEM_LIMIT):
    return pltpu.CompilerParams(dimension_semantics=("arbitrary",), vmem_limit_bytes=vmem)


def _in_hbm(a):
    return pltpu.with_memory_space_constraint(a, pltpu.HBM)


def _full(shape):
    nd = len(shape)
    return pl.BlockSpec(shape, lambda i: (0,) * nd)


def _rows(tm, cols):
    return pl.BlockSpec((tm, cols), lambda i: (i, 0))


def _layer_rows(layer, tm, cols):
    return pl.BlockSpec((None, tm, cols), lambda i: (layer, i, 0))


def _mm(a, b):
    return jnp.dot(a, b, preferred_element_type=F32)


def _mm_nt(a, b):
    return lax.dot_general(a, b, (((1,), (1,)), ((), ())), preferred_element_type=F32)


def _mm_tn(a, b):
    return lax.dot_general(a, b, (((0,), (0,)), ((), ())), preferred_element_type=F32)


def _erf(x):
    ax = jnp.abs(x)
    t = 1.0 / (1.0 + 0.3275911 * ax)
    poly = t * (0.254829592 + t * (-0.284496736 + t * (1.421413741 + t * (-1.453152027 + t * 1.061405429))))
    y = 1.0 - poly * jnp.exp(-ax * ax)
    return jnp.where(x < 0, -y, y)


def _gelu_and_grad(x):
    cdf = 0.5 * (1.0 + _erf(x * (1.0 / math.sqrt(2.0))))
    pdf = jnp.exp(-0.5 * x * x) * (1.0 / math.sqrt(2.0 * math.pi))
    return x * cdf, cdf + x * pdf


def _rms(x, g):
    rstd = lax.rsqrt(jnp.mean(x * x, axis=-1, keepdims=True) + RMS_EPS)
    xhat = x * rstd
    return xhat * g, xhat, rstd


def _rms_bwd(dy, g, xhat, rstd):
    dg = jnp.sum(dy * xhat, axis=0, keepdims=True)
    dxh = dy * g
    dx = rstd * (dxh - xhat * jnp.mean(dxh * xhat, axis=-1, keepdims=True))
    return dx, dg


def _shift_down(ext, k):
    return pltpu.roll(ext, k, 0)[HALO:, :]


def _shift_up(ext, k):
    n = ext.shape[0]
    return pltpu.roll(ext, n - k, 0)[: n - HALO, :]


def _pool_select(s2, s4, s8, s16):
    lane = lax.broadcasted_iota(jnp.int32, s2.shape, 1)
    return jnp.where(lane < 64, s2, jnp.where(lane < 128, s4, jnp.where(lane < 192, s8, s16)))


def _pool_inv_count(tile_start, tm):
    pos = lax.broadcasted_iota(jnp.int32, (tm, D_C), 0) + tile_start + 1
    lane = lax.broadcasted_iota(jnp.int32, (tm, D_C), 1)
    win = jnp.where(lane < 64, 2, jnp.where(lane < 128, 4, jnp.where(lane < 192, 8, 16)))
    return 1.0 / jnp.minimum(pos, win).astype(F32)


def _head_halves(a):
    lane = lax.broadcasted_iota(jnp.int32, a.shape, 1)
    even = (lane & 64) == 0
    return jnp.where(even, a, 0.0).astype(BF16), jnp.where(even, 0.0, a).astype(BF16)


def _head_stack(lo, hi, j, nch):
    return jnp.concatenate(
        [jnp.concatenate([lo[c * CHUNK:(c + 1) * CHUNK, j * 128:(j + 1) * 128], hi[c * CHUNK:(c + 1) * CHUNK, j * 128:(j + 1) * 128]], axis=0)
         for c in range(nch)], axis=1)


def _chunks_to_lanes(a, j, nch):
    return jnp.concatenate([a[c * CHUNK:(c + 1) * CHUNK, j * 128:(j + 1) * 128] for c in range(nch)], axis=1)


def _lanes_to_chunks(o, nch):
    return jnp.concatenate([o[:, c * CHUNK:(c + 1) * CHUNK] for c in range(nch)], axis=0)


def _loads(pairs, sem):
    return [pltpu.make_async_copy(src, dst, sem.at[n]) for n, (src, dst) in enumerate(pairs)]


def _load_all(loads):
    for cp in loads:
        cp.start()
    for cp in loads:
        cp.wait()


def _stage_bf16(acc, stage):
    rows = acc.shape[0]
    strip = min(rows, 128)

    @pl.loop(0, rows // strip)
    def _(n):
        sl = pl.ds(pl.multiple_of(n * strip, strip), strip)
        stage[sl, :] = acc[sl, :].astype(BF16)


def _mixers_fwd(pf, halo_hc, halo_zc, tile_start, prm):
    tm = pf.shape[0]
    nch = tm // CHUNK
    u, v = pf[:, 0:D_A], pf[:, D_A:2 * D_A]
    zb, gb, gc = pf[:, 768:1152], pf[:, 1152:1536], pf[:, 1536:1920]
    zc = pf[:, 1920:2176]
    r = {}
    gu, r["dgelu_u"] = _gelu_and_grad(u)
    gv, r["dgelu_v"] = _gelu_and_grad(v)
    pmat = prm["pmat"][...]
    mu = _mm(gv.astype(BF16), pmat)
    dv = gv - mu
    var = _mm((dv * dv).astype(BF16), pmat)
    rstd = lax.rsqrt(var + LN_EPS)
    xh = dv * rstd
    vlo, vhi = _head_halves(xh * prm["ln_g"][...] + prm["ln_b"][...])
    cols, v2s = [], []
    for j in range(3):
        v2 = _head_stack(vlo, vhi, j, nch)
        v2s.append(v2)
        cols.append(_lanes_to_chunks(_mm(prm["wcat"][j], v2), nch))
    mixed = jnp.concatenate(cols, axis=1) + jnp.concatenate([prm["bmat"][...]] * nch, axis=0)
    ya = gu * mixed
    r.update(gu=gu, mixed=mixed, v2s=v2s, xh=xh, ln_rstd=rstd)
    w0, w1, w2 = prm["conv_w"][0:1, :], prm["conv_w"][1:2, :], prm["conv_w"][2:3, :]
    hc = gc * zb
    ext = jnp.concatenate([halo_hc, hc], axis=0)
    h1, h2 = _shift_down(ext, 1), _shift_down(ext, 2)
    yc = w2 * hc + w1 * h1 + w0 * h2
    yb = gb * yc
    r.update(hc=hc, h1=h1, h2=h2, yc=yc, zb=zb, gb=gb, gc=gc, w0=w0, w1=w1, w2=w2)
    ext = jnp.concatenate([halo_zc, zc], axis=0)
    s2 = ext + pltpu.roll(ext, 1, 0)
    s4 = s2 + pltpu.roll(s2, 2, 0)
    s8 = s4 + pltpu.roll(s4, 4, 0)
    s16 = s8 + pltpu.roll(s8, 8, 0)
    inv = _pool_inv_count(tile_start, tm)
    pooled = _pool_select(s2, s4, s8, s16)[HALO:, :] * inv - zc
    pooledb = pooled.astype(BF16)
    pm = _mm(pooledb, prm["bd"][...])
    scale = prm["pool_scale"][...]
    ycm = pm * scale
    r.update(inv=inv, pooledb=pooledb, pm=pm, scale=scale, zc=zc)
    r["ycat"] = jnp.concatenate([ya, yb, ycm], axis=1)
    return r


_MIX_PARAM_NAMES = ("pmat", "ln_g", "ln_b", "wcat", "bmat", "conv_w", "bd", "pool_scale")


def _mix_param_specs():
    return [_full((D_A, D_A)), _full((1, D_A)), _full((1, D_A)), _full((3, CHUNK, 2 * CHUNK)), _full((CHUNK, D_A)),
            _full((3, D_B)), _full((D_C, D_C)), _full((1, D_C))]


def _mix_fwd_call(x, g_mix, w_in_t, w_out, mp, tm, run, layer):
    t = x.shape[0]
    nt = t // tm

    def body(x_ref, g_ref, pmat, ln_g, ln_b, wcat, bmat, conv_w, bd, pool_scale, win_hbm, wout_hbm,
             x1_ref, proj_ref, win_s, wout_s, halo_hc, halo_zc, load_sem):
        i = pl.program_id(0)
        loads = _loads([(win_hbm, win_s), (wout_hbm, wout_s)], load_sem)

        @pl.when(i == 0)
        def _():
            _load_all(loads)
            halo_hc[...] = jnp.zeros_like(halo_hc)
            halo_zc[...] = jnp.zeros_like(halo_zc)

        prm = dict(pmat=pmat, ln_g=ln_g, ln_b=ln_b, wcat=wcat, bmat=bmat, conv_w=conv_w, bd=bd, pool_scale=pool_scale)
        xv = x_ref[...]
        h, _, _ = _rms(xv, g_ref[...])
        pf = _mm_nt(h.astype(BF16), win_s[...])
        proj_ref[...] = pf.astype(BF16)
        hh, hz = halo_hc[...], halo_zc[...]
        parts = []
        for c in range(tm // run):
            r = _mixers_fwd(pf[c * run:(c + 1) * run, :], hh, hz, i * tm + c * run, prm)
            hh, hz = r["hc"][run - HALO:, :], r["zc"][run - HALO:, :]
            parts.append(r["ycat"].astype(BF16))
        halo_hc[...] = hh
        halo_zc[...] = hz
        x1_ref[...] = xv + _mm(jnp.concatenate(parts, axis=0), wout_s[...])

    return pl.pallas_call(
        body, name=f"mix_fwd_{layer}", grid=(nt,),
        in_specs=[_rows(tm, D), _full((1, D))] + _mix_param_specs() + [HBM, HBM],
        out_specs=[_rows(tm, D), _rows(tm, D_IN)],
        out_shape=[jax.ShapeDtypeStruct((t, D), F32), jax.ShapeDtypeStruct((t, D_IN), BF16)],
        scratch_shapes=[pltpu.VMEM((D_IN, D), BF16), pltpu.VMEM((D, D), BF16),
                        pltpu.VMEM((HALO, D_B), F32), pltpu.VMEM((HALO, D_C), F32), pltpu.SemaphoreType.DMA((2,))],
        compiler_params=_params(),
    )(x, g_mix, *[mp[k] for k in _MIX_PARAM_NAMES], _in_hbm(w_in_t), _in_hbm(w_out))


def _ffn_fwd_call(x1, p, g_ff, g_ple, w1, w2, wg, wp, tm, layer):
    t = x1.shape[0]
    nt = t // tm

    def body(x1_ref, p_ref, gff_ref, gple_ref, w1_hbm, w2_hbm, wg_hbm, wp_hbm,
             x2_ref, x3_ref, r_ref, gate_ref, w1_s, w2_s, wg_s, wp_s, load_sem):
        i = pl.program_id(0)
        loads = _loads([(w1_hbm, w1_s), (w2_hbm, w2_s), (wg_hbm, wg_s), (wp_hbm, wp_s)], load_sem)

        @pl.when(i == 0)
        def _():
            _load_all(loads)

        x1v = x1_ref[...]
        h2, _, _ = _rms(x1v, gff_ref[...])
        h2b = h2.astype(BF16)
        acc = x1v
        for j in range(D_FF // FF_BLK):
            blk = slice(j * FF_BLK, (j + 1) * FF_BLK)
            rj = jnp.maximum(_mm(h2b, w1_s[:, blk]), 0.0)
            r_ref[:, blk] = rj.astype(BF16)
            acc = acc + _mm((rj * rj).astype(BF16), w2_s[blk, :])
        x2_ref[...] = acc
        n3, _, _ = _rms(acc, gple_ref[...])
        gate = jax.nn.sigmoid(_mm(n3.astype(BF16), wg_s[...]))
        gate_ref[...] = gate.astype(BF16)
        pp = _mm(p_ref[...].astype(BF16), wp_s[...])
        x3_ref[...] = acc + pp * gate

    return pl.pallas_call(
        body, name=f"ffn_fwd_{layer}", grid=(nt,),
        in_specs=[_rows(tm, D), _layer_rows(layer, tm, D_PLE), _full((1, D)), _full((1, D)), HBM, HBM, HBM, HBM],
        out_specs=[_rows(tm, D), _rows(tm, D), _rows(tm, D_FF), _rows(tm, D)],
        out_shape=[jax.ShapeDtypeStruct((t, D), F32), jax.ShapeDtypeStruct((t, D), F32),
                   jax.ShapeDtypeStruct((t, D_FF), BF16), jax.ShapeDtypeStruct((t, D), BF16)],
        scratch_shapes=[pltpu.VMEM((D, D_FF), BF16), pltpu.VMEM((D_FF, D), BF16),
                        pltpu.VMEM((D, D), BF16), pltpu.VMEM((D_PLE, D), BF16), pltpu.SemaphoreType.DMA((4,))],
        compiler_params=_params(),
    )(x1, p, g_ff, g_ple, _in_hbm(w1), _in_hbm(w2), _in_hbm(wg), _in_hbm(wp))


def _loss_call(xl, target, final_g, tm):
    t = xl.shape[0]
    nt = t // tm

    def body(x_ref, t_ref, g_ref, sq_ref, dx_ref, dg_ref):
        i = pl.program_id(0)

        @pl.when(i == 0)
        def _():
            sq_ref[...] = jnp.zeros_like(sq_ref)
            dg_ref[...] = jnp.zeros_like(dg_ref)

        g = g_ref[...]
        y, xhat, rstd = _rms(x_ref[...], g)
        err = y - t_ref[...]
        sq_ref[...] += jnp.sum(err * err, axis=0, keepdims=True)
        dx, dg = _rms_bwd(err * (1.0 / D), g, xhat, rstd)
        dx_ref[...] = dx
        dg_ref[...] += dg

    return pl.pallas_call(
        body, name="loss_head", grid=(nt,),
        in_specs=[_rows(tm, D), _rows(tm, D), _full((1, D))],
        out_specs=[_full((1, D)), _rows(tm, D), _full((1, D))],
        out_shape=[jax.ShapeDtypeStruct((1, D), F32), jax.ShapeDtypeStruct((t, D), F32), jax.ShapeDtypeStruct((1, D), F32)],
        compiler_params=_params(),
    )(xl, target, final_g)


def _ple_bwd_call(dx3, x2, gate, p, g_ple, wg, wp, tm, layer):
    t = dx3.shape[0]
    nt = t // tm

    def body(dx3_ref, x2_ref, gate_ref, p_ref, g_ref, wg_hbm, wp_hbm,
             dx2_ref, dg_ref, dwg_hbm, dwp_hbm, wg_s, wp_s, dwg_acc, dwp_acc, load_sem):
        i = pl.program_id(0)
        loads = _loads([(wp_hbm, wp_s), (wg_hbm, wg_s)], load_sem)

        @pl.when(i == 0)
        def _():
            _load_all(loads)
            dwg_acc[...] = jnp.zeros_like(dwg_acc)
            dwp_acc[...] = jnp.zeros_like(dwp_acc)
            dg_ref[...] = jnp.zeros_like(dg_ref)

        g = g_ref[...]
        dx3v = dx3_ref[...]
        gatev = gate_ref[...].astype(F32)
        pb = p_ref[...].astype(BF16)
        pp = _mm(pb, wp_s[...])
        dwp_acc[...] += _mm_tn(pb, (dx3v * gatev).astype(BF16))
        dgpre = (dx3v * pp * gatev * (1.0 - gatev)).astype(BF16)
        n3, xhat, rstd = _rms(x2_ref[...], g)
        dwg_acc[...] += _mm_tn(n3.astype(BF16), dgpre)
        dn3 = _mm_nt(dgpre, wg_s[...])
        dx, dg = _rms_bwd(dn3, g, xhat, rstd)
        dx2_ref[...] = dx3v + dx
        dg_ref[...] += dg

        @pl.when(i == nt - 1)
        def _():
            _stage_bf16(dwg_acc, wg_s)
            _stage_bf16(dwp_acc, wp_s)
            pltpu.sync_copy(wg_s, dwg_hbm)
            pltpu.sync_copy(wp_s, dwp_hbm)

    return pl.pallas_call(
        body, name=f"ple_bwd_{layer}", grid=(nt,),
        in_specs=[_rows(tm, D), _rows(tm, D), _rows(tm, D), _layer_rows(layer, tm, D_PLE), _full((1, D)), HBM, HBM],
        out_specs=[_rows(tm, D), _full((1, D)), HBM, HBM],
        out_shape=[jax.ShapeDtypeStruct((t, D), F32), jax.ShapeDtypeStruct((1, D), F32),
                   pltpu.HBM((D, D), BF16), pltpu.HBM((D_PLE, D), BF16)],
        scratch_shapes=[pltpu.VMEM((D, D), BF16), pltpu.VMEM((D_PLE, D), BF16),
                        pltpu.VMEM((D, D), F32), pltpu.VMEM((D_PLE, D), F32), pltpu.SemaphoreType.DMA((2,))],
        compiler_params=_params(),
    )(dx3, x2, gate, p, g_ple, _in_hbm(wg), _in_hbm(wp))


def _ffn_bwd_hidden_call(dx2, r, w2, tm, layer):
    t = dx2.shape[0]
    nt = t // tm

    def body(dx2_ref, r_ref, w2_hbm, da_ref, dw2_hbm, w2_s, dw2_acc, load_sem):
        i = pl.program_id(0)
        loads = _loads([(w2_hbm, w2_s)], load_sem)

        @pl.when(i == 0)
        def _():
            _load_all(loads)
            dw2_acc[...] = jnp.zeros_like(dw2_acc)

        dxb = dx2_ref[...].astype(BF16)
        for j in range(D_FF // FF_BLK):
            blk = slice(j * FF_BLK, (j + 1) * FF_BLK)
            rj = r_ref[:, blk].astype(F32)
            ds = _mm_nt(dxb, w2_s[blk, :])
            da_ref[:, blk] = (2.0 * rj * ds).astype(BF16)
            dw2_acc[blk, :] += _mm_tn((rj * rj).astype(BF16), dxb)

        @pl.when(i == nt - 1)
        def _():
            _stage_bf16(dw2_acc, w2_s)
            pltpu.sync_copy(w2_s, dw2_hbm)

    return pl.pallas_call(
        body, name=f"ffn_bwd_hidden_{layer}", grid=(nt,),
        in_specs=[_rows(tm, D), _rows(tm, D_FF), HBM],
        out_specs=[_rows(tm, D_FF), HBM],
        out_shape=[jax.ShapeDtypeStruct((t, D_FF), BF16), pltpu.HBM((D_FF, D), BF16)],
        scratch_shapes=[pltpu.VMEM((D_FF, D), BF16), pltpu.VMEM((D_FF, D), F32), pltpu.SemaphoreType.DMA((1,))],
        compiler_params=_params(),
    )(dx2, r, _in_hbm(w2))


def _ffn_bwd_input_call(da, x1, dx2, g_ff, w1, tm, layer):
    t = dx2.shape[0]
    nt = t // tm

    def body(da_ref, x1_ref, dx2_ref, g_ref, w1_hbm, dx1_ref, dg_ref, dw1_hbm, w1_s, dw1_acc, load_sem):
        i = pl.program_id(0)
        loads = _loads([(w1_hbm, w1_s)], load_sem)

        @pl.when(i == 0)
        def _():
            _load_all(loads)
            dw1_acc[...] = jnp.zeros_like(dw1_acc)
            dg_ref[...] = jnp.zeros_like(dg_ref)

        g = g_ref[...]
        h2, xhat, rstd = _rms(x1_ref[...], g)
        h2b = h2.astype(BF16)
        dh2 = jnp.zeros((tm, D), F32)
        for j in range(D_FF // FF_BLK):
            blk = slice(j * FF_BLK, (j + 1) * FF_BLK)
            daj = da_ref[:, blk]
            dh2 = dh2 + _mm_nt(daj, w1_s[:, blk])
            dw1_acc[:, blk] += _mm_tn(h2b, daj)
        dx, dg = _rms_bwd(dh2, g, xhat, rstd)
        dx1_ref[...] = dx2_ref[...] + dx
        dg_ref[...] += dg

        @pl.when(i == nt - 1)
        def _():
            _stage_bf16(dw1_acc, w1_s)
            pltpu.sync_copy(w1_s, dw1_hbm)

    return pl.pallas_call(
        body, name=f"ffn_bwd_input_{layer}", grid=(nt,),
        in_specs=[_rows(tm, D_FF), _rows(tm, D), _rows(tm, D), _full((1, D)), HBM],
        out_specs=[_rows(tm, D), _full((1, D)), HBM],
        out_shape=[jax.ShapeDtypeStruct((t, D), F32), jax.ShapeDtypeStruct((1, D), F32), pltpu.HBM((D, D_FF), BF16)],
        scratch_shapes=[pltpu.VMEM((D, D_FF), BF16), pltpu.VMEM((D, D_FF), F32), pltpu.SemaphoreType.DMA((1,))],
        compiler_params=_params(),
    )(da, x1, dx2, g_ff, _in_hbm(w1))


def _mix_bwd_call(dx1, x, proj, g_mix, w_in_t, w_out, mp, wtcat, trilcat, headsel, tm, run, layer):
    t = dx1.shape[0]
    nt = t // tm
    nrun = tm // run
    nch = run // CHUNK
    hb = tm // HALO

    def rev(i):
        return nt - 1 - i

    def body(dx1_ref, x_ref, proj_ref, halo_ref, g_ref, pmat, ln_g, ln_b, wcat, bmat, conv_w, bd, pool_scale,
             wtcat_ref, tril_ref, sel_ref, win_hbm, wout_hbm,
             dx_ref, dg_ref, dwcat_ref, dsb_ref, dlng_ref, dlnb_ref, dconv_ref, dbd_ref, dscale_ref, dwin_hbm, dwout_hbm,
             win_s, wout_s, dwin_acc, dwout_acc, dbm_acc, carry_yc, carry_q, load_sem):
        i = pl.program_id(0)
        ri = nt - 1 - i
        loads = _loads([(wout_hbm, wout_s), (win_hbm, win_s)], load_sem)

        @pl.when(i == 0)
        def _():
            _load_all(loads)
            for ref in (dwin_acc, dwout_acc, dbm_acc, carry_yc, carry_q, dg_ref, dwcat_ref, dlng_ref, dlnb_ref,
                        dconv_ref, dbd_ref, dscale_ref):
                ref[...] = jnp.zeros_like(ref)

        prm = dict(pmat=pmat, ln_g=ln_g, ln_b=ln_b, wcat=wcat, bmat=bmat, conv_w=conv_w, bd=bd, pool_scale=pool_scale)
        g = g_ref[...]
        h, xhat, rstd = _rms(x_ref[...], g)
        hb16 = h.astype(BF16)
        dx1v = dx1_ref[...]
        dx1b = dx1v.astype(BF16)
        dycat = _mm_nt(dx1b, wout_s[...])
        lng = ln_g[...]
        pm_ = pmat[...]
        cy, cq = carry_yc[...], carry_q[...]
        ycat_parts, dproj_parts = [None] * nrun, [None] * nrun
        dbm = dlng = dlnb = dscale = dcv0 = dcv1 = dcv2 = None
        add = lambda tot, v: v if tot is None else tot + v
        for c in reversed(range(nrun)):
            rows = slice(c * run, (c + 1) * run)
            pf = proj_ref[rows, :].astype(F32)
            if c > 0:
                ph = proj_ref[c * run - HALO:c * run, :].astype(F32)
            else:
                ph = halo_ref[...].astype(F32) * (ri > 0).astype(F32)
            r = _mixers_fwd(pf, ph[:, 1536:1920] * ph[:, 768:1152], ph[:, 1920:2176], ri * tm + c * run, prm)
            ycat_parts[c] = r["ycat"].astype(BF16)
            dya, dyb, dyc = dycat[rows, 0:D_A], dycat[rows, D_A:D_A + D_B], dycat[rows, D_A + D_B:D]

            dgu = dya * r["mixed"]
            dmix = dya * r["gu"]
            dmix_b = dmix.astype(BF16)
            dlo, dhi = _head_halves(dmix)
            for k in range(nch):
                dbm = add(dbm, dmix[k * CHUNK:(k + 1) * CHUNK, :])
            dvn_cols = []
            for j in range(3):
                dwcat_ref[j] += _mm_nt(_chunks_to_lanes(dmix_b, j, nch), r["v2s"][j])
                dvn_cols.append(_lanes_to_chunks(_mm(wtcat_ref[j], _head_stack(dlo, dhi, j, nch)), nch))
            dvn = jnp.concatenate(dvn_cols, axis=1)
            xh = r["xh"]
            dlng = add(dlng, jnp.sum(dvn * xh, axis=0, keepdims=True))
            dlnb = add(dlnb, jnp.sum(dvn, axis=0, keepdims=True))
            dxh = dvn * lng
            m1 = _mm(dxh.astype(BF16), pm_)
            m2 = _mm((dxh * xh).astype(BF16), pm_)
            dgv = r["ln_rstd"] * (dxh - m1 - xh * m2)
            du = dgu * r["dgelu_u"]
            dv = dgv * r["dgelu_v"]

            dgb = dyb * r["yc"]
            dyc2 = dyb * r["gb"]
            dcv0 = add(dcv0, jnp.sum(dyc2 * r["h2"], axis=0, keepdims=True))
            dcv1 = add(dcv1, jnp.sum(dyc2 * r["h1"], axis=0, keepdims=True))
            dcv2 = add(dcv2, jnp.sum(dyc2 * r["hc"], axis=0, keepdims=True))
            ext = jnp.concatenate([dyc2, cy], axis=0)
            dhc = r["w2"] * dyc2 + r["w1"] * _shift_up(ext, 1) + r["w0"] * _shift_up(ext, 2)
            cy = dyc2[0:HALO, :]
            dgc = dhc * r["zb"]
            dzb = dhc * r["gc"]

            dscale = add(dscale, jnp.sum(dyc * r["pm"], axis=0, keepdims=True))
            dpm = (dyc * r["scale"]).astype(BF16)
            dbd_ref[...] += _mm_tn(r["pooledb"], dpm)
            dpooled = _mm_nt(dpm, bd[...])
            q = dpooled * r["inv"]
            ext = jnp.concatenate([q, cq], axis=0)
            n = run + HALO
            r2 = ext + pltpu.roll(ext, n - 1, 0)
            r4 = r2 + pltpu.roll(r2, n - 2, 0)
            r8 = r4 + pltpu.roll(r4, n - 4, 0)
            r16 = r8 + pltpu.roll(r8, n - 8, 0)
            dzc = _pool_select(r2, r4, r8, r16)[0:run, :] - dpooled
            cq = q[0:HALO, :]
            dproj_parts[c] = jnp.concatenate([du, dv, dzb, dgb, dgc, dzc], axis=1).astype(BF16)

        carry_yc[...] = cy
        carry_q[...] = cq
        dbm_acc[...] += dbm
        dlng_ref[...] += dlng
        dlnb_ref[...] += dlnb
        dscale_ref[...] += dscale
        dconv_ref[0:1, :] += dcv0
        dconv_ref[1:2, :] += dcv1
        dconv_ref[2:3, :] += dcv2
        dwout_acc[...] += _mm_tn(jnp.concatenate(ycat_parts, axis=0), dx1b)
        dproj = jnp.concatenate(dproj_parts, axis=0)
        dwin_acc[...] += _mm_tn(dproj, hb16)
        dh = _mm(dproj, win_s[...])
        dx, dg = _rms_bwd(dh, g, xhat, rstd)
        dx_ref[...] = dx1v + dx
        dg_ref[...] += dg

        @pl.when(i == nt - 1)
        def _():
            _stage_bf16(dwin_acc, win_s)
            pltpu.sync_copy(win_s, dwin_hbm)
            _stage_bf16(dwout_acc, wout_s)
            pltpu.sync_copy(wout_s, dwout_hbm)
            for j in range(3):
                dwcat_ref[j] = dwcat_ref[j] * tril_ref[...]
            acc = dbm_acc[...]
            hi = acc.astype(BF16)
            lo = (acc - hi.astype(F32)).astype(BF16)
            dsb_ref[...] = _mm(hi, sel_ref[...]) + _mm(lo, sel_ref[...])

    return pl.pallas_call(
        body, name=f"mix_bwd_{layer}", grid=(nt,),
        in_specs=[pl.BlockSpec((tm, D), lambda i: (rev(i), 0)), pl.BlockSpec((tm, D), lambda i: (rev(i), 0)),
                  pl.BlockSpec((tm, D_IN), lambda i: (rev(i), 0)),
                  pl.BlockSpec((HALO, D_IN), lambda i: (jnp.maximum(rev(i) * hb - 1, 0), 0)),
                  _full((1, D))] + _mix_param_specs()
                 + [_full((3, CHUNK, 2 * CHUNK)), _full((CHUNK, 2 * CHUNK)), _full((D_A, CHUNK)), HBM, HBM],
        out_specs=[pl.BlockSpec((tm, D), lambda i: (rev(i), 0)), _full((1, D)), _full((3, CHUNK, 2 * CHUNK)),
                   _full((CHUNK, CHUNK)), _full((1, D_A)), _full((1, D_A)), _full((3, D_B)), _full((D_C, D_C)),
                   _full((1, D_C)), HBM, HBM],
        out_shape=[jax.ShapeDtypeStruct((t, D), F32), jax.ShapeDtypeStruct((1, D), F32),
                   jax.ShapeDtypeStruct((3, CHUNK, 2 * CHUNK), F32), jax.ShapeDtypeStruct((CHUNK, CHUNK), F32),
                   jax.ShapeDtypeStruct((1, D_A), F32), jax.ShapeDtypeStruct((1, D_A), F32),
                   jax.ShapeDtypeStruct((3, D_B), F32), jax.ShapeDtypeStruct((D_C, D_C), F32),
                   jax.ShapeDtypeStruct((1, D_C), F32), pltpu.HBM((D_IN, D), BF16), pltpu.HBM((D, D), BF16)],
        scratch_shapes=[pltpu.VMEM((D_IN, D), BF16), pltpu.VMEM((D, D), BF16),
                        pltpu.VMEM((D_IN, D), F32), pltpu.VMEM((D, D), F32), pltpu.VMEM((CHUNK, D_A), F32),
                        pltpu.VMEM((HALO, D_B), F32), pltpu.VMEM((HALO, D_C), F32), pltpu.SemaphoreType.DMA((2,))],
        compiler_params=_params(),
    )(dx1, x, proj, proj, g_mix, *[mp[k] for k in _MIX_PARAM_NAMES], wtcat, trilcat, headsel, _in_hbm(w_in_t), _in_hbm(w_out))


def _coords():
    return lax.axis_index("x"), lax.axis_index("y"), lax.axis_index("c")


EFFECT = pltpu.SideEffectType.DATAFLOW_SIDE_EFFECTING


def _peer(k):
    x, y, c = _coords()
    px, py, pc = x ^ (k >> 2), y ^ ((k >> 1) & 1), c ^ (k & 1)
    return (px, py, pc), 4 * px + 2 * py + pc


def _landing_shape(shape, mode):
    if mode == "block":
        return (N_DEV,) + shape
    if mode == "slot":
        return shape
    if mode == "cols_in":
        return (shape[0], N_DEV * shape[1])
    return (N_DEV, shape[0], shape[1] // N_DEV)


def _pieces(src, land, mode, src_idx, land_idx):
    if mode == "block":
        return src, land.at[land_idx]
    if mode == "slot":
        return src.at[src_idx], land.at[land_idx]
    if mode == "cols_in":
        cw = src.shape[1]
        return src, land.at[:, pl.ds(pl.multiple_of(land_idx * cw, 128), cw)]
    cw = land.shape[2]
    return src.at[:, pl.ds(pl.multiple_of(src_idx * cw, 128), cw)], land.at[land_idx]


def _exchange_copy(src, land, mode, send_sem, recv_sem, ai, k, starting):
    x, y, c = _coords()
    peer, pidx = _peer(k)
    s, d = _pieces(src, land, mode, pidx, 4 * x + 2 * y + c if starting else pidx)
    i = ai * (N_DEV - 1) + k - 1
    return pltpu.make_async_remote_copy(src_ref=s, dst_ref=d, send_sem=send_sem.at[i], recv_sem=recv_sem.at[i],
                                        device_id=peer, device_id_type=MESH)


def _own_copy(src, land, mode, local_sem, ai):
    x, y, c = _coords()
    me = 4 * x + 2 * y + c
    s, d = _pieces(src, land, mode, me, me)
    return pltpu.make_async_copy(s, d, local_sem.at[ai])


def _item_src(ins, item):
    a, sub = item
    return ins[a] if sub is None else ins[a].at[sub]


def _exchange_start(srcs, items, modes, groups, name):
    n, ni, ng = len(srcs), len(items), len(groups)
    shapes = [srcs[a].shape if sub is None else srcs[a].shape[1:] for a, sub in items]
    land_shapes = [pltpu.HBM(_landing_shape(sh, m), srcs[a].dtype) for sh, m, (a, _) in zip(shapes, modes, items)]

    def body(*refs):
        ins = refs[:n]
        sems = refs[n:n + 3 * ng]
        land_refs = refs[n + 3 * ng:n + 3 * ng + ni]
        token = refs[-1]
        for g, idxs in enumerate(groups):
            for ai, it in enumerate(idxs):
                src = _item_src(ins, items[it])
                _own_copy(src, land_refs[it], modes[it], sems[3 * g + 2], ai).start()
                for k in range(1, N_DEV):
                    _exchange_copy(src, land_refs[it], modes[it], sems[3 * g], sems[3 * g + 1], ai, k, True).start()
        token[...] = jnp.zeros_like(token)

    sem_shapes = []
    for idxs in groups:
        sem_shapes += [pltpu.SemaphoreType.DMA((len(idxs) * (N_DEV - 1),))] * 2 + [pltpu.SemaphoreType.DMA((len(idxs),))]
    out = pl.pallas_call(
        body, name=name,
        out_shape=tuple(sem_shapes) + tuple(land_shapes) + (jax.ShapeDtypeStruct((8, 128), F32),),
        in_specs=[HBM] * n,
        out_specs=tuple([SEM] * (3 * ng) + [HBM] * ni + [pl.BlockSpec(memory_space=pltpu.VMEM)]),
        compiler_params=pltpu.CompilerParams(has_side_effects=EFFECT),
    )(*[pltpu.with_memory_space_constraint(s, pltpu.HBM) for s in srcs])
    sems = [tuple(out[3 * g:3 * g + 3]) for g in range(ng)]
    return sems, list(out[3 * ng:3 * ng + ni]), out[-1]


def _exchange_wait(sems, srcs, items, lands, modes, groups, after, name):
    n, ni, ng = len(srcs), len(items), len(groups)

    def body(*refs):
        ins, land_refs = refs[:n], refs[n:n + ni]
        sem_refs = refs[n + ni:n + ni + 3 * ng]
        for g, idxs in enumerate(groups):
            for ai, it in enumerate(idxs):
                src = _item_src(ins, items[it])
                _own_copy(src, land_refs[it], modes[it], sem_refs[3 * g + 2], ai).wait()
                for k in range(1, N_DEV):
                    cp = _exchange_copy(src, land_refs[it], modes[it], sem_refs[3 * g], sem_refs[3 * g + 1], ai, k, False)
                    cp.wait_send()
                    cp.wait_recv()

    flat_sems = [s for trio in sems for s in trio]
    afters = list(after) if isinstance(after, (list, tuple)) else [after]
    out = pl.pallas_call(
        body, name=name,
        out_shape=tuple(pltpu.HBM(l.shape, l.dtype) for l in lands),
        in_specs=[HBM] * (n + ni) + [SEM] * (3 * ng) + [ANY] * len(afters),
        out_specs=tuple([HBM] * ni),
        input_output_aliases={n + i: i for i in range(ni)},
        compiler_params=pltpu.CompilerParams(has_side_effects=EFFECT),
    )(*srcs, *lands, *flat_sems, *afters)
    return list(out)


def _small_all_reduce_call(buf):
    rows = buf.shape[0]

    def body(in_ref, out_ref, pair_ref, slots_ref, send_sems, recv_sems):
        x, y, c = _coords()
        chip = 2 * x + y
        sib = pltpu.make_async_remote_copy(src_ref=in_ref, dst_ref=pair_ref, send_sem=send_sems.at[0], recv_sem=recv_sems.at[0],
                                           device_id=(x, y, 1 - c), device_id_type=MESH)
        sib.start()
        sib.wait_recv()
        slots_ref[chip] = in_ref[...] + pair_ref[...]
        sends = []
        for k in range(1, 4):
            px, py = x ^ (k >> 1), y ^ (k & 1)
            cp = pltpu.make_async_remote_copy(src_ref=slots_ref.at[chip], dst_ref=slots_ref.at[chip],
                                              send_sem=send_sems.at[k], recv_sem=recv_sems.at[k],
                                              device_id=(px, py, c), device_id_type=MESH)
            cp.start()
            sends.append(cp)
        for cp in sends:
            cp.wait_recv()
        out_ref[...] = (slots_ref[0] + slots_ref[1]) + (slots_ref[2] + slots_ref[3])
        sib.wait_send()
        for cp in sends:
            cp.wait_send()

    vm = pl.BlockSpec(memory_space=pltpu.VMEM)
    return pl.pallas_call(
        body, name="small_grads_all_reduce",
        in_specs=[vm], out_specs=vm,
        out_shape=jax.ShapeDtypeStruct((rows, 128), F32),
        scratch_shapes=[pltpu.VMEM((rows, 128), F32), pltpu.VMEM((4, rows, 128), F32),
                        pltpu.SemaphoreType.DMA((4,)), pltpu.SemaphoreType.DMA((4,))],
        compiler_params=pltpu.CompilerParams(vmem_limit_bytes=V7X_VMEM_LIMIT),
    )(buf)


def _adamw(w, g, m, v):
    m = ADAM_B1 * m + (1.0 - ADAM_B1) * g
    v = ADAM_B2 * v + (1.0 - ADAM_B2) * (g * g)
    m_hat = m / (1.0 - ADAM_B1 ** ADAM_STEP)
    v_hat = v / (1.0 - ADAM_B2 ** ADAM_STEP)
    delta = -ADAM_LR * (m_hat / (jnp.sqrt(v_hat) + ADAM_EPS) + ADAM_WD * w)
    return delta, m, v


def _reduce_adamw_call(recvs, w, m, v, name):
    nl = len(recvs)
    _, r, c = recvs[0].shape
    rb = 256 if r % 256 == 0 else r
    nb = r // rb

    def body(*refs):
        recv_refs = refs[:nl]
        w_ref, m_ref, v_ref, g_ref, d_ref, nm_ref, nv_ref = refs[nl:]
        for l in range(nl):
            @pl.when(pl.program_id(0) == l)
            def _(l=l):
                g = recv_refs[l][0].astype(F32)
                for j in range(1, N_DEV):
                    g = g + recv_refs[l][j].astype(F32)
                delta, nm, nv = _adamw(w_ref[0], g, m_ref[0], v_ref[0])
                g_ref[0] = g
                d_ref[0] = delta
                nm_ref[0] = nm
                nv_ref[0] = nv

    def recv_spec(l):
        return pl.BlockSpec((N_DEV, rb, c), lambda lg, i: (0, jnp.where(lg == l, i, jnp.where(lg < l, 0, nb - 1)), 0))

    blk = pl.BlockSpec((1, rb, c), lambda lg, i: (lg, i, 0))
    shp = jax.ShapeDtypeStruct((nl, r, c), F32)
    return pl.pallas_call(
        body, name=name, grid=(nl, nb),
        in_specs=[recv_spec(l) for l in range(nl)] + [blk, blk, blk],
        out_specs=[blk, blk, blk, blk], out_shape=[shp, shp, shp, shp],
        compiler_params=pltpu.CompilerParams(dimension_semantics=("arbitrary", "arbitrary"), vmem_limit_bytes=V7X_VMEM_LIMIT),
    )(*recvs, w, m, v)


def _small_adamw_call(w, g, m, v):
    def body(w_ref, g_ref, m_ref, v_ref, d_ref, nm_ref, nv_ref):
        delta, nm, nv = _adamw(w_ref[...], g_ref[...], m_ref[...], v_ref[...])
        d_ref[...] = delta
        nm_ref[...] = nm
        nv_ref[...] = nv

    shp = jax.ShapeDtypeStruct(w.shape, F32)
    vm = pl.BlockSpec(memory_space=pltpu.VMEM)
    return pl.pallas_call(body, name="small_adamw", in_specs=[vm] * 4, out_specs=[vm] * 3, out_shape=[shp, shp, shp],
                          compiler_params=pltpu.CompilerParams(vmem_limit_bytes=V7X_VMEM_LIMIT))(w, g, m, v)


_GATHER_MODE = dict(w_in="block", w_out="block", w_ff1="cols_in", w_ff2="block", w_ple_gate="block", w_ple_proj="cols_in")
_SCATTER_MODE = dict(w_in="slot", w_out="slot", w_ff1="cols_out", w_ff2="slot", w_ple_gate="slot", w_ple_proj="cols_out")
_GROUP_A = ("w_in", "w_out")
_GROUP_B = ("w_ff1", "w_ff2", "w_ple_gate", "w_ple_proj")


def _gathered_full(k, landed):
    if _GATHER_MODE[k] == "cols_in":
        return landed
    n, r, c = landed.shape
    return landed.reshape(n * r, c)


def _grad_send(k, g):
    if _SCATTER_MODE[k] == "cols_out":
        return g
    r8, c = g.shape
    return g.reshape(N_DEV, r8 // N_DEV, c)


_SMALL_ORDER = ("norm_mix_g", "sgu_w", "sgu_b", "sgu_ln_g", "sgu_ln_b", "conv_w", "pool_w", "pool_scale",
                "norm_ff_g", "norm_ple_g", "final_g")


def _pack_small(d):
    pieces, layout = [], []
    for k in _SMALL_ORDER:
        flat = d[k].reshape(-1)
        n = flat.shape[0]
        pad = (-n) % 128
        pieces.append(jnp.pad(flat, (0, pad)))
        layout.append((k, d[k].shape, n, n + pad))
    flat = jnp.concatenate(pieces)
    pad = (-flat.shape[0]) % 1024
    return jnp.pad(flat, (0, pad)).reshape(-1, 128), layout


def _unpack_small(buf, layout):
    flat = buf.reshape(-1)
    out, off = {}, 0
    for k, shape, n, padded in layout:
        out[k] = flat[off:off + n].reshape(shape)
        off += padded
    return out


def kernel(x, p, norm_mix_g, w_in, sgu_w, sgu_b, sgu_ln_g, sgu_ln_b, conv_w, pool_w, pool_scale, w_out, norm_ff_g, w_ff1, w_ff2, norm_ple_g, w_ple_gate, w_ple_proj, final_g, loss_target, m_norm_mix_g, m_w_in, m_sgu_w, m_sgu_b, m_sgu_ln_g, m_sgu_ln_b, m_conv_w, m_pool_w, m_pool_scale, m_w_out, m_norm_ff_g, m_w_ff1, m_w_ff2, m_norm_ple_g, m_w_ple_gate, m_w_ple_proj, m_final_g, v_norm_mix_g, v_w_in, v_sgu_w, v_sgu_b, v_sgu_ln_g, v_sgu_ln_b, v_conv_w, v_pool_w, v_pool_scale, v_w_out, v_norm_ff_g, v_w_ff1, v_w_ff2, v_norm_ple_g, v_w_ple_gate, v_w_ple_proj, v_final_g):
    t = x.shape[1]
    xc, yc_, cc = _coords()
    me = 4 * xc + 2 * yc_ + cc
    tm = lambda want: min(want, t)

    shard_names = _GROUP_A + _GROUP_B
    swap = lambda a: jnp.transpose(a, (0, 2, 1))
    shard = dict(w_in=swap(w_in), w_out=w_out, w_ff1=w_ff1, w_ff2=w_ff2, w_ple_gate=w_ple_gate, w_ple_proj=w_ple_proj)
    conv_pad = jnp.zeros((16, 128), F32).at[0:DEPTH * 3, 0:D_B // N_DEV].set(conv_w.reshape(DEPTH * 3, D_B // N_DEV))
    ag_srcs = [shard[k].astype(BF16) for k in shard_names] + [conv_pad]
    ag_items, ag_modes, ag_groups = [], [], []
    for l in range(DEPTH):
        for names in (_GROUP_A, _GROUP_B):
            ag_groups.append(list(range(len(ag_items), len(ag_items) + len(names))))
            ag_items += [(shard_names.index(k), l) for k in names]
            ag_modes += [_GATHER_MODE[k] for k in names]
            if l == 0 and names is _GROUP_A:
                ag_groups[-1].append(len(ag_items))
                ag_items.append((len(shard_names), None))
                ag_modes.append("block")
    ag_sems, ag_lands, _ = _exchange_start(ag_srcs, ag_items, ag_modes, ag_groups, "weights_gather_start")

    def gathered(l, which, after):
        idxs = ag_groups[2 * l + which]
        landed = _exchange_wait([ag_sems[2 * l + which]], ag_srcs, [ag_items[i] for i in idxs], [ag_lands[i] for i in idxs],
                                [ag_modes[i] for i in idxs], [list(range(len(idxs)))], after, f"weights_gather_wait_{l}_{which}")
        full = {k: _gathered_full(k, got) for k, got in zip((_GROUP_A, _GROUP_B)[which], landed)}
        if l == 0 and which == 0:
            full["conv_w"] = jnp.transpose(landed[-1][:, 0:DEPTH * 3, 0:D_B // N_DEV].reshape(N_DEV, DEPTH, 3, D_B // N_DEV),
                                           (1, 2, 0, 3)).reshape(DEPTH, 3, D_B)
        return full

    idx = jnp.arange(D_A)
    pmat = ((idx[:, None] // 64) == (idx[None, :] // 64)).astype(BF16) * (1.0 / 64.0)
    pmat = pmat.astype(BF16)
    tril = jnp.tril(jnp.ones((CHUNK, CHUNK), F32))
    trilcat = jnp.concatenate([tril, tril], axis=1)
    headsel = ((idx[:, None] // 64) == jnp.arange(CHUNK)[None, :]).astype(BF16)
    row = lambda a: a.reshape(1, -1)

    def mix_params(l):
        wm = sgu_w[l] * tril[None]
        wcat = jnp.stack([jnp.concatenate([wm[2 * j], wm[2 * j + 1]], axis=1) for j in range(3)]).astype(BF16)
        wtcat = jnp.stack([jnp.concatenate([wm[2 * j].T, wm[2 * j + 1].T], axis=1) for j in range(3)]).astype(BF16)
        bmat = jnp.repeat(sgu_b[l].T, 64, axis=1)
        bd = jnp.zeros((D_C, D_C), F32)
        for gi in range(4):
            bd = bd.at[gi * 64:(gi + 1) * 64, gi * 64:(gi + 1) * 64].set(pool_w[l, gi])
        mp = dict(pmat=pmat, ln_g=row(sgu_ln_g[l]), ln_b=row(sgu_ln_b[l]), wcat=wcat, bmat=bmat, conv_w=conv_full[l],
                  bd=bd.astype(BF16), pool_scale=row(pool_scale[l]))
        return mp, wtcat

    xs = x.reshape(t, D)
    p_layers = p.reshape(DEPTH, t, D_PLE)
    saved, full_w = [], []
    conv_full = None
    for l in range(DEPTH):
        wa = gathered(l, 0, xs)
        if l == 0:
            conv_full = wa["conv_w"]
        mp, _ = mix_params(l)
        x1, proj = _mix_fwd_call(xs, row(norm_mix_g[l]), wa["w_in"], wa["w_out"], mp, tm(TM_MIX_FWD), tm(RUN_MIX_FWD), l)
        wb = gathered(l, 1, x1)
        x2, x3, r, gate = _ffn_fwd_call(x1, p_layers, row(norm_ff_g[l]), row(norm_ple_g[l]), wb["w_ff1"], wb["w_ff2"],
                                        wb["w_ple_gate"], wb["w_ple_proj"], tm(TM_FFN_FWD), l)
        saved.append((xs, proj, x1, r, x2, gate))
        full_w.append({**wa, **wb})
        xs = x3

    sq, dx, dfinal = _loss_call(xs, loss_target.reshape(t, D), row(final_g), tm(TM_LOSS))
    loss = lax.psum(jnp.sum(sq) * (0.5 / D), ("x", "y", "c"))

    small = {k: [None] * DEPTH for k in _SMALL_ORDER if k != "final_g"}
    ex = {}
    token = None

    def after_start(g):
        return g if token is None else g + token[0:1, 0:1]

    def start_exchange(l, which, grads):
        names = (_GROUP_A, _GROUP_B)[which]
        sends = [_grad_send(k, grads[k]) for k in names]
        sems, lands, tok = _exchange_start(sends, [(i, None) for i in range(len(names))], [_SCATTER_MODE[k] for k in names],
                                           [list(range(len(names)))], f"grads_exchange_start_{l}_{which}")
        ex[(l, which)] = (sems[0], sends, lands)
        return tok

    for l in reversed(range(DEPTH)):
        x0, proj, x1, r, x2, gate = saved[l]
        fw = full_w[l]
        dx2, dgple, dwg, dwp = _ple_bwd_call(dx, x2, gate, p_layers, after_start(row(norm_ple_g[l])), fw["w_ple_gate"], fw["w_ple_proj"],
                                             tm(TM_PLE_BWD), l)
        da, dw2 = _ffn_bwd_hidden_call(dx2, r, fw["w_ff2"], tm(TM_FFN_BWD), l)
        dx1, dgff, dw1 = _ffn_bwd_input_call(da, x1, dx2, row(norm_ff_g[l]), fw["w_ff1"], tm(TM_FFN_BWD), l)
        token = start_exchange(l, 1, dict(w_ff1=dw1, w_ff2=dw2, w_ple_gate=dwg, w_ple_proj=dwp))
        mp, wtcat = mix_params(l)
        (dx, dgmix, dwcat, dsb, dlng, dlnb, dconv, dbd, dscale, dwin, dwout) = _mix_bwd_call(
            dx1, x0, proj, after_start(row(norm_mix_g[l])), fw["w_in"], fw["w_out"], mp, wtcat, trilcat, headsel, tm(TM_MIX_BWD), tm(RUN_MIX_BWD), l)
        token = start_exchange(l, 0, dict(w_in=dwin, w_out=dwout))
        small["norm_mix_g"][l] = dgmix[0]
        small["sgu_w"][l] = jnp.stack([dwcat[h // 2][:, (h % 2) * CHUNK:(h % 2 + 1) * CHUNK] for h in range(6)])
        small["sgu_b"][l] = dsb[:, 0:6].T
        small["sgu_ln_g"][l], small["sgu_ln_b"][l] = dlng[0], dlnb[0]
        small["conv_w"][l] = dconv
        small["pool_w"][l] = jnp.stack([dbd[gi * 64:(gi + 1) * 64, gi * 64:(gi + 1) * 64] for gi in range(4)])
        small["pool_scale"][l] = dscale[0]
        small["norm_ff_g"][l], small["norm_ple_g"][l] = dgff[0], dgple[0]
    grad_x = dx.reshape(1, t, D)

    state = dict(w_in=(swap(w_in), swap(m_w_in), swap(v_w_in)), w_out=(w_out, m_w_out, v_w_out), w_ff1=(w_ff1, m_w_ff1, v_w_ff1),
                 w_ff2=(w_ff2, m_w_ff2, v_w_ff2), w_ple_gate=(w_ple_gate, m_w_ple_gate, v_w_ple_gate),
                 w_ple_proj=(w_ple_proj, m_w_ple_proj, v_w_ple_proj))
    res = {}

    def finish_group(which, after):
        names = (_GROUP_A, _GROUP_B)[which]
        n = len(names)
        sems = [ex[(l, which)][0] for l in range(DEPTH)]
        sends = [s_ for l in range(DEPTH) for s_ in ex[(l, which)][1]]
        lands = [a_ for l in range(DEPTH) for a_ in ex[(l, which)][2]]
        landed = _exchange_wait(sems, sends, [(i, None) for i in range(DEPTH * n)], lands, [_SCATTER_MODE[k] for k in names] * DEPTH,
                                [list(range(l * n, (l + 1) * n)) for l in range(DEPTH)], after, f"grads_exchange_wait_{which}")
        for i, k in enumerate(names):
            res[k] = _reduce_adamw_call([landed[l * n + i] for l in range(DEPTH)], *state[k], "reduce_adamw_" + k)

    finish_group(1, dx)

    small_g = {k: jnp.stack(vs) for k, vs in small.items()}
    small_g["final_g"] = dfinal[0]
    gbuf, layout = _pack_small(small_g)
    gsum = _small_all_reduce_call(gbuf)
    conv_cols = lambda a: lax.dynamic_slice_in_dim(a, me * (D_B // N_DEV), D_B // N_DEV, axis=2)
    pad_conv = lambda a: jnp.zeros((DEPTH, 3, D_B), F32).at[:, :, 0:D_B // N_DEV].set(a)
    small_w = dict(norm_mix_g=norm_mix_g, sgu_w=sgu_w, sgu_b=sgu_b, sgu_ln_g=sgu_ln_g, sgu_ln_b=sgu_ln_b, conv_w=pad_conv(conv_w),
                   pool_w=pool_w, pool_scale=pool_scale, norm_ff_g=norm_ff_g, norm_ple_g=norm_ple_g, final_g=final_g)
    small_m = dict(norm_mix_g=m_norm_mix_g, sgu_w=m_sgu_w, sgu_b=m_sgu_b, sgu_ln_g=m_sgu_ln_g, sgu_ln_b=m_sgu_ln_b,
                   conv_w=pad_conv(m_conv_w), pool_w=m_pool_w, pool_scale=m_pool_scale, norm_ff_g=m_norm_ff_g,
                   norm_ple_g=m_norm_ple_g, final_g=m_final_g)
    small_v = dict(norm_mix_g=v_norm_mix_g, sgu_w=v_sgu_w, sgu_b=v_sgu_b, sgu_ln_g=v_sgu_ln_g, sgu_ln_b=v_sgu_ln_b,
                   conv_w=pad_conv(v_conv_w), pool_w=v_pool_w, pool_scale=v_pool_scale,
                   norm_ff_g=v_norm_ff_g, norm_ple_g=v_norm_ple_g, final_g=v_final_g)
    gs = _unpack_small(gsum, layout)
    gs_local = dict(gs)
    gs_local["conv_w"] = pad_conv(conv_cols(gs["conv_w"]))
    g_loc, _ = _pack_small(gs_local)
    wbuf, _ = _pack_small(small_w)
    mbuf, _ = _pack_small(small_m)
    vbuf, _ = _pack_small(small_v)
    dbuf, nmbuf, nvbuf = _small_adamw_call(wbuf, g_loc, mbuf, vbuf)
    sd, sm, sv = _unpack_small(dbuf, layout), _unpack_small(nmbuf, layout), _unpack_small(nvbuf, layout)
    unconv = lambda a: a[:, :, 0:D_B // N_DEV]
    for dct in (gs_local, sd, sm, sv):
        dct["conv_w"] = unconv(dct["conv_w"])

    finish_group(0, [res[_GROUP_B[-1]][0], dbuf])

    order = ["norm_mix_g", "w_in", "sgu_w", "sgu_b", "sgu_ln_g", "sgu_ln_b", "conv_w", "pool_w", "pool_scale", "w_out",
             "norm_ff_g", "w_ff1", "w_ff2", "norm_ple_g", "w_ple_gate", "w_ple_proj", "final_g"]
    outs = [loss, grad_x]
    for which in range(4):
        for k in order:
            if k in res:
                outs.append(swap(res[k][which]) if k == "w_in" else res[k][which])
            else:
                outs.append((gs_local, sd, sm, sv)[which][k])
    return tuple(outs)
```

```python
import functools
import math

import jax
import jax.numpy as jnp
from jax import lax
from jax.experimental import pallas as pl
from jax.experimental.pallas import tpu as pltpu

F32 = jnp.float32
BF16 = jnp.bfloat16

D = 1024
D_IN = 2176
D_A = 384
D_B = 384
D_C = 256
D_FF = 4096
D_PLE = 256
DEPTH = 4
CHUNK = 128
HALO = 16
FF_BLK = 1024
N_DEV = 8
RMS_EPS = 1e-6
LN_EPS = 1e-5
ADAM_LR = 0.001
ADAM_B1 = 0.9
ADAM_B2 = 0.999
ADAM_EPS = 1e-08
ADAM_WD = 0.01
ADAM_STEP = 10

TM_MIX_FWD = 1024
TM_FFN_FWD = 512
TM_LOSS = 512
TM_PLE_BWD = 1024
TM_FFN_BWD = 512
TM_MIX_BWD = 512
RUN_MIX_FWD = 1024
RUN_MIX_BWD = 512
V7X_VMEM_LIMIT = 60000 * 1024

ANY = pl.BlockSpec(memory_space=pl.ANY)
HBM = pl.BlockSpec(memory_space=pltpu.HBM)
SEM = pl.BlockSpec(memory_space=pltpu.SEMAPHORE)
MESH = pl.DeviceIdType.MESH


def _params(vmem=V7X_VMEM_LIMIT):
    return pltpu.CompilerParams(dimension_semantics=("arbitrary",), vmem_limit_bytes=vmem)


def _in_hbm(a):
    return pltpu.with_memory_space_constraint(a, pltpu.HBM)


def _full(shape):
    nd = len(shape)
    return pl.BlockSpec(shape, lambda i: (0,) * nd)


def _rows(tm, cols):
    return pl.BlockSpec((tm, cols), lambda i: (i, 0))


def _layer_rows(layer, tm, cols):
    return pl.BlockSpec((None, tm, cols), lambda i: (layer, i, 0))


def _mm(a, b):
    return jnp.dot(a, b, preferred_element_type=F32)


def _mm_nt(a, b):
    return lax.dot_general(a, b, (((1,), (1,)), ((), ())), preferred_element_type=F32)


def _mm_tn(a, b):
    return lax.dot_general(a, b, (((0,), (0,)), ((), ())), preferred_element_type=F32)


def _erf(x):
    ax = jnp.abs(x)
    t = 1.0 / (1.0 + 0.3275911 * ax)
    poly = t * (0.254829592 + t * (-0.284496736 + t * (1.421413741 + t * (-1.453152027 + t * 1.061405429))))
    y = 1.0 - poly * jnp.exp(-ax * ax)
    return jnp.where(x < 0, -y, y)


def _gelu_and_grad(x):
    cdf = 0.5 * (1.0 + _erf(x * (1.0 / math.sqrt(2.0))))
    pdf = jnp.exp(-0.5 * x * x) * (1.0 / math.sqrt(2.0 * math.pi))
    return x * cdf, cdf + x * pdf


def _rms(x, g):
    rstd = lax.rsqrt(jnp.mean(x * x, axis=-1, keepdims=True) + RMS_EPS)
    xhat = x * rstd
    return xhat * g, xhat, rstd


def _rms_bwd(dy, g, xhat, rstd):
    dg = jnp.sum(dy * xhat, axis=0, keepdims=True)
    dxh = dy * g
    dx = rstd * (dxh - xhat * jnp.mean(dxh * xhat, axis=-1, keepdims=True))
    return dx, dg


def _shift_down(ext, k):
    return pltpu.roll(ext, k, 0)[HALO:, :]


def _shift_up(ext, k):
    n = ext.shape[0]
    return pltpu.roll(ext, n - k, 0)[: n - HALO, :]


def _pool_select(s2, s4, s8, s16):
    lane = lax.broadcasted_iota(jnp.int32, s2.shape, 1)
    return jnp.where(lane < 64, s2, jnp.where(lane < 128, s4, jnp.where(lane < 192, s8, s16)))


def _pool_inv_count(tile_start, tm):
    pos = lax.broadcasted_iota(jnp.int32, (tm, D_C), 0) + tile_start + 1
    lane = lax.broadcasted_iota(jnp.int32, (tm, D_C), 1)
    win = jnp.where(lane < 64, 2, jnp.where(lane < 128, 4, jnp.where(lane < 192, 8, 16)))
    return 1.0 / jnp.minimum(pos, win).astype(F32)


def _head_halves(a):
    lane = lax.broadcasted_iota(jnp.int32, a.shape, 1)
    even = (lane & 64) == 0
    return jnp.where(even, a, 0.0).astype(BF16), jnp.where(even, 0.0, a).astype(BF16)


def _head_stack(lo, hi, j, nch):
    return jnp.concatenate(
        [jnp.concatenate([lo[c * CHUNK:(c + 1) * CHUNK, j * 128:(j + 1) * 128], hi[c * CHUNK:(c + 1) * CHUNK, j * 128:(j + 1) * 128]], axis=0)
         for c in range(nch)], axis=1)


def _chunks_to_lanes(a, j, nch):
    return jnp.concatenate([a[c * CHUNK:(c + 1) * CHUNK, j * 128:(j + 1) * 128] for c in range(nch)], axis=1)


def _lanes_to_chunks(o, nch):
    return jnp.concatenate([o[:, c * CHUNK:(c + 1) * CHUNK] for c in range(nch)], axis=0)


def _loads(pairs, sem):
    return [pltpu.make_async_copy(src, dst, sem.at[n]) for n, (src, dst) in enumerate(pairs)]


def _load_all(loads):
    for cp in loads:
        cp.start()
    for cp in loads:
        cp.wait()


def _stage_bf16(acc, stage):
    rows = acc.shape[0]
    strip = min(rows, 128)

    @pl.loop(0, rows // strip)
    def _(n):
        sl = pl.ds(pl.multiple_of(n * strip, strip), strip)
        stage[sl, :] = acc[sl, :].astype(BF16)


def _mixers_fwd(pf, halo_hc, halo_zc, tile_start, prm):
    tm = pf.shape[0]
    nch = tm // CHUNK
    u, v = pf[:, 0:D_A], pf[:, D_A:2 * D_A]
    zb, gb, gc = pf[:, 768:1152], pf[:, 1152:1536], pf[:, 1536:1920]
    zc = pf[:, 1920:2176]
    r = {}
    gu, r["dgelu_u"] = _gelu_and_grad(u)
    gv, r["dgelu_v"] = _gelu_and_grad(v)
    pmat = prm["pmat"][...]
    mu = _mm(gv.astype(BF16), pmat)
    dv = gv - mu
    var = _mm((dv * dv).astype(BF16), pmat)
    rstd = lax.rsqrt(var + LN_EPS)
    xh = dv * rstd
    vlo, vhi = _head_halves(xh * prm["ln_g"][...] + prm["ln_b"][...])
    cols, v2s = [], []
    for j in range(3):
        v2 = _head_stack(vlo, vhi, j, nch)
        v2s.append(v2)
        cols.append(_lanes_to_chunks(_mm(prm["wcat"][j], v2), nch))
    mixed = jnp.concatenate(cols, axis=1) + jnp.concatenate([prm["bmat"][...]] * nch, axis=0)
    ya = gu * mixed
    r.update(gu=gu, mixed=mixed, v2s=v2s, xh=xh, ln_rstd=rstd)
    w0, w1, w2 = prm["conv_w"][0:1, :], prm["conv_w"][1:2, :], prm["conv_w"][2:3, :]
    hc = gc * zb
    ext = jnp.concatenate([halo_hc, hc], axis=0)
    h1, h2 = _shift_down(ext, 1), _shift_down(ext, 2)
    yc = w2 * hc + w1 * h1 + w0 * h2
    yb = gb * yc
    r.update(hc=hc, h1=h1, h2=h2, yc=yc, zb=zb, gb=gb, gc=gc, w0=w0, w1=w1, w2=w2)
    ext = jnp.concatenate([halo_zc, zc], axis=0)
    s2 = ext + pltpu.roll(ext, 1, 0)
    s4 = s2 + pltpu.roll(s2, 2, 0)
    s8 = s4 + pltpu.roll(s4, 4, 0)
    s16 = s8 + pltpu.roll(s8, 8, 0)
    inv = _pool_inv_count(tile_start, tm)
    pooled = _pool_select(s2, s4, s8, s16)[HALO:, :] * inv - zc
    pooledb = pooled.astype(BF16)
    pm = _mm(pooledb, prm["bd"][...])
    scale = prm["pool_scale"][...]
    ycm = pm * scale
    r.update(inv=inv, pooledb=pooledb, pm=pm, scale=scale, zc=zc)
    r["ycat"] = jnp.concatenate([ya, yb, ycm], axis=1)
    return r


_MIX_PARAM_NAMES = ("pmat", "ln_g", "ln_b", "wcat", "bmat", "conv_w", "bd", "pool_scale")


def _mix_param_specs():
    return [_full((D_A, D_A)), _full((1, D_A)), _full((1, D_A)), _full((3, CHUNK, 2 * CHUNK)), _full((CHUNK, D_A)),
            _full((3, D_B)), _full((D_C, D_C)), _full((1, D_C))]


def _mix_fwd_call(x, g_mix, w_in_t, w_out, mp, tm, run, layer):
    t = x.shape[0]
    nt = t // tm

    def body(x_ref, g_ref, pmat, ln_g, ln_b, wcat, bmat, conv_w, bd, pool_scale, win_hbm, wout_hbm,
             x1_ref, proj_ref, win_s, wout_s, halo_hc, halo_zc, load_sem):
        i = pl.program_id(0)
        loads = _loads([(win_hbm, win_s), (wout_hbm, wout_s)], load_sem)

        @pl.when(i == 0)
        def _():
            _load_all(loads)
            halo_hc[...] = jnp.zeros_like(halo_hc)
            halo_zc[...] = jnp.zeros_like(halo_zc)

        prm = dict(pmat=pmat, ln_g=ln_g, ln_b=ln_b, wcat=wcat, bmat=bmat, conv_w=conv_w, bd=bd, pool_scale=pool_scale)
        xv = x_ref[...]
        h, _, _ = _rms(xv, g_ref[...])
        pf = _mm_nt(h.astype(BF16), win_s[...])
        proj_ref[...] = pf.astype(BF16)
        hh, hz = halo_hc[...], halo_zc[...]
        parts = []
        for c in range(tm // run):
            r = _mixers_fwd(pf[c * run:(c + 1) * run, :], hh, hz, i * tm + c * run, prm)
            hh, hz = r["hc"][run - HALO:, :], r["zc"][run - HALO:, :]
            parts.append(r["ycat"].astype(BF16))
        halo_hc[...] = hh
        halo_zc[...] = hz
        x1_ref[...] = xv + _mm(jnp.concatenate(parts, axis=0), wout_s[...])

    return pl.pallas_call(
        body, name=f"mix_fwd_{layer}", grid=(nt,),
        in_specs=[_rows(tm, D), _full((1, D))] + _mix_param_specs() + [HBM, HBM],
        out_specs=[_rows(tm, D), _rows(tm, D_IN)],
        out_shape=[jax.ShapeDtypeStruct((t, D), F32), jax.ShapeDtypeStruct((t, D_IN), BF16)],
        scratch_shapes=[pltpu.VMEM((D_IN, D), BF16), pltpu.VMEM((D, D), BF16),
                        pltpu.VMEM((HALO, D_B), F32), pltpu.VMEM((HALO, D_C), F32), pltpu.SemaphoreType.DMA((2,))],
        compiler_params=_params(),
    )(x, g_mix, *[mp[k] for k in _MIX_PARAM_NAMES], _in_hbm(w_in_t), _in_hbm(w_out))


def _ffn_fwd_call(x1, p, g_ff, g_ple, w1, w2, wg, wp, tm, layer):
    t = x1.shape[0]
    nt = t // tm

    def body(x1_ref, p_ref, gff_ref, gple_ref, w1_hbm, w2_hbm, wg_hbm, wp_hbm,
             x2_ref, x3_ref, r_ref, gate_ref, w1_s, w2_s, wg_s, wp_s, load_sem):
        i = pl.program_id(0)
        loads = _loads([(w1_hbm, w1_s), (w2_hbm, w2_s), (wg_hbm, wg_s), (wp_hbm, wp_s)], load_sem)

        @pl.when(i == 0)
        def _():
            _load_all(loads)

        x1v = x1_ref[...]
        h2, _, _ = _rms(x1v, gff_ref[...])
        h2b = h2.astype(BF16)
        acc = x1v
        for j in range(D_FF // FF_BLK):
            blk = slice(j * FF_BLK, (j + 1) * FF_BLK)
            rj = jnp.maximum(_mm(h2b, w1_s[:, blk]), 0.0)
            r_ref[:, blk] = rj.astype(BF16)
            acc = acc + _mm((rj * rj).astype(BF16), w2_s[blk, :])
        x2_ref[...] = acc
        n3, _, _ = _rms(acc, gple_ref[...])
        gate = jax.nn.sigmoid(_mm(n3.astype(BF16), wg_s[...]))
        gate_ref[...] = gate.astype(BF16)
        pp = _mm(p_ref[...].astype(BF16), wp_s[...])
        x3_ref[...] = acc + pp * gate

    return pl.pallas_call(
        body, name=f"ffn_fwd_{layer}", grid=(nt,),
        in_specs=[_rows(tm, D), _layer_rows(layer, tm, D_PLE), _full((1, D)), _full((1, D)), HBM, HBM, HBM, HBM],
        out_specs=[_rows(tm, D), _rows(tm, D), _rows(tm, D_FF), _rows(tm, D)],
        out_shape=[jax.ShapeDtypeStruct((t, D), F32), jax.ShapeDtypeStruct((t, D), F32),
                   jax.ShapeDtypeStruct((t, D_FF), BF16), jax.ShapeDtypeStruct((t, D), BF16)],
        scratch_shapes=[pltpu.VMEM((D, D_FF), BF16), pltpu.VMEM((D_FF, D), BF16),
                        pltpu.VMEM((D, D), BF16), pltpu.VMEM((D_PLE, D), BF16), pltpu.SemaphoreType.DMA((4,))],
        compiler_params=_params(),
    )(x1, p, g_ff, g_ple, _in_hbm(w1), _in_hbm(w2), _in_hbm(wg), _in_hbm(wp))


def _loss_call(xl, target, final_g, tm):
    t = xl.shape[0]
    nt = t // tm

    def body(x_ref, t_ref, g_ref, sq_ref, dx_ref, dg_ref):
        i = pl.program_id(0)

        @pl.when(i == 0)
        def _():
            sq_ref[...] = jnp.zeros_like(sq_ref)
            dg_ref[...] = jnp.zeros_like(dg_ref)

        g = g_ref[...]
        y, xhat, rstd = _rms(x_ref[...], g)
        err = y - t_ref[...]
        sq_ref[...] += jnp.sum(err * err, axis=0, keepdims=True)
        dx, dg = _rms_bwd(err * (1.0 / D), g, xhat, rstd)
        dx_ref[...] = dx
        dg_ref[...] += dg

    return pl.pallas_call(
        body, name="loss_head", grid=(nt,),
        in_specs=[_rows(tm, D), _rows(tm, D), _full((1, D))],
        out_specs=[_full((1, D)), _rows(tm, D), _full((1, D))],
        out_shape=[jax.ShapeDtypeStruct((1, D), F32), jax.ShapeDtypeStruct((t, D), F32), jax.ShapeDtypeStruct((1, D), F32)],
        compiler_params=_params(),
    )(xl, target, final_g)


def _ple_bwd_call(dx3, x2, gate, p, g_ple, wg, wp, tm, layer):
    t = dx3.shape[0]
    nt = t // tm

    def body(dx3_ref, x2_ref, gate_ref, p_ref, g_ref, wg_hbm, wp_hbm,
             dx2_ref, dg_ref, dwg_hbm, dwp_hbm, wg_s, wp_s, dwg_acc, dwp_acc, load_sem):
        i = pl.program_id(0)
        loads = _loads([(wp_hbm, wp_s), (wg_hbm, wg_s)], load_sem)

        @pl.when(i == 0)
        def _():
            _load_all(loads)
            dwg_acc[...] = jnp.zeros_like(dwg_acc)
            dwp_acc[...] = jnp.zeros_like(dwp_acc)
            dg_ref[...] = jnp.zeros_like(dg_ref)

        g = g_ref[...]
        dx3v = dx3_ref[...]
        gatev = gate_ref[...].astype(F32)
        pb = p_ref[...].astype(BF16)
        pp = _mm(pb, wp_s[...])
        dwp_acc[...] += _mm_tn(pb, (dx3v * gatev).astype(BF16))
        dgpre = (dx3v * pp * gatev * (1.0 - gatev)).astype(BF16)
        n3, xhat, rstd = _rms(x2_ref[...], g)
        dwg_acc[...] += _mm_tn(n3.astype(BF16), dgpre)
        dn3 = _mm_nt(dgpre, wg_s[...])
        dx, dg = _rms_bwd(dn3, g, xhat, rstd)
        dx2_ref[...] = dx3v + dx
        dg_ref[...] += dg

        @pl.when(i == nt - 1)
        def _():
            _stage_bf16(dwg_acc, wg_s)
            _stage_bf16(dwp_acc, wp_s)
            pltpu.sync_copy(wg_s, dwg_hbm)
            pltpu.sync_copy(wp_s, dwp_hbm)

    return pl.pallas_call(
        body, name=f"ple_bwd_{layer}", grid=(nt,),
        in_specs=[_rows(tm, D), _rows(tm, D), _rows(tm, D), _layer_rows(layer, tm, D_PLE), _full((1, D)), HBM, HBM],
        out_specs=[_rows(tm, D), _full((1, D)), HBM, HBM],
        out_shape=[jax.ShapeDtypeStruct((t, D), F32), jax.ShapeDtypeStruct((1, D), F32),
                   pltpu.HBM((D, D), BF16), pltpu.HBM((D_PLE, D), BF16)],
        scratch_shapes=[pltpu.VMEM((D, D), BF16), pltpu.VMEM((D_PLE, D), BF16),
                        pltpu.VMEM((D, D), F32), pltpu.VMEM((D_PLE, D), F32), pltpu.SemaphoreType.DMA((2,))],
        compiler_params=_params(),
    )(dx3, x2, gate, p, g_ple, _in_hbm(wg), _in_hbm(wp))


def _ffn_bwd_hidden_call(dx2, r, w2, tm, layer):
    t = dx2.shape[0]
    nt = t // tm

    def body(dx2_ref, r_ref, w2_hbm, da_ref, dw2_hbm, w2_s, dw2_acc, load_sem):
        i = pl.program_id(0)
        loads = _loads([(w2_hbm, w2_s)], load_sem)

        @pl.when(i == 0)
        def _():
            _load_all(loads)
            dw2_acc[...] = jnp.zeros_like(dw2_acc)

        dxb = dx2_ref[...].astype(BF16)
        for j in range(D_FF // FF_BLK):
            blk = slice(j * FF_BLK, (j + 1) * FF_BLK)
            rj = r_ref[:, blk].astype(F32)
            ds = _mm_nt(dxb, w2_s[blk, :])
            da_ref[:, blk] = (2.0 * rj * ds).astype(BF16)
            dw2_acc[blk, :] += _mm_tn((rj * rj).astype(BF16), dxb)

        @pl.when(i == nt - 1)
        def _():
            _stage_bf16(dw2_acc, w2_s)
            pltpu.sync_copy(w2_s, dw2_hbm)

    return pl.pallas_call(
        body, name=f"ffn_bwd_hidden_{layer}", grid=(nt,),
        in_specs=[_rows(tm, D), _rows(tm, D_FF), HBM],
        out_specs=[_rows(tm, D_FF), HBM],
        out_shape=[jax.ShapeDtypeStruct((t, D_FF), BF16), pltpu.HBM((D_FF, D), BF16)],
        scratch_shapes=[pltpu.VMEM((D_FF, D), BF16), pltpu.VMEM((D_FF, D), F32), pltpu.SemaphoreType.DMA((1,))],
        compiler_params=_params(),
    )(dx2, r, _in_hbm(w2))


def _ffn_bwd_input_call(da, x1, dx2, g_ff, w1, tm, layer):
    t = dx2.shape[0]
    nt = t // tm

    def body(da_ref, x1_ref, dx2_ref, g_ref, w1_hbm, dx1_ref, dg_ref, dw1_hbm, w1_s, dw1_acc, load_sem):
        i = pl.program_id(0)
        loads = _loads([(w1_hbm, w1_s)], load_sem)

        @pl.when(i == 0)
        def _():
            _load_all(loads)
            dw1_acc[...] = jnp.zeros_like(dw1_acc)
            dg_ref[...] = jnp.zeros_like(dg_ref)

        g = g_ref[...]
        h2, xhat, rstd = _rms(x1_ref[...], g)
        h2b = h2.astype(BF16)
        dh2 = jnp.zeros((tm, D), F32)
        for j in range(D_FF // FF_BLK):
            blk = slice(j * FF_BLK, (j + 1) * FF_BLK)
            daj = da_ref[:, blk]
            dh2 = dh2 + _mm_nt(daj, w1_s[:, blk])
            dw1_acc[:, blk] += _mm_tn(h2b, daj)
        dx, dg = _rms_bwd(dh2, g, xhat, rstd)
        dx1_ref[...] = dx2_ref[...] + dx
        dg_ref[...] += dg

        @pl.when(i == nt - 1)
        def _():
            _stage_bf16(dw1_acc, w1_s)
            pltpu.sync_copy(w1_s, dw1_hbm)

    return pl.pallas_call(
        body, name=f"ffn_bwd_input_{layer}", grid=(nt,),
        in_specs=[_rows(tm, D_FF), _rows(tm, D), _rows(tm, D), _full((1, D)), HBM],
        out_specs=[_rows(tm, D), _full((1, D)), HBM],
        out_shape=[jax.ShapeDtypeStruct((t, D), F32), jax.ShapeDtypeStruct((1, D), F32), pltpu.HBM((D, D_FF), BF16)],
        scratch_shapes=[pltpu.VMEM((D, D_FF), BF16), pltpu.VMEM((D, D_FF), F32), pltpu.SemaphoreType.DMA((1,))],
        compiler_params=_params(),
    )(da, x1, dx2, g_ff, _in_hbm(w1))


def _mix_bwd_call(dx1, x, proj, g_mix, w_in_t, w_out, mp, wtcat, trilcat, headsel, tm, run, layer):
    t = dx1.shape[0]
    nt = t // tm
    nrun = tm // run
    nch = run // CHUNK
    hb = tm // HALO

    def rev(i):
        return nt - 1 - i

    def body(dx1_ref, x_ref, proj_ref, halo_ref, g_ref, pmat, ln_g, ln_b, wcat, bmat, conv_w, bd, pool_scale,
             wtcat_ref, tril_ref, sel_ref, win_hbm, wout_hbm,
             dx_ref, dg_ref, dwcat_ref, dsb_ref, dlng_ref, dlnb_ref, dconv_ref, dbd_ref, dscale_ref, dwin_hbm, dwout_hbm,
             win_s, wout_s, dwin_acc, dwout_acc, dbm_acc, carry_yc, carry_q, late_ycat, late_dx1, late_dproj, late_h, load_sem):
        i = pl.program_id(0)
        ri = nt - 1 - i
        loads = _loads([(wout_hbm, wout_s), (win_hbm, win_s)], load_sem)

        @pl.when(i == 0)
        def _():
            _load_all(loads)
            for ref in (dwin_acc, dwout_acc, dbm_acc, carry_yc, carry_q, dg_ref, dwcat_ref, dlng_ref, dlnb_ref,
                        dconv_ref, dbd_ref, dscale_ref, late_ycat, late_dx1, late_dproj, late_h):
                ref[...] = jnp.zeros_like(ref)

        def late_dwout():
            dwout_acc[...] += _mm_tn(late_ycat[...], late_dx1[...])

        def late_dwin():
            dwin_acc[...] += _mm_tn(late_dproj[...], late_h[...])

        late_dwout()
        late_dwin()

        prm = dict(pmat=pmat, ln_g=ln_g, ln_b=ln_b, wcat=wcat, bmat=bmat, conv_w=conv_w, bd=bd, pool_scale=pool_scale)
        g = g_ref[...]
        h, xhat, rstd = _rms(x_ref[...], g)
        hb16 = h.astype(BF16)
        dx1v = dx1_ref[...]
        dx1b = dx1v.astype(BF16)
        dycat = _mm_nt(dx1b, wout_s[...])
        lng = ln_g[...]
        pm_ = pmat[...]
        cy, cq = carry_yc[...], carry_q[...]
        ycat_parts, dproj_parts = [None] * nrun, [None] * nrun
        dbm = dlng = dlnb = dscale = dcv0 = dcv1 = dcv2 = None
        add = lambda tot, v: v if tot is None else tot + v
        for c in reversed(range(nrun)):
            rows = slice(c * run, (c + 1) * run)
            pf = proj_ref[rows, :].astype(F32)
            if c > 0:
                ph = proj_ref[c * run - HALO:c * run, :].astype(F32)
            else:
                ph = halo_ref[...].astype(F32) * (ri > 0).astype(F32)
            r = _mixers_fwd(pf, ph[:, 1536:1920] * ph[:, 768:1152], ph[:, 1920:2176], ri * tm + c * run, prm)
            ycat_parts[c] = r["ycat"].astype(BF16)
            dya, dyb, dyc = dycat[rows, 0:D_A], dycat[rows, D_A:D_A + D_B], dycat[rows, D_A + D_B:D]

            dgu = dya * r["mixed"]
            dmix = dya * r["gu"]
            dmix_b = dmix.astype(BF16)
            dlo, dhi = _head_halves(dmix)
            for k in range(nch):
                dbm = add(dbm, dmix[k * CHUNK:(k + 1) * CHUNK, :])
            dvn_cols = []
            for j in range(3):
                dwcat_ref[j] += _mm_nt(_chunks_to_lanes(dmix_b, j, nch), r["v2s"][j])
                dvn_cols.append(_lanes_to_chunks(_mm(wtcat_ref[j], _head_stack(dlo, dhi, j, nch)), nch))
            dvn = jnp.concatenate(dvn_cols, axis=1)
            xh = r["xh"]
            dlng = add(dlng, jnp.sum(dvn * xh, axis=0, keepdims=True))
            dlnb = add(dlnb, jnp.sum(dvn, axis=0, keepdims=True))
            dxh = dvn * lng
            m1 = _mm(dxh.astype(BF16), pm_)
            m2 = _mm((dxh * xh).astype(BF16), pm_)
            dgv = r["ln_rstd"] * (dxh - m1 - xh * m2)
            du = dgu * r["dgelu_u"]
            dv = dgv * r["dgelu_v"]

            dgb = dyb * r["yc"]
            dyc2 = dyb * r["gb"]
            dcv0 = add(dcv0, jnp.sum(dyc2 * r["h2"], axis=0, keepdims=True))
            dcv1 = add(dcv1, jnp.sum(dyc2 * r["h1"], axis=0, keepdims=True))
            dcv2 = add(dcv2, jnp.sum(dyc2 * r["hc"], axis=0, keepdims=True))
            ext = jnp.concatenate([dyc2, cy], axis=0)
            dhc = r["w2"] * dyc2 + r["w1"] * _shift_up(ext, 1) + r["w0"] * _shift_up(ext, 2)
            cy = dyc2[0:HALO, :]
            dgc = dhc * r["zb"]
            dzb = dhc * r["gc"]

            dscale = add(dscale, jnp.sum(dyc * r["pm"], axis=0, keepdims=True))
            dpm = (dyc * r["scale"]).astype(BF16)
            dbd_ref[...] += _mm_tn(r["pooledb"], dpm)
            dpooled = _mm_nt(dpm, bd[...])
            q = dpooled * r["inv"]
            ext = jnp.concatenate([q, cq], axis=0)
            n = run + HALO
            r2 = ext + pltpu.roll(ext, n - 1, 0)
            r4 = r2 + pltpu.roll(r2, n - 2, 0)
            r8 = r4 + pltpu.roll(r4, n - 4, 0)
            r16 = r8 + pltpu.roll(r8, n - 8, 0)
            dzc = _pool_select(r2, r4, r8, r16)[0:run, :] - dpooled
            cq = q[0:HALO, :]
            dproj_parts[c] = jnp.concatenate([du, dv, dzb, dgb, dgc, dzc], axis=1).astype(BF16)

        carry_yc[...] = cy
        carry_q[...] = cq
        dbm_acc[...] += dbm
        dlng_ref[...] += dlng
        dlnb_ref[...] += dlnb
        dscale_ref[...] += dscale
        dconv_ref[0:1, :] += dcv0
        dconv_ref[1:2, :] += dcv1
        dconv_ref[2:3, :] += dcv2
        dproj = jnp.concatenate(dproj_parts, axis=0)
        dh = _mm(dproj, win_s[...])
        dx, dg = _rms_bwd(dh, g, xhat, rstd)
        dx_ref[...] = dx1v + dx
        dg_ref[...] += dg
        late_ycat[...] = jnp.concatenate(ycat_parts, axis=0)
        late_dx1[...] = dx1b
        late_dproj[...] = dproj
        late_h[...] = hb16

        @pl.when(i == nt - 1)
        def _():
            late_dwout()
            late_dwin()
            _stage_bf16(dwin_acc, win_s)
            pltpu.sync_copy(win_s, dwin_hbm)
            _stage_bf16(dwout_acc, wout_s)
            pltpu.sync_copy(wout_s, dwout_hbm)
            for j in range(3):
                dwcat_ref[j] = dwcat_ref[j] * tril_ref[...]
            acc = dbm_acc[...]
            hi = acc.astype(BF16)
            lo = (acc - hi.astype(F32)).astype(BF16)
            dsb_ref[...] = _mm(hi, sel_ref[...]) + _mm(lo, sel_ref[...])

    return pl.pallas_call(
        body, name=f"mix_bwd_{layer}", grid=(nt,),
        in_specs=[pl.BlockSpec((tm, D), lambda i: (rev(i), 0)), pl.BlockSpec((tm, D), lambda i: (rev(i), 0)),
                  pl.BlockSpec((tm, D_IN), lambda i: (rev(i), 0)),
                  pl.BlockSpec((HALO, D_IN), lambda i: (jnp.maximum(rev(i) * hb - 1, 0), 0)),
                  _full((1, D))] + _mix_param_specs()
                 + [_full((3, CHUNK, 2 * CHUNK)), _full((CHUNK, 2 * CHUNK)), _full((D_A, CHUNK)), HBM, HBM],
        out_specs=[pl.BlockSpec((tm, D), lambda i: (rev(i), 0)), _full((1, D)), _full((3, CHUNK, 2 * CHUNK)),
                   _full((CHUNK, CHUNK)), _full((1, D_A)), _full((1, D_A)), _full((3, D_B)), _full((D_C, D_C)),
                   _full((1, D_C)), HBM, HBM],
        out_shape=[jax.ShapeDtypeStruct((t, D), F32), jax.ShapeDtypeStruct((1, D), F32),
                   jax.ShapeDtypeStruct((3, CHUNK, 2 * CHUNK), F32), jax.ShapeDtypeStruct((CHUNK, CHUNK), F32),
                   jax.ShapeDtypeStruct((1, D_A), F32), jax.ShapeDtypeStruct((1, D_A), F32),
                   jax.ShapeDtypeStruct((3, D_B), F32), jax.ShapeDtypeStruct((D_C, D_C), F32),
                   jax.ShapeDtypeStruct((1, D_C), F32), pltpu.HBM((D_IN, D), BF16), pltpu.HBM((D, D), BF16)],
        scratch_shapes=[pltpu.VMEM((D_IN, D), BF16), pltpu.VMEM((D, D), BF16),
                        pltpu.VMEM((D_IN, D), F32), pltpu.VMEM((D, D), F32), pltpu.VMEM((CHUNK, D_A), F32),
                        pltpu.VMEM((HALO, D_B), F32), pltpu.VMEM((HALO, D_C), F32),
                        pltpu.VMEM((tm, D), BF16), pltpu.VMEM((tm, D), BF16), pltpu.VMEM((tm, D_IN), BF16), pltpu.VMEM((tm, D), BF16),
                        pltpu.SemaphoreType.DMA((2,))],
        compiler_params=_params(),
    )(dx1, x, proj, proj, g_mix, *[mp[k] for k in _MIX_PARAM_NAMES], wtcat, trilcat, headsel, _in_hbm(w_in_t), _in_hbm(w_out))


def _coords():
    return lax.axis_index("x"), lax.axis_index("y"), lax.axis_index("c")


EFFECT = pltpu.SideEffectType.DATAFLOW_SIDE_EFFECTING


def _peer(k):
    x, y, c = _coords()
    px, py, pc = x ^ (k >> 2), y ^ ((k >> 1) & 1), c ^ (k & 1)
    return (px, py, pc), 4 * px + 2 * py + pc


def _landing_shape(shape, mode):
    if mode == "block":
        return (N_DEV,) + shape
    if mode == "slot":
        return shape
    if mode == "cols_in":
        return (shape[0], N_DEV * shape[1])
    return (N_DEV, shape[0], shape[1] // N_DEV)


def _pieces(src, land, mode, src_idx, land_idx):
    if mode == "block":
        return src, land.at[land_idx]
    if mode == "slot":
        return src.at[src_idx], land.at[land_idx]
    if mode == "cols_in":
        cw = src.shape[1]
        return src, land.at[:, pl.ds(pl.multiple_of(land_idx * cw, 128), cw)]
    cw = land.shape[2]
    return src.at[:, pl.ds(pl.multiple_of(src_idx * cw, 128), cw)], land.at[land_idx]


def _exchange_copy(src, land, mode, send_sem, recv_sem, ai, k, starting):
    x, y, c = _coords()
    peer, pidx = _peer(k)
    s, d = _pieces(src, land, mode, pidx, 4 * x + 2 * y + c if starting else pidx)
    i = ai * (N_DEV - 1) + k - 1
    return pltpu.make_async_remote_copy(src_ref=s, dst_ref=d, send_sem=send_sem.at[i], recv_sem=recv_sem.at[i],
                                        device_id=peer, device_id_type=MESH)


def _own_copy(src, land, mode, local_sem, ai):
    x, y, c = _coords()
    me = 4 * x + 2 * y + c
    s, d = _pieces(src, land, mode, me, me)
    return pltpu.make_async_copy(s, d, local_sem.at[ai])


def _item_src(ins, item):
    a, sub = item
    return ins[a] if sub is None else ins[a].at[sub]


def _exchange_start(srcs, items, modes, groups, name):
    n, ni, ng = len(srcs), len(items), len(groups)
    shapes = [srcs[a].shape if sub is None else srcs[a].shape[1:] for a, sub in items]
    land_shapes = [pltpu.HBM(_landing_shape(sh, m), srcs[a].dtype) for sh, m, (a, _) in zip(shapes, modes, items)]

    def body(*refs):
        ins = refs[:n]
        sems = refs[n:n + 3 * ng]
        land_refs = refs[n + 3 * ng:n + 3 * ng + ni]
        token = refs[-1]
        for g, idxs in enumerate(groups):
            for ai, it in enumerate(idxs):
                src = _item_src(ins, items[it])
                _own_copy(src, land_refs[it], modes[it], sems[3 * g + 2], ai).start()
                for k in range(1, N_DEV):
                    _exchange_copy(src, land_refs[it], modes[it], sems[3 * g], sems[3 * g + 1], ai, k, True).start()
        token[...] = jnp.zeros_like(token)

    sem_shapes = []
    for idxs in groups:
        sem_shapes += [pltpu.SemaphoreType.DMA((len(idxs) * (N_DEV - 1),))] * 2 + [pltpu.SemaphoreType.DMA((len(idxs),))]
    out = pl.pallas_call(
        body, name=name,
        out_shape=tuple(sem_shapes) + tuple(land_shapes) + (jax.ShapeDtypeStruct((8, 128), F32),),
        in_specs=[HBM] * n,
        out_specs=tuple([SEM] * (3 * ng) + [HBM] * ni + [pl.BlockSpec(memory_space=pltpu.VMEM)]),
        compiler_params=pltpu.CompilerParams(has_side_effects=EFFECT),
    )(*[pltpu.with_memory_space_constraint(s, pltpu.HBM) for s in srcs])
    sems = [tuple(out[3 * g:3 * g + 3]) for g in range(ng)]
    return sems, list(out[3 * ng:3 * ng + ni]), out[-1]


def _exchange_wait(sems, srcs, items, lands, modes, groups, after, name):
    n, ni, ng = len(srcs), len(items), len(groups)

    def body(*refs):
        ins, land_refs = refs[:n], refs[n:n + ni]
        sem_refs = refs[n + ni:n + ni + 3 * ng]
        for g, idxs in enumerate(groups):
            for ai, it in enumerate(idxs):
                src = _item_src(ins, items[it])
                _own_copy(src, land_refs[it], modes[it], sem_refs[3 * g + 2], ai).wait()
                for k in range(1, N_DEV):
                    cp = _exchange_copy(src, land_refs[it], modes[it], sem_refs[3 * g], sem_refs[3 * g + 1], ai, k, False)
                    cp.wait_send()
                    cp.wait_recv()

    flat_sems = [s for trio in sems for s in trio]
    afters = list(after) if isinstance(after, (list, tuple)) else [after]
    out = pl.pallas_call(
        body, name=name,
        out_shape=tuple(pltpu.HBM(l.shape, l.dtype) for l in lands),
        in_specs=[HBM] * (n + ni) + [SEM] * (3 * ng) + [ANY] * len(afters),
        out_specs=tuple([HBM] * ni),
        input_output_aliases={n + i: i for i in range(ni)},
        compiler_params=pltpu.CompilerParams(has_side_effects=EFFECT),
    )(*srcs, *lands, *flat_sems, *afters)
    return list(out)


def _slot_sum_call(landed):
    n = len(landed)

    def body(*refs):
        for src, dst in zip(refs[:n], refs[n:]):
            tot = src[0]
            for j in range(1, N_DEV):
                tot = tot + src[j]
            dst[...] = tot

    vm = pl.BlockSpec(memory_space=pltpu.VMEM)
    return pl.pallas_call(
        body, name="small_grads_sum", in_specs=[vm] * n, out_specs=[vm] * n,
        out_shape=[jax.ShapeDtypeStruct(a.shape[1:], F32) for a in landed],
        compiler_params=pltpu.CompilerParams(vmem_limit_bytes=V7X_VMEM_LIMIT),
    )(*landed)


def _adamw(w, g, m, v):
    m = ADAM_B1 * m + (1.0 - ADAM_B1) * g
    v = ADAM_B2 * v + (1.0 - ADAM_B2) * (g * g)
    m_hat = m / (1.0 - ADAM_B1 ** ADAM_STEP)
    v_hat = v / (1.0 - ADAM_B2 ** ADAM_STEP)
    delta = -ADAM_LR * (m_hat / (jnp.sqrt(v_hat) + ADAM_EPS) + ADAM_WD * w)
    return delta, m, v


def _reduce_adamw_call(recvs, w, m, v, name):
    nl = len(recvs)
    _, r, c = recvs[0].shape
    rb = 256 if r % 256 == 0 else r
    nb = r // rb

    def body(*refs):
        recv_refs = refs[:nl]
        w_ref, m_ref, v_ref, g_ref, d_ref, nm_ref, nv_ref = refs[nl:]
        for l in range(nl):
            @pl.when(pl.program_id(0) == l)
            def _(l=l):
                g = recv_refs[l][0].astype(F32)
                for j in range(1, N_DEV):
                    g = g + recv_refs[l][j].astype(F32)
                delta, nm, nv = _adamw(w_ref[0], g, m_ref[0], v_ref[0])
                g_ref[0] = g
                d_ref[0] = delta
                nm_ref[0] = nm
                nv_ref[0] = nv

    def recv_spec(l):
        return pl.BlockSpec((N_DEV, rb, c), lambda lg, i: (0, jnp.where(lg == l, i, jnp.where(lg < l, 0, nb - 1)), 0))

    blk = pl.BlockSpec((1, rb, c), lambda lg, i: (lg, i, 0))
    shp = jax.ShapeDtypeStruct((nl, r, c), F32)
    return pl.pallas_call(
        body, name=name, grid=(nl, nb),
        in_specs=[recv_spec(l) for l in range(nl)] + [blk, blk, blk],
        out_specs=[blk, blk, blk, blk], out_shape=[shp, shp, shp, shp],
        compiler_params=pltpu.CompilerParams(dimension_semantics=("arbitrary", "arbitrary"), vmem_limit_bytes=V7X_VMEM_LIMIT),
    )(*recvs, w, m, v)


def _small_adamw_call(w, g, m, v):
    def body(w_ref, g_ref, m_ref, v_ref, d_ref, nm_ref, nv_ref):
        delta, nm, nv = _adamw(w_ref[...], g_ref[...], m_ref[...], v_ref[...])
        d_ref[...] = delta
        nm_ref[...] = nm
        nv_ref[...] = nv

    shp = jax.ShapeDtypeStruct(w.shape, F32)
    vm = pl.BlockSpec(memory_space=pltpu.VMEM)
    return pl.pallas_call(body, name="small_adamw", in_specs=[vm] * 4, out_specs=[vm] * 3, out_shape=[shp, shp, shp],
                          compiler_params=pltpu.CompilerParams(vmem_limit_bytes=V7X_VMEM_LIMIT))(w, g, m, v)


_GATHER_MODE = dict(w_in="block", w_out="block", w_ff1="cols_in", w_ff2="block", w_ple_gate="block", w_ple_proj="cols_in")
_SCATTER_MODE = dict(w_in="slot", w_out="slot", w_ff1="cols_out", w_ff2="slot", w_ple_gate="slot", w_ple_proj="cols_out")
_GROUP_A = ("w_in", "w_out")
_GROUP_B = ("w_ff1", "w_ff2", "w_ple_gate", "w_ple_proj")


def _gathered_full(k, landed):
    if _GATHER_MODE[k] == "cols_in":
        return landed
    n, r, c = landed.shape
    return landed.reshape(n * r, c)


def _grad_send(k, g):
    if _SCATTER_MODE[k] == "cols_out":
        return g
    r8, c = g.shape
    return g.reshape(N_DEV, r8 // N_DEV, c)


_SMALL_ORDER = ("norm_mix_g", "sgu_w", "sgu_b", "sgu_ln_g", "sgu_ln_b", "conv_w", "pool_w", "pool_scale",
                "norm_ff_g", "norm_ple_g", "final_g")


def _pack_small(d, order=_SMALL_ORDER):
    pieces, layout = [], []
    for k in order:
        flat = d[k].reshape(-1)
        n = flat.shape[0]
        pad = (-n) % 128
        pieces.append(jnp.pad(flat, (0, pad)))
        layout.append((k, d[k].shape, n, n + pad))
    flat = jnp.concatenate(pieces)
    pad = (-flat.shape[0]) % 1024
    return jnp.pad(flat, (0, pad)).reshape(-1, 128), layout


def _unpack_small(buf, layout):
    flat = buf.reshape(-1)
    out, off = {}, 0
    for k, shape, n, padded in layout:
        out[k] = flat[off:off + n].reshape(shape)
        off += padded
    return out


def kernel(x, p, norm_mix_g, w_in, sgu_w, sgu_b, sgu_ln_g, sgu_ln_b, conv_w, pool_w, pool_scale, w_out, norm_ff_g, w_ff1, w_ff2, norm_ple_g, w_ple_gate, w_ple_proj, final_g, loss_target, m_norm_mix_g, m_w_in, m_sgu_w, m_sgu_b, m_sgu_ln_g, m_sgu_ln_b, m_conv_w, m_pool_w, m_pool_scale, m_w_out, m_norm_ff_g, m_w_ff1, m_w_ff2, m_norm_ple_g, m_w_ple_gate, m_w_ple_proj, m_final_g, v_norm_mix_g, v_w_in, v_sgu_w, v_sgu_b, v_sgu_ln_g, v_sgu_ln_b, v_conv_w, v_pool_w, v_pool_scale, v_w_out, v_norm_ff_g, v_w_ff1, v_w_ff2, v_norm_ple_g, v_w_ple_gate, v_w_ple_proj, v_final_g):
    t = x.shape[1]
    xc, yc_, cc = _coords()
    me = 4 * xc + 2 * yc_ + cc
    tm = lambda want: min(want, t)

    shard_names = _GROUP_A + _GROUP_B
    swap = lambda a: jnp.transpose(a, (0, 2, 1))
    shard = dict(w_in=swap(w_in), w_out=w_out, w_ff1=w_ff1, w_ff2=w_ff2, w_ple_gate=w_ple_gate, w_ple_proj=w_ple_proj)
    conv_pad = jnp.zeros((16, 128), F32).at[0:DEPTH * 3, 0:D_B // N_DEV].set(conv_w.reshape(DEPTH * 3, D_B // N_DEV))
    ag_srcs = [shard[k].astype(BF16) for k in shard_names] + [conv_pad]
    ag_items, ag_modes, ag_groups = [], [], []
    for l in range(DEPTH):
        for names in (_GROUP_A, _GROUP_B):
            ag_groups.append(list(range(len(ag_items), len(ag_items) + len(names))))
            ag_items += [(shard_names.index(k), l) for k in names]
            ag_modes += [_GATHER_MODE[k] for k in names]
            if l == 0 and names is _GROUP_A:
                ag_groups[-1].append(len(ag_items))
                ag_items.append((len(shard_names), None))
                ag_modes.append("block")
    ag_sems, ag_lands, _ = _exchange_start(ag_srcs, ag_items, ag_modes, ag_groups, "weights_gather_start")

    def gathered(l, which, after):
        idxs = ag_groups[2 * l + which]
        landed = _exchange_wait([ag_sems[2 * l + which]], ag_srcs, [ag_items[i] for i in idxs], [ag_lands[i] for i in idxs],
                                [ag_modes[i] for i in idxs], [list(range(len(idxs)))], after, f"weights_gather_wait_{l}_{which}")
        full = {k: _gathered_full(k, got) for k, got in zip((_GROUP_A, _GROUP_B)[which], landed)}
        if l == 0 and which == 0:
            full["conv_w"] = jnp.transpose(landed[-1][:, 0:DEPTH * 3, 0:D_B // N_DEV].reshape(N_DEV, DEPTH, 3, D_B // N_DEV),
                                           (1, 2, 0, 3)).reshape(DEPTH, 3, D_B)
        return full

    idx = jnp.arange(D_A)
    pmat = ((idx[:, None] // 64) == (idx[None, :] // 64)).astype(BF16) * (1.0 / 64.0)
    pmat = pmat.astype(BF16)
    tril = jnp.tril(jnp.ones((CHUNK, CHUNK), F32))
    trilcat = jnp.concatenate([tril, tril], axis=1)
    headsel = ((idx[:, None] // 64) == jnp.arange(CHUNK)[None, :]).astype(BF16)
    row = lambda a: a.reshape(1, -1)

    def mix_params(l):
        wm = sgu_w[l] * tril[None]
        wcat = jnp.stack([jnp.concatenate([wm[2 * j], wm[2 * j + 1]], axis=1) for j in range(3)]).astype(BF16)
        wtcat = jnp.stack([jnp.concatenate([wm[2 * j].T, wm[2 * j + 1].T], axis=1) for j in range(3)]).astype(BF16)
        bmat = jnp.repeat(sgu_b[l].T, 64, axis=1)
        bd = jnp.zeros((D_C, D_C), F32)
        for gi in range(4):
            bd = bd.at[gi * 64:(gi + 1) * 64, gi * 64:(gi + 1) * 64].set(pool_w[l, gi])
        mp = dict(pmat=pmat, ln_g=row(sgu_ln_g[l]), ln_b=row(sgu_ln_b[l]), wcat=wcat, bmat=bmat, conv_w=conv_full[l],
                  bd=bd.astype(BF16), pool_scale=row(pool_scale[l]))
        return mp, wtcat

    xs = x.reshape(t, D)
    p_layers = p.reshape(DEPTH, t, D_PLE)
    saved, full_w = [], []
    conv_full = None
    for l in range(DEPTH):
        wa = gathered(l, 0, xs)
        if l == 0:
            conv_full = wa["conv_w"]
        mp, _ = mix_params(l)
        x1, proj = _mix_fwd_call(xs, row(norm_mix_g[l]), wa["w_in"], wa["w_out"], mp, tm(TM_MIX_FWD), tm(RUN_MIX_FWD), l)
        wb = gathered(l, 1, x1)
        x2, x3, r, gate = _ffn_fwd_call(x1, p_layers, row(norm_ff_g[l]), row(norm_ple_g[l]), wb["w_ff1"], wb["w_ff2"],
                                        wb["w_ple_gate"], wb["w_ple_proj"], tm(TM_FFN_FWD), l)
        saved.append((xs, proj, x1, r, x2, gate))
        full_w.append({**wa, **wb})
        xs = x3

    sq, dx, dfinal = _loss_call(xs, loss_target.reshape(t, D), row(final_g), tm(TM_LOSS))
    loss = lax.psum(jnp.sum(sq) * (0.5 / D), ("x", "y", "c"))

    layer_keys = tuple(k for k in _SMALL_ORDER if k != "final_g")
    small = {k: [None] * DEPTH for k in layer_keys}
    small_ex, small_layout = [None] * DEPTH, None
    ex = {}
    token = None

    def after_start(g):
        return g if token is None else g + token[0:1, 0:1]

    def start_exchange(l, which, grads):
        names = (_GROUP_A, _GROUP_B)[which]
        sends = [_grad_send(k, grads[k]) for k in names]
        sems, lands, tok = _exchange_start(sends, [(i, None) for i in range(len(names))], [_SCATTER_MODE[k] for k in names],
                                           [list(range(len(names)))], f"grads_exchange_start_{l}_{which}")
        ex[(l, which)] = (sems[0], sends, lands)
        return tok

    for l in reversed(range(DEPTH)):
        x0, proj, x1, r, x2, gate = saved[l]
        fw = full_w[l]
        dx2, dgple, dwg, dwp = _ple_bwd_call(dx, x2, gate, p_layers, after_start(row(norm_ple_g[l])), fw["w_ple_gate"], fw["w_ple_proj"],
                                             tm(TM_PLE_BWD), l)
        da, dw2 = _ffn_bwd_hidden_call(dx2, r, fw["w_ff2"], tm(TM_FFN_BWD), l)
        dx1, dgff, dw1 = _ffn_bwd_input_call(da, x1, dx2, row(norm_ff_g[l]), fw["w_ff1"], tm(TM_FFN_BWD), l)
        token = start_exchange(l, 1, dict(w_ff1=dw1, w_ff2=dw2, w_ple_gate=dwg, w_ple_proj=dwp))
        mp, wtcat = mix_params(l)
        (dx, dgmix, dwcat, dsb, dlng, dlnb, dconv, dbd, dscale, dwin, dwout) = _mix_bwd_call(
            dx1, x0, proj, after_start(row(norm_mix_g[l])), fw["w_in"], fw["w_out"], mp, wtcat, trilcat, headsel, tm(TM_MIX_BWD), tm(RUN_MIX_BWD), l)
        token = start_exchange(l, 0, dict(w_in=dwin, w_out=dwout))
        small["norm_mix_g"][l] = dgmix[0]
        small["sgu_w"][l] = jnp.stack([dwcat[h // 2][:, (h % 2) * CHUNK:(h % 2 + 1) * CHUNK] for h in range(6)])
        small["sgu_b"][l] = dsb[:, 0:6].T
        small["sgu_ln_g"][l], small["sgu_ln_b"][l] = dlng[0], dlnb[0]
        small["conv_w"][l] = dconv
        small["pool_w"][l] = jnp.stack([dbd[gi * 64:(gi + 1) * 64, gi * 64:(gi + 1) * 64] for gi in range(4)])
        small["pool_scale"][l] = dscale[0]
        small["norm_ff_g"][l], small["norm_ple_g"][l] = dgff[0], dgple[0]
        layer_small = {k: small[k][l] for k in layer_keys}
        layer_small["final_g"] = dfinal[0] if l == DEPTH - 1 else jnp.zeros_like(dfinal[0])
        sbuf, small_layout = _pack_small(layer_small, layer_keys + ("final_g",))
        ssems, slands, stok = _exchange_start([sbuf], [(0, None)], ["block"], [[0]], f"small_grads_start_{l}")
        small_ex[l] = (ssems[0], sbuf, slands[0])
        token = token + stok
    grad_x = dx.reshape(1, t, D)

    state = dict(w_in=(swap(w_in), swap(m_w_in), swap(v_w_in)), w_out=(w_out, m_w_out, v_w_out), w_ff1=(w_ff1, m_w_ff1, v_w_ff1),
                 w_ff2=(w_ff2, m_w_ff2, v_w_ff2), w_ple_gate=(w_ple_gate, m_w_ple_gate, v_w_ple_gate),
                 w_ple_proj=(w_ple_proj, m_w_ple_proj, v_w_ple_proj))
    res = {}

    def finish_group(which, after):
        names = (_GROUP_A, _GROUP_B)[which]
        n = len(names)
        sems = [ex[(l, which)][0] for l in range(DEPTH)]
        sends = [s_ for l in range(DEPTH) for s_ in ex[(l, which)][1]]
        lands = [a_ for l in range(DEPTH) for a_ in ex[(l, which)][2]]
        landed = _exchange_wait(sems, sends, [(i, None) for i in range(DEPTH * n)], lands, [_SCATTER_MODE[k] for k in names] * DEPTH,
                                [list(range(l * n, (l + 1) * n)) for l in range(DEPTH)], after, f"grads_exchange_wait_{which}")
        for i, k in enumerate(names):
            res[k] = _reduce_adamw_call([landed[l * n + i] for l in range(DEPTH)], *state[k], "reduce_adamw_" + k)

    finish_group(1, dx)

    landed = _exchange_wait([small_ex[l][0] for l in range(DEPTH)], [small_ex[l][1] for l in range(DEPTH)],
                            [(l, None) for l in range(DEPTH)], [small_ex[l][2] for l in range(DEPTH)], ["block"] * DEPTH,
                            [[l] for l in range(DEPTH)], res[_GROUP_B[-1]][0], "small_grads_wait")
    sums = [_unpack_small(b_, small_layout) for b_ in _slot_sum_call(landed)]
    gs = {k: jnp.stack([sums[l][k] for l in range(DEPTH)]) for k in layer_keys}
    gs["final_g"] = sums[DEPTH - 1]["final_g"]
    conv_cols = lambda a: lax.dynamic_slice_in_dim(a, me * (D_B // N_DEV), D_B // N_DEV, axis=2)
    pad_conv = lambda a: jnp.zeros((DEPTH, 3, D_B), F32).at[:, :, 0:D_B // N_DEV].set(a)
    small_w = dict(norm_mix_g=norm_mix_g, sgu_w=sgu_w, sgu_b=sgu_b, sgu_ln_g=sgu_ln_g, sgu_ln_b=sgu_ln_b, conv_w=pad_conv(conv_w),
                   pool_w=pool_w, pool_scale=pool_scale, norm_ff_g=norm_ff_g, norm_ple_g=norm_ple_g, final_g=final_g)
    small_m = dict(norm_mix_g=m_norm_mix_g, sgu_w=m_sgu_w, sgu_b=m_sgu_b, sgu_ln_g=m_sgu_ln_g, sgu_ln_b=m_sgu_ln_b,
                   conv_w=pad_conv(m_conv_w), pool_w=m_pool_w, pool_scale=m_pool_scale, norm_ff_g=m_norm_ff_g,
                   norm_ple_g=m_norm_ple_g, final_g=m_final_g)
    small_v = dict(norm_mix_g=v_norm_mix_g, sgu_w=v_sgu_w, sgu_b=v_sgu_b, sgu_ln_g=v_sgu_ln_g, sgu_ln_b=v_sgu_ln_b,
                   conv_w=pad_conv(v_conv_w), pool_w=v_pool_w, pool_scale=v_pool_scale,
                   norm_ff_g=v_norm_ff_g, norm_ple_g=v_norm_ple_g, final_g=v_final_g)
    gs_local = dict(gs)
    gs_local["conv_w"] = pad_conv(conv_cols(gs["conv_w"]))
    g_loc, layout = _pack_small(gs_local)
    wbuf, _ = _pack_small(small_w)
    mbuf, _ = _pack_small(small_m)
    vbuf, _ = _pack_small(small_v)
    dbuf, nmbuf, nvbuf = _small_adamw_call(wbuf, g_loc, mbuf, vbuf)
    sd, sm, sv = _unpack_small(dbuf, layout), _unpack_small(nmbuf, layout), _unpack_small(nvbuf, layout)
    unconv = lambda a: a[:, :, 0:D_B // N_DEV]
    for dct in (gs_local, sd, sm, sv):
        dct["conv_w"] = unconv(dct["conv_w"])

    finish_group(0, [res[_GROUP_B[-1]][0], dbuf])

    order = ["norm_mix_g", "w_in", "sgu_w", "sgu_b", "sgu_ln_g", "sgu_ln_b", "conv_w", "pool_w", "pool_scale", "w_out",
             "norm_ff_g", "w_ff1", "w_ff2", "norm_ple_g", "w_ple_gate", "w_ple_proj", "final_g"]
    outs = [loss, grad_x]
    for which in range(4):
        for k in order:
            if k in res:
                outs.append(swap(res[k][which]) if k == "w_in" else res[k][which])
            else:
                outs.append((gs_local, sd, sm, sv)[which][k])
    return tuple(outs)
```

```python
import functools
import math

import jax
import jax.numpy as jnp
from jax import lax
from jax.experimental import pallas as pl
from jax.experimental.pallas import tpu as pltpu

F32 = jnp.float32
BF16 = jnp.bfloat16

D = 1024
D_IN = 2176
D_A = 384
D_B = 384
D_C = 256
D_FF = 4096
D_PLE = 256
DEPTH = 4
CHUNK = 128
HALO = 16
FF_BLK = 1024
N_DEV = 8
RMS_EPS = 1e-6
LN_EPS = 1e-5
ADAM_LR = 0.001
ADAM_B1 = 0.9
ADAM_B2 = 0.999
ADAM_EPS = 1e-08
ADAM_WD = 0.01
ADAM_STEP = 10

TM_MIX_FWD = 1024
TM_FFN_FWD = 512
TM_LOSS = 512
TM_PLE_BWD = 1024
TM_FFN_BWD = 512
TM_MIX_BWD = 512
RUN_MIX_FWD = 1024
RUN_MIX_BWD = 512
V7X_VMEM_LIMIT = 60000 * 1024

ANY = pl.BlockSpec(memory_space=pl.ANY)
HBM = pl.BlockSpec(memory_space=pltpu.HBM)
SEM = pl.BlockSpec(memory_space=pltpu.SEMAPHORE)
MESH = pl.DeviceIdType.MESH


def _params(vmem=V7X_VMEM_LIMIT):
    return pltpu.CompilerParams(dimension_semantics=("arbitrary",), vmem_limit_bytes=vmem)


def _in_hbm(a):
    return pltpu.with_memory_space_constraint(a, pltpu.HBM)


def _full(shape):
    nd = len(shape)
    return pl.BlockSpec(shape, lambda i: (0,) * nd)


def _rows(tm, cols):
    return pl.BlockSpec((tm, cols), lambda i: (i, 0))


def _layer_rows(layer, tm, cols):
    return pl.BlockSpec((None, tm, cols), lambda i: (layer, i, 0))


def _mm(a, b):
    return jnp.dot(a, b, preferred_element_type=F32)


def _mm_nt(a, b):
    return lax.dot_general(a, b, (((1,), (1,)), ((), ())), preferred_element_type=F32)


def _mm_tn(a, b):
    return lax.dot_general(a, b, (((0,), (0,)), ((), ())), preferred_element_type=F32)


def _gelu_and_grad(x):
    ax = jnp.abs(x) * (1.0 / math.sqrt(2.0))
    t = 1.0 / (1.0 + 0.3275911 * ax)
    poly = t * (0.254829592 + t * (-0.284496736 + t * (1.421413741 + t * (-1.453152027 + t * 1.061405429))))
    e = jnp.exp(-0.5 * x * x)
    half = 0.5 * poly * e
    cdf = jnp.where(x < 0, half, 1.0 - half)
    return x * cdf, cdf + x * (e * (1.0 / math.sqrt(2.0 * math.pi)))


def _rms(x, g):
    rstd = lax.rsqrt(jnp.mean(x * x, axis=-1, keepdims=True) + RMS_EPS)
    xhat = x * rstd
    return xhat * g, xhat, rstd


def _rms_bwd(dy, g, xhat, rstd):
    dg = jnp.sum(dy * xhat, axis=0, keepdims=True)
    dxh = dy * g
    dx = rstd * (dxh - xhat * jnp.mean(dxh * xhat, axis=-1, keepdims=True))
    return dx, dg


def _shift_down(ext, k):
    return pltpu.roll(ext, k, 0)[HALO:, :]


def _shift_up(ext, k):
    n = ext.shape[0]
    return pltpu.roll(ext, n - k, 0)[: n - HALO, :]


def _pool_select(s2, s4, s8, s16):
    lane = lax.broadcasted_iota(jnp.int32, s2.shape, 1)
    return jnp.where(lane < 64, s2, jnp.where(lane < 128, s4, jnp.where(lane < 192, s8, s16)))


def _pool_inv_count(tile_start, tm):
    pos = lax.broadcasted_iota(jnp.int32, (tm, D_C), 0) + tile_start + 1
    lane = lax.broadcasted_iota(jnp.int32, (tm, D_C), 1)
    win = jnp.where(lane < 64, 2, jnp.where(lane < 128, 4, jnp.where(lane < 192, 8, 16)))
    return 1.0 / jnp.minimum(pos, win).astype(F32)


def _head_halves(a):
    lane = lax.broadcasted_iota(jnp.int32, a.shape, 1)
    even = (lane & 64) == 0
    return jnp.where(even, a, 0.0).astype(BF16), jnp.where(even, 0.0, a).astype(BF16)


def _head_stack(lo, hi, j, nch):
    return jnp.concatenate(
        [jnp.concatenate([lo[c * CHUNK:(c + 1) * CHUNK, j * 128:(j + 1) * 128], hi[c * CHUNK:(c + 1) * CHUNK, j * 128:(j + 1) * 128]], axis=0)
         for c in range(nch)], axis=1)


def _chunks_to_lanes(a, j, nch):
    return jnp.concatenate([a[c * CHUNK:(c + 1) * CHUNK, j * 128:(j + 1) * 128] for c in range(nch)], axis=1)


def _lanes_to_chunks(o, nch):
    return jnp.concatenate([o[:, c * CHUNK:(c + 1) * CHUNK] for c in range(nch)], axis=0)


def _loads(pairs, sem):
    return [pltpu.make_async_copy(src, dst, sem.at[n]) for n, (src, dst) in enumerate(pairs)]


def _load_all(loads):
    for cp in loads:
        cp.start()
    for cp in loads:
        cp.wait()


def _stage_bf16(acc, stage):
    rows = acc.shape[0]
    strip = min(rows, 128)

    @pl.loop(0, rows // strip)
    def _(n):
        sl = pl.ds(pl.multiple_of(n * strip, strip), strip)
        stage[sl, :] = acc[sl, :].astype(BF16)


N_SAVED = 5


def _mixers_fwd(pf, halo_hc, halo_zc, tile_start, prm, saved=None):
    tm = pf.shape[0]
    nch = tm // CHUNK
    u, v = pf[:, 0:D_A], pf[:, D_A:2 * D_A]
    zb, gb, gc = pf[:, 768:1152], pf[:, 1152:1536], pf[:, 1536:1920]
    zc = pf[:, 1920:2176]
    r = {}
    if saved is None:
        gu, r["dgelu_u"] = _gelu_and_grad(u)
        gv, r["dgelu_v"] = _gelu_and_grad(v)
        pmat = prm["pmat"][...]
        mu = _mm(gv.astype(BF16), pmat)
        dv = gv - mu
        var = _mm((dv * dv).astype(BF16), pmat)
        rstd = lax.rsqrt(var + LN_EPS)
        xh = dv * rstd
        r["saved"] = jnp.concatenate([gu, r["dgelu_u"], r["dgelu_v"], xh, rstd], axis=1).astype(BF16)
    else:
        gu, r["dgelu_u"], r["dgelu_v"], xh, rstd = (saved[:, n * D_A:(n + 1) * D_A] for n in range(N_SAVED))
    vlo, vhi = _head_halves(xh * prm["ln_g"][...] + prm["ln_b"][...])
    cols, v2s = [], []
    for j in range(3):
        v2 = _head_stack(vlo, vhi, j, nch)
        v2s.append(v2)
        cols.append(_lanes_to_chunks(_mm(prm["wcat"][j], v2), nch))
    mixed = jnp.concatenate(cols, axis=1) + jnp.concatenate([prm["bmat"][...]] * nch, axis=0)
    ya = gu * mixed
    r.update(gu=gu, mixed=mixed, v2s=v2s, xh=xh, ln_rstd=rstd)
    w0, w1, w2 = prm["conv_w"][0:1, :], prm["conv_w"][1:2, :], prm["conv_w"][2:3, :]
    hc = gc * zb
    ext = jnp.concatenate([halo_hc, hc], axis=0)
    h1, h2 = _shift_down(ext, 1), _shift_down(ext, 2)
    yc = w2 * hc + w1 * h1 + w0 * h2
    yb = gb * yc
    r.update(hc=hc, h1=h1, h2=h2, yc=yc, zb=zb, gb=gb, gc=gc, w0=w0, w1=w1, w2=w2)
    ext = jnp.concatenate([halo_zc, zc], axis=0)
    s2 = ext + pltpu.roll(ext, 1, 0)
    s4 = s2 + pltpu.roll(s2, 2, 0)
    s8 = s4 + pltpu.roll(s4, 4, 0)
    s16 = s8 + pltpu.roll(s8, 8, 0)
    inv = _pool_inv_count(tile_start, tm)
    pooled = _pool_select(s2, s4, s8, s16)[HALO:, :] * inv - zc
    pooledb = pooled.astype(BF16)
    pm = _mm(pooledb, prm["bd"][...])
    scale = prm["pool_scale"][...]
    ycm = pm * scale
    r.update(inv=inv, pooledb=pooledb, pm=pm, scale=scale, zc=zc)
    r["ycat"] = jnp.concatenate([ya, yb, ycm], axis=1)
    return r


_MIX_PARAM_NAMES = ("pmat", "ln_g", "ln_b", "wcat", "bmat", "conv_w", "bd", "pool_scale")


def _mix_param_specs():
    return [_full((D_A, D_A)), _full((1, D_A)), _full((1, D_A)), _full((3, CHUNK, 2 * CHUNK)), _full((CHUNK, D_A)),
            _full((3, D_B)), _full((D_C, D_C)), _full((1, D_C))]


def _mix_fwd_call(x, g_mix, w_in_t, w_out, mp, tm, run, layer):
    t = x.shape[0]
    nt = t // tm

    def body(x_ref, g_ref, pmat, ln_g, ln_b, wcat, bmat, conv_w, bd, pool_scale, win_hbm, wout_hbm,
             x1_ref, proj_ref, saved_ref, win_s, wout_s, halo_hc, halo_zc, load_sem):
        i = pl.program_id(0)
        loads = _loads([(win_hbm, win_s), (wout_hbm, wout_s)], load_sem)

        @pl.when(i == 0)
        def _():
            _load_all(loads)
            halo_hc[...] = jnp.zeros_like(halo_hc)
            halo_zc[...] = jnp.zeros_like(halo_zc)

        prm = dict(pmat=pmat, ln_g=ln_g, ln_b=ln_b, wcat=wcat, bmat=bmat, conv_w=conv_w, bd=bd, pool_scale=pool_scale)
        xv = x_ref[...]
        h, _, _ = _rms(xv, g_ref[...])
        pf = _mm_nt(h.astype(BF16), win_s[...])
        proj_ref[...] = pf.astype(BF16)
        hh, hz = halo_hc[...], halo_zc[...]
        parts = []
        for c in range(tm // run):
            r = _mixers_fwd(pf[c * run:(c + 1) * run, :], hh, hz, i * tm + c * run, prm)
            hh, hz = r["hc"][run - HALO:, :], r["zc"][run - HALO:, :]
            parts.append(r["ycat"].astype(BF16))
            saved_ref[c * run:(c + 1) * run, :] = r["saved"]
        halo_hc[...] = hh
        halo_zc[...] = hz
        x1_ref[...] = xv + _mm(jnp.concatenate(parts, axis=0), wout_s[...])

    return pl.pallas_call(
        body, name=f"mix_fwd_{layer}", grid=(nt,),
        in_specs=[_rows(tm, D), _full((1, D))] + _mix_param_specs() + [HBM, HBM],
        out_specs=[_rows(tm, D), _rows(tm, D_IN), _rows(tm, N_SAVED * D_A)],
        out_shape=[jax.ShapeDtypeStruct((t, D), F32), jax.ShapeDtypeStruct((t, D_IN), BF16),
                   jax.ShapeDtypeStruct((t, N_SAVED * D_A), BF16)],
        scratch_shapes=[pltpu.VMEM((D_IN, D), BF16), pltpu.VMEM((D, D), BF16),
                        pltpu.VMEM((HALO, D_B), F32), pltpu.VMEM((HALO, D_C), F32), pltpu.SemaphoreType.DMA((2,))],
        compiler_params=_params(),
    )(x, g_mix, *[mp[k] for k in _MIX_PARAM_NAMES], _in_hbm(w_in_t), _in_hbm(w_out))


def _ffn_fwd_call(x1, p, g_ff, g_ple, w1, w2, wg, wp, tm, layer):
    t = x1.shape[0]
    nt = t // tm

    def body(x1_ref, p_ref, gff_ref, gple_ref, w1_hbm, w2_hbm, wg_hbm, wp_hbm,
             x2_ref, x3_ref, r_ref, gate_ref, w1_s, w2_s, wg_s, wp_s, load_sem):
        i = pl.program_id(0)
        loads = _loads([(w1_hbm, w1_s), (w2_hbm, w2_s), (wg_hbm, wg_s), (wp_hbm, wp_s)], load_sem)

        @pl.when(i == 0)
        def _():
            _load_all(loads)

        x1v = x1_ref[...]
        h2, _, _ = _rms(x1v, gff_ref[...])
        h2b = h2.astype(BF16)
        acc = x1v
        for j in range(D_FF // FF_BLK):
            blk = slice(j * FF_BLK, (j + 1) * FF_BLK)
            rj = jnp.maximum(_mm(h2b, w1_s[:, blk]), 0.0)
            r_ref[:, blk] = rj.astype(BF16)
            acc = acc + _mm((rj * rj).astype(BF16), w2_s[blk, :])
        x2_ref[...] = acc
        n3, _, _ = _rms(acc, gple_ref[...])
        gate = jax.nn.sigmoid(_mm(n3.astype(BF16), wg_s[...]))
        gate_ref[...] = gate.astype(BF16)
        pp = _mm(p_ref[...].astype(BF16), wp_s[...])
        x3_ref[...] = acc + pp * gate

    return pl.pallas_call(
        body, name=f"ffn_fwd_{layer}", grid=(nt,),
        in_specs=[_rows(tm, D), _layer_rows(layer, tm, D_PLE), _full((1, D)), _full((1, D)), HBM, HBM, HBM, HBM],
        out_specs=[_rows(tm, D), _rows(tm, D), _rows(tm, D_FF), _rows(tm, D)],
        out_shape=[jax.ShapeDtypeStruct((t, D), F32), jax.ShapeDtypeStruct((t, D), F32),
                   jax.ShapeDtypeStruct((t, D_FF), BF16), jax.ShapeDtypeStruct((t, D), BF16)],
        scratch_shapes=[pltpu.VMEM((D, D_FF), BF16), pltpu.VMEM((D_FF, D), BF16),
                        pltpu.VMEM((D, D), BF16), pltpu.VMEM((D_PLE, D), BF16), pltpu.SemaphoreType.DMA((4,))],
        compiler_params=_params(),
    )(x1, p, g_ff, g_ple, _in_hbm(w1), _in_hbm(w2), _in_hbm(wg), _in_hbm(wp))


def _loss_call(xl, target, final_g, tm):
    t = xl.shape[0]
    nt = t // tm

    def body(x_ref, t_ref, g_ref, sq_ref, dx_ref, dg_ref):
        i = pl.program_id(0)

        @pl.when(i == 0)
        def _():
            sq_ref[...] = jnp.zeros_like(sq_ref)
            dg_ref[...] = jnp.zeros_like(dg_ref)

        g = g_ref[...]
        y, xhat, rstd = _rms(x_ref[...], g)
        err = y - t_ref[...]
        sq_ref[...] += jnp.sum(err * err, axis=0, keepdims=True)
        dx, dg = _rms_bwd(err * (1.0 / D), g, xhat, rstd)
        dx_ref[...] = dx
        dg_ref[...] += dg

    return pl.pallas_call(
        body, name="loss_head", grid=(nt,),
        in_specs=[_rows(tm, D), _rows(tm, D), _full((1, D))],
        out_specs=[_full((1, D)), _rows(tm, D), _full((1, D))],
        out_shape=[jax.ShapeDtypeStruct((1, D), F32), jax.ShapeDtypeStruct((t, D), F32), jax.ShapeDtypeStruct((1, D), F32)],
        compiler_params=_params(),
    )(xl, target, final_g)


def _ple_bwd_call(dx3, x2, gate, p, g_ple, wg, wp, tm, layer):
    t = dx3.shape[0]
    nt = t // tm

    def body(dx3_ref, x2_ref, gate_ref, p_ref, g_ref, wg_hbm, wp_hbm,
             dx2_ref, dg_ref, dwg_hbm, dwp_hbm, wg_s, wp_s, dwg_acc, dwp_acc, load_sem):
        i = pl.program_id(0)
        loads = _loads([(wp_hbm, wp_s), (wg_hbm, wg_s)], load_sem)

        @pl.when(i == 0)
        def _():
            _load_all(loads)
            dwg_acc[...] = jnp.zeros_like(dwg_acc)
            dwp_acc[...] = jnp.zeros_like(dwp_acc)
            dg_ref[...] = jnp.zeros_like(dg_ref)

        g = g_ref[...]
        dx3v = dx3_ref[...]
        gatev = gate_ref[...].astype(F32)
        pb = p_ref[...].astype(BF16)
        pp = _mm(pb, wp_s[...])
        dwp_acc[...] += _mm_tn(pb, (dx3v * gatev).astype(BF16))
        dgpre = (dx3v * pp * gatev * (1.0 - gatev)).astype(BF16)
        n3, xhat, rstd = _rms(x2_ref[...], g)
        dwg_acc[...] += _mm_tn(n3.astype(BF16), dgpre)
        dn3 = _mm_nt(dgpre, wg_s[...])
        dx, dg = _rms_bwd(dn3, g, xhat, rstd)
        dx2_ref[...] = dx3v + dx
        dg_ref[...] += dg

        @pl.when(i == nt - 1)
        def _():
            _stage_bf16(dwg_acc, wg_s)
            _stage_bf16(dwp_acc, wp_s)
            pltpu.sync_copy(wg_s, dwg_hbm)
            pltpu.sync_copy(wp_s, dwp_hbm)

    return pl.pallas_call(
        body, name=f"ple_bwd_{layer}", grid=(nt,),
        in_specs=[_rows(tm, D), _rows(tm, D), _rows(tm, D), _layer_rows(layer, tm, D_PLE), _full((1, D)), HBM, HBM],
        out_specs=[_rows(tm, D), _full((1, D)), HBM, HBM],
        out_shape=[jax.ShapeDtypeStruct((t, D), F32), jax.ShapeDtypeStruct((1, D), F32),
                   pltpu.HBM((D, D), BF16), pltpu.HBM((D_PLE, D), BF16)],
        scratch_shapes=[pltpu.VMEM((D, D), BF16), pltpu.VMEM((D_PLE, D), BF16),
                        pltpu.VMEM((D, D), F32), pltpu.VMEM((D_PLE, D), F32), pltpu.SemaphoreType.DMA((2,))],
        compiler_params=_params(),
    )(dx3, x2, gate, p, g_ple, _in_hbm(wg), _in_hbm(wp))


def _ffn_bwd_hidden_call(dx2, r, w2, tm, layer):
    t = dx2.shape[0]
    nt = t // tm

    def body(dx2_ref, r_ref, w2_hbm, da_ref, dw2_hbm, w2_s, dw2_acc, load_sem):
        i = pl.program_id(0)
        loads = _loads([(w2_hbm, w2_s)], load_sem)

        @pl.when(i == 0)
        def _():
            _load_all(loads)
            dw2_acc[...] = jnp.zeros_like(dw2_acc)

        dxb = dx2_ref[...].astype(BF16)
        for j in range(D_FF // FF_BLK):
            blk = slice(j * FF_BLK, (j + 1) * FF_BLK)
            rj = r_ref[:, blk].astype(F32)
            ds = _mm_nt(dxb, w2_s[blk, :])
            da_ref[:, blk] = (2.0 * rj * ds).astype(BF16)
            dw2_acc[blk, :] += _mm_tn((rj * rj).astype(BF16), dxb)

        @pl.when(i == nt - 1)
        def _():
            _stage_bf16(dw2_acc, w2_s)
            pltpu.sync_copy(w2_s, dw2_hbm)

    return pl.pallas_call(
        body, name=f"ffn_bwd_hidden_{layer}", grid=(nt,),
        in_specs=[_rows(tm, D), _rows(tm, D_FF), HBM],
        out_specs=[_rows(tm, D_FF), HBM],
        out_shape=[jax.ShapeDtypeStruct((t, D_FF), BF16), pltpu.HBM((D_FF, D), BF16)],
        scratch_shapes=[pltpu.VMEM((D_FF, D), BF16), pltpu.VMEM((D_FF, D), F32), pltpu.SemaphoreType.DMA((1,))],
        compiler_params=_params(),
    )(dx2, r, _in_hbm(w2))


def _ffn_bwd_input_call(da, x1, dx2, g_ff, w1, tm, layer):
    t = dx2.shape[0]
    nt = t // tm

    def body(da_ref, x1_ref, dx2_ref, g_ref, w1_hbm, dx1_ref, dg_ref, dw1_hbm, w1_s, dw1_acc, load_sem):
        i = pl.program_id(0)
        loads = _loads([(w1_hbm, w1_s)], load_sem)

        @pl.when(i == 0)
        def _():
            _load_all(loads)
            dw1_acc[...] = jnp.zeros_like(dw1_acc)
            dg_ref[...] = jnp.zeros_like(dg_ref)

        g = g_ref[...]
        h2, xhat, rstd = _rms(x1_ref[...], g)
        h2b = h2.astype(BF16)
        dh2 = jnp.zeros((tm, D), F32)
        for j in range(D_FF // FF_BLK):
            blk = slice(j * FF_BLK, (j + 1) * FF_BLK)
            daj = da_ref[:, blk]
            dh2 = dh2 + _mm_nt(daj, w1_s[:, blk])
            dw1_acc[:, blk] += _mm_tn(h2b, daj)
        dx, dg = _rms_bwd(dh2, g, xhat, rstd)
        dx1_ref[...] = dx2_ref[...] + dx
        dg_ref[...] += dg

        @pl.when(i == nt - 1)
        def _():
            _stage_bf16(dw1_acc, w1_s)
            pltpu.sync_copy(w1_s, dw1_hbm)

    return pl.pallas_call(
        body, name=f"ffn_bwd_input_{layer}", grid=(nt,),
        in_specs=[_rows(tm, D_FF), _rows(tm, D), _rows(tm, D), _full((1, D)), HBM],
        out_specs=[_rows(tm, D), _full((1, D)), HBM],
        out_shape=[jax.ShapeDtypeStruct((t, D), F32), jax.ShapeDtypeStruct((1, D), F32), pltpu.HBM((D, D_FF), BF16)],
        scratch_shapes=[pltpu.VMEM((D, D_FF), BF16), pltpu.VMEM((D, D_FF), F32), pltpu.SemaphoreType.DMA((1,))],
        compiler_params=_params(),
    )(da, x1, dx2, g_ff, _in_hbm(w1))


def _mix_bwd_call(dx1, x, proj, saved, g_mix, w_in_t, w_out, mp, wtcat, trilcat, headsel, tm, run, layer):
    t = dx1.shape[0]
    nt = t // tm
    nrun = tm // run
    nch = run // CHUNK
    hb = tm // HALO

    def rev(i):
        return nt - 1 - i

    def body(dx1_ref, x_ref, proj_ref, halo_ref, saved_ref, g_ref, pmat, ln_g, ln_b, wcat, bmat, conv_w, bd, pool_scale,
             wtcat_ref, tril_ref, sel_ref, win_hbm, wout_hbm,
             dx_ref, dg_ref, dwcat_ref, dsb_ref, dlng_ref, dlnb_ref, dconv_ref, dbd_ref, dscale_ref, dwin_hbm, dwout_hbm,
             win_s, wout_s, dwin_acc, dwout_acc, dbm_acc, carry_yc, carry_q, load_sem):
        i = pl.program_id(0)
        ri = nt - 1 - i
        loads = _loads([(wout_hbm, wout_s), (win_hbm, win_s)], load_sem)

        @pl.when(i == 0)
        def _():
            _load_all(loads)
            for ref in (dwin_acc, dwout_acc, dbm_acc, carry_yc, carry_q, dg_ref, dwcat_ref, dlng_ref, dlnb_ref,
                        dconv_ref, dbd_ref, dscale_ref):
                ref[...] = jnp.zeros_like(ref)

        prm = dict(pmat=pmat, ln_g=ln_g, ln_b=ln_b, wcat=wcat, bmat=bmat, conv_w=conv_w, bd=bd, pool_scale=pool_scale)
        g = g_ref[...]
        h, xhat, rstd = _rms(x_ref[...], g)
        hb16 = h.astype(BF16)
        dx1v = dx1_ref[...]
        dx1b = dx1v.astype(BF16)
        dycat = _mm_nt(dx1b, wout_s[...])
        lng = ln_g[...]
        pm_ = pmat[...]
        cy, cq = carry_yc[...], carry_q[...]
        ycat_parts, dproj_parts = [None] * nrun, [None] * nrun
        dbm = dlng = dlnb = dscale = dcv0 = dcv1 = dcv2 = None
        add = lambda tot, v: v if tot is None else tot + v
        for c in reversed(range(nrun)):
            rows = slice(c * run, (c + 1) * run)
            pf = proj_ref[rows, :].astype(F32)
            if c > 0:
                ph = proj_ref[c * run - HALO:c * run, :].astype(F32)
            else:
                ph = halo_ref[...].astype(F32) * (ri > 0).astype(F32)
            r = _mixers_fwd(pf, ph[:, 1536:1920] * ph[:, 768:1152], ph[:, 1920:2176], ri * tm + c * run, prm,
                            saved_ref[rows, :].astype(F32))
            ycat_parts[c] = r["ycat"].astype(BF16)
            dya, dyb, dyc = dycat[rows, 0:D_A], dycat[rows, D_A:D_A + D_B], dycat[rows, D_A + D_B:D]

            dgu = dya * r["mixed"]
            dmix = dya * r["gu"]
            dmix_b = dmix.astype(BF16)
            dlo, dhi = _head_halves(dmix)
            for k in range(nch):
                dbm = add(dbm, dmix[k * CHUNK:(k + 1) * CHUNK, :])
            dvn_cols = []
            for j in range(3):
                dwcat_ref[j] += _mm_nt(_chunks_to_lanes(dmix_b, j, nch), r["v2s"][j])
                dvn_cols.append(_lanes_to_chunks(_mm(wtcat_ref[j], _head_stack(dlo, dhi, j, nch)), nch))
            dvn = jnp.concatenate(dvn_cols, axis=1)
            xh = r["xh"]
            dlng = add(dlng, jnp.sum(dvn * xh, axis=0, keepdims=True))
            dlnb = add(dlnb, jnp.sum(dvn, axis=0, keepdims=True))
            dxh = dvn * lng
            m1 = _mm(dxh.astype(BF16), pm_)
            m2 = _mm((dxh * xh).astype(BF16), pm_)
            dgv = r["ln_rstd"] * (dxh - m1 - xh * m2)
            du = dgu * r["dgelu_u"]
            dv = dgv * r["dgelu_v"]

            dgb = dyb * r["yc"]
            dyc2 = dyb * r["gb"]
            dcv0 = add(dcv0, jnp.sum(dyc2 * r["h2"], axis=0, keepdims=True))
            dcv1 = add(dcv1, jnp.sum(dyc2 * r["h1"], axis=0, keepdims=True))
            dcv2 = add(dcv2, jnp.sum(dyc2 * r["hc"], axis=0, keepdims=True))
            ext = jnp.concatenate([dyc2, cy], axis=0)
            dhc = r["w2"] * dyc2 + r["w1"] * _shift_up(ext, 1) + r["w0"] * _shift_up(ext, 2)
            cy = dyc2[0:HALO, :]
            dgc = dhc * r["zb"]
            dzb = dhc * r["gc"]

            dscale = add(dscale, jnp.sum(dyc * r["pm"], axis=0, keepdims=True))
            dpm = (dyc * r["scale"]).astype(BF16)
            dbd_ref[...] += _mm_tn(r["pooledb"], dpm)
            dpooled = _mm_nt(dpm, bd[...])
            q = dpooled * r["inv"]
            ext = jnp.concatenate([q, cq], axis=0)
            n = run + HALO
            r2 = ext + pltpu.roll(ext, n - 1, 0)
            r4 = r2 + pltpu.roll(r2, n - 2, 0)
            r8 = r4 + pltpu.roll(r4, n - 4, 0)
            r16 = r8 + pltpu.roll(r8, n - 8, 0)
            dzc = _pool_select(r2, r4, r8, r16)[0:run, :] - dpooled
            cq = q[0:HALO, :]
            dproj_parts[c] = jnp.concatenate([du, dv, dzb, dgb, dgc, dzc], axis=1).astype(BF16)

        carry_yc[...] = cy
        carry_q[...] = cq
        dbm_acc[...] += dbm
        dlng_ref[...] += dlng
        dlnb_ref[...] += dlnb
        dscale_ref[...] += dscale
        dconv_ref[0:1, :] += dcv0
        dconv_ref[1:2, :] += dcv1
        dconv_ref[2:3, :] += dcv2
        dwout_acc[...] += _mm_tn(jnp.concatenate(ycat_parts, axis=0), dx1b)
        dproj = jnp.concatenate(dproj_parts, axis=0)
        dwin_acc[...] += _mm_tn(dproj, hb16)
        dh = _mm(dproj, win_s[...])
        dx, dg = _rms_bwd(dh, g, xhat, rstd)
        dx_ref[...] = dx1v + dx
        dg_ref[...] += dg

        @pl.when(i == nt - 1)
        def _():
            _stage_bf16(dwin_acc, win_s)
            pltpu.sync_copy(win_s, dwin_hbm)
            _stage_bf16(dwout_acc, wout_s)
            pltpu.sync_copy(wout_s, dwout_hbm)
            for j in range(3):
                dwcat_ref[j] = dwcat_ref[j] * tril_ref[...]
            acc = dbm_acc[...]
            hi = acc.astype(BF16)
            lo = (acc - hi.astype(F32)).astype(BF16)
            dsb_ref[...] = _mm(hi, sel_ref[...]) + _mm(lo, sel_ref[...])

    return pl.pallas_call(
        body, name=f"mix_bwd_{layer}", grid=(nt,),
        in_specs=[pl.BlockSpec((tm, D), lambda i: (rev(i), 0)), pl.BlockSpec((tm, D), lambda i: (rev(i), 0)),
                  pl.BlockSpec((tm, D_IN), lambda i: (rev(i), 0)),
                  pl.BlockSpec((HALO, D_IN), lambda i: (jnp.maximum(rev(i) * hb - 1, 0), 0)),
                  pl.BlockSpec((tm, N_SAVED * D_A), lambda i: (rev(i), 0)), _full((1, D))] + _mix_param_specs()
                 + [_full((3, CHUNK, 2 * CHUNK)), _full((CHUNK, 2 * CHUNK)), _full((D_A, CHUNK)), HBM, HBM],
        out_specs=[pl.BlockSpec((tm, D), lambda i: (rev(i), 0)), _full((1, D)), _full((3, CHUNK, 2 * CHUNK)),
                   _full((CHUNK, CHUNK)), _full((1, D_A)), _full((1, D_A)), _full((3, D_B)), _full((D_C, D_C)),
                   _full((1, D_C)), HBM, HBM],
        out_shape=[jax.ShapeDtypeStruct((t, D), F32), jax.ShapeDtypeStruct((1, D), F32),
                   jax.ShapeDtypeStruct((3, CHUNK, 2 * CHUNK), F32), jax.ShapeDtypeStruct((CHUNK, CHUNK), F32),
                   jax.ShapeDtypeStruct((1, D_A), F32), jax.ShapeDtypeStruct((1, D_A), F32),
                   jax.ShapeDtypeStruct((3, D_B), F32), jax.ShapeDtypeStruct((D_C, D_C), F32),
                   jax.ShapeDtypeStruct((1, D_C), F32), pltpu.HBM((D_IN, D), BF16), pltpu.HBM((D, D), BF16)],
        scratch_shapes=[pltpu.VMEM((D_IN, D), BF16), pltpu.VMEM((D, D), BF16),
                        pltpu.VMEM((D_IN, D), F32), pltpu.VMEM((D, D), F32), pltpu.VMEM((CHUNK, D_A), F32),
                        pltpu.VMEM((HALO, D_B), F32), pltpu.VMEM((HALO, D_C), F32), pltpu.SemaphoreType.DMA((2,))],
        compiler_params=_params(),
    )(dx1, x, proj, proj, saved, g_mix, *[mp[k] for k in _MIX_PARAM_NAMES], wtcat, trilcat, headsel, _in_hbm(w_in_t), _in_hbm(w_out))


def _coords():
    return lax.axis_index("x"), lax.axis_index("y"), lax.axis_index("c")


EFFECT = pltpu.SideEffectType.DATAFLOW_SIDE_EFFECTING


def _peer(k):
    x, y, c = _coords()
    px, py, pc = x ^ (k >> 2), y ^ ((k >> 1) & 1), c ^ (k & 1)
    return (px, py, pc), 4 * px + 2 * py + pc


def _landing_shape(shape, mode):
    if mode == "block":
        return (N_DEV,) + shape
    if mode == "slot":
        return shape
    if mode == "cols_in":
        return (shape[0], N_DEV * shape[1])
    return (N_DEV, shape[0], shape[1] // N_DEV)


def _pieces(src, land, mode, src_idx, land_idx):
    if mode == "block":
        return src, land.at[land_idx]
    if mode == "slot":
        return src.at[src_idx], land.at[land_idx]
    if mode == "cols_in":
        cw = src.shape[1]
        return src, land.at[:, pl.ds(pl.multiple_of(land_idx * cw, 128), cw)]
    cw = land.shape[2]
    return src.at[:, pl.ds(pl.multiple_of(src_idx * cw, 128), cw)], land.at[land_idx]


def _exchange_copy(src, land, mode, send_sem, recv_sem, ai, k, starting):
    x, y, c = _coords()
    peer, pidx = _peer(k)
    s, d = _pieces(src, land, mode, pidx, 4 * x + 2 * y + c if starting else pidx)
    i = ai * (N_DEV - 1) + k - 1
    return pltpu.make_async_remote_copy(src_ref=s, dst_ref=d, send_sem=send_sem.at[i], recv_sem=recv_sem.at[i],
                                        device_id=peer, device_id_type=MESH)


def _own_copy(src, land, mode, local_sem, ai):
    x, y, c = _coords()
    me = 4 * x + 2 * y + c
    s, d = _pieces(src, land, mode, me, me)
    return pltpu.make_async_copy(s, d, local_sem.at[ai])


def _item_src(ins, item):
    a, sub = item
    return ins[a] if sub is None else ins[a].at[sub]


def _exchange_start(srcs, items, modes, groups, name):
    n, ni, ng = len(srcs), len(items), len(groups)
    shapes = [srcs[a].shape if sub is None else srcs[a].shape[1:] for a, sub in items]
    land_shapes = [pltpu.HBM(_landing_shape(sh, m), srcs[a].dtype) for sh, m, (a, _) in zip(shapes, modes, items)]

    def body(*refs):
        ins = refs[:n]
        sems = refs[n:n + 3 * ng]
        land_refs = refs[n + 3 * ng:n + 3 * ng + ni]
        token = refs[-1]
        for g, idxs in enumerate(groups):
            for ai, it in enumerate(idxs):
                src = _item_src(ins, items[it])
                _own_copy(src, land_refs[it], modes[it], sems[3 * g + 2], ai).start()
                for k in range(1, N_DEV):
                    _exchange_copy(src, land_refs[it], modes[it], sems[3 * g], sems[3 * g + 1], ai, k, True).start()
        token[...] = jnp.zeros_like(token)

    sem_shapes = []
    for idxs in groups:
        sem_shapes += [pltpu.SemaphoreType.DMA((len(idxs) * (N_DEV - 1),))] * 2 + [pltpu.SemaphoreType.DMA((len(idxs),))]
    out = pl.pallas_call(
        body, name=name,
        out_shape=tuple(sem_shapes) + tuple(land_shapes) + (jax.ShapeDtypeStruct((8, 128), F32),),
        in_specs=[HBM] * n,
        out_specs=tuple([SEM] * (3 * ng) + [HBM] * ni + [pl.BlockSpec(memory_space=pltpu.VMEM)]),
        compiler_params=pltpu.CompilerParams(has_side_effects=EFFECT),
    )(*[pltpu.with_memory_space_constraint(s, pltpu.HBM) for s in srcs])
    sems = [tuple(out[3 * g:3 * g + 3]) for g in range(ng)]
    return sems, list(out[3 * ng:3 * ng + ni]), out[-1]


def _exchange_wait(sems, srcs, items, lands, modes, groups, after, name):
    n, ni, ng = len(srcs), len(items), len(groups)

    def body(*refs):
        ins, land_refs = refs[:n], refs[n:n + ni]
        sem_refs = refs[n + ni:n + ni + 3 * ng]
        for g, idxs in enumerate(groups):
            for ai, it in enumerate(idxs):
                src = _item_src(ins, items[it])
                _own_copy(src, land_refs[it], modes[it], sem_refs[3 * g + 2], ai).wait()
                for k in range(1, N_DEV):
                    cp = _exchange_copy(src, land_refs[it], modes[it], sem_refs[3 * g], sem_refs[3 * g + 1], ai, k, False)
                    cp.wait_send()
                    cp.wait_recv()

    flat_sems = [s for trio in sems for s in trio]
    afters = list(after) if isinstance(after, (list, tuple)) else [after]
    out = pl.pallas_call(
        body, name=name,
        out_shape=tuple(pltpu.HBM(l.shape, l.dtype) for l in lands),
        in_specs=[HBM] * (n + ni) + [SEM] * (3 * ng) + [ANY] * len(afters),
        out_specs=tuple([HBM] * ni),
        input_output_aliases={n + i: i for i in range(ni)},
        compiler_params=pltpu.CompilerParams(has_side_effects=EFFECT),
    )(*srcs, *lands, *flat_sems, *afters)
    return list(out)


def _slot_sum_call(landed):
    n = len(landed)

    def body(*refs):
        for src, dst in zip(refs[:n], refs[n:]):
            tot = src[0]
            for j in range(1, N_DEV):
                tot = tot + src[j]
            dst[...] = tot

    vm = pl.BlockSpec(memory_space=pltpu.VMEM)
    return pl.pallas_call(
        body, name="small_grads_sum", in_specs=[vm] * n, out_specs=[vm] * n,
        out_shape=[jax.ShapeDtypeStruct(a.shape[1:], F32) for a in landed],
        compiler_params=pltpu.CompilerParams(vmem_limit_bytes=V7X_VMEM_LIMIT),
    )(*landed)


def _adamw(w, g, m, v):
    m = ADAM_B1 * m + (1.0 - ADAM_B1) * g
    v = ADAM_B2 * v + (1.0 - ADAM_B2) * (g * g)
    m_hat = m / (1.0 - ADAM_B1 ** ADAM_STEP)
    v_hat = v / (1.0 - ADAM_B2 ** ADAM_STEP)
    delta = -ADAM_LR * (m_hat / (jnp.sqrt(v_hat) + ADAM_EPS) + ADAM_WD * w)
    return delta, m, v


def _reduce_adamw_call(recvs, w, m, v, name):
    nl = len(recvs)
    _, r, c = recvs[0].shape
    rb = 256 if r % 256 == 0 else r
    nb = r // rb

    def body(*refs):
        recv_refs = refs[:nl]
        w_ref, m_ref, v_ref, g_ref, d_ref, nm_ref, nv_ref = refs[nl:]
        for l in range(nl):
            @pl.when(pl.program_id(0) == l)
            def _(l=l):
                g = recv_refs[l][0].astype(F32)
                for j in range(1, N_DEV):
                    g = g + recv_refs[l][j].astype(F32)
                delta, nm, nv = _adamw(w_ref[0], g, m_ref[0], v_ref[0])
                g_ref[0] = g
                d_ref[0] = delta
                nm_ref[0] = nm
                nv_ref[0] = nv

    def recv_spec(l):
        return pl.BlockSpec((N_DEV, rb, c), lambda lg, i: (0, jnp.where(lg == l, i, jnp.where(lg < l, 0, nb - 1)), 0))

    blk = pl.BlockSpec((1, rb, c), lambda lg, i: (lg, i, 0))
    shp = jax.ShapeDtypeStruct((nl, r, c), F32)
    return pl.pallas_call(
        body, name=name, grid=(nl, nb),
        in_specs=[recv_spec(l) for l in range(nl)] + [blk, blk, blk],
        out_specs=[blk, blk, blk, blk], out_shape=[shp, shp, shp, shp],
        compiler_params=pltpu.CompilerParams(dimension_semantics=("arbitrary", "arbitrary"), vmem_limit_bytes=V7X_VMEM_LIMIT),
    )(*recvs, w, m, v)


def _small_adamw_call(w, g, m, v):
    def body(w_ref, g_ref, m_ref, v_ref, d_ref, nm_ref, nv_ref):
        delta, nm, nv = _adamw(w_ref[...], g_ref[...], m_ref[...], v_ref[...])
        d_ref[...] = delta
        nm_ref[...] = nm
        nv_ref[...] = nv

    shp = jax.ShapeDtypeStruct(w.shape, F32)
    vm = pl.BlockSpec(memory_space=pltpu.VMEM)
    return pl.pallas_call(body, name="small_adamw", in_specs=[vm] * 4, out_specs=[vm] * 3, out_shape=[shp, shp, shp],
                          compiler_params=pltpu.CompilerParams(vmem_limit_bytes=V7X_VMEM_LIMIT))(w, g, m, v)


_GATHER_MODE = dict(w_in="block", w_out="block", w_ff1="cols_in", w_ff2="block", w_ple_gate="block", w_ple_proj="cols_in")
_SCATTER_MODE = dict(w_in="slot", w_out="slot", w_ff1="cols_out", w_ff2="slot", w_ple_gate="slot", w_ple_proj="cols_out")
_GROUP_A = ("w_in", "w_out")
_GROUP_B = ("w_ff1", "w_ff2", "w_ple_gate", "w_ple_proj")


def _gathered_full(k, landed):
    if _GATHER_MODE[k] == "cols_in":
        return landed
    n, r, c = landed.shape
    return landed.reshape(n * r, c)


def _grad_send(k, g):
    if _SCATTER_MODE[k] == "cols_out":
        return g
    r8, c = g.shape
    return g.reshape(N_DEV, r8 // N_DEV, c)


_SMALL_ORDER = ("norm_mix_g", "sgu_w", "sgu_b", "sgu_ln_g", "sgu_ln_b", "conv_w", "pool_w", "pool_scale",
                "norm_ff_g", "norm_ple_g", "final_g")


def _pack_small(d, order=_SMALL_ORDER):
    pieces, layout = [], []
    for k in order:
        flat = d[k].reshape(-1)
        n = flat.shape[0]
        pad = (-n) % 128
        pieces.append(jnp.pad(flat, (0, pad)))
        layout.append((k, d[k].shape, n, n + pad))
    flat = jnp.concatenate(pieces)
    pad = (-flat.shape[0]) % 1024
    return jnp.pad(flat, (0, pad)).reshape(-1, 128), layout


def _unpack_small(buf, layout):
    flat = buf.reshape(-1)
    out, off = {}, 0
    for k, shape, n, padded in layout:
        out[k] = flat[off:off + n].reshape(shape)
        off += padded
    return out


def kernel(x, p, norm_mix_g, w_in, sgu_w, sgu_b, sgu_ln_g, sgu_ln_b, conv_w, pool_w, pool_scale, w_out, norm_ff_g, w_ff1, w_ff2, norm_ple_g, w_ple_gate, w_ple_proj, final_g, loss_target, m_norm_mix_g, m_w_in, m_sgu_w, m_sgu_b, m_sgu_ln_g, m_sgu_ln_b, m_conv_w, m_pool_w, m_pool_scale, m_w_out, m_norm_ff_g, m_w_ff1, m_w_ff2, m_norm_ple_g, m_w_ple_gate, m_w_ple_proj, m_final_g, v_norm_mix_g, v_w_in, v_sgu_w, v_sgu_b, v_sgu_ln_g, v_sgu_ln_b, v_conv_w, v_pool_w, v_pool_scale, v_w_out, v_norm_ff_g, v_w_ff1, v_w_ff2, v_norm_ple_g, v_w_ple_gate, v_w_ple_proj, v_final_g):
    t = x.shape[1]
    xc, yc_, cc = _coords()
    me = 4 * xc + 2 * yc_ + cc
    tm = lambda want: min(want, t)

    shard_names = _GROUP_A + _GROUP_B
    swap = lambda a: jnp.transpose(a, (0, 2, 1))
    shard = dict(w_in=swap(w_in), w_out=w_out, w_ff1=w_ff1, w_ff2=w_ff2, w_ple_gate=w_ple_gate, w_ple_proj=w_ple_proj)
    conv_pad = jnp.zeros((16, 128), F32).at[0:DEPTH * 3, 0:D_B // N_DEV].set(conv_w.reshape(DEPTH * 3, D_B // N_DEV))
    ag_srcs = [shard[k].astype(BF16) for k in shard_names] + [conv_pad]
    ag_items, ag_modes, ag_groups = [], [], []
    for l in range(DEPTH):
        for names in (_GROUP_A, _GROUP_B):
            ag_groups.append(list(range(len(ag_items), len(ag_items) + len(names))))
            ag_items += [(shard_names.index(k), l) for k in names]
            ag_modes += [_GATHER_MODE[k] for k in names]
            if l == 0 and names is _GROUP_A:
                ag_groups[-1].append(len(ag_items))
                ag_items.append((len(shard_names), None))
                ag_modes.append("block")
    ag_sems, ag_lands, _ = _exchange_start(ag_srcs, ag_items, ag_modes, ag_groups, "weights_gather_start")

    def gathered(l, which, after):
        idxs = ag_groups[2 * l + which]
        landed = _exchange_wait([ag_sems[2 * l + which]], ag_srcs, [ag_items[i] for i in idxs], [ag_lands[i] for i in idxs],
                                [ag_modes[i] for i in idxs], [list(range(len(idxs)))], after, f"weights_gather_wait_{l}_{which}")
        full = {k: _gathered_full(k, got) for k, got in zip((_GROUP_A, _GROUP_B)[which], landed)}
        if l == 0 and which == 0:
            full["conv_w"] = jnp.transpose(landed[-1][:, 0:DEPTH * 3, 0:D_B // N_DEV].reshape(N_DEV, DEPTH, 3, D_B // N_DEV),
                                           (1, 2, 0, 3)).reshape(DEPTH, 3, D_B)
        return full

    idx = jnp.arange(D_A)
    pmat = ((idx[:, None] // 64) == (idx[None, :] // 64)).astype(BF16) * (1.0 / 64.0)
    pmat = pmat.astype(BF16)
    tril = jnp.tril(jnp.ones((CHUNK, CHUNK), F32))
    trilcat = jnp.concatenate([tril, tril], axis=1)
    headsel = ((idx[:, None] // 64) == jnp.arange(CHUNK)[None, :]).astype(BF16)
    row = lambda a: a.reshape(1, -1)

    def mix_params(l):
        wm = sgu_w[l] * tril[None]
        wcat = jnp.stack([jnp.concatenate([wm[2 * j], wm[2 * j + 1]], axis=1) for j in range(3)]).astype(BF16)
        wtcat = jnp.stack([jnp.concatenate([wm[2 * j].T, wm[2 * j + 1].T], axis=1) for j in range(3)]).astype(BF16)
        bmat = jnp.repeat(sgu_b[l].T, 64, axis=1)
        bd = jnp.zeros((D_C, D_C), F32)
        for gi in range(4):
            bd = bd.at[gi * 64:(gi + 1) * 64, gi * 64:(gi + 1) * 64].set(pool_w[l, gi])
        mp = dict(pmat=pmat, ln_g=row(sgu_ln_g[l]), ln_b=row(sgu_ln_b[l]), wcat=wcat, bmat=bmat, conv_w=conv_full[l],
                  bd=bd.astype(BF16), pool_scale=row(pool_scale[l]))
        return mp, wtcat

    xs = x.reshape(t, D)
    p_layers = p.reshape(DEPTH, t, D_PLE)
    saved, full_w = [], []
    conv_full = None
    for l in range(DEPTH):
        wa = gathered(l, 0, xs)
        if l == 0:
            conv_full = wa["conv_w"]
        mp, _ = mix_params(l)
        x1, proj, sgu_saved = _mix_fwd_call(xs, row(norm_mix_g[l]), wa["w_in"], wa["w_out"], mp, tm(TM_MIX_FWD), tm(RUN_MIX_FWD), l)
        wb = gathered(l, 1, x1)
        x2, x3, r, gate = _ffn_fwd_call(x1, p_layers, row(norm_ff_g[l]), row(norm_ple_g[l]), wb["w_ff1"], wb["w_ff2"],
                                        wb["w_ple_gate"], wb["w_ple_proj"], tm(TM_FFN_FWD), l)
        saved.append((xs, proj, sgu_saved, x1, r, x2, gate))
        full_w.append({**wa, **wb})
        xs = x3

    sq, dx, dfinal = _loss_call(xs, loss_target.reshape(t, D), row(final_g), tm(TM_LOSS))
    loss = lax.psum(jnp.sum(sq) * (0.5 / D), ("x", "y", "c"))

    layer_keys = tuple(k for k in _SMALL_ORDER if k != "final_g")
    small = {k: [None] * DEPTH for k in layer_keys}
    small_ex, small_layout = [None] * DEPTH, None
    ex = {}
    token = None

    def after_start(g):
        return g if token is None else g + token[0:1, 0:1]

    def start_exchange(l, which, grads):
        names = (_GROUP_A, _GROUP_B)[which]
        sends = [_grad_send(k, grads[k]) for k in names]
        sems, lands, tok = _exchange_start(sends, [(i, None) for i in range(len(names))], [_SCATTER_MODE[k] for k in names],
                                           [list(range(len(names)))], f"grads_exchange_start_{l}_{which}")
        ex[(l, which)] = (sems[0], sends, lands)
        return tok

    for l in reversed(range(DEPTH)):
        x0, proj, sgu_saved, x1, r, x2, gate = saved[l]
        fw = full_w[l]
        dx2, dgple, dwg, dwp = _ple_bwd_call(dx, x2, gate, p_layers, after_start(row(norm_ple_g[l])), fw["w_ple_gate"], fw["w_ple_proj"],
                                             tm(TM_PLE_BWD), l)
        da, dw2 = _ffn_bwd_hidden_call(dx2, r, fw["w_ff2"], tm(TM_FFN_BWD), l)
        dx1, dgff, dw1 = _ffn_bwd_input_call(da, x1, dx2, row(norm_ff_g[l]), fw["w_ff1"], tm(TM_FFN_BWD), l)
        token = start_exchange(l, 1, dict(w_ff1=dw1, w_ff2=dw2, w_ple_gate=dwg, w_ple_proj=dwp))
        mp, wtcat = mix_params(l)
        (dx, dgmix, dwcat, dsb, dlng, dlnb, dconv, dbd, dscale, dwin, dwout) = _mix_bwd_call(
            dx1, x0, proj, sgu_saved, after_start(row(norm_mix_g[l])), fw["w_in"], fw["w_out"], mp, wtcat, trilcat, headsel, tm(TM_MIX_BWD), tm(RUN_MIX_BWD), l)
        token = start_exchange(l, 0, dict(w_in=dwin, w_out=dwout))
        small["norm_mix_g"][l] = dgmix[0]
        small["sgu_w"][l] = jnp.stack([dwcat[h // 2][:, (h % 2) * CHUNK:(h % 2 + 1) * CHUNK] for h in range(6)])
        small["sgu_b"][l] = dsb[:, 0:6].T
        small["sgu_ln_g"][l], small["sgu_ln_b"][l] = dlng[0], dlnb[0]
        small["conv_w"][l] = dconv
        small["pool_w"][l] = jnp.stack([dbd[gi * 64:(gi + 1) * 64, gi * 64:(gi + 1) * 64] for gi in range(4)])
        small["pool_scale"][l] = dscale[0]
        small["norm_ff_g"][l], small["norm_ple_g"][l] = dgff[0], dgple[0]
        layer_small = {k: small[k][l] for k in layer_keys}
        layer_small["final_g"] = dfinal[0] if l == DEPTH - 1 else jnp.zeros_like(dfinal[0])
        sbuf, small_layout = _pack_small(layer_small, layer_keys + ("final_g",))
        ssems, slands, stok = _exchange_start([sbuf], [(0, None)], ["block"], [[0]], f"small_grads_start_{l}")
        small_ex[l] = (ssems[0], sbuf, slands[0])
        token = token + stok
    grad_x = dx.reshape(1, t, D)

    state = dict(w_in=(swap(w_in), swap(m_w_in), swap(v_w_in)), w_out=(w_out, m_w_out, v_w_out), w_ff1=(w_ff1, m_w_ff1, v_w_ff1),
                 w_ff2=(w_ff2, m_w_ff2, v_w_ff2), w_ple_gate=(w_ple_gate, m_w_ple_gate, v_w_ple_gate),
                 w_ple_proj=(w_ple_proj, m_w_ple_proj, v_w_ple_proj))
    res = {}

    def finish_group(which, after):
        names = (_GROUP_A, _GROUP_B)[which]
        n = len(names)
        sems = [ex[(l, which)][0] for l in range(DEPTH)]
        sends = [s_ for l in range(DEPTH) for s_ in ex[(l, which)][1]]
        lands = [a_ for l in range(DEPTH) for a_ in ex[(l, which)][2]]
        landed = _exchange_wait(sems, sends, [(i, None) for i in range(DEPTH * n)], lands, [_SCATTER_MODE[k] for k in names] * DEPTH,
                                [list(range(l * n, (l + 1) * n)) for l in range(DEPTH)], after, f"grads_exchange_wait_{which}")
        for i, k in enumerate(names):
            res[k] = _reduce_adamw_call([landed[l * n + i] for l in range(DEPTH)], *state[k], "reduce_adamw_" + k)

    finish_group(1, dx)

    landed = _exchange_wait([small_ex[l][0] for l in range(DEPTH)], [small_ex[l][1] for l in range(DEPTH)],
                            [(l, None) for l in range(DEPTH)], [small_ex[l][2] for l in range(DEPTH)], ["block"] * DEPTH,
                            [[l] for l in range(DEPTH)], res[_GROUP_B[-1]][0], "small_grads_wait")
    sums = [_unpack_small(b_, small_layout) for b_ in _slot_sum_call(landed)]
    gs = {k: jnp.stack([sums[l][k] for l in range(DEPTH)]) for k in layer_keys}
    gs["final_g"] = sums[DEPTH - 1]["final_g"]
    conv_cols = lambda a: lax.dynamic_slice_in_dim(a, me * (D_B // N_DEV), D_B // N_DEV, axis=2)
    pad_conv = lambda a: jnp.zeros((DEPTH, 3, D_B), F32).at[:, :, 0:D_B // N_DEV].set(a)
    small_w = dict(norm_mix_g=norm_mix_g, sgu_w=sgu_w, sgu_b=sgu_b, sgu_ln_g=sgu_ln_g, sgu_ln_b=sgu_ln_b, conv_w=pad_conv(conv_w),
                   pool_w=pool_w, pool_scale=pool_scale, norm_ff_g=norm_ff_g, norm_ple_g=norm_ple_g, final_g=final_g)
    small_m = dict(norm_mix_g=m_norm_mix_g, sgu_w=m_sgu_w, sgu_b=m_sgu_b, sgu_ln_g=m_sgu_ln_g, sgu_ln_b=m_sgu_ln_b,
                   conv_w=pad_conv(m_conv_w), pool_w=m_pool_w, pool_scale=m_pool_scale, norm_ff_g=m_norm_ff_g,
                   norm_ple_g=m_norm_ple_g, final_g=m_final_g)
    small_v = dict(norm_mix_g=v_norm_mix_g, sgu_w=v_sgu_w, sgu_b=v_sgu_b, sgu_ln_g=v_sgu_ln_g, sgu_ln_b=v_sgu_ln_b,
                   conv_w=pad_conv(v_conv_w), pool_w=v_pool_w, pool_scale=v_pool_scale,
                   norm_ff_g=v_norm_ff_g, norm_ple_g=v_norm_ple_g, final_g=v_final_g)
    gs_local = dict(gs)
    gs_local["conv_w"] = pad_conv(conv_cols(gs["conv_w"]))
    g_loc, layout = _pack_small(gs_local)
    wbuf, _ = _pack_small(small_w)
    mbuf, _ = _pack_small(small_m)
    vbuf, _ = _pack_small(small_v)
    dbuf, nmbuf, nvbuf = _small_adamw_call(wbuf, g_loc, mbuf, vbuf)
    sd, sm, sv = _unpack_small(dbuf, layout), _unpack_small(nmbuf, layout), _unpack_small(nvbuf, layout)
    unconv = lambda a: a[:, :, 0:D_B // N_DEV]
    for dct in (gs_local, sd, sm, sv):
        dct["conv_w"] = unconv(dct["conv_w"])

    finish_group(0, [res[_GROUP_B[-1]][0], dbuf])

    order = ["norm_mix_g", "w_in", "sgu_w", "sgu_b", "sgu_ln_g", "sgu_ln_b", "conv_w", "pool_w", "pool_scale", "w_out",
             "norm_ff_g", "w_ff1", "w_ff2", "norm_ple_g", "w_ple_gate", "w_ple_proj", "final_g"]
    outs = [loss, grad_x]
    for which in range(4):
        for k in order:
            if k in res:
                outs.append(swap(res[k][which]) if k == "w_in" else res[k][which])
            else:
                outs.append((gs_local, sd, sm, sv)[which][k])
    return tuple(outs)
```

```python
import functools
import math

import jax
import jax.numpy as jnp
from jax import lax
from jax.experimental import pallas as pl
from jax.experimental.pallas import tpu as pltpu

F32 = jnp.float32
BF16 = jnp.bfloat16

D = 1024
D_IN = 2176
D_A = 384
D_B = 384
D_C = 256
D_FF = 4096
D_PLE = 256
DEPTH = 4
CHUNK = 128
HALO = 16
FF_BLK = 1024
N_DEV = 8
RMS_EPS = 1e-6
LN_EPS = 1e-5
ADAM_LR = 0.001
ADAM_B1 = 0.9
ADAM_B2 = 0.999
ADAM_EPS = 1e-08
ADAM_WD = 0.01
ADAM_STEP = 10

TM_MIX_FWD = 1024
TM_FFN_FWD = 512
TM_LOSS = 512
TM_PLE_BWD = 1024
TM_FFN_BWD = 512
TM_MIX_BWD = 512
RUN_MIX_FWD = 1024
RUN_MIX_BWD = 512
V7X_VMEM_LIMIT = 60000 * 1024

ANY = pl.BlockSpec(memory_space=pl.ANY)
HBM = pl.BlockSpec(memory_space=pltpu.HBM)
SEM = pl.BlockSpec(memory_space=pltpu.SEMAPHORE)
MESH = pl.DeviceIdType.MESH


def _params(vmem=V7X_VMEM_LIMIT):
    return pltpu.CompilerParams(dimension_semantics=("arbitrary",), vmem_limit_bytes=vmem)


def _in_hbm(a):
    return pltpu.with_memory_space_constraint(a, pltpu.HBM)


def _full(shape):
    nd = len(shape)
    return pl.BlockSpec(shape, lambda i: (0,) * nd)


def _rows(tm, cols):
    return pl.BlockSpec((tm, cols), lambda i: (i, 0))


def _layer_rows(layer, tm, cols):
    return pl.BlockSpec((None, tm, cols), lambda i: (layer, i, 0))


def _mm(a, b):
    return jnp.dot(a, b, preferred_element_type=F32)


def _mm_nt(a, b):
    return lax.dot_general(a, b, (((1,), (1,)), ((), ())), preferred_element_type=F32)


def _mm_tn(a, b):
    return lax.dot_general(a, b, (((0,), (0,)), ((), ())), preferred_element_type=F32)


def _gelu_and_grad(x):
    ax = jnp.abs(x) * (1.0 / math.sqrt(2.0))
    t = 1.0 / (1.0 + 0.3275911 * ax)
    poly = t * (0.254829592 + t * (-0.284496736 + t * (1.421413741 + t * (-1.453152027 + t * 1.061405429))))
    e = jnp.exp(-0.5 * x * x)
    half = 0.5 * poly * e
    cdf = jnp.where(x < 0, half, 1.0 - half)
    return x * cdf, cdf + x * (e * (1.0 / math.sqrt(2.0 * math.pi)))


def _rms(x, g):
    rstd = lax.rsqrt(jnp.mean(x * x, axis=-1, keepdims=True) + RMS_EPS)
    xhat = x * rstd
    return xhat * g, xhat, rstd


RSTD_LANES = 128


def _save_norm(xhat, rstd, xhat_ref, rstd_ref):
    xhat_ref[...] = xhat.astype(BF16)
    rstd_ref[...] = jnp.broadcast_to(rstd, rstd_ref.shape)


def _saved_norm(xhat_ref, rstd_ref, g):
    xhat = xhat_ref[...].astype(F32)
    return xhat * g, xhat, rstd_ref[:, 0:1]


def _rms_bwd(dy, g, xhat, rstd):
    dg = jnp.sum(dy * xhat, axis=0, keepdims=True)
    dxh = dy * g
    dx = rstd * (dxh - xhat * jnp.mean(dxh * xhat, axis=-1, keepdims=True))
    return dx, dg


def _shift_down(ext, k):
    return pltpu.roll(ext, k, 0)[HALO:, :]


def _shift_up(ext, k):
    n = ext.shape[0]
    return pltpu.roll(ext, n - k, 0)[: n - HALO, :]


def _pool_select(s2, s4, s8, s16):
    lane = lax.broadcasted_iota(jnp.int32, s2.shape, 1)
    return jnp.where(lane < 64, s2, jnp.where(lane < 128, s4, jnp.where(lane < 192, s8, s16)))


def _pool_inv_count(tile_start, tm):
    pos = lax.broadcasted_iota(jnp.int32, (tm, D_C), 0) + tile_start + 1
    lane = lax.broadcasted_iota(jnp.int32, (tm, D_C), 1)
    win = jnp.where(lane < 64, 2, jnp.where(lane < 128, 4, jnp.where(lane < 192, 8, 16)))
    return 1.0 / jnp.minimum(pos, win).astype(F32)


def _head_halves(a):
    lane = lax.broadcasted_iota(jnp.int32, a.shape, 1)
    even = (lane & 64) == 0
    return jnp.where(even, a, 0.0).astype(BF16), jnp.where(even, 0.0, a).astype(BF16)


def _head_stack(lo, hi, j, nch):
    return jnp.concatenate(
        [jnp.concatenate([lo[c * CHUNK:(c + 1) * CHUNK, j * 128:(j + 1) * 128], hi[c * CHUNK:(c + 1) * CHUNK, j * 128:(j + 1) * 128]], axis=0)
         for c in range(nch)], axis=1)


def _chunks_to_lanes(a, j, nch):
    return jnp.concatenate([a[c * CHUNK:(c + 1) * CHUNK, j * 128:(j + 1) * 128] for c in range(nch)], axis=1)


def _lanes_to_chunks(o, nch):
    return jnp.concatenate([o[:, c * CHUNK:(c + 1) * CHUNK] for c in range(nch)], axis=0)


def _loads(pairs, sem):
    return [pltpu.make_async_copy(src, dst, sem.at[n]) for n, (src, dst) in enumerate(pairs)]


def _load_all(loads):
    for cp in loads:
        cp.start()
    for cp in loads:
        cp.wait()


def _stage_bf16(acc, stage):
    rows = acc.shape[0]
    strip = min(rows, 128)

    @pl.loop(0, rows // strip)
    def _(n):
        sl = pl.ds(pl.multiple_of(n * strip, strip), strip)
        stage[sl, :] = acc[sl, :].astype(BF16)


N_SAVED = 5


def _mixers_fwd(pf, halo_hc, halo_zc, tile_start, prm, saved=None):
    tm = pf.shape[0]
    nch = tm // CHUNK
    u, v = pf[:, 0:D_A], pf[:, D_A:2 * D_A]
    zb, gb, gc = pf[:, 768:1152], pf[:, 1152:1536], pf[:, 1536:1920]
    zc = pf[:, 1920:2176]
    r = {}
    if saved is None:
        gu, r["dgelu_u"] = _gelu_and_grad(u)
        gv, r["dgelu_v"] = _gelu_and_grad(v)
        pmat = prm["pmat"][...]
        mu = _mm(gv.astype(BF16), pmat)
        dv = gv - mu
        var = _mm((dv * dv).astype(BF16), pmat)
        rstd = lax.rsqrt(var + LN_EPS)
        xh = dv * rstd
        r["saved"] = jnp.concatenate([gu, r["dgelu_u"], r["dgelu_v"], xh, rstd], axis=1).astype(BF16)
    else:
        gu, r["dgelu_u"], r["dgelu_v"], xh, rstd = (saved[:, n * D_A:(n + 1) * D_A] for n in range(N_SAVED))
    vlo, vhi = _head_halves(xh * prm["ln_g"][...] + prm["ln_b"][...])
    cols, v2s = [], []
    for j in range(3):
        v2 = _head_stack(vlo, vhi, j, nch)
        v2s.append(v2)
        cols.append(_lanes_to_chunks(_mm(prm["wcat"][j], v2), nch))
    mixed = jnp.concatenate(cols, axis=1) + jnp.concatenate([prm["bmat"][...]] * nch, axis=0)
    ya = gu * mixed
    r.update(gu=gu, mixed=mixed, v2s=v2s, xh=xh, ln_rstd=rstd)
    w0, w1, w2 = prm["conv_w"][0:1, :], prm["conv_w"][1:2, :], prm["conv_w"][2:3, :]
    hc = gc * zb
    ext = jnp.concatenate([halo_hc, hc], axis=0)
    h1, h2 = _shift_down(ext, 1), _shift_down(ext, 2)
    yc = w2 * hc + w1 * h1 + w0 * h2
    yb = gb * yc
    r.update(hc=hc, h1=h1, h2=h2, yc=yc, zb=zb, gb=gb, gc=gc, w0=w0, w1=w1, w2=w2)
    ext = jnp.concatenate([halo_zc, zc], axis=0)
    s2 = ext + pltpu.roll(ext, 1, 0)
    s4 = s2 + pltpu.roll(s2, 2, 0)
    s8 = s4 + pltpu.roll(s4, 4, 0)
    s16 = s8 + pltpu.roll(s8, 8, 0)
    inv = _pool_inv_count(tile_start, tm)
    pooled = _pool_select(s2, s4, s8, s16)[HALO:, :] * inv - zc
    pooledb = pooled.astype(BF16)
    pm = _mm(pooledb, prm["bd"][...])
    scale = prm["pool_scale"][...]
    ycm = pm * scale
    r.update(inv=inv, pooledb=pooledb, pm=pm, scale=scale, zc=zc)
    r["ycat"] = jnp.concatenate([ya, yb, ycm], axis=1)
    return r


_MIX_PARAM_NAMES = ("pmat", "ln_g", "ln_b", "wcat", "bmat", "conv_w", "bd", "pool_scale")


def _mix_param_specs():
    return [_full((D_A, D_A)), _full((1, D_A)), _full((1, D_A)), _full((3, CHUNK, 2 * CHUNK)), _full((CHUNK, D_A)),
            _full((3, D_B)), _full((D_C, D_C)), _full((1, D_C))]


def _mix_fwd_call(x, g_mix, w_in_t, w_out, mp, tm, run, layer):
    t = x.shape[0]
    nt = t // tm

    def body(x_ref, g_ref, pmat, ln_g, ln_b, wcat, bmat, conv_w, bd, pool_scale, win_hbm, wout_hbm,
             x1_ref, proj_ref, saved_ref, xhat_ref, rstd_ref, win_s, wout_s, halo_hc, halo_zc, load_sem):
        i = pl.program_id(0)
        loads = _loads([(win_hbm, win_s), (wout_hbm, wout_s)], load_sem)

        @pl.when(i == 0)
        def _():
            _load_all(loads)
            halo_hc[...] = jnp.zeros_like(halo_hc)
            halo_zc[...] = jnp.zeros_like(halo_zc)

        prm = dict(pmat=pmat, ln_g=ln_g, ln_b=ln_b, wcat=wcat, bmat=bmat, conv_w=conv_w, bd=bd, pool_scale=pool_scale)
        xv = x_ref[...]
        h, xhat, rstd = _rms(xv, g_ref[...])
        _save_norm(xhat, rstd, xhat_ref, rstd_ref)
        pf = _mm_nt(h.astype(BF16), win_s[...])
        proj_ref[...] = pf.astype(BF16)
        hh, hz = halo_hc[...], halo_zc[...]
        parts = []
        for c in range(tm // run):
            r = _mixers_fwd(pf[c * run:(c + 1) * run, :], hh, hz, i * tm + c * run, prm)
            hh, hz = r["hc"][run - HALO:, :], r["zc"][run - HALO:, :]
            parts.append(r["ycat"].astype(BF16))
            saved_ref[c * run:(c + 1) * run, :] = r["saved"]
        halo_hc[...] = hh
        halo_zc[...] = hz
        x1_ref[...] = xv + _mm(jnp.concatenate(parts, axis=0), wout_s[...])

    return pl.pallas_call(
        body, name=f"mix_fwd_{layer}", grid=(nt,),
        in_specs=[_rows(tm, D), _full((1, D))] + _mix_param_specs() + [HBM, HBM],
        out_specs=[_rows(tm, D), _rows(tm, D_IN), _rows(tm, N_SAVED * D_A), _rows(tm, D), _rows(tm, RSTD_LANES)],
        out_shape=[jax.ShapeDtypeStruct((t, D), F32), jax.ShapeDtypeStruct((t, D_IN), BF16),
                   jax.ShapeDtypeStruct((t, N_SAVED * D_A), BF16), jax.ShapeDtypeStruct((t, D), BF16),
                   jax.ShapeDtypeStruct((t, RSTD_LANES), F32)],
        scratch_shapes=[pltpu.VMEM((D_IN, D), BF16), pltpu.VMEM((D, D), BF16),
                        pltpu.VMEM((HALO, D_B), F32), pltpu.VMEM((HALO, D_C), F32), pltpu.SemaphoreType.DMA((2,))],
        compiler_params=_params(),
    )(x, g_mix, *[mp[k] for k in _MIX_PARAM_NAMES], _in_hbm(w_in_t), _in_hbm(w_out))


def _ffn_fwd_call(x1, p, g_ff, g_ple, w1, w2, wg, wp, tm, layer):
    t = x1.shape[0]
    nt = t // tm

    def body(x1_ref, p_ref, gff_ref, gple_ref, w1_hbm, w2_hbm, wg_hbm, wp_hbm,
             x3_ref, r_ref, gate_ref, xh1_ref, rs1_ref, xh2_ref, rs2_ref, w1_s, w2_s, wg_s, wp_s, load_sem):
        i = pl.program_id(0)
        loads = _loads([(w1_hbm, w1_s), (w2_hbm, w2_s), (wg_hbm, wg_s), (wp_hbm, wp_s)], load_sem)

        @pl.when(i == 0)
        def _():
            _load_all(loads)

        x1v = x1_ref[...]
        h2, xhat, rstd = _rms(x1v, gff_ref[...])
        _save_norm(xhat, rstd, xh1_ref, rs1_ref)
        h2b = h2.astype(BF16)
        acc = x1v
        for j in range(D_FF // FF_BLK):
            blk = slice(j * FF_BLK, (j + 1) * FF_BLK)
            rj = jnp.maximum(_mm(h2b, w1_s[:, blk]), 0.0)
            r_ref[:, blk] = rj.astype(BF16)
            acc = acc + _mm((rj * rj).astype(BF16), w2_s[blk, :])
        n3, xhat, rstd = _rms(acc, gple_ref[...])
        _save_norm(xhat, rstd, xh2_ref, rs2_ref)
        gate = jax.nn.sigmoid(_mm(n3.astype(BF16), wg_s[...]))
        gate_ref[...] = gate.astype(BF16)
        pp = _mm(p_ref[...].astype(BF16), wp_s[...])
        x3_ref[...] = acc + pp * gate

    return pl.pallas_call(
        body, name=f"ffn_fwd_{layer}", grid=(nt,),
        in_specs=[_rows(tm, D), _layer_rows(layer, tm, D_PLE), _full((1, D)), _full((1, D)), HBM, HBM, HBM, HBM],
        out_specs=[_rows(tm, D), _rows(tm, D_FF), _rows(tm, D), _rows(tm, D), _rows(tm, RSTD_LANES), _rows(tm, D),
                   _rows(tm, RSTD_LANES)],
        out_shape=[jax.ShapeDtypeStruct((t, D), F32), jax.ShapeDtypeStruct((t, D_FF), BF16), jax.ShapeDtypeStruct((t, D), BF16),
                   jax.ShapeDtypeStruct((t, D), BF16), jax.ShapeDtypeStruct((t, RSTD_LANES), F32),
                   jax.ShapeDtypeStruct((t, D), BF16), jax.ShapeDtypeStruct((t, RSTD_LANES), F32)],
        scratch_shapes=[pltpu.VMEM((D, D_FF), BF16), pltpu.VMEM((D_FF, D), BF16),
                        pltpu.VMEM((D, D), BF16), pltpu.VMEM((D_PLE, D), BF16), pltpu.SemaphoreType.DMA((4,))],
        compiler_params=_params(),
    )(x1, p, g_ff, g_ple, _in_hbm(w1), _in_hbm(w2), _in_hbm(wg), _in_hbm(wp))


def _loss_call(xl, target, final_g, tm):
    t = xl.shape[0]
    nt = t // tm

    def body(x_ref, t_ref, g_ref, sq_ref, dx_ref, dg_ref):
        i = pl.program_id(0)

        @pl.when(i == 0)
        def _():
            sq_ref[...] = jnp.zeros_like(sq_ref)
            dg_ref[...] = jnp.zeros_like(dg_ref)

        g = g_ref[...]
        y, xhat, rstd = _rms(x_ref[...], g)
        err = y - t_ref[...]
        sq_ref[...] += jnp.sum(err * err, axis=0, keepdims=True)
        dx, dg = _rms_bwd(err * (1.0 / D), g, xhat, rstd)
        dx_ref[...] = dx
        dg_ref[...] += dg

    return pl.pallas_call(
        body, name="loss_head", grid=(nt,),
        in_specs=[_rows(tm, D), _rows(tm, D), _full((1, D))],
        out_specs=[_full((1, D)), _rows(tm, D), _full((1, D))],
        out_shape=[jax.ShapeDtypeStruct((1, D), F32), jax.ShapeDtypeStruct((t, D), F32), jax.ShapeDtypeStruct((1, D), F32)],
        compiler_params=_params(),
    )(xl, target, final_g)


def _ple_bwd_call(dx3, xhat2, rstd2, gate, p, g_ple, wg, wp, after, tm, layer):
    t = dx3.shape[0]
    nt = t // tm

    def body(dx3_ref, xhat_ref, rstd_ref, gate_ref, p_ref, g_ref, wg_hbm, wp_hbm, after_ref,
             dx2_ref, dg_ref, dwg_hbm, dwp_hbm, wg_s, wp_s, dwg_acc, dwp_acc, load_sem):
        i = pl.program_id(0)
        loads = _loads([(wp_hbm, wp_s), (wg_hbm, wg_s)], load_sem)

        @pl.when(i == 0)
        def _():
            _load_all(loads)
            dwg_acc[...] = jnp.zeros_like(dwg_acc)
            dwp_acc[...] = jnp.zeros_like(dwp_acc)
            dg_ref[...] = jnp.zeros_like(dg_ref)

        g = g_ref[...]
        dx3v = dx3_ref[...]
        gatev = gate_ref[...].astype(F32)
        pb = p_ref[...].astype(BF16)
        pp = _mm(pb, wp_s[...])
        dwp_acc[...] += _mm_tn(pb, (dx3v * gatev).astype(BF16))
        dgpre = (dx3v * pp * gatev * (1.0 - gatev)).astype(BF16)
        n3, xhat, rstd = _saved_norm(xhat_ref, rstd_ref, g)
        dwg_acc[...] += _mm_tn(n3.astype(BF16), dgpre)
        dn3 = _mm_nt(dgpre, wg_s[...])
        dx, dg = _rms_bwd(dn3, g, xhat, rstd)
        dx2_ref[...] = dx3v + dx
        dg_ref[...] += dg

        @pl.when(i == nt - 1)
        def _():
            _stage_bf16(dwg_acc, wg_s)
            _stage_bf16(dwp_acc, wp_s)
            pltpu.sync_copy(wg_s, dwg_hbm)
            pltpu.sync_copy(wp_s, dwp_hbm)

    return pl.pallas_call(
        body, name=f"ple_bwd_{layer}", grid=(nt,),
        in_specs=[_rows(tm, D), _rows(tm, D), _rows(tm, RSTD_LANES), _rows(tm, D), _layer_rows(layer, tm, D_PLE), _full((1, D)),
                  HBM, HBM, ANY],
        out_specs=[_rows(tm, D), _full((1, D)), HBM, HBM],
        out_shape=[jax.ShapeDtypeStruct((t, D), F32), jax.ShapeDtypeStruct((1, D), F32),
                   pltpu.HBM((D, D), BF16), pltpu.HBM((D_PLE, D), BF16)],
        scratch_shapes=[pltpu.VMEM((D, D), BF16), pltpu.VMEM((D_PLE, D), BF16),
                        pltpu.VMEM((D, D), F32), pltpu.VMEM((D_PLE, D), F32), pltpu.SemaphoreType.DMA((2,))],
        compiler_params=_params(),
    )(dx3, xhat2, rstd2, gate, p, g_ple, _in_hbm(wg), _in_hbm(wp), after)


def _ffn_bwd_hidden_call(dx2, r, w2, tm, layer):
    t = dx2.shape[0]
    nt = t // tm

    def body(dx2_ref, r_ref, w2_hbm, da_ref, dw2_hbm, w2_s, dw2_acc, load_sem):
        i = pl.program_id(0)
        loads = _loads([(w2_hbm, w2_s)], load_sem)

        @pl.when(i == 0)
        def _():
            _load_all(loads)
            dw2_acc[...] = jnp.zeros_like(dw2_acc)

        dxb = dx2_ref[...].astype(BF16)
        for j in range(D_FF // FF_BLK):
            blk = slice(j * FF_BLK, (j + 1) * FF_BLK)
            rj = r_ref[:, blk].astype(F32)
            ds = _mm_nt(dxb, w2_s[blk, :])
            da_ref[:, blk] = (2.0 * rj * ds).astype(BF16)
            dw2_acc[blk, :] += _mm_tn((rj * rj).astype(BF16), dxb)

        @pl.when(i == nt - 1)
        def _():
            _stage_bf16(dw2_acc, w2_s)
            pltpu.sync_copy(w2_s, dw2_hbm)

    return pl.pallas_call(
        body, name=f"ffn_bwd_hidden_{layer}", grid=(nt,),
        in_specs=[_rows(tm, D), _rows(tm, D_FF), HBM],
        out_specs=[_rows(tm, D_FF), HBM],
        out_shape=[jax.ShapeDtypeStruct((t, D_FF), BF16), pltpu.HBM((D_FF, D), BF16)],
        scratch_shapes=[pltpu.VMEM((D_FF, D), BF16), pltpu.VMEM((D_FF, D), F32), pltpu.SemaphoreType.DMA((1,))],
        compiler_params=_params(),
    )(dx2, r, _in_hbm(w2))


def _ffn_bwd_input_call(da, xhat1, rstd1, dx2, g_ff, w1, tm, layer):
    t = dx2.shape[0]
    nt = t // tm

    def body(da_ref, xhat_ref, rstd_ref, dx2_ref, g_ref, w1_hbm, dx1_ref, dg_ref, dw1_hbm, w1_s, dw1_acc, load_sem):
        i = pl.program_id(0)
        loads = _loads([(w1_hbm, w1_s)], load_sem)

        @pl.when(i == 0)
        def _():
            _load_all(loads)
            dw1_acc[...] = jnp.zeros_like(dw1_acc)
            dg_ref[...] = jnp.zeros_like(dg_ref)

        g = g_ref[...]
        h2, xhat, rstd = _saved_norm(xhat_ref, rstd_ref, g)
        h2b = h2.astype(BF16)
        dh2 = jnp.zeros((tm, D), F32)
        for j in range(D_FF // FF_BLK):
            blk = slice(j * FF_BLK, (j + 1) * FF_BLK)
            daj = da_ref[:, blk]
            dh2 = dh2 + _mm_nt(daj, w1_s[:, blk])
            dw1_acc[:, blk] += _mm_tn(h2b, daj)
        dx, dg = _rms_bwd(dh2, g, xhat, rstd)
        dx1_ref[...] = dx2_ref[...] + dx
        dg_ref[...] += dg

        @pl.when(i == nt - 1)
        def _():
            _stage_bf16(dw1_acc, w1_s)
            pltpu.sync_copy(w1_s, dw1_hbm)

    return pl.pallas_call(
        body, name=f"ffn_bwd_input_{layer}", grid=(nt,),
        in_specs=[_rows(tm, D_FF), _rows(tm, D), _rows(tm, RSTD_LANES), _rows(tm, D), _full((1, D)), HBM],
        out_specs=[_rows(tm, D), _full((1, D)), HBM],
        out_shape=[jax.ShapeDtypeStruct((t, D), F32), jax.ShapeDtypeStruct((1, D), F32), pltpu.HBM((D, D_FF), BF16)],
        scratch_shapes=[pltpu.VMEM((D, D_FF), BF16), pltpu.VMEM((D, D_FF), F32), pltpu.SemaphoreType.DMA((1,))],
        compiler_params=_params(),
    )(da, xhat1, rstd1, dx2, g_ff, _in_hbm(w1))


def _mix_bwd_call(dx1, xhat0, rstd0, proj, saved, g_mix, w_in_t, w_out, mp, wtcat, trilcat, headsel, after, tm, run, layer):
    t = dx1.shape[0]
    nt = t // tm
    nrun = tm // run
    nch = run // CHUNK
    hb = tm // HALO

    def rev(i):
        return nt - 1 - i

    def body(dx1_ref, xhat_ref, rstd_ref, proj_ref, halo_ref, saved_ref, g_ref, pmat, ln_g, ln_b, wcat, bmat, conv_w, bd, pool_scale,
             wtcat_ref, tril_ref, sel_ref, win_hbm, wout_hbm, after_ref,
             dx_ref, dg_ref, dwcat_ref, dsb_ref, dlng_ref, dlnb_ref, dconv_ref, dbd_ref, dscale_ref, dwin_hbm, dwout_hbm,
             win_s, wout_s, dwin_acc, dwout_acc, dbm_acc, carry_yc, carry_q, load_sem):
        i = pl.program_id(0)
        ri = nt - 1 - i
        loads = _loads([(wout_hbm, wout_s), (win_hbm, win_s)], load_sem)

        @pl.when(i == 0)
        def _():
            _load_all(loads)
            for ref in (dwin_acc, dwout_acc, dbm_acc, carry_yc, carry_q, dg_ref, dwcat_ref, dlng_ref, dlnb_ref,
                        dconv_ref, dbd_ref, dscale_ref):
                ref[...] = jnp.zeros_like(ref)

        prm = dict(pmat=pmat, ln_g=ln_g, ln_b=ln_b, wcat=wcat, bmat=bmat, conv_w=conv_w, bd=bd, pool_scale=pool_scale)
        g = g_ref[...]
        h, xhat, rstd = _saved_norm(xhat_ref, rstd_ref, g)
        hb16 = h.astype(BF16)
        dx1v = dx1_ref[...]
        dx1b = dx1v.astype(BF16)
        dycat = _mm_nt(dx1b, wout_s[...])
        lng = ln_g[...]
        pm_ = pmat[...]
        cy, cq = carry_yc[...], carry_q[...]
        ycat_parts, dproj_parts = [None] * nrun, [None] * nrun
        dbm = dlng = dlnb = dscale = dcv0 = dcv1 = dcv2 = None
        add = lambda tot, v: v if tot is None else tot + v
        for c in reversed(range(nrun)):
            rows = slice(c * run, (c + 1) * run)
            pf = proj_ref[rows, :].astype(F32)
            if c > 0:
                ph = proj_ref[c * run - HALO:c * run, :].astype(F32)
            else:
                ph = halo_ref[...].astype(F32) * (ri > 0).astype(F32)
            r = _mixers_fwd(pf, ph[:, 1536:1920] * ph[:, 768:1152], ph[:, 1920:2176], ri * tm + c * run, prm,
                            saved_ref[rows, :].astype(F32))
            ycat_parts[c] = r["ycat"].astype(BF16)
            dya, dyb, dyc = dycat[rows, 0:D_A], dycat[rows, D_A:D_A + D_B], dycat[rows, D_A + D_B:D]

            dgu = dya * r["mixed"]
            dmix = dya * r["gu"]
            dmix_b = dmix.astype(BF16)
            dlo, dhi = _head_halves(dmix)
            for k in range(nch):
                dbm = add(dbm, dmix[k * CHUNK:(k + 1) * CHUNK, :])
            dvn_cols = []
            for j in range(3):
                dwcat_ref[j] += _mm_nt(_chunks_to_lanes(dmix_b, j, nch), r["v2s"][j])
                dvn_cols.append(_lanes_to_chunks(_mm(wtcat_ref[j], _head_stack(dlo, dhi, j, nch)), nch))
            dvn = jnp.concatenate(dvn_cols, axis=1)
            xh = r["xh"]
            dlng = add(dlng, jnp.sum(dvn * xh, axis=0, keepdims=True))
            dlnb = add(dlnb, jnp.sum(dvn, axis=0, keepdims=True))
            dxh = dvn * lng
            m1 = _mm(dxh.astype(BF16), pm_)
            m2 = _mm((dxh * xh).astype(BF16), pm_)
            dgv = r["ln_rstd"] * (dxh - m1 - xh * m2)
            du = dgu * r["dgelu_u"]
            dv = dgv * r["dgelu_v"]

            dgb = dyb * r["yc"]
            dyc2 = dyb * r["gb"]
            dcv0 = add(dcv0, jnp.sum(dyc2 * r["h2"], axis=0, keepdims=True))
            dcv1 = add(dcv1, jnp.sum(dyc2 * r["h1"], axis=0, keepdims=True))
            dcv2 = add(dcv2, jnp.sum(dyc2 * r["hc"], axis=0, keepdims=True))
            ext = jnp.concatenate([dyc2, cy], axis=0)
            dhc = r["w2"] * dyc2 + r["w1"] * _shift_up(ext, 1) + r["w0"] * _shift_up(ext, 2)
            cy = dyc2[0:HALO, :]
            dgc = dhc * r["zb"]
            dzb = dhc * r["gc"]

            dscale = add(dscale, jnp.sum(dyc * r["pm"], axis=0, keepdims=True))
            dpm = (dyc * r["scale"]).astype(BF16)
            dbd_ref[...] += _mm_tn(r["pooledb"], dpm)
            dpooled = _mm_nt(dpm, bd[...])
            q = dpooled * r["inv"]
            ext = jnp.concatenate([q, cq], axis=0)
            n = run + HALO
            r2 = ext + pltpu.roll(ext, n - 1, 0)
            r4 = r2 + pltpu.roll(r2, n - 2, 0)
            r8 = r4 + pltpu.roll(r4, n - 4, 0)
            r16 = r8 + pltpu.roll(r8, n - 8, 0)
            dzc = _pool_select(r2, r4, r8, r16)[0:run, :] - dpooled
            cq = q[0:HALO, :]
            dproj_parts[c] = jnp.concatenate([du, dv, dzb, dgb, dgc, dzc], axis=1).astype(BF16)

        carry_yc[...] = cy
        carry_q[...] = cq
        dbm_acc[...] += dbm
        dlng_ref[...] += dlng
        dlnb_ref[...] += dlnb
        dscale_ref[...] += dscale
        dconv_ref[0:1, :] += dcv0
        dconv_ref[1:2, :] += dcv1
        dconv_ref[2:3, :] += dcv2
        dwout_acc[...] += _mm_tn(jnp.concatenate(ycat_parts, axis=0), dx1b)
        dproj = jnp.concatenate(dproj_parts, axis=0)
        dwin_acc[...] += _mm_tn(dproj, hb16)
        dh = _mm(dproj, win_s[...])
        dx, dg = _rms_bwd(dh, g, xhat, rstd)
        dx_ref[...] = dx1v + dx
        dg_ref[...] += dg

        @pl.when(i == nt - 1)
        def _():
            _stage_bf16(dwin_acc, win_s)
            pltpu.sync_copy(win_s, dwin_hbm)
            _stage_bf16(dwout_acc, wout_s)
            pltpu.sync_copy(wout_s, dwout_hbm)
            for j in range(3):
                dwcat_ref[j] = dwcat_ref[j] * tril_ref[...]
            acc = dbm_acc[...]
            hi = acc.astype(BF16)
            lo = (acc - hi.astype(F32)).astype(BF16)
            dsb_ref[...] = _mm(hi, sel_ref[...]) + _mm(lo, sel_ref[...])

    return pl.pallas_call(
        body, name=f"mix_bwd_{layer}", grid=(nt,),
        in_specs=[pl.BlockSpec((tm, D), lambda i: (rev(i), 0)), pl.BlockSpec((tm, D), lambda i: (rev(i), 0)),
                  pl.BlockSpec((tm, RSTD_LANES), lambda i: (rev(i), 0)), pl.BlockSpec((tm, D_IN), lambda i: (rev(i), 0)),
                  pl.BlockSpec((HALO, D_IN), lambda i: (jnp.maximum(rev(i) * hb - 1, 0), 0)),
                  pl.BlockSpec((tm, N_SAVED * D_A), lambda i: (rev(i), 0)), _full((1, D))] + _mix_param_specs()
                 + [_full((3, CHUNK, 2 * CHUNK)), _full((CHUNK, 2 * CHUNK)), _full((D_A, CHUNK)), HBM, HBM, ANY],
        out_specs=[pl.BlockSpec((tm, D), lambda i: (rev(i), 0)), _full((1, D)), _full((3, CHUNK, 2 * CHUNK)),
                   _full((CHUNK, CHUNK)), _full((1, D_A)), _full((1, D_A)), _full((3, D_B)), _full((D_C, D_C)),
                   _full((1, D_C)), HBM, HBM],
        out_shape=[jax.ShapeDtypeStruct((t, D), F32), jax.ShapeDtypeStruct((1, D), F32),
                   jax.ShapeDtypeStruct((3, CHUNK, 2 * CHUNK), F32), jax.ShapeDtypeStruct((CHUNK, CHUNK), F32),
                   jax.ShapeDtypeStruct((1, D_A), F32), jax.ShapeDtypeStruct((1, D_A), F32),
                   jax.ShapeDtypeStruct((3, D_B), F32), jax.ShapeDtypeStruct((D_C, D_C), F32),
                   jax.ShapeDtypeStruct((1, D_C), F32), pltpu.HBM((D_IN, D), BF16), pltpu.HBM((D, D), BF16)],
        scratch_shapes=[pltpu.VMEM((D_IN, D), BF16), pltpu.VMEM((D, D), BF16),
                        pltpu.VMEM((D_IN, D), F32), pltpu.VMEM((D, D), F32), pltpu.VMEM((CHUNK, D_A), F32),
                        pltpu.VMEM((HALO, D_B), F32), pltpu.VMEM((HALO, D_C), F32), pltpu.SemaphoreType.DMA((2,))],
        compiler_params=_params(),
    )(dx1, xhat0, rstd0, proj, proj, saved, g_mix, *[mp[k] for k in _MIX_PARAM_NAMES], wtcat, trilcat, headsel, _in_hbm(w_in_t), _in_hbm(w_out), after)


def _coords():
    return lax.axis_index("x"), lax.axis_index("y"), lax.axis_index("c")


EFFECT = pltpu.SideEffectType.DATAFLOW_SIDE_EFFECTING


def _peer(k):
    x, y, c = _coords()
    px, py, pc = x ^ (k >> 2), y ^ ((k >> 1) & 1), c ^ (k & 1)
    return (px, py, pc), 4 * px + 2 * py + pc


def _landing_shape(shape, mode):
    if mode == "block":
        return (N_DEV,) + shape
    if mode == "slot":
        return shape
    if mode == "cols_in":
        return (shape[0], N_DEV * shape[1])
    return (N_DEV, shape[0], shape[1] // N_DEV)


def _pieces(src, land, mode, src_idx, land_idx):
    if mode == "block":
        return src, land.at[land_idx]
    if mode == "slot":
        return src.at[src_idx], land.at[land_idx]
    if mode == "cols_in":
        cw = src.shape[1]
        return src, land.at[:, pl.ds(pl.multiple_of(land_idx * cw, 128), cw)]
    cw = land.shape[2]
    return src.at[:, pl.ds(pl.multiple_of(src_idx * cw, 128), cw)], land.at[land_idx]


def _exchange_copy(src, land, mode, send_sem, recv_sem, ai, k, starting):
    x, y, c = _coords()
    peer, pidx = _peer(k)
    s, d = _pieces(src, land, mode, pidx, 4 * x + 2 * y + c if starting else pidx)
    i = ai * (N_DEV - 1) + k - 1
    return pltpu.make_async_remote_copy(src_ref=s, dst_ref=d, send_sem=send_sem.at[i], recv_sem=recv_sem.at[i],
                                        device_id=peer, device_id_type=MESH)


def _own_copy(src, land, mode, local_sem, ai):
    x, y, c = _coords()
    me = 4 * x + 2 * y + c
    s, d = _pieces(src, land, mode, me, me)
    return pltpu.make_async_copy(s, d, local_sem.at[ai])


def _item_src(ins, item):
    a, sub = item
    return ins[a] if sub is None else ins[a].at[sub]


def _exchange_start(srcs, items, modes, groups, name):
    n, ni, ng = len(srcs), len(items), len(groups)
    shapes = [srcs[a].shape if sub is None else srcs[a].shape[1:] for a, sub in items]
    land_shapes = [pltpu.HBM(_landing_shape(sh, m), srcs[a].dtype) for sh, m, (a, _) in zip(shapes, modes, items)]

    def body(*refs):
        ins = refs[:n]
        sems = refs[n:n + 3 * ng]
        land_refs = refs[n + 3 * ng:n + 3 * ng + ni]
        token = refs[-1]
        for g, idxs in enumerate(groups):
            for ai, it in enumerate(idxs):
                src = _item_src(ins, items[it])
                _own_copy(src, land_refs[it], modes[it], sems[3 * g + 2], ai).start()
                for k in range(1, N_DEV):
                    _exchange_copy(src, land_refs[it], modes[it], sems[3 * g], sems[3 * g + 1], ai, k, True).start()
        token[...] = jnp.zeros_like(token)

    sem_shapes = []
    for idxs in groups:
        sem_shapes += [pltpu.SemaphoreType.DMA((len(idxs) * (N_DEV - 1),))] * 2 + [pltpu.SemaphoreType.DMA((len(idxs),))]
    out = pl.pallas_call(
        body, name=name,
        out_shape=tuple(sem_shapes) + tuple(land_shapes) + (jax.ShapeDtypeStruct((8, 128), F32),),
        in_specs=[HBM] * n,
        out_specs=tuple([SEM] * (3 * ng) + [HBM] * ni + [pl.BlockSpec(memory_space=pltpu.VMEM)]),
        compiler_params=pltpu.CompilerParams(has_side_effects=EFFECT),
    )(*[pltpu.with_memory_space_constraint(s, pltpu.HBM) for s in srcs])
    sems = [tuple(out[3 * g:3 * g + 3]) for g in range(ng)]
    return sems, list(out[3 * ng:3 * ng + ni]), out[-1]


def _exchange_wait(sems, srcs, items, lands, modes, groups, after, name):
    n, ni, ng = len(srcs), len(items), len(groups)

    def body(*refs):
        ins, land_refs = refs[:n], refs[n:n + ni]
        sem_refs = refs[n + ni:n + ni + 3 * ng]
        for g, idxs in enumerate(groups):
            for ai, it in enumerate(idxs):
                src = _item_src(ins, items[it])
                _own_copy(src, land_refs[it], modes[it], sem_refs[3 * g + 2], ai).wait()
                for k in range(1, N_DEV):
                    cp = _exchange_copy(src, land_refs[it], modes[it], sem_refs[3 * g], sem_refs[3 * g + 1], ai, k, False)
                    cp.wait_send()
                    cp.wait_recv()

    flat_sems = [s for trio in sems for s in trio]
    afters = list(after) if isinstance(after, (list, tuple)) else [after]
    out = pl.pallas_call(
        body, name=name,
        out_shape=tuple(pltpu.HBM(l.shape, l.dtype) for l in lands),
        in_specs=[HBM] * (n + ni) + [SEM] * (3 * ng) + [ANY] * len(afters),
        out_specs=tuple([HBM] * ni),
        input_output_aliases={n + i: i for i in range(ni)},
        compiler_params=pltpu.CompilerParams(has_side_effects=EFFECT),
    )(*srcs, *lands, *flat_sems, *afters)
    return list(out)


def _slot_sum_call(landed):
    n = len(landed)

    def body(*refs):
        for src, dst in zip(refs[:n], refs[n:]):
            tot = src[0]
            for j in range(1, N_DEV):
                tot = tot + src[j]
            dst[...] = tot

    vm = pl.BlockSpec(memory_space=pltpu.VMEM)
    return pl.pallas_call(
        body, name="small_grads_sum", in_specs=[vm] * n, out_specs=[vm] * n,
        out_shape=[jax.ShapeDtypeStruct(a.shape[1:], F32) for a in landed],
        compiler_params=pltpu.CompilerParams(vmem_limit_bytes=V7X_VMEM_LIMIT),
    )(*landed)


def _adamw(w, g, m, v):
    m = ADAM_B1 * m + (1.0 - ADAM_B1) * g
    v = ADAM_B2 * v + (1.0 - ADAM_B2) * (g * g)
    m_hat = m / (1.0 - ADAM_B1 ** ADAM_STEP)
    v_hat = v / (1.0 - ADAM_B2 ** ADAM_STEP)
    delta = -ADAM_LR * (m_hat / (jnp.sqrt(v_hat) + ADAM_EPS) + ADAM_WD * w)
    return delta, m, v


def _reduce_adamw_call(recvs, w, m, v, name):
    nl = len(recvs)
    _, r, c = recvs[0].shape
    rb = 256 if r % 256 == 0 else r
    nb = r // rb

    def body(*refs):
        recv_refs = refs[:nl]
        w_ref, m_ref, v_ref, g_ref, d_ref, nm_ref, nv_ref = refs[nl:]
        for l in range(nl):
            @pl.when(pl.program_id(0) == l)
            def _(l=l):
                g = recv_refs[l][0].astype(F32)
                for j in range(1, N_DEV):
                    g = g + recv_refs[l][j].astype(F32)
                delta, nm, nv = _adamw(w_ref[0], g, m_ref[0], v_ref[0])
                g_ref[0] = g
                d_ref[0] = delta
                nm_ref[0] = nm
                nv_ref[0] = nv

    def recv_spec(l):
        return pl.BlockSpec((N_DEV, rb, c), lambda lg, i: (0, jnp.where(lg == l, i, jnp.where(lg < l, 0, nb - 1)), 0))

    blk = pl.BlockSpec((1, rb, c), lambda lg, i: (lg, i, 0))
    shp = jax.ShapeDtypeStruct((nl, r, c), F32)
    return pl.pallas_call(
        body, name=name, grid=(nl, nb),
        in_specs=[recv_spec(l) for l in range(nl)] + [blk, blk, blk],
        out_specs=[blk, blk, blk, blk], out_shape=[shp, shp, shp, shp],
        compiler_params=pltpu.CompilerParams(dimension_semantics=("arbitrary", "arbitrary"), vmem_limit_bytes=V7X_VMEM_LIMIT),
    )(*recvs, w, m, v)


def _small_adamw_call(w, g, m, v):
    def body(w_ref, g_ref, m_ref, v_ref, d_ref, nm_ref, nv_ref):
        delta, nm, nv = _adamw(w_ref[...], g_ref[...], m_ref[...], v_ref[...])
        d_ref[...] = delta
        nm_ref[...] = nm
        nv_ref[...] = nv

    shp = jax.ShapeDtypeStruct(w.shape, F32)
    vm = pl.BlockSpec(memory_space=pltpu.VMEM)
    return pl.pallas_call(body, name="small_adamw", in_specs=[vm] * 4, out_specs=[vm] * 3, out_shape=[shp, shp, shp],
                          compiler_params=pltpu.CompilerParams(vmem_limit_bytes=V7X_VMEM_LIMIT))(w, g, m, v)


_GATHER_MODE = dict(w_in="block", w_out="block", w_ff1="cols_in", w_ff2="block", w_ple_gate="block", w_ple_proj="cols_in")
_SCATTER_MODE = dict(w_in="slot", w_out="slot", w_ff1="cols_out", w_ff2="slot", w_ple_gate="slot", w_ple_proj="cols_out")
_GROUP_A = ("w_in", "w_out")
_GROUP_B = ("w_ff1", "w_ff2", "w_ple_gate", "w_ple_proj")


def _gathered_full(k, landed):
    if _GATHER_MODE[k] == "cols_in":
        return landed
    n, r, c = landed.shape
    return landed.reshape(n * r, c)


def _grad_send(k, g):
    if _SCATTER_MODE[k] == "cols_out":
        return g
    r8, c = g.shape
    return g.reshape(N_DEV, r8 // N_DEV, c)


_SMALL_ORDER = ("norm_mix_g", "sgu_w", "sgu_b", "sgu_ln_g", "sgu_ln_b", "conv_w", "pool_w", "pool_scale",
                "norm_ff_g", "norm_ple_g", "final_g")


def _pack_small(d, order=_SMALL_ORDER):
    pieces, layout = [], []
    for k in order:
        flat = d[k].reshape(-1)
        n = flat.shape[0]
        pad = (-n) % 128
        pieces.append(jnp.pad(flat, (0, pad)))
        layout.append((k, d[k].shape, n, n + pad))
    flat = jnp.concatenate(pieces)
    pad = (-flat.shape[0]) % 1024
    return jnp.pad(flat, (0, pad)).reshape(-1, 128), layout


def _unpack_small(buf, layout):
    flat = buf.reshape(-1)
    out, off = {}, 0
    for k, shape, n, padded in layout:
        out[k] = flat[off:off + n].reshape(shape)
        off += padded
    return out


def kernel(x, p, norm_mix_g, w_in, sgu_w, sgu_b, sgu_ln_g, sgu_ln_b, conv_w, pool_w, pool_scale, w_out, norm_ff_g, w_ff1, w_ff2, norm_ple_g, w_ple_gate, w_ple_proj, final_g, loss_target, m_norm_mix_g, m_w_in, m_sgu_w, m_sgu_b, m_sgu_ln_g, m_sgu_ln_b, m_conv_w, m_pool_w, m_pool_scale, m_w_out, m_norm_ff_g, m_w_ff1, m_w_ff2, m_norm_ple_g, m_w_ple_gate, m_w_ple_proj, m_final_g, v_norm_mix_g, v_w_in, v_sgu_w, v_sgu_b, v_sgu_ln_g, v_sgu_ln_b, v_conv_w, v_pool_w, v_pool_scale, v_w_out, v_norm_ff_g, v_w_ff1, v_w_ff2, v_norm_ple_g, v_w_ple_gate, v_w_ple_proj, v_final_g):
    t = x.shape[1]
    xc, yc_, cc = _coords()
    me = 4 * xc + 2 * yc_ + cc
    tm = lambda want: min(want, t)

    shard_names = _GROUP_A + _GROUP_B
    swap = lambda a: jnp.transpose(a, (0, 2, 1))
    shard = dict(w_in=swap(w_in), w_out=w_out, w_ff1=w_ff1, w_ff2=w_ff2, w_ple_gate=w_ple_gate, w_ple_proj=w_ple_proj)
    conv_pad = jnp.zeros((16, 128), F32).at[0:DEPTH * 3, 0:D_B // N_DEV].set(conv_w.reshape(DEPTH * 3, D_B // N_DEV))
    ag_srcs = [shard[k].astype(BF16) for k in shard_names] + [conv_pad]
    ag_items, ag_modes, ag_groups = [], [], []
    for l in range(DEPTH):
        for names in (_GROUP_A, _GROUP_B):
            ag_groups.append(list(range(len(ag_items), len(ag_items) + len(names))))
            ag_items += [(shard_names.index(k), l) for k in names]
            ag_modes += [_GATHER_MODE[k] for k in names]
            if l == 0 and names is _GROUP_A:
                ag_groups[-1].append(len(ag_items))
                ag_items.append((len(shard_names), None))
                ag_modes.append("block")
    ag_sems, ag_lands, _ = _exchange_start(ag_srcs, ag_items, ag_modes, ag_groups, "weights_gather_start")

    def gathered(l, which, after):
        idxs = ag_groups[2 * l + which]
        landed = _exchange_wait([ag_sems[2 * l + which]], ag_srcs, [ag_items[i] for i in idxs], [ag_lands[i] for i in idxs],
                                [ag_modes[i] for i in idxs], [list(range(len(idxs)))], after, f"weights_gather_wait_{l}_{which}")
        full = {k: _gathered_full(k, got) for k, got in zip((_GROUP_A, _GROUP_B)[which], landed)}
        if l == 0 and which == 0:
            full["conv_w"] = jnp.transpose(landed[-1][:, 0:DEPTH * 3, 0:D_B // N_DEV].reshape(N_DEV, DEPTH, 3, D_B // N_DEV),
                                           (1, 2, 0, 3)).reshape(DEPTH, 3, D_B)
        return full

    idx = jnp.arange(D_A)
    pmat = ((idx[:, None] // 64) == (idx[None, :] // 64)).astype(BF16) * (1.0 / 64.0)
    pmat = pmat.astype(BF16)
    tril = jnp.tril(jnp.ones((CHUNK, CHUNK), F32))
    trilcat = jnp.concatenate([tril, tril], axis=1)
    headsel = ((idx[:, None] // 64) == jnp.arange(CHUNK)[None, :]).astype(BF16)
    row = lambda a: a.reshape(1, -1)

    def mix_params(l):
        wm = sgu_w[l] * tril[None]
        wcat = jnp.stack([jnp.concatenate([wm[2 * j], wm[2 * j + 1]], axis=1) for j in range(3)]).astype(BF16)
        wtcat = jnp.stack([jnp.concatenate([wm[2 * j].T, wm[2 * j + 1].T], axis=1) for j in range(3)]).astype(BF16)
        bmat = jnp.repeat(sgu_b[l].T, 64, axis=1)
        bd = jnp.zeros((D_C, D_C), F32)
        for gi in range(4):
            bd = bd.at[gi * 64:(gi + 1) * 64, gi * 64:(gi + 1) * 64].set(pool_w[l, gi])
        mp = dict(pmat=pmat, ln_g=row(sgu_ln_g[l]), ln_b=row(sgu_ln_b[l]), wcat=wcat, bmat=bmat, conv_w=conv_full[l],
                  bd=bd.astype(BF16), pool_scale=row(pool_scale[l]))
        return mp, wtcat

    xs = x.reshape(t, D)
    p_layers = p.reshape(DEPTH, t, D_PLE)
    saved, full_w = [], []
    conv_full = None
    for l in range(DEPTH):
        wa = gathered(l, 0, xs)
        if l == 0:
            conv_full = wa["conv_w"]
        mp, _ = mix_params(l)
        x1, proj, sgu_saved, xh0, rs0 = _mix_fwd_call(xs, row(norm_mix_g[l]), wa["w_in"], wa["w_out"], mp, tm(TM_MIX_FWD), tm(RUN_MIX_FWD), l)
        wb = gathered(l, 1, x1)
        x3, r, gate, xh1, rs1, xh2, rs2 = _ffn_fwd_call(x1, p_layers, row(norm_ff_g[l]), row(norm_ple_g[l]), wb["w_ff1"], wb["w_ff2"],
                                        wb["w_ple_gate"], wb["w_ple_proj"], tm(TM_FFN_FWD), l)
        saved.append((proj, sgu_saved, r, gate, (xh0, rs0), (xh1, rs1), (xh2, rs2)))
        full_w.append({**wa, **wb})
        xs = x3

    sq, dx, dfinal = _loss_call(xs, loss_target.reshape(t, D), row(final_g), tm(TM_LOSS))
    loss = lax.psum(jnp.sum(sq) * (0.5 / D), ("x", "y", "c"))

    layer_keys = tuple(k for k in _SMALL_ORDER if k != "final_g")
    small = {k: [None] * DEPTH for k in layer_keys}
    small_ex, small_layout = [None] * DEPTH, None
    ex = {}
    token = jnp.zeros((8, 128), F32)

    for l in reversed(range(DEPTH)):
        proj, sgu_saved, r, gate, norm0, norm1, norm2 = saved[l]
        fw = full_w[l]
        dx2, dgple, dwg, dwp = _ple_bwd_call(dx, *norm2, gate, p_layers, row(norm_ple_g[l]), fw["w_ple_gate"], fw["w_ple_proj"], token,
                                             tm(TM_PLE_BWD), l)
        da, dw2 = _ffn_bwd_hidden_call(dx2, r, fw["w_ff2"], tm(TM_FFN_BWD), l)
        dx1, dgff, dw1 = _ffn_bwd_input_call(da, *norm1, dx2, row(norm_ff_g[l]), fw["w_ff1"], tm(TM_FFN_BWD), l)
        sends = [_grad_send(k, g_) for k, g_ in zip(_GROUP_B, (dw1, dw2, dwg, dwp))]
        sems, lands, token = _exchange_start(sends, [(i, None) for i in range(len(sends))], [_SCATTER_MODE[k] for k in _GROUP_B],
                                             [list(range(len(sends)))], f"grads_exchange_start_{l}_1")
        ex[(l, 1)] = (sems[0], sends, lands)
        mp, wtcat = mix_params(l)
        (dx, dgmix, dwcat, dsb, dlng, dlnb, dconv, dbd, dscale, dwin, dwout) = _mix_bwd_call(
            dx1, *norm0, proj, sgu_saved, row(norm_mix_g[l]), fw["w_in"], fw["w_out"], mp, wtcat, trilcat, headsel, token,
            tm(TM_MIX_BWD), tm(RUN_MIX_BWD), l)
        small["norm_mix_g"][l] = dgmix[0]
        small["sgu_w"][l] = jnp.stack([dwcat[h // 2][:, (h % 2) * CHUNK:(h % 2 + 1) * CHUNK] for h in range(6)])
        small["sgu_b"][l] = dsb[:, 0:6].T
        small["sgu_ln_g"][l], small["sgu_ln_b"][l] = dlng[0], dlnb[0]
        small["conv_w"][l] = dconv
        small["pool_w"][l] = jnp.stack([dbd[gi * 64:(gi + 1) * 64, gi * 64:(gi + 1) * 64] for gi in range(4)])
        small["pool_scale"][l] = dscale[0]
        small["norm_ff_g"][l], small["norm_ple_g"][l] = dgff[0], dgple[0]
        layer_small = {k: small[k][l] for k in layer_keys}
        layer_small["final_g"] = dfinal[0] if l == DEPTH - 1 else jnp.zeros_like(dfinal[0])
        sbuf, small_layout = _pack_small(layer_small, layer_keys + ("final_g",))
        sends = [_grad_send("w_in", dwin), _grad_send("w_out", dwout)]
        sems, lands, token = _exchange_start(sends + [sbuf], [(i, None) for i in range(3)],
                                             [_SCATTER_MODE["w_in"], _SCATTER_MODE["w_out"], "block"], [[0, 1], [2]],
                                             f"grads_exchange_start_{l}_0")
        ex[(l, 0)] = (sems[0], sends, lands[0:2])
        small_ex[l] = (sems[1], sbuf, lands[2])
    grad_x = dx.reshape(1, t, D)

    state = dict(w_in=(swap(w_in), swap(m_w_in), swap(v_w_in)), w_out=(w_out, m_w_out, v_w_out), w_ff1=(w_ff1, m_w_ff1, v_w_ff1),
                 w_ff2=(w_ff2, m_w_ff2, v_w_ff2), w_ple_gate=(w_ple_gate, m_w_ple_gate, v_w_ple_gate),
                 w_ple_proj=(w_ple_proj, m_w_ple_proj, v_w_ple_proj))
    res = {}

    def finish_group(which, after):
        names = (_GROUP_A, _GROUP_B)[which]
        n = len(names)
        sems = [ex[(l, which)][0] for l in range(DEPTH)]
        sends = [s_ for l in range(DEPTH) for s_ in ex[(l, which)][1]]
        lands = [a_ for l in range(DEPTH) for a_ in ex[(l, which)][2]]
        landed = _exchange_wait(sems, sends, [(i, None) for i in range(DEPTH * n)], lands, [_SCATTER_MODE[k] for k in names] * DEPTH,
                                [list(range(l * n, (l + 1) * n)) for l in range(DEPTH)], after, f"grads_exchange_wait_{which}")
        for i, k in enumerate(names):
            res[k] = _reduce_adamw_call([landed[l * n + i] for l in range(DEPTH)], *state[k], "reduce_adamw_" + k)

    finish_group(1, dx)

    landed = _exchange_wait([small_ex[l][0] for l in range(DEPTH)], [small_ex[l][1] for l in range(DEPTH)],
                            [(l, None) for l in range(DEPTH)], [small_ex[l][2] for l in range(DEPTH)], ["block"] * DEPTH,
                            [[l] for l in range(DEPTH)], res[_GROUP_B[-1]][0], "small_grads_wait")
    sums = [_unpack_small(b_, small_layout) for b_ in _slot_sum_call(landed)]
    gs = {k: jnp.stack([sums[l][k] for l in range(DEPTH)]) for k in layer_keys}
    gs["final_g"] = sums[DEPTH - 1]["final_g"]
    conv_cols = lambda a: lax.dynamic_slice_in_dim(a, me * (D_B // N_DEV), D_B // N_DEV, axis=2)
    pad_conv = lambda a: jnp.zeros((DEPTH, 3, D_B), F32).at[:, :, 0:D_B // N_DEV].set(a)
    small_w = dict(norm_mix_g=norm_mix_g, sgu_w=sgu_w, sgu_b=sgu_b, sgu_ln_g=sgu_ln_g, sgu_ln_b=sgu_ln_b, conv_w=pad_conv(conv_w),
                   pool_w=pool_w, pool_scale=pool_scale, norm_ff_g=norm_ff_g, norm_ple_g=norm_ple_g, final_g=final_g)
    small_m = dict(norm_mix_g=m_norm_mix_g, sgu_w=m_sgu_w, sgu_b=m_sgu_b, sgu_ln_g=m_sgu_ln_g, sgu_ln_b=m_sgu_ln_b,
                   conv_w=pad_conv(m_conv_w), pool_w=m_pool_w, pool_scale=m_pool_scale, norm_ff_g=m_norm_ff_g,
                   norm_ple_g=m_norm_ple_g, final_g=m_final_g)
    small_v = dict(norm_mix_g=v_norm_mix_g, sgu_w=v_sgu_w, sgu_b=v_sgu_b, sgu_ln_g=v_sgu_ln_g, sgu_ln_b=v_sgu_ln_b,
                   conv_w=pad_conv(v_conv_w), pool_w=v_pool_w, pool_scale=v_pool_scale,
                   norm_ff_g=v_norm_ff_g, norm_ple_g=v_norm_ple_g, final_g=v_final_g)
    gs_local = dict(gs)
    gs_local["conv_w"] = pad_conv(conv_cols(gs["conv_w"]))
    g_loc, layout = _pack_small(gs_local)
    wbuf, _ = _pack_small(small_w)
    mbuf, _ = _pack_small(small_m)
    vbuf, _ = _pack_small(small_v)
    dbuf, nmbuf, nvbuf = _small_adamw_call(wbuf, g_loc, mbuf, vbuf)
    sd, sm, sv = _unpack_small(dbuf, layout), _unpack_small(nmbuf, layout), _unpack_small(nvbuf, layout)
    unconv = lambda a: a[:, :, 0:D_B // N_DEV]
    for dct in (gs_local, sd, sm, sv):
        dct["conv_w"] = unconv(dct["conv_w"])

    finish_group(0, [res[_GROUP_B[-1]][0], dbuf])

    order = ["norm_mix_g", "w_in", "sgu_w", "sgu_b", "sgu_ln_g", "sgu_ln_b", "conv_w", "pool_w", "pool_scale", "w_out",
             "norm_ff_g", "w_ff1", "w_ff2", "norm_ple_g", "w_ple_gate", "w_ple_proj", "final_g"]
    outs = [loss, grad_x]
    for which in range(4):
        for k in order:
            if k in res:
                outs.append(swap(res[k][which]) if k == "w_in" else res[k][which])
            else:
                outs.append((gs_local, sd, sm, sv)[which][k])
    return tuple(outs)
```

```python
import functools
import math

import jax
import jax.numpy as jnp
from jax import lax
from jax.experimental import pallas as pl
from jax.experimental.pallas import tpu as pltpu

F32 = jnp.float32
BF16 = jnp.bfloat16

D = 1024
D_IN = 2176
D_A = 384
D_B = 384
D_C = 256
D_FF = 4096
D_PLE = 256
DEPTH = 4
CHUNK = 128
HALO = 16
FF_BLK = 1024
N_DEV = 8
RMS_EPS = 1e-6
LN_EPS = 1e-5
ADAM_LR = 0.001
ADAM_B1 = 0.9
ADAM_B2 = 0.999
ADAM_EPS = 1e-08
ADAM_WD = 0.01
ADAM_STEP = 10

TM_MIX_FWD = 1024
TM_FFN_FWD = 512
TM_LOSS = 512
TM_PLE_BWD = 1024
TM_FFN_BWD = 512
TM_MIX_BWD = 512
RUN_MIX_FWD = 1024
RUN_MIX_BWD = 512
V7X_VMEM_LIMIT = 60000 * 1024

ANY = pl.BlockSpec(memory_space=pl.ANY)
HBM = pl.BlockSpec(memory_space=pltpu.HBM)
SEM = pl.BlockSpec(memory_space=pltpu.SEMAPHORE)
MESH = pl.DeviceIdType.MESH


def _params(vmem=V7X_VMEM_LIMIT):
    return pltpu.CompilerParams(dimension_semantics=("arbitrary",), vmem_limit_bytes=vmem)


def _in_hbm(a):
    return pltpu.with_memory_space_constraint(a, pltpu.HBM)


def _full(shape):
    nd = len(shape)
    return pl.BlockSpec(shape, lambda i: (0,) * nd)


def _rows(tm, cols):
    return pl.BlockSpec((tm, cols), lambda i: (i, 0))


def _layer_rows(layer, tm, cols):
    return pl.BlockSpec((None, tm, cols), lambda i: (layer, i, 0))


def _mm(a, b):
    return jnp.dot(a, b, preferred_element_type=F32)


def _mm_nt(a, b):
    return lax.dot_general(a, b, (((1,), (1,)), ((), ())), preferred_element_type=F32)


def _mm_tn(a, b):
    return lax.dot_general(a, b, (((0,), (0,)), ((), ())), preferred_element_type=F32)


def _gelu_and_grad(x):
    ax = jnp.abs(x) * (1.0 / math.sqrt(2.0))
    t = 1.0 / (1.0 + 0.3275911 * ax)
    poly = t * (0.254829592 + t * (-0.284496736 + t * (1.421413741 + t * (-1.453152027 + t * 1.061405429))))
    e = jnp.exp(-0.5 * x * x)
    half = 0.5 * poly * e
    cdf = jnp.where(x < 0, half, 1.0 - half)
    return x * cdf, cdf + x * (e * (1.0 / math.sqrt(2.0 * math.pi)))


def _rms(x, g):
    rstd = lax.rsqrt(jnp.mean(x * x, axis=-1, keepdims=True) + RMS_EPS)
    xhat = x * rstd
    return xhat * g, xhat, rstd


RSTD_LANES = 128


def _save_norm(xhat, rstd, xhat_ref, rstd_ref):
    xhat_ref[...] = xhat.astype(BF16)
    rstd_ref[...] = jnp.broadcast_to(rstd, rstd_ref.shape)


def _saved_norm(xhat_ref, rstd_ref, g):
    xhat = xhat_ref[...].astype(F32)
    return xhat * g, xhat, rstd_ref[:, 0:1]


def _rms_bwd(dy, g, xhat, rstd):
    dg = jnp.sum(dy * xhat, axis=0, keepdims=True)
    dxh = dy * g
    dx = rstd * (dxh - xhat * jnp.mean(dxh * xhat, axis=-1, keepdims=True))
    return dx, dg


def _shift_down(ext, k):
    return pltpu.roll(ext, k, 0)[HALO:, :]


def _shift_up(ext, k):
    n = ext.shape[0]
    return pltpu.roll(ext, n - k, 0)[: n - HALO, :]


def _pool_select(s2, s4, s8, s16):
    lane = lax.broadcasted_iota(jnp.int32, s2.shape, 1)
    return jnp.where(lane < 64, s2, jnp.where(lane < 128, s4, jnp.where(lane < 192, s8, s16)))


def _pool_inv_count(tile_start, tm):
    pos = lax.broadcasted_iota(jnp.int32, (tm, D_C), 0) + tile_start + 1
    lane = lax.broadcasted_iota(jnp.int32, (tm, D_C), 1)
    win = jnp.where(lane < 64, 2, jnp.where(lane < 128, 4, jnp.where(lane < 192, 8, 16)))
    return 1.0 / jnp.minimum(pos, win).astype(F32)


def _head_halves(a):
    lane = lax.broadcasted_iota(jnp.int32, a.shape, 1)
    even = (lane & 64) == 0
    return jnp.where(even, a, 0.0).astype(BF16), jnp.where(even, 0.0, a).astype(BF16)


def _head_stack(lo, hi, j, nch):
    return jnp.concatenate(
        [jnp.concatenate([lo[c * CHUNK:(c + 1) * CHUNK, j * 128:(j + 1) * 128], hi[c * CHUNK:(c + 1) * CHUNK, j * 128:(j + 1) * 128]], axis=0)
         for c in range(nch)], axis=1)


def _chunks_to_lanes(a, j, nch):
    return jnp.concatenate([a[c * CHUNK:(c + 1) * CHUNK, j * 128:(j + 1) * 128] for c in range(nch)], axis=1)


def _lanes_to_chunks(o, nch):
    return jnp.concatenate([o[:, c * CHUNK:(c + 1) * CHUNK] for c in range(nch)], axis=0)


def _loads(pairs, sem):
    return [pltpu.make_async_copy(src, dst, sem.at[n]) for n, (src, dst) in enumerate(pairs)]


def _load_all(loads):
    for cp in loads:
        cp.start()
    for cp in loads:
        cp.wait()


def _stage_bf16(acc, stage):
    rows = acc.shape[0]
    strip = min(rows, 128)

    @pl.loop(0, rows // strip)
    def _(n):
        sl = pl.ds(pl.multiple_of(n * strip, strip), strip)
        stage[sl, :] = acc[sl, :].astype(BF16)


N_SAVED = 5


def _mixers_fwd(pf, halo_hc, halo_zc, tile_start, prm, saved=None):
    tm = pf.shape[0]
    nch = tm // CHUNK
    u, v = pf[:, 0:D_A], pf[:, D_A:2 * D_A]
    zb, gb, gc = pf[:, 768:1152], pf[:, 1152:1536], pf[:, 1536:1920]
    zc = pf[:, 1920:2176]
    r = {}
    if saved is None:
        gu, r["dgelu_u"] = _gelu_and_grad(u)
        gv, r["dgelu_v"] = _gelu_and_grad(v)
        pmat = prm["pmat"][...]
        mu = _mm(gv.astype(BF16), pmat)
        dv = gv - mu
        var = _mm((dv * dv).astype(BF16), pmat)
        rstd = lax.rsqrt(var + LN_EPS)
        xh = dv * rstd
        r["saved"] = jnp.concatenate([gu, r["dgelu_u"], r["dgelu_v"], xh, rstd], axis=1).astype(BF16)
    else:
        gu, r["dgelu_u"], r["dgelu_v"], xh, rstd = (saved[:, n * D_A:(n + 1) * D_A] for n in range(N_SAVED))
    vlo, vhi = _head_halves(xh * prm["ln_g"][...] + prm["ln_b"][...])
    cols, v2s = [], []
    for j in range(3):
        v2 = _head_stack(vlo, vhi, j, nch)
        v2s.append(v2)
        cols.append(_lanes_to_chunks(_mm(prm["wcat"][j], v2), nch))
    mixed = jnp.concatenate(cols, axis=1) + jnp.concatenate([prm["bmat"][...]] * nch, axis=0)
    ya = gu * mixed
    r.update(gu=gu, mixed=mixed, v2s=v2s, xh=xh, ln_rstd=rstd)
    w0, w1, w2 = prm["conv_w"][0:1, :], prm["conv_w"][1:2, :], prm["conv_w"][2:3, :]
    hc = gc * zb
    ext = jnp.concatenate([halo_hc, hc], axis=0)
    h1, h2 = _shift_down(ext, 1), _shift_down(ext, 2)
    yc = w2 * hc + w1 * h1 + w0 * h2
    yb = gb * yc
    r.update(hc=hc, h1=h1, h2=h2, yc=yc, zb=zb, gb=gb, gc=gc, w0=w0, w1=w1, w2=w2)
    ext = jnp.concatenate([halo_zc, zc], axis=0)
    s2 = ext + pltpu.roll(ext, 1, 0)
    s4 = s2 + pltpu.roll(s2, 2, 0)
    s8 = s4 + pltpu.roll(s4, 4, 0)
    s16 = s8 + pltpu.roll(s8, 8, 0)
    inv = _pool_inv_count(tile_start, tm)
    pooled = _pool_select(s2, s4, s8, s16)[HALO:, :] * inv - zc
    pooledb = pooled.astype(BF16)
    pm = _mm(pooledb, prm["bd"][...])
    scale = prm["pool_scale"][...]
    ycm = pm * scale
    r.update(inv=inv, pooledb=pooledb, pm=pm, scale=scale, zc=zc)
    r["ycat"] = jnp.concatenate([ya, yb, ycm], axis=1)
    return r


_MIX_PARAM_NAMES = ("pmat", "ln_g", "ln_b", "wcat", "bmat", "conv_w", "bd", "pool_scale")


def _mix_param_specs():
    return [_full((D_A, D_A)), _full((1, D_A)), _full((1, D_A)), _full((3, CHUNK, 2 * CHUNK)), _full((CHUNK, D_A)),
            _full((3, D_B)), _full((D_C, D_C)), _full((1, D_C))]


def _mix_fwd_call(x, g_mix, w_in_t, w_out, mp, tm, run, layer):
    t = x.shape[0]
    nt = t // tm

    def body(x_ref, g_ref, pmat, ln_g, ln_b, wcat, bmat, conv_w, bd, pool_scale, win_hbm, wout_hbm,
             x1_ref, proj_ref, saved_ref, xhat_ref, rstd_ref, win_s, wout_s, halo_hc, halo_zc, load_sem):
        i = pl.program_id(0)
        loads = _loads([(win_hbm, win_s), (wout_hbm, wout_s)], load_sem)

        @pl.when(i == 0)
        def _():
            _load_all(loads)
            halo_hc[...] = jnp.zeros_like(halo_hc)
            halo_zc[...] = jnp.zeros_like(halo_zc)

        prm = dict(pmat=pmat, ln_g=ln_g, ln_b=ln_b, wcat=wcat, bmat=bmat, conv_w=conv_w, bd=bd, pool_scale=pool_scale)
        xv = x_ref[...]
        h, xhat, rstd = _rms(xv, g_ref[...])
        _save_norm(xhat, rstd, xhat_ref, rstd_ref)
        pf = _mm_nt(h.astype(BF16), win_s[...])
        proj_ref[...] = pf.astype(BF16)
        hh, hz = halo_hc[...], halo_zc[...]
        parts = []
        for c in range(tm // run):
            r = _mixers_fwd(pf[c * run:(c + 1) * run, :], hh, hz, i * tm + c * run, prm)
            hh, hz = r["hc"][run - HALO:, :], r["zc"][run - HALO:, :]
            parts.append(r["ycat"].astype(BF16))
            saved_ref[c * run:(c + 1) * run, :] = r["saved"]
        halo_hc[...] = hh
        halo_zc[...] = hz
        x1_ref[...] = xv + _mm(jnp.concatenate(parts, axis=0), wout_s[...])

    return pl.pallas_call(
        body, name=f"mix_fwd_{layer}", grid=(nt,),
        in_specs=[_rows(tm, D), _full((1, D))] + _mix_param_specs() + [HBM, HBM],
        out_specs=[_rows(tm, D), _rows(tm, D_IN), _rows(tm, N_SAVED * D_A), _rows(tm, D), _rows(tm, RSTD_LANES)],
        out_shape=[jax.ShapeDtypeStruct((t, D), F32), jax.ShapeDtypeStruct((t, D_IN), BF16),
                   jax.ShapeDtypeStruct((t, N_SAVED * D_A), BF16), jax.ShapeDtypeStruct((t, D), BF16),
                   jax.ShapeDtypeStruct((t, RSTD_LANES), F32)],
        scratch_shapes=[pltpu.VMEM((D_IN, D), BF16), pltpu.VMEM((D, D), BF16),
                        pltpu.VMEM((HALO, D_B), F32), pltpu.VMEM((HALO, D_C), F32), pltpu.SemaphoreType.DMA((2,))],
        compiler_params=_params(),
    )(x, g_mix, *[mp[k] for k in _MIX_PARAM_NAMES], _in_hbm(w_in_t), _in_hbm(w_out))


def _ffn_fwd_call(x1, p, g_ff, g_ple, w1, w2, wg, wp, tm, layer):
    t = x1.shape[0]
    nt = t // tm

    def body(x1_ref, p_ref, gff_ref, gple_ref, w1_hbm, w2_hbm, wg_hbm, wp_hbm,
             x3_ref, r_ref, gate_ref, xh1_ref, rs1_ref, xh2_ref, rs2_ref, w1_s, w2_s, wg_s, wp_s, load_sem):
        i = pl.program_id(0)
        loads = _loads([(w1_hbm, w1_s), (w2_hbm, w2_s), (wg_hbm, wg_s), (wp_hbm, wp_s)], load_sem)

        @pl.when(i == 0)
        def _():
            _load_all(loads)

        x1v = x1_ref[...]
        h2, xhat, rstd = _rms(x1v, gff_ref[...])
        _save_norm(xhat, rstd, xh1_ref, rs1_ref)
        h2b = h2.astype(BF16)
        acc = x1v
        for j in range(D_FF // FF_BLK):
            blk = slice(j * FF_BLK, (j + 1) * FF_BLK)
            rj = jnp.maximum(_mm(h2b, w1_s[:, blk]), 0.0)
            r_ref[:, blk] = rj.astype(BF16)
            acc = acc + _mm((rj * rj).astype(BF16), w2_s[blk, :])
        n3, xhat, rstd = _rms(acc, gple_ref[...])
        _save_norm(xhat, rstd, xh2_ref, rs2_ref)
        gate = jax.nn.sigmoid(_mm(n3.astype(BF16), wg_s[...]))
        gate_ref[...] = gate.astype(BF16)
        pp = _mm(p_ref[...].astype(BF16), wp_s[...])
        x3_ref[...] = acc + pp * gate

    return pl.pallas_call(
        body, name=f"ffn_fwd_{layer}", grid=(nt,),
        in_specs=[_rows(tm, D), _layer_rows(layer, tm, D_PLE), _full((1, D)), _full((1, D)), HBM, HBM, HBM, HBM],
        out_specs=[_rows(tm, D), _rows(tm, D_FF), _rows(tm, D), _rows(tm, D), _rows(tm, RSTD_LANES), _rows(tm, D),
                   _rows(tm, RSTD_LANES)],
        out_shape=[jax.ShapeDtypeStruct((t, D), F32), jax.ShapeDtypeStruct((t, D_FF), BF16), jax.ShapeDtypeStruct((t, D), BF16),
                   jax.ShapeDtypeStruct((t, D), BF16), jax.ShapeDtypeStruct((t, RSTD_LANES), F32),
                   jax.ShapeDtypeStruct((t, D), BF16), jax.ShapeDtypeStruct((t, RSTD_LANES), F32)],
        scratch_shapes=[pltpu.VMEM((D, D_FF), BF16), pltpu.VMEM((D_FF, D), BF16),
                        pltpu.VMEM((D, D), BF16), pltpu.VMEM((D_PLE, D), BF16), pltpu.SemaphoreType.DMA((4,))],
        compiler_params=_params(),
    )(x1, p, g_ff, g_ple, _in_hbm(w1), _in_hbm(w2), _in_hbm(wg), _in_hbm(wp))


def _loss_call(xl, target, final_g, tm):
    t = xl.shape[0]
    nt = t // tm

    def body(x_ref, t_ref, g_ref, sq_ref, dx_ref, dg_ref):
        i = pl.program_id(0)

        @pl.when(i == 0)
        def _():
            sq_ref[...] = jnp.zeros_like(sq_ref)
            dg_ref[...] = jnp.zeros_like(dg_ref)

        g = g_ref[...]
        y, xhat, rstd = _rms(x_ref[...], g)
        err = y - t_ref[...]
        sq_ref[...] += jnp.sum(err * err, axis=0, keepdims=True)
        dx, dg = _rms_bwd(err * (1.0 / D), g, xhat, rstd)
        dx_ref[...] = dx
        dg_ref[...] += dg

    return pl.pallas_call(
        body, name="loss_head", grid=(nt,),
        in_specs=[_rows(tm, D), _rows(tm, D), _full((1, D))],
        out_specs=[_full((1, D)), _rows(tm, D), _full((1, D))],
        out_shape=[jax.ShapeDtypeStruct((1, D), F32), jax.ShapeDtypeStruct((t, D), F32), jax.ShapeDtypeStruct((1, D), F32)],
        compiler_params=_params(),
    )(xl, target, final_g)


def _ple_bwd_call(dx3, xhat2, rstd2, gate, p, g_ple, wg, wp, after, tm, layer):
    t = dx3.shape[0]
    nt = t // tm

    def body(dx3_ref, xhat_ref, rstd_ref, gate_ref, p_ref, g_ref, wg_hbm, wp_hbm, after_ref,
             dx2_ref, dg_ref, dwg_hbm, dwp_hbm, wg_s, wp_s, dwg_acc, dwp_acc, load_sem):
        i = pl.program_id(0)
        loads = _loads([(wp_hbm, wp_s), (wg_hbm, wg_s)], load_sem)

        @pl.when(i == 0)
        def _():
            _load_all(loads)
            dwg_acc[...] = jnp.zeros_like(dwg_acc)
            dwp_acc[...] = jnp.zeros_like(dwp_acc)
            dg_ref[...] = jnp.zeros_like(dg_ref)

        g = g_ref[...]
        dx3v = dx3_ref[...]
        gatev = gate_ref[...].astype(F32)
        pb = p_ref[...].astype(BF16)
        pp = _mm(pb, wp_s[...])
        dwp_acc[...] += _mm_tn(pb, (dx3v * gatev).astype(BF16))
        dgpre = (dx3v * pp * gatev * (1.0 - gatev)).astype(BF16)
        n3, xhat, rstd = _saved_norm(xhat_ref, rstd_ref, g)
        dwg_acc[...] += _mm_tn(n3.astype(BF16), dgpre)
        dn3 = _mm_nt(dgpre, wg_s[...])
        dx, dg = _rms_bwd(dn3, g, xhat, rstd)
        dx2_ref[...] = dx3v + dx
        dg_ref[...] += dg

        @pl.when(i == nt - 1)
        def _():
            _stage_bf16(dwg_acc, wg_s)
            _stage_bf16(dwp_acc, wp_s)
            pltpu.sync_copy(wg_s, dwg_hbm)
            pltpu.sync_copy(wp_s, dwp_hbm)

    return pl.pallas_call(
        body, name=f"ple_bwd_{layer}", grid=(nt,),
        in_specs=[_rows(tm, D), _rows(tm, D), _rows(tm, RSTD_LANES), _rows(tm, D), _layer_rows(layer, tm, D_PLE), _full((1, D)),
                  HBM, HBM, ANY],
        out_specs=[_rows(tm, D), _full((1, D)), HBM, HBM],
        out_shape=[jax.ShapeDtypeStruct((t, D), F32), jax.ShapeDtypeStruct((1, D), F32),
                   pltpu.HBM((D, D), BF16), pltpu.HBM((D_PLE, D), BF16)],
        scratch_shapes=[pltpu.VMEM((D, D), BF16), pltpu.VMEM((D_PLE, D), BF16),
                        pltpu.VMEM((D, D), F32), pltpu.VMEM((D_PLE, D), F32), pltpu.SemaphoreType.DMA((2,))],
        compiler_params=_params(),
    )(dx3, xhat2, rstd2, gate, p, g_ple, _in_hbm(wg), _in_hbm(wp), after)


def _ffn_bwd_hidden_call(dx2, r, w2, tm, layer):
    t = dx2.shape[0]
    nt = t // tm

    def body(dx2_ref, r_ref, w2_hbm, da_ref, dw2_hbm, w2_s, dw2_acc, load_sem):
        i = pl.program_id(0)
        loads = _loads([(w2_hbm, w2_s)], load_sem)

        @pl.when(i == 0)
        def _():
            _load_all(loads)
            dw2_acc[...] = jnp.zeros_like(dw2_acc)

        dxb = dx2_ref[...].astype(BF16)
        for j in range(D_FF // FF_BLK):
            blk = slice(j * FF_BLK, (j + 1) * FF_BLK)
            rj = r_ref[:, blk].astype(F32)
            ds = _mm_nt(dxb, w2_s[blk, :])
            da_ref[:, blk] = (2.0 * rj * ds).astype(BF16)
            dw2_acc[blk, :] += _mm_tn((rj * rj).astype(BF16), dxb)

        @pl.when(i == nt - 1)
        def _():
            _stage_bf16(dw2_acc, w2_s)
            pltpu.sync_copy(w2_s, dw2_hbm)

    return pl.pallas_call(
        body, name=f"ffn_bwd_hidden_{layer}", grid=(nt,),
        in_specs=[_rows(tm, D), _rows(tm, D_FF), HBM],
        out_specs=[_rows(tm, D_FF), HBM],
        out_shape=[jax.ShapeDtypeStruct((t, D_FF), BF16), pltpu.HBM((D_FF, D), BF16)],
        scratch_shapes=[pltpu.VMEM((D_FF, D), BF16), pltpu.VMEM((D_FF, D), F32), pltpu.SemaphoreType.DMA((1,))],
        compiler_params=_params(),
    )(dx2, r, _in_hbm(w2))


def _ffn_bwd_input_call(da, xhat1, rstd1, dx2, g_ff, w1, tm, layer):
    t = dx2.shape[0]
    nt = t // tm

    def body(da_ref, xhat_ref, rstd_ref, dx2_ref, g_ref, w1_hbm, dx1_ref, dg_ref, dw1_hbm, w1_s, dw1_acc, load_sem):
        i = pl.program_id(0)
        loads = _loads([(w1_hbm, w1_s)], load_sem)

        @pl.when(i == 0)
        def _():
            _load_all(loads)
            dw1_acc[...] = jnp.zeros_like(dw1_acc)
            dg_ref[...] = jnp.zeros_like(dg_ref)

        g = g_ref[...]
        h2, xhat, rstd = _saved_norm(xhat_ref, rstd_ref, g)
        h2b = h2.astype(BF16)
        dh2 = jnp.zeros((tm, D), F32)
        for j in range(D_FF // FF_BLK):
            blk = slice(j * FF_BLK, (j + 1) * FF_BLK)
            daj = da_ref[:, blk]
            dh2 = dh2 + _mm_nt(daj, w1_s[:, blk])
            dw1_acc[:, blk] += _mm_tn(h2b, daj)
        dx, dg = _rms_bwd(dh2, g, xhat, rstd)
        dx1_ref[...] = dx2_ref[...] + dx
        dg_ref[...] += dg

        @pl.when(i == nt - 1)
        def _():
            _stage_bf16(dw1_acc, w1_s)
            pltpu.sync_copy(w1_s, dw1_hbm)

    return pl.pallas_call(
        body, name=f"ffn_bwd_input_{layer}", grid=(nt,),
        in_specs=[_rows(tm, D_FF), _rows(tm, D), _rows(tm, RSTD_LANES), _rows(tm, D), _full((1, D)), HBM],
        out_specs=[_rows(tm, D), _full((1, D)), HBM],
        out_shape=[jax.ShapeDtypeStruct((t, D), F32), jax.ShapeDtypeStruct((1, D), F32), pltpu.HBM((D, D_FF), BF16)],
        scratch_shapes=[pltpu.VMEM((D, D_FF), BF16), pltpu.VMEM((D, D_FF), F32), pltpu.SemaphoreType.DMA((1,))],
        compiler_params=_params(),
    )(da, xhat1, rstd1, dx2, g_ff, _in_hbm(w1))


def _mix_bwd_call(dx1, xhat0, rstd0, proj, saved, g_mix, w_in_t, w_out, mp, wtcat, trilcat, headsel, after, tm, run, layer):
    t = dx1.shape[0]
    nt = t // tm
    nrun = tm // run
    nch = run // CHUNK
    hb = tm // HALO

    def rev(i):
        return nt - 1 - i

    def body(dx1_ref, xhat_ref, rstd_ref, proj_ref, halo_ref, saved_ref, g_ref, pmat, ln_g, ln_b, wcat, bmat, conv_w, bd, pool_scale,
             wtcat_ref, tril_ref, sel_ref, win_hbm, wout_hbm, after_ref,
             dx_ref, dg_ref, dwcat_ref, dsb_ref, dlng_ref, dlnb_ref, dconv_ref, dbd_ref, dscale_ref, dwin_hbm, dwout_hbm,
             win_s, wout_s, dwin_acc, dwout_acc, dbm_acc, carry_yc, carry_q, load_sem):
        i = pl.program_id(0)
        ri = nt - 1 - i
        loads = _loads([(wout_hbm, wout_s), (win_hbm, win_s)], load_sem)

        @pl.when(i == 0)
        def _():
            _load_all(loads)
            for ref in (dwin_acc, dwout_acc, dbm_acc, carry_yc, carry_q, dg_ref, dwcat_ref, dlng_ref, dlnb_ref,
                        dconv_ref, dbd_ref, dscale_ref):
                ref[...] = jnp.zeros_like(ref)

        prm = dict(pmat=pmat, ln_g=ln_g, ln_b=ln_b, wcat=wcat, bmat=bmat, conv_w=conv_w, bd=bd, pool_scale=pool_scale)
        g = g_ref[...]
        h, xhat, rstd = _saved_norm(xhat_ref, rstd_ref, g)
        hb16 = h.astype(BF16)
        dx1v = dx1_ref[...]
        dx1b = dx1v.astype(BF16)
        dycat = _mm_nt(dx1b, wout_s[...])
        lng = ln_g[...]
        pm_ = pmat[...]
        cy, cq = carry_yc[...], carry_q[...]
        ycat_parts, dproj_parts = [None] * nrun, [None] * nrun
        dbm = dlng = dlnb = dscale = dcv0 = dcv1 = dcv2 = None
        add = lambda tot, v: v if tot is None else tot + v
        for c in reversed(range(nrun)):
            rows = slice(c * run, (c + 1) * run)
            pf = proj_ref[rows, :].astype(F32)
            if c > 0:
                ph = proj_ref[c * run - HALO:c * run, :].astype(F32)
            else:
                ph = halo_ref[...].astype(F32) * (ri > 0).astype(F32)
            r = _mixers_fwd(pf, ph[:, 1536:1920] * ph[:, 768:1152], ph[:, 1920:2176], ri * tm + c * run, prm,
                            saved_ref[rows, :].astype(F32))
            ycat_parts[c] = r["ycat"].astype(BF16)
            dya, dyb, dyc = dycat[rows, 0:D_A], dycat[rows, D_A:D_A + D_B], dycat[rows, D_A + D_B:D]

            dgu = dya * r["mixed"]
            dmix = dya * r["gu"]
            dmix_b = dmix.astype(BF16)
            dlo, dhi = _head_halves(dmix)
            for k in range(nch):
                dbm = add(dbm, dmix[k * CHUNK:(k + 1) * CHUNK, :])
            dvn_cols = []
            for j in range(3):
                dwcat_ref[j] += _mm_nt(_chunks_to_lanes(dmix_b, j, nch), r["v2s"][j])
                dvn_cols.append(_lanes_to_chunks(_mm(wtcat_ref[j], _head_stack(dlo, dhi, j, nch)), nch))
            dvn = jnp.concatenate(dvn_cols, axis=1)
            xh = r["xh"]
            dlng = add(dlng, jnp.sum(dvn * xh, axis=0, keepdims=True))
            dlnb = add(dlnb, jnp.sum(dvn, axis=0, keepdims=True))
            dxh = dvn * lng
            m1 = _mm(dxh.astype(BF16), pm_)
            m2 = _mm((dxh * xh).astype(BF16), pm_)
            dgv = r["ln_rstd"] * (dxh - m1 - xh * m2)
            du = dgu * r["dgelu_u"]
            dv = dgv * r["dgelu_v"]

            dgb = dyb * r["yc"]
            dyc2 = dyb * r["gb"]
            dcv0 = add(dcv0, jnp.sum(dyc2 * r["h2"], axis=0, keepdims=True))
            dcv1 = add(dcv1, jnp.sum(dyc2 * r["h1"], axis=0, keepdims=True))
            dcv2 = add(dcv2, jnp.sum(dyc2 * r["hc"], axis=0, keepdims=True))
            ext = jnp.concatenate([dyc2, cy], axis=0)
            dhc = r["w2"] * dyc2 + r["w1"] * _shift_up(ext, 1) + r["w0"] * _shift_up(ext, 2)
            cy = dyc2[0:HALO, :]
            dgc = dhc * r["zb"]
            dzb = dhc * r["gc"]

            dscale = add(dscale, jnp.sum(dyc * r["pm"], axis=0, keepdims=True))
            dpm = (dyc * r["scale"]).astype(BF16)
            dbd_ref[...] += _mm_tn(r["pooledb"], dpm)
            dpooled = _mm_nt(dpm, bd[...])
            q = dpooled * r["inv"]
            ext = jnp.concatenate([q, cq], axis=0)
            n = run + HALO
            r2 = ext + pltpu.roll(ext, n - 1, 0)
            r4 = r2 + pltpu.roll(r2, n - 2, 0)
            r8 = r4 + pltpu.roll(r4, n - 4, 0)
            r16 = r8 + pltpu.roll(r8, n - 8, 0)
            dzc = _pool_select(r2, r4, r8, r16)[0:run, :] - dpooled
            cq = q[0:HALO, :]
            dproj_parts[c] = jnp.concatenate([du, dv, dzb, dgb, dgc, dzc], axis=1).astype(BF16)

        carry_yc[...] = cy
        carry_q[...] = cq
        dbm_acc[...] += dbm
        dlng_ref[...] += dlng
        dlnb_ref[...] += dlnb
        dscale_ref[...] += dscale
        dconv_ref[0:1, :] += dcv0
        dconv_ref[1:2, :] += dcv1
        dconv_ref[2:3, :] += dcv2
        dwout_acc[...] += _mm_tn(jnp.concatenate(ycat_parts, axis=0), dx1b)
        dproj = jnp.concatenate(dproj_parts, axis=0)
        dwin_acc[...] += _mm_tn(dproj, hb16)
        dh = _mm(dproj, win_s[...])
        dx, dg = _rms_bwd(dh, g, xhat, rstd)
        dx_ref[...] = dx1v + dx
        dg_ref[...] += dg

        @pl.when(i == nt - 1)
        def _():
            _stage_bf16(dwin_acc, win_s)
            pltpu.sync_copy(win_s, dwin_hbm)
            _stage_bf16(dwout_acc, wout_s)
            pltpu.sync_copy(wout_s, dwout_hbm)
            for j in range(3):
                dwcat_ref[j] = dwcat_ref[j] * tril_ref[...]
            acc = dbm_acc[...]
            hi = acc.astype(BF16)
            lo = (acc - hi.astype(F32)).astype(BF16)
            dsb_ref[...] = _mm(hi, sel_ref[...]) + _mm(lo, sel_ref[...])

    return pl.pallas_call(
        body, name=f"mix_bwd_{layer}", grid=(nt,),
        in_specs=[pl.BlockSpec((tm, D), lambda i: (rev(i), 0)), pl.BlockSpec((tm, D), lambda i: (rev(i), 0)),
                  pl.BlockSpec((tm, RSTD_LANES), lambda i: (rev(i), 0)), pl.BlockSpec((tm, D_IN), lambda i: (rev(i), 0)),
                  pl.BlockSpec((HALO, D_IN), lambda i: (jnp.maximum(rev(i) * hb - 1, 0), 0)),
                  pl.BlockSpec((tm, N_SAVED * D_A), lambda i: (rev(i), 0)), _full((1, D))] + _mix_param_specs()
                 + [_full((3, CHUNK, 2 * CHUNK)), _full((CHUNK, 2 * CHUNK)), _full((D_A, CHUNK)), HBM, HBM, ANY],
        out_specs=[pl.BlockSpec((tm, D), lambda i: (rev(i), 0)), _full((1, D)), _full((3, CHUNK, 2 * CHUNK)),
                   _full((CHUNK, CHUNK)), _full((1, D_A)), _full((1, D_A)), _full((3, D_B)), _full((D_C, D_C)),
                   _full((1, D_C)), HBM, HBM],
        out_shape=[jax.ShapeDtypeStruct((t, D), F32), jax.ShapeDtypeStruct((1, D), F32),
                   jax.ShapeDtypeStruct((3, CHUNK, 2 * CHUNK), F32), jax.ShapeDtypeStruct((CHUNK, CHUNK), F32),
                   jax.ShapeDtypeStruct((1, D_A), F32), jax.ShapeDtypeStruct((1, D_A), F32),
                   jax.ShapeDtypeStruct((3, D_B), F32), jax.ShapeDtypeStruct((D_C, D_C), F32),
                   jax.ShapeDtypeStruct((1, D_C), F32), pltpu.HBM((D_IN, D), BF16), pltpu.HBM((D, D), BF16)],
        scratch_shapes=[pltpu.VMEM((D_IN, D), BF16), pltpu.VMEM((D, D), BF16),
                        pltpu.VMEM((D_IN, D), F32), pltpu.VMEM((D, D), F32), pltpu.VMEM((CHUNK, D_A), F32),
                        pltpu.VMEM((HALO, D_B), F32), pltpu.VMEM((HALO, D_C), F32), pltpu.SemaphoreType.DMA((2,))],
        compiler_params=_params(),
    )(dx1, xhat0, rstd0, proj, proj, saved, g_mix, *[mp[k] for k in _MIX_PARAM_NAMES], wtcat, trilcat, headsel, _in_hbm(w_in_t), _in_hbm(w_out), after)


def _coords():
    return lax.axis_index("x"), lax.axis_index("y"), lax.axis_index("c")


EFFECT = pltpu.SideEffectType.DATAFLOW_SIDE_EFFECTING


def _peer(k):
    x, y, c = _coords()
    px, py, pc = x ^ (k >> 2), y ^ ((k >> 1) & 1), c ^ (k & 1)
    return (px, py, pc), 4 * px + 2 * py + pc


def _landing_shape(shape, mode):
    if mode == "block":
        return (N_DEV,) + shape
    if mode == "slot":
        return shape
    if mode == "cols_in":
        return (shape[0], N_DEV * shape[1])
    return (N_DEV, shape[0], shape[1] // N_DEV)


def _pieces(src, land, mode, src_idx, land_idx):
    if mode == "block":
        return src, land.at[land_idx]
    if mode == "slot":
        return src.at[src_idx], land.at[land_idx]
    if mode == "cols_in":
        cw = src.shape[1]
        return src, land.at[:, pl.ds(pl.multiple_of(land_idx * cw, 128), cw)]
    cw = land.shape[2]
    return src.at[:, pl.ds(pl.multiple_of(src_idx * cw, 128), cw)], land.at[land_idx]


def _exchange_copy(src, land, mode, send_sem, recv_sem, ai, k, starting):
    x, y, c = _coords()
    peer, pidx = _peer(k)
    s, d = _pieces(src, land, mode, pidx, 4 * x + 2 * y + c if starting else pidx)
    i = ai * (N_DEV - 1) + k - 1
    return pltpu.make_async_remote_copy(src_ref=s, dst_ref=d, send_sem=send_sem.at[i], recv_sem=recv_sem.at[i],
                                        device_id=peer, device_id_type=MESH)


def _own_copy(src, land, mode, local_sem, ai):
    x, y, c = _coords()
    me = 4 * x + 2 * y + c
    s, d = _pieces(src, land, mode, me, me)
    return pltpu.make_async_copy(s, d, local_sem.at[ai])


def _item_src(ins, item):
    a, sub = item
    return ins[a] if sub is None else ins[a].at[sub]


def _exchange_start(srcs, items, modes, groups, name):
    n, ni, ng = len(srcs), len(items), len(groups)
    shapes = [srcs[a].shape if sub is None else srcs[a].shape[1:] for a, sub in items]
    land_shapes = [pltpu.HBM(_landing_shape(sh, m), srcs[a].dtype) for sh, m, (a, _) in zip(shapes, modes, items)]

    def body(*refs):
        ins = refs[:n]
        sems = refs[n:n + 3 * ng]
        land_refs = refs[n + 3 * ng:n + 3 * ng + ni]
        token = refs[-1]
        for g, idxs in enumerate(groups):
            for ai, it in enumerate(idxs):
                src = _item_src(ins, items[it])
                _own_copy(src, land_refs[it], modes[it], sems[3 * g + 2], ai).start()
                for k in range(1, N_DEV):
                    _exchange_copy(src, land_refs[it], modes[it], sems[3 * g], sems[3 * g + 1], ai, k, True).start()
        token[...] = jnp.zeros_like(token)

    sem_shapes = []
    for idxs in groups:
        sem_shapes += [pltpu.SemaphoreType.DMA((len(idxs) * (N_DEV - 1),))] * 2 + [pltpu.SemaphoreType.DMA((len(idxs),))]
    out = pl.pallas_call(
        body, name=name,
        out_shape=tuple(sem_shapes) + tuple(land_shapes) + (jax.ShapeDtypeStruct((8, 128), F32),),
        in_specs=[HBM] * n,
        out_specs=tuple([SEM] * (3 * ng) + [HBM] * ni + [pl.BlockSpec(memory_space=pltpu.VMEM)]),
        compiler_params=pltpu.CompilerParams(has_side_effects=EFFECT),
    )(*[pltpu.with_memory_space_constraint(s, pltpu.HBM) for s in srcs])
    sems = [tuple(out[3 * g:3 * g + 3]) for g in range(ng)]
    return sems, list(out[3 * ng:3 * ng + ni]), out[-1]


def _exchange_wait(sems, srcs, items, lands, modes, groups, after, name):
    n, ni, ng = len(srcs), len(items), len(groups)

    def body(*refs):
        ins, land_refs = refs[:n], refs[n:n + ni]
        sem_refs = refs[n + ni:n + ni + 3 * ng]
        for g, idxs in enumerate(groups):
            for ai, it in enumerate(idxs):
                src = _item_src(ins, items[it])
                _own_copy(src, land_refs[it], modes[it], sem_refs[3 * g + 2], ai).wait()
                for k in range(1, N_DEV):
                    cp = _exchange_copy(src, land_refs[it], modes[it], sem_refs[3 * g], sem_refs[3 * g + 1], ai, k, False)
                    cp.wait_send()
                    cp.wait_recv()

    flat_sems = [s for trio in sems for s in trio]
    afters = list(after) if isinstance(after, (list, tuple)) else [after]
    out = pl.pallas_call(
        body, name=name,
        out_shape=tuple(pltpu.HBM(l.shape, l.dtype) for l in lands),
        in_specs=[HBM] * (n + ni) + [SEM] * (3 * ng) + [ANY] * len(afters),
        out_specs=tuple([HBM] * ni),
        input_output_aliases={n + i: i for i in range(ni)},
        compiler_params=pltpu.CompilerParams(has_side_effects=EFFECT),
    )(*srcs, *lands, *flat_sems, *afters)
    return list(out)


def _slot_sum_call(landed):
    n = len(landed)

    def body(*refs):
        for src, dst in zip(refs[:n], refs[n:]):
            tot = src[0]
            for j in range(1, N_DEV):
                tot = tot + src[j]
            dst[...] = tot

    vm = pl.BlockSpec(memory_space=pltpu.VMEM)
    return pl.pallas_call(
        body, name="small_grads_sum", in_specs=[vm] * n, out_specs=[vm] * n,
        out_shape=[jax.ShapeDtypeStruct(a.shape[1:], F32) for a in landed],
        compiler_params=pltpu.CompilerParams(vmem_limit_bytes=V7X_VMEM_LIMIT),
    )(*landed)


def _adamw(w, g, m, v):
    m = ADAM_B1 * m + (1.0 - ADAM_B1) * g
    v = ADAM_B2 * v + (1.0 - ADAM_B2) * (g * g)
    m_hat = m / (1.0 - ADAM_B1 ** ADAM_STEP)
    v_hat = v / (1.0 - ADAM_B2 ** ADAM_STEP)
    delta = -ADAM_LR * (m_hat / (jnp.sqrt(v_hat) + ADAM_EPS) + ADAM_WD * w)
    return delta, m, v


def _reduce_adamw_call(recvs, w, m, v, name):
    nl = len(recvs)
    _, r, c = recvs[0].shape
    rb = 256 if r % 256 == 0 else r
    nb = r // rb

    def body(*refs):
        recv_refs = refs[:nl]
        w_ref, m_ref, v_ref, g_ref, d_ref, nm_ref, nv_ref = refs[nl:]
        for l in range(nl):
            @pl.when(pl.program_id(0) == l)
            def _(l=l):
                g = recv_refs[l][0].astype(F32)
                for j in range(1, N_DEV):
                    g = g + recv_refs[l][j].astype(F32)
                delta, nm, nv = _adamw(w_ref[0], g, m_ref[0], v_ref[0])
                g_ref[0] = g
                d_ref[0] = delta
                nm_ref[0] = nm
                nv_ref[0] = nv

    def recv_spec(l):
        return pl.BlockSpec((N_DEV, rb, c), lambda lg, i: (0, jnp.where(lg == l, i, jnp.where(lg < l, 0, nb - 1)), 0))

    blk = pl.BlockSpec((1, rb, c), lambda lg, i: (lg, i, 0))
    shp = jax.ShapeDtypeStruct((nl, r, c), F32)
    return pl.pallas_call(
        body, name=name, grid=(nl, nb),
        in_specs=[recv_spec(l) for l in range(nl)] + [blk, blk, blk],
        out_specs=[blk, blk, blk, blk], out_shape=[shp, shp, shp, shp],
        compiler_params=pltpu.CompilerParams(dimension_semantics=("arbitrary", "arbitrary"), vmem_limit_bytes=V7X_VMEM_LIMIT),
    )(*recvs, w, m, v)


def _small_adamw_call(w, g, m, v):
    def body(w_ref, g_ref, m_ref, v_ref, d_ref, nm_ref, nv_ref):
        delta, nm, nv = _adamw(w_ref[...], g_ref[...], m_ref[...], v_ref[...])
        d_ref[...] = delta
        nm_ref[...] = nm
        nv_ref[...] = nv

    shp = jax.ShapeDtypeStruct(w.shape, F32)
    vm = pl.BlockSpec(memory_space=pltpu.VMEM)
    return pl.pallas_call(body, name="small_adamw", in_specs=[vm] * 4, out_specs=[vm] * 3, out_shape=[shp, shp, shp],
                          compiler_params=pltpu.CompilerParams(vmem_limit_bytes=V7X_VMEM_LIMIT))(w, g, m, v)


_GATHER_MODE = dict(w_in="block", w_out="block", w_ff1="cols_in", w_ff2="block", w_ple_gate="block", w_ple_proj="cols_in")
_SCATTER_MODE = dict(w_in="slot", w_out="slot", w_ff1="cols_out", w_ff2="slot", w_ple_gate="slot", w_ple_proj="cols_out")
_GROUP_A = ("w_in", "w_out")
_GROUP_B = ("w_ff1", "w_ff2", "w_ple_gate", "w_ple_proj")


def _gathered_full(k, landed):
    if _GATHER_MODE[k] == "cols_in":
        return landed
    n, r, c = landed.shape
    return landed.reshape(n * r, c)


def _grad_send(k, g):
    if _SCATTER_MODE[k] == "cols_out":
        return g
    r8, c = g.shape
    return g.reshape(N_DEV, r8 // N_DEV, c)


_SMALL_ORDER = ("norm_mix_g", "sgu_w", "sgu_b", "sgu_ln_g", "sgu_ln_b", "conv_w", "pool_w", "pool_scale",
                "norm_ff_g", "norm_ple_g", "final_g")


def _pack_small(d, order=_SMALL_ORDER):
    pieces, layout = [], []
    for k in order:
        flat = d[k].reshape(-1)
        n = flat.shape[0]
        pad = (-n) % 128
        pieces.append(jnp.pad(flat, (0, pad)))
        layout.append((k, d[k].shape, n, n + pad))
    flat = jnp.concatenate(pieces)
    pad = (-flat.shape[0]) % 1024
    return jnp.pad(flat, (0, pad)).reshape(-1, 128), layout


def _unpack_small(buf, layout):
    flat = buf.reshape(-1)
    out, off = {}, 0
    for k, shape, n, padded in layout:
        out[k] = flat[off:off + n].reshape(shape)
        off += padded
    return out


def kernel(x, p, norm_mix_g, w_in, sgu_w, sgu_b, sgu_ln_g, sgu_ln_b, conv_w, pool_w, pool_scale, w_out, norm_ff_g, w_ff1, w_ff2, norm_ple_g, w_ple_gate, w_ple_proj, final_g, loss_target, m_norm_mix_g, m_w_in, m_sgu_w, m_sgu_b, m_sgu_ln_g, m_sgu_ln_b, m_conv_w, m_pool_w, m_pool_scale, m_w_out, m_norm_ff_g, m_w_ff1, m_w_ff2, m_norm_ple_g, m_w_ple_gate, m_w_ple_proj, m_final_g, v_norm_mix_g, v_w_in, v_sgu_w, v_sgu_b, v_sgu_ln_g, v_sgu_ln_b, v_conv_w, v_pool_w, v_pool_scale, v_w_out, v_norm_ff_g, v_w_ff1, v_w_ff2, v_norm_ple_g, v_w_ple_gate, v_w_ple_proj, v_final_g):
    t = x.shape[1]
    xc, yc_, cc = _coords()
    me = 4 * xc + 2 * yc_ + cc
    tm = lambda want: min(want, t)

    shard_names = _GROUP_A + _GROUP_B
    swap = lambda a: jnp.transpose(a, (0, 2, 1))
    shard = dict(w_in=swap(w_in), w_out=w_out, w_ff1=w_ff1, w_ff2=w_ff2, w_ple_gate=w_ple_gate, w_ple_proj=w_ple_proj)
    conv_pad = jnp.zeros((16, 128), F32).at[0:DEPTH * 3, 0:D_B // N_DEV].set(conv_w.reshape(DEPTH * 3, D_B // N_DEV))
    ag_srcs = [shard[k].astype(BF16) for k in shard_names] + [conv_pad]
    ag_items, ag_modes, ag_groups = [], [], []
    for l in range(DEPTH):
        for names in (_GROUP_A, _GROUP_B):
            ag_groups.append(list(range(len(ag_items), len(ag_items) + len(names))))
            ag_items += [(shard_names.index(k), l) for k in names]
            ag_modes += [_GATHER_MODE[k] for k in names]
            if l == 0 and names is _GROUP_A:
                ag_groups[-1].append(len(ag_items))
                ag_items.append((len(shard_names), None))
                ag_modes.append("block")
    ag_sems, ag_lands, _ = _exchange_start(ag_srcs, ag_items, ag_modes, ag_groups, "weights_gather_start")

    def gathered(l, which, after):
        idxs = ag_groups[2 * l + which]
        landed = _exchange_wait([ag_sems[2 * l + which]], ag_srcs, [ag_items[i] for i in idxs], [ag_lands[i] for i in idxs],
                                [ag_modes[i] for i in idxs], [list(range(len(idxs)))], after, f"weights_gather_wait_{l}_{which}")
        full = {k: _gathered_full(k, got) for k, got in zip((_GROUP_A, _GROUP_B)[which], landed)}
        if l == 0 and which == 0:
            full["conv_w"] = jnp.transpose(landed[-1][:, 0:DEPTH * 3, 0:D_B // N_DEV].reshape(N_DEV, DEPTH, 3, D_B // N_DEV),
                                           (1, 2, 0, 3)).reshape(DEPTH, 3, D_B)
        return full

    idx = jnp.arange(D_A)
    pmat = ((idx[:, None] // 64) == (idx[None, :] // 64)).astype(BF16) * (1.0 / 64.0)
    pmat = pmat.astype(BF16)
    tril = jnp.tril(jnp.ones((CHUNK, CHUNK), F32))
    trilcat = jnp.concatenate([tril, tril], axis=1)
    headsel = ((idx[:, None] // 64) == jnp.arange(CHUNK)[None, :]).astype(BF16)
    row = lambda a: a.reshape(1, -1)

    def mix_params(l):
        wm = sgu_w[l] * tril[None]
        wcat = jnp.stack([jnp.concatenate([wm[2 * j], wm[2 * j + 1]], axis=1) for j in range(3)]).astype(BF16)
        wtcat = jnp.stack([jnp.concatenate([wm[2 * j].T, wm[2 * j + 1].T], axis=1) for j in range(3)]).astype(BF16)
        bmat = jnp.repeat(sgu_b[l].T, 64, axis=1)
        bd = jnp.zeros((D_C, D_C), F32)
        for gi in range(4):
            bd = bd.at[gi * 64:(gi + 1) * 64, gi * 64:(gi + 1) * 64].set(pool_w[l, gi])
        mp = dict(pmat=pmat, ln_g=row(sgu_ln_g[l]), ln_b=row(sgu_ln_b[l]), wcat=wcat, bmat=bmat, conv_w=conv_full[l],
                  bd=bd.astype(BF16), pool_scale=row(pool_scale[l]))
        return mp, wtcat

    xs = x.reshape(t, D)
    p_layers = p.reshape(DEPTH, t, D_PLE)
    saved, full_w = [], []
    conv_full = None
    for l in range(DEPTH):
        wa = gathered(l, 0, xs)
        if l == 0:
            conv_full = wa["conv_w"]
        mp, _ = mix_params(l)
        x1, proj, sgu_saved, xh0, rs0 = _mix_fwd_call(xs, row(norm_mix_g[l]), wa["w_in"], wa["w_out"], mp, tm(TM_MIX_FWD), tm(RUN_MIX_FWD), l)
        wb = gathered(l, 1, x1)
        x3, r, gate, xh1, rs1, xh2, rs2 = _ffn_fwd_call(x1, p_layers, row(norm_ff_g[l]), row(norm_ple_g[l]), wb["w_ff1"], wb["w_ff2"],
                                        wb["w_ple_gate"], wb["w_ple_proj"], tm(TM_FFN_FWD), l)
        saved.append((proj, sgu_saved, r, gate, (xh0, rs0), (xh1, rs1), (xh2, rs2)))
        full_w.append({**wa, **wb})
        xs = x3

    sq, dx, dfinal = _loss_call(xs, loss_target.reshape(t, D), row(final_g), tm(TM_LOSS))
    loss = lax.psum(jnp.sum(sq) * (0.5 / D), ("x", "y", "c"))

    layer_keys = tuple(k for k in _SMALL_ORDER if k != "final_g")
    small = {k: [None] * DEPTH for k in layer_keys}
    small_ex, small_layout = [None] * DEPTH, None
    ex = {}
    token = jnp.zeros((8, 128), F32)

    for l in reversed(range(DEPTH)):
        proj, sgu_saved, r, gate, norm0, norm1, norm2 = saved[l]
        fw = full_w[l]
        dx2, dgple, dwg, dwp = _ple_bwd_call(dx, *norm2, gate, p_layers, row(norm_ple_g[l]), fw["w_ple_gate"], fw["w_ple_proj"], token,
                                             tm(TM_PLE_BWD), l)
        da, dw2 = _ffn_bwd_hidden_call(dx2, r, fw["w_ff2"], tm(TM_FFN_BWD), l)
        dx1, dgff, dw1 = _ffn_bwd_input_call(da, *norm1, dx2, row(norm_ff_g[l]), fw["w_ff1"], tm(TM_FFN_BWD), l)
        sends = [_grad_send(k, g_) for k, g_ in zip(_GROUP_B, (dw1, dw2, dwg, dwp))]
        sems, lands, token = _exchange_start(sends, [(i, None) for i in range(len(sends))], [_SCATTER_MODE[k] for k in _GROUP_B],
                                             [list(range(len(sends)))], f"grads_exchange_start_{l}_1")
        ex[(l, 1)] = (sems[0], sends, lands)
        mp, wtcat = mix_params(l)
        (dx, dgmix, dwcat, dsb, dlng, dlnb, dconv, dbd, dscale, dwin, dwout) = _mix_bwd_call(
            dx1, *norm0, proj, sgu_saved, row(norm_mix_g[l]), fw["w_in"], fw["w_out"], mp, wtcat, trilcat, headsel, token,
            tm(TM_MIX_BWD), tm(RUN_MIX_BWD), l)
        small["norm_mix_g"][l] = dgmix[0]
        small["sgu_w"][l] = jnp.stack([dwcat[h // 2][:, (h % 2) * CHUNK:(h % 2 + 1) * CHUNK] for h in range(6)])
        small["sgu_b"][l] = dsb[:, 0:6].T
        small["sgu_ln_g"][l], small["sgu_ln_b"][l] = dlng[0], dlnb[0]
        small["conv_w"][l] = dconv
        small["pool_w"][l] = jnp.stack([dbd[gi * 64:(gi + 1) * 64, gi * 64:(gi + 1) * 64] for gi in range(4)])
        small["pool_scale"][l] = dscale[0]
        small["norm_ff_g"][l], small["norm_ple_g"][l] = dgff[0], dgple[0]
        layer_small = {k: small[k][l] for k in layer_keys}
        layer_small["final_g"] = dfinal[0] if l == DEPTH - 1 else jnp.zeros_like(dfinal[0])
        sbuf, small_layout = _pack_small(layer_small, layer_keys + ("final_g",))
        sends = [_grad_send("w_in", dwin), _grad_send("w_out", dwout)]
        sems, lands, token = _exchange_start([sbuf] + sends, [(i, None) for i in range(3)],
                                             ["block", _SCATTER_MODE["w_in"], _SCATTER_MODE["w_out"]], [[0], [1, 2]],
                                             f"grads_exchange_start_{l}_0")
        small_ex[l] = (sems[0], sbuf, lands[0])
        ex[(l, 0)] = (sems[1], sends, lands[1:3])
    grad_x = dx.reshape(1, t, D)

    state = dict(w_in=(swap(w_in), swap(m_w_in), swap(v_w_in)), w_out=(w_out, m_w_out, v_w_out), w_ff1=(w_ff1, m_w_ff1, v_w_ff1),
                 w_ff2=(w_ff2, m_w_ff2, v_w_ff2), w_ple_gate=(w_ple_gate, m_w_ple_gate, v_w_ple_gate),
                 w_ple_proj=(w_ple_proj, m_w_ple_proj, v_w_ple_proj))
    res = {}

    def finish_group(which, after):
        names = (_GROUP_A, _GROUP_B)[which]
        n = len(names)
        sems = [ex[(l, which)][0] for l in range(DEPTH)]
        sends = [s_ for l in range(DEPTH) for s_ in ex[(l, which)][1]]
        lands = [a_ for l in range(DEPTH) for a_ in ex[(l, which)][2]]
        landed = _exchange_wait(sems, sends, [(i, None) for i in range(DEPTH * n)], lands, [_SCATTER_MODE[k] for k in names] * DEPTH,
                                [list(range(l * n, (l + 1) * n)) for l in range(DEPTH)], after, f"grads_exchange_wait_{which}")
        for i, k in enumerate(names):
            res[k] = _reduce_adamw_call([landed[l * n + i] for l in range(DEPTH)], *state[k], "reduce_adamw_" + k)

    finish_group(1, dx)

    landed = _exchange_wait([small_ex[l][0] for l in range(DEPTH)], [small_ex[l][1] for l in range(DEPTH)],
                            [(l, None) for l in range(DEPTH)], [small_ex[l][2] for l in range(DEPTH)], ["block"] * DEPTH,
                            [[l] for l in range(DEPTH)], res[_GROUP_B[-1]][0], "small_grads_wait")
    sums = [_unpack_small(b_, small_layout) for b_ in _slot_sum_call(landed)]
    gs = {k: jnp.stack([sums[l][k] for l in range(DEPTH)]) for k in layer_keys}
    gs["final_g"] = sums[DEPTH - 1]["final_g"]
    conv_cols = lambda a: lax.dynamic_slice_in_dim(a, me * (D_B // N_DEV), D_B // N_DEV, axis=2)
    pad_conv = lambda a: jnp.zeros((DEPTH, 3, D_B), F32).at[:, :, 0:D_B // N_DEV].set(a)
    small_w = dict(norm_mix_g=norm_mix_g, sgu_w=sgu_w, sgu_b=sgu_b, sgu_ln_g=sgu_ln_g, sgu_ln_b=sgu_ln_b, conv_w=pad_conv(conv_w),
                   pool_w=pool_w, pool_scale=pool_scale, norm_ff_g=norm_ff_g, norm_ple_g=norm_ple_g, final_g=final_g)
    small_m = dict(norm_mix_g=m_norm_mix_g, sgu_w=m_sgu_w, sgu_b=m_sgu_b, sgu_ln_g=m_sgu_ln_g, sgu_ln_b=m_sgu_ln_b,
                   conv_w=pad_conv(m_conv_w), pool_w=m_pool_w, pool_scale=m_pool_scale, norm_ff_g=m_norm_ff_g,
                   norm_ple_g=m_norm_ple_g, final_g=m_final_g)
    small_v = dict(norm_mix_g=v_norm_mix_g, sgu_w=v_sgu_w, sgu_b=v_sgu_b, sgu_ln_g=v_sgu_ln_g, sgu_ln_b=v_sgu_ln_b,
                   conv_w=pad_conv(v_conv_w), pool_w=v_pool_w, pool_scale=v_pool_scale,
                   norm_ff_g=v_norm_ff_g, norm_ple_g=v_norm_ple_g, final_g=v_final_g)
    gs_local = dict(gs)
    gs_local["conv_w"] = pad_conv(conv_cols(gs["conv_w"]))
    g_loc, layout = _pack_small(gs_local)
    wbuf, _ = _pack_small(small_w)
    mbuf, _ = _pack_small(small_m)
    vbuf, _ = _pack_small(small_v)
    dbuf, nmbuf, nvbuf = _small_adamw_call(wbuf, g_loc, mbuf, vbuf)
    sd, sm, sv = _unpack_small(dbuf, layout), _unpack_small(nmbuf, layout), _unpack_small(nvbuf, layout)
    unconv = lambda a: a[:, :, 0:D_B // N_DEV]
    for dct in (gs_local, sd, sm, sv):
        dct["conv_w"] = unconv(dct["conv_w"])

    finish_group(0, [res[_GROUP_B[-1]][0], dbuf])

    order = ["norm_mix_g", "w_in", "sgu_w", "sgu_b", "sgu_ln_g", "sgu_ln_b", "conv_w", "pool_w", "pool_scale", "w_out",
             "norm_ff_g", "w_ff1", "w_ff2", "norm_ple_g", "w_ple_gate", "w_ple_proj", "final_g"]
    outs = [loss, grad_x]
    for which in range(4):
        for k in order:
            if k in res:
                outs.append(swap(res[k][which]) if k == "w_in" else res[k][which])
            else:
                outs.append((gs_local, sd, sm, sv)[which][k])
    return tuple(outs)
```

```python
import functools
import math

import jax
import jax.numpy as jnp
from jax import lax
from jax.experimental import pallas as pl
from jax.experimental.pallas import tpu as pltpu

F32 = jnp.float32
BF16 = jnp.bfloat16

D = 1024
D_IN = 2176
D_A = 384
D_B = 384
D_C = 256
D_FF = 4096
D_PLE = 256
DEPTH = 4
CHUNK = 128
HALO = 16
FF_BLK = 1024
N_DEV = 8
RMS_EPS = 1e-6
LN_EPS = 1e-5
ADAM_LR = 0.001
ADAM_B1 = 0.9
ADAM_B2 = 0.999
ADAM_EPS = 1e-08
ADAM_WD = 0.01
ADAM_STEP = 10

TM_MIX_FWD = 1024
TM_FFN_FWD = 512
TM_LOSS = 512
TM_PLE_BWD = 1024
TM_FFN_BWD = 512
TM_MIX_BWD = 512
RUN_MIX_FWD = 1024
RUN_MIX_BWD = 512
V7X_VMEM_LIMIT = 60000 * 1024

ANY = pl.BlockSpec(memory_space=pl.ANY)
HBM = pl.BlockSpec(memory_space=pltpu.HBM)
SEM = pl.BlockSpec(memory_space=pltpu.SEMAPHORE)
MESH = pl.DeviceIdType.MESH


def _params(vmem=V7X_VMEM_LIMIT):
    return pltpu.CompilerParams(dimension_semantics=("arbitrary",), vmem_limit_bytes=vmem)


def _in_hbm(a):
    return pltpu.with_memory_space_constraint(a, pltpu.HBM)


def _full(shape):
    nd = len(shape)
    return pl.BlockSpec(shape, lambda i: (0,) * nd)


def _rows(tm, cols):
    return pl.BlockSpec((tm, cols), lambda i: (i, 0))


def _layer_rows(layer, tm, cols):
    return pl.BlockSpec((None, tm, cols), lambda i: (layer, i, 0))


def _mm(a, b):
    return jnp.dot(a, b, preferred_element_type=F32)


def _mm_nt(a, b):
    return lax.dot_general(a, b, (((1,), (1,)), ((), ())), preferred_element_type=F32)


def _mm_tn(a, b):
    return lax.dot_general(a, b, (((0,), (0,)), ((), ())), preferred_element_type=F32)


def _gelu_and_grad(x):
    ax = jnp.abs(x) * (1.0 / math.sqrt(2.0))
    t = 1.0 / (1.0 + 0.3275911 * ax)
    poly = t * (0.254829592 + t * (-0.284496736 + t * (1.421413741 + t * (-1.453152027 + t * 1.061405429))))
    e = jnp.exp(-0.5 * x * x)
    half = 0.5 * poly * e
    cdf = jnp.where(x < 0, half, 1.0 - half)
    return x * cdf, cdf + x * (e * (1.0 / math.sqrt(2.0 * math.pi)))


def _rms(x, g):
    rstd = lax.rsqrt(jnp.mean(x * x, axis=-1, keepdims=True) + RMS_EPS)
    xhat = x * rstd
    return xhat * g, xhat, rstd


RSTD_LANES = 128


def _save_norm(xhat, rstd, xhat_ref, rstd_ref):
    xhat_ref[...] = xhat.astype(BF16)
    rstd_ref[...] = jnp.broadcast_to(rstd, rstd_ref.shape)


def _saved_norm(xhat_ref, rstd_ref, g):
    xhat = xhat_ref[...].astype(F32)
    return xhat * g, xhat, rstd_ref[:, 0:1]


def _rms_bwd(dy, g, xhat, rstd):
    dg = jnp.sum(dy * xhat, axis=0, keepdims=True)
    dxh = dy * g
    dx = rstd * (dxh - xhat * jnp.mean(dxh * xhat, axis=-1, keepdims=True))
    return dx, dg


def _shift_down(ext, k):
    return pltpu.roll(ext, k, 0)[HALO:, :]


def _shift_up(ext, k):
    n = ext.shape[0]
    return pltpu.roll(ext, n - k, 0)[: n - HALO, :]


def _pool_select(s2, s4, s8, s16):
    lane = lax.broadcasted_iota(jnp.int32, s2.shape, 1)
    return jnp.where(lane < 64, s2, jnp.where(lane < 128, s4, jnp.where(lane < 192, s8, s16)))


def _pool_inv_count(tile_start, tm):
    pos = lax.broadcasted_iota(jnp.int32, (tm, D_C), 0) + tile_start + 1
    lane = lax.broadcasted_iota(jnp.int32, (tm, D_C), 1)
    win = jnp.where(lane < 64, 2, jnp.where(lane < 128, 4, jnp.where(lane < 192, 8, 16)))
    return 1.0 / jnp.minimum(pos, win).astype(F32)


def _head_halves(a):
    lane = lax.broadcasted_iota(jnp.int32, a.shape, 1)
    even = (lane & 64) == 0
    return jnp.where(even, a, 0.0).astype(BF16), jnp.where(even, 0.0, a).astype(BF16)


def _head_stack(lo, hi, j, nch):
    return jnp.concatenate(
        [jnp.concatenate([lo[c * CHUNK:(c + 1) * CHUNK, j * 128:(j + 1) * 128], hi[c * CHUNK:(c + 1) * CHUNK, j * 128:(j + 1) * 128]], axis=0)
         for c in range(nch)], axis=1)


def _chunks_to_lanes(a, j, nch):
    return jnp.concatenate([a[c * CHUNK:(c + 1) * CHUNK, j * 128:(j + 1) * 128] for c in range(nch)], axis=1)


def _lanes_to_chunks(o, nch):
    return jnp.concatenate([o[:, c * CHUNK:(c + 1) * CHUNK] for c in range(nch)], axis=0)


def _loads(pairs, sem):
    return [pltpu.make_async_copy(src, dst, sem.at[n]) for n, (src, dst) in enumerate(pairs)]


def _load_all(loads, zero=()):
    for cp in loads:
        cp.start()
    for ref in zero:
        ref[...] = jnp.zeros_like(ref)
    for cp in loads:
        cp.wait()


def _write_out(triples, sem):
    copies = []
    for n, (acc, stage, out) in enumerate(triples):
        _stage_bf16(acc, stage)
        cp = pltpu.make_async_copy(stage, out, sem.at[n])
        cp.start()
        copies.append(cp)
    for cp in copies:
        cp.wait()


def _halves(ref):
    half = ref.shape[0] // 2
    return ref.at[pl.ds(0, half)], ref.at[pl.ds(half, half)]


def _stage_bf16(acc, stage):
    rows = acc.shape[0]
    strip = min(rows, 128)

    @pl.loop(0, rows // strip)
    def _(n):
        sl = pl.ds(pl.multiple_of(n * strip, strip), strip)
        stage[sl, :] = acc[sl, :].astype(BF16)


N_SAVED = 5


def _mixers_fwd(pf, halo_hc, halo_zc, tile_start, prm, saved=None):
    tm = pf.shape[0]
    nch = tm // CHUNK
    u, v = pf[:, 0:D_A], pf[:, D_A:2 * D_A]
    zb, gb, gc = pf[:, 768:1152], pf[:, 1152:1536], pf[:, 1536:1920]
    zc = pf[:, 1920:2176]
    r = {}
    if saved is None:
        gu, r["dgelu_u"] = _gelu_and_grad(u)
        gv, r["dgelu_v"] = _gelu_and_grad(v)
        pmat = prm["pmat"][...]
        mu = _mm(gv.astype(BF16), pmat)
        dv = gv - mu
        var = _mm((dv * dv).astype(BF16), pmat)
        rstd = lax.rsqrt(var + LN_EPS)
        xh = dv * rstd
        r["saved"] = jnp.concatenate([gu, r["dgelu_u"], r["dgelu_v"], xh, rstd], axis=1).astype(BF16)
    else:
        gu, r["dgelu_u"], r["dgelu_v"], xh, rstd = (saved[:, n * D_A:(n + 1) * D_A] for n in range(N_SAVED))
    vlo, vhi = _head_halves(xh * prm["ln_g"][...] + prm["ln_b"][...])
    cols, v2s = [], []
    for j in range(3):
        v2 = _head_stack(vlo, vhi, j, nch)
        v2s.append(v2)
        cols.append(_lanes_to_chunks(_mm(prm["wcat"][j], v2), nch))
    mixed = jnp.concatenate(cols, axis=1) + jnp.concatenate([prm["bmat"][...]] * nch, axis=0)
    ya = gu * mixed
    r.update(gu=gu, mixed=mixed, v2s=v2s, xh=xh, ln_rstd=rstd)
    w0, w1, w2 = prm["conv_w"][0:1, :], prm["conv_w"][1:2, :], prm["conv_w"][2:3, :]
    hc = gc * zb
    ext = jnp.concatenate([halo_hc, hc], axis=0)
    h1, h2 = _shift_down(ext, 1), _shift_down(ext, 2)
    yc = w2 * hc + w1 * h1 + w0 * h2
    yb = gb * yc
    r.update(hc=hc, h1=h1, h2=h2, yc=yc, zb=zb, gb=gb, gc=gc, w0=w0, w1=w1, w2=w2)
    ext = jnp.concatenate([halo_zc, zc], axis=0)
    s2 = ext + pltpu.roll(ext, 1, 0)
    s4 = s2 + pltpu.roll(s2, 2, 0)
    s8 = s4 + pltpu.roll(s4, 4, 0)
    s16 = s8 + pltpu.roll(s8, 8, 0)
    inv = _pool_inv_count(tile_start, tm)
    pooled = _pool_select(s2, s4, s8, s16)[HALO:, :] * inv - zc
    pooledb = pooled.astype(BF16)
    pm = _mm(pooledb, prm["bd"][...])
    scale = prm["pool_scale"][...]
    ycm = pm * scale
    r.update(inv=inv, pooledb=pooledb, pm=pm, scale=scale, zc=zc)
    r["ycat"] = jnp.concatenate([ya, yb, ycm], axis=1)
    return r


_MIX_PARAM_NAMES = ("pmat", "ln_g", "ln_b", "wcat", "bmat", "conv_w", "bd", "pool_scale")


def _mix_param_specs():
    return [_full((D_A, D_A)), _full((1, D_A)), _full((1, D_A)), _full((3, CHUNK, 2 * CHUNK)), _full((CHUNK, D_A)),
            _full((3, D_B)), _full((D_C, D_C)), _full((1, D_C))]


def _mix_fwd_call(x, g_mix, w_in_t, w_out, mp, tm, run, layer):
    t = x.shape[0]
    nt = t // tm

    def body(x_ref, g_ref, pmat, ln_g, ln_b, wcat, bmat, conv_w, bd, pool_scale, win_hbm, wout_hbm,
             x1_ref, proj_ref, saved_ref, xhat_ref, rstd_ref, win_s, wout_s, halo_hc, halo_zc, load_sem):
        i = pl.program_id(0)
        loads = _loads([(win_hbm, win_s), (wout_hbm, wout_s)], load_sem)

        @pl.when(i == 0)
        def _():
            _load_all(loads, (halo_hc, halo_zc))

        prm = dict(pmat=pmat, ln_g=ln_g, ln_b=ln_b, wcat=wcat, bmat=bmat, conv_w=conv_w, bd=bd, pool_scale=pool_scale)
        xv = x_ref[...]
        h, xhat, rstd = _rms(xv, g_ref[...])
        _save_norm(xhat, rstd, xhat_ref, rstd_ref)
        pf = _mm_nt(h.astype(BF16), win_s[...])
        proj_ref[...] = pf.astype(BF16)
        hh, hz = halo_hc[...], halo_zc[...]
        parts = []
        for c in range(tm // run):
            r = _mixers_fwd(pf[c * run:(c + 1) * run, :], hh, hz, i * tm + c * run, prm)
            hh, hz = r["hc"][run - HALO:, :], r["zc"][run - HALO:, :]
            parts.append(r["ycat"].astype(BF16))
            saved_ref[c * run:(c + 1) * run, :] = r["saved"]
        halo_hc[...] = hh
        halo_zc[...] = hz
        x1_ref[...] = xv + _mm(jnp.concatenate(parts, axis=0), wout_s[...])

    return pl.pallas_call(
        body, name=f"mix_fwd_{layer}", grid=(nt,),
        in_specs=[_rows(tm, D), _full((1, D))] + _mix_param_specs() + [HBM, HBM],
        out_specs=[_rows(tm, D), _rows(tm, D_IN), _rows(tm, N_SAVED * D_A), _rows(tm, D), _rows(tm, RSTD_LANES)],
        out_shape=[jax.ShapeDtypeStruct((t, D), F32), jax.ShapeDtypeStruct((t, D_IN), BF16),
                   jax.ShapeDtypeStruct((t, N_SAVED * D_A), BF16), jax.ShapeDtypeStruct((t, D), BF16),
                   jax.ShapeDtypeStruct((t, RSTD_LANES), F32)],
        scratch_shapes=[pltpu.VMEM((D_IN, D), BF16), pltpu.VMEM((D, D), BF16),
                        pltpu.VMEM((HALO, D_B), F32), pltpu.VMEM((HALO, D_C), F32), pltpu.SemaphoreType.DMA((2,))],
        compiler_params=_params(),
    )(x, g_mix, *[mp[k] for k in _MIX_PARAM_NAMES], _in_hbm(w_in_t), _in_hbm(w_out))


def _ffn_fwd_call(x1, p, g_ff, g_ple, w1, w2, wg, wp, tm, layer):
    t = x1.shape[0]
    nt = t // tm

    def body(x1_ref, p_ref, gff_ref, gple_ref, w1_hbm, w2_hbm, wg_hbm, wp_hbm,
             x3_ref, r_ref, gate_ref, xh1_ref, rs1_ref, xh2_ref, rs2_ref, w1_s, w2_s, wg_s, wp_s, load_sem):
        i = pl.program_id(0)
        loads = _loads([(w1_hbm, w1_s), (w2_hbm, w2_s), (wg_hbm, wg_s), (wp_hbm, wp_s)], load_sem)

        @pl.when(i == 0)
        def _():
            _load_all(loads)

        x1v = x1_ref[...]
        h2, xhat, rstd = _rms(x1v, gff_ref[...])
        _save_norm(xhat, rstd, xh1_ref, rs1_ref)
        h2b = h2.astype(BF16)
        acc = x1v
        for j in range(D_FF // FF_BLK):
            blk = slice(j * FF_BLK, (j + 1) * FF_BLK)
            rj = jnp.maximum(_mm(h2b, w1_s[:, blk]), 0.0)
            r_ref[:, blk] = rj.astype(BF16)
            acc = acc + _mm((rj * rj).astype(BF16), w2_s[blk, :])
        n3, xhat, rstd = _rms(acc, gple_ref[...])
        _save_norm(xhat, rstd, xh2_ref, rs2_ref)
        gate = jax.nn.sigmoid(_mm(n3.astype(BF16), wg_s[...]))
        gate_ref[...] = gate.astype(BF16)
        pp = _mm(p_ref[...].astype(BF16), wp_s[...])
        x3_ref[...] = acc + pp * gate

    return pl.pallas_call(
        body, name=f"ffn_fwd_{layer}", grid=(nt,),
        in_specs=[_rows(tm, D), _layer_rows(layer, tm, D_PLE), _full((1, D)), _full((1, D)), HBM, HBM, HBM, HBM],
        out_specs=[_rows(tm, D), _rows(tm, D_FF), _rows(tm, D), _rows(tm, D), _rows(tm, RSTD_LANES), _rows(tm, D),
                   _rows(tm, RSTD_LANES)],
        out_shape=[jax.ShapeDtypeStruct((t, D), F32), jax.ShapeDtypeStruct((t, D_FF), BF16), jax.ShapeDtypeStruct((t, D), BF16),
                   jax.ShapeDtypeStruct((t, D), BF16), jax.ShapeDtypeStruct((t, RSTD_LANES), F32),
                   jax.ShapeDtypeStruct((t, D), BF16), jax.ShapeDtypeStruct((t, RSTD_LANES), F32)],
        scratch_shapes=[pltpu.VMEM((D, D_FF), BF16), pltpu.VMEM((D_FF, D), BF16),
                        pltpu.VMEM((D, D), BF16), pltpu.VMEM((D_PLE, D), BF16), pltpu.SemaphoreType.DMA((4,))],
        compiler_params=_params(),
    )(x1, p, g_ff, g_ple, _in_hbm(w1), _in_hbm(w2), _in_hbm(wg), _in_hbm(wp))


def _loss_call(xl, target, final_g, tm):
    t = xl.shape[0]
    nt = t // tm

    def body(x_ref, t_ref, g_ref, sq_ref, dx_ref, dg_ref):
        i = pl.program_id(0)

        @pl.when(i == 0)
        def _():
            sq_ref[...] = jnp.zeros_like(sq_ref)
            dg_ref[...] = jnp.zeros_like(dg_ref)

        g = g_ref[...]
        y, xhat, rstd = _rms(x_ref[...], g)
        err = y - t_ref[...]
        sq_ref[...] += jnp.sum(err * err, axis=0, keepdims=True)
        dx, dg = _rms_bwd(err * (1.0 / D), g, xhat, rstd)
        dx_ref[...] = dx
        dg_ref[...] += dg

    return pl.pallas_call(
        body, name="loss_head", grid=(nt,),
        in_specs=[_rows(tm, D), _rows(tm, D), _full((1, D))],
        out_specs=[_full((1, D)), _rows(tm, D), _full((1, D))],
        out_shape=[jax.ShapeDtypeStruct((1, D), F32), jax.ShapeDtypeStruct((t, D), F32), jax.ShapeDtypeStruct((1, D), F32)],
        compiler_params=_params(),
    )(xl, target, final_g)


def _ple_bwd_call(dx3, xhat2, rstd2, gate, p, g_ple, wg, wp, after, tm, layer):
    t = dx3.shape[0]
    nt = t // tm

    def body(dx3_ref, xhat_ref, rstd_ref, gate_ref, p_ref, g_ref, wg_hbm, wp_hbm, after_ref,
             dx2_ref, dg_ref, dwg_hbm, dwp_hbm, wg_s, wp_s, dwg_acc, dwp_acc, load_sem):
        i = pl.program_id(0)
        loads = _loads([(wp_hbm, wp_s), (wg_hbm, wg_s)], load_sem)

        @pl.when(i == 0)
        def _():
            _load_all(loads, (dwg_acc, dwp_acc, dg_ref))

        g = g_ref[...]
        dx3v = dx3_ref[...]
        gatev = gate_ref[...].astype(F32)
        pb = p_ref[...].astype(BF16)
        pp = _mm(pb, wp_s[...])
        dwp_acc[...] += _mm_tn(pb, (dx3v * gatev).astype(BF16))
        dgpre = (dx3v * pp * gatev * (1.0 - gatev)).astype(BF16)
        n3, xhat, rstd = _saved_norm(xhat_ref, rstd_ref, g)
        dwg_acc[...] += _mm_tn(n3.astype(BF16), dgpre)
        dn3 = _mm_nt(dgpre, wg_s[...])
        dx, dg = _rms_bwd(dn3, g, xhat, rstd)
        dx2_ref[...] = dx3v + dx
        dg_ref[...] += dg

        @pl.when(i == nt - 1)
        def _():
            _write_out([(dwg_acc, wg_s, dwg_hbm), (dwp_acc, wp_s, dwp_hbm)], load_sem)

    return pl.pallas_call(
        body, name=f"ple_bwd_{layer}", grid=(nt,),
        in_specs=[_rows(tm, D), _rows(tm, D), _rows(tm, RSTD_LANES), _rows(tm, D), _layer_rows(layer, tm, D_PLE), _full((1, D)),
                  HBM, HBM, ANY],
        out_specs=[_rows(tm, D), _full((1, D)), HBM, HBM],
        out_shape=[jax.ShapeDtypeStruct((t, D), F32), jax.ShapeDtypeStruct((1, D), F32),
                   pltpu.HBM((D, D), BF16), pltpu.HBM((D_PLE, D), BF16)],
        scratch_shapes=[pltpu.VMEM((D, D), BF16), pltpu.VMEM((D_PLE, D), BF16),
                        pltpu.VMEM((D, D), F32), pltpu.VMEM((D_PLE, D), F32), pltpu.SemaphoreType.DMA((2,))],
        compiler_params=_params(),
    )(dx3, xhat2, rstd2, gate, p, g_ple, _in_hbm(wg), _in_hbm(wp), after)


def _ffn_bwd_hidden_call(dx2, r, w2, tm, layer):
    t = dx2.shape[0]
    nt = t // tm

    def body(dx2_ref, r_ref, w2_hbm, da_ref, dw2_hbm, w2_s, dw2_acc, load_sem):
        i = pl.program_id(0)
        loads = _loads([(w2_hbm, w2_s)], load_sem)

        @pl.when(i == 0)
        def _():
            _load_all(loads, (dw2_acc,))

        dxb = dx2_ref[...].astype(BF16)
        for j in range(D_FF // FF_BLK):
            blk = slice(j * FF_BLK, (j + 1) * FF_BLK)
            rj = r_ref[:, blk].astype(F32)
            ds = _mm_nt(dxb, w2_s[blk, :])
            da_ref[:, blk] = (2.0 * rj * ds).astype(BF16)
            dw2_acc[blk, :] += _mm_tn((rj * rj).astype(BF16), dxb)

        @pl.when(i == nt - 1)
        def _():
            _write_out(list(zip(_halves(dw2_acc), _halves(w2_s), _halves(dw2_hbm))), load_sem)

    return pl.pallas_call(
        body, name=f"ffn_bwd_hidden_{layer}", grid=(nt,),
        in_specs=[_rows(tm, D), _rows(tm, D_FF), HBM],
        out_specs=[_rows(tm, D_FF), HBM],
        out_shape=[jax.ShapeDtypeStruct((t, D_FF), BF16), pltpu.HBM((D_FF, D), BF16)],
        scratch_shapes=[pltpu.VMEM((D_FF, D), BF16), pltpu.VMEM((D_FF, D), F32), pltpu.SemaphoreType.DMA((2,))],
        compiler_params=_params(),
    )(dx2, r, _in_hbm(w2))


def _ffn_bwd_input_call(da, xhat1, rstd1, dx2, g_ff, w1, tm, layer):
    t = dx2.shape[0]
    nt = t // tm

    def body(da_ref, xhat_ref, rstd_ref, dx2_ref, g_ref, w1_hbm, dx1_ref, dg_ref, dw1_hbm, w1_s, dw1_acc, load_sem):
        i = pl.program_id(0)
        loads = _loads([(w1_hbm, w1_s)], load_sem)

        @pl.when(i == 0)
        def _():
            _load_all(loads, (dw1_acc, dg_ref))

        g = g_ref[...]
        h2, xhat, rstd = _saved_norm(xhat_ref, rstd_ref, g)
        h2b = h2.astype(BF16)
        dh2 = jnp.zeros((tm, D), F32)
        for j in range(D_FF // FF_BLK):
            blk = slice(j * FF_BLK, (j + 1) * FF_BLK)
            daj = da_ref[:, blk]
            dh2 = dh2 + _mm_nt(daj, w1_s[:, blk])
            dw1_acc[:, blk] += _mm_tn(h2b, daj)
        dx, dg = _rms_bwd(dh2, g, xhat, rstd)
        dx1_ref[...] = dx2_ref[...] + dx
        dg_ref[...] += dg

        @pl.when(i == nt - 1)
        def _():
            _write_out(list(zip(_halves(dw1_acc), _halves(w1_s), _halves(dw1_hbm))), load_sem)

    return pl.pallas_call(
        body, name=f"ffn_bwd_input_{layer}", grid=(nt,),
        in_specs=[_rows(tm, D_FF), _rows(tm, D), _rows(tm, RSTD_LANES), _rows(tm, D), _full((1, D)), HBM],
        out_specs=[_rows(tm, D), _full((1, D)), HBM],
        out_shape=[jax.ShapeDtypeStruct((t, D), F32), jax.ShapeDtypeStruct((1, D), F32), pltpu.HBM((D, D_FF), BF16)],
        scratch_shapes=[pltpu.VMEM((D, D_FF), BF16), pltpu.VMEM((D, D_FF), F32), pltpu.SemaphoreType.DMA((2,))],
        compiler_params=_params(),
    )(da, xhat1, rstd1, dx2, g_ff, _in_hbm(w1))


def _mix_bwd_call(dx1, xhat0, rstd0, proj, saved, g_mix, w_in_t, w_out, mp, wtcat, trilcat, headsel, after, tm, run, layer):
    t = dx1.shape[0]
    nt = t // tm
    nrun = tm // run
    nch = run // CHUNK
    hb = tm // HALO

    def rev(i):
        return nt - 1 - i

    def body(dx1_ref, xhat_ref, rstd_ref, proj_ref, halo_ref, saved_ref, g_ref, pmat, ln_g, ln_b, wcat, bmat, conv_w, bd, pool_scale,
             wtcat_ref, tril_ref, sel_ref, win_hbm, wout_hbm, after_ref,
             dx_ref, dg_ref, dwcat_ref, dsb_ref, dlng_ref, dlnb_ref, dconv_ref, dbd_ref, dscale_ref, dwin_hbm, dwout_hbm,
             win_s, wout_s, dwin_acc, dwout_acc, dbm_acc, carry_yc, carry_q, load_sem):
        i = pl.program_id(0)
        ri = nt - 1 - i
        loads = _loads([(wout_hbm, wout_s), (win_hbm, win_s)], load_sem)

        @pl.when(i == 0)
        def _():
            _load_all(loads, (dwin_acc, dwout_acc, dbm_acc, carry_yc, carry_q, dg_ref, dwcat_ref, dlng_ref, dlnb_ref,
                              dconv_ref, dbd_ref, dscale_ref))

        prm = dict(pmat=pmat, ln_g=ln_g, ln_b=ln_b, wcat=wcat, bmat=bmat, conv_w=conv_w, bd=bd, pool_scale=pool_scale)
        g = g_ref[...]
        h, xhat, rstd = _saved_norm(xhat_ref, rstd_ref, g)
        hb16 = h.astype(BF16)
        dx1v = dx1_ref[...]
        dx1b = dx1v.astype(BF16)
        dycat = _mm_nt(dx1b, wout_s[...])
        lng = ln_g[...]
        pm_ = pmat[...]
        cy, cq = carry_yc[...], carry_q[...]
        ycat_parts, dproj_parts = [None] * nrun, [None] * nrun
        dbm = dlng = dlnb = dscale = dcv0 = dcv1 = dcv2 = None
        add = lambda tot, v: v if tot is None else tot + v
        for c in reversed(range(nrun)):
            rows = slice(c * run, (c + 1) * run)
            pf = proj_ref[rows, :].astype(F32)
            if c > 0:
                ph = proj_ref[c * run - HALO:c * run, :].astype(F32)
            else:
                ph = halo_ref[...].astype(F32) * (ri > 0).astype(F32)
            r = _mixers_fwd(pf, ph[:, 1536:1920] * ph[:, 768:1152], ph[:, 1920:2176], ri * tm + c * run, prm,
                            saved_ref[rows, :].astype(F32))
            ycat_parts[c] = r["ycat"].astype(BF16)
            dya, dyb, dyc = dycat[rows, 0:D_A], dycat[rows, D_A:D_A + D_B], dycat[rows, D_A + D_B:D]

            dgu = dya * r["mixed"]
            dmix = dya * r["gu"]
            dmix_b = dmix.astype(BF16)
            dlo, dhi = _head_halves(dmix)
            for k in range(nch):
                dbm = add(dbm, dmix[k * CHUNK:(k + 1) * CHUNK, :])
            dvn_cols = []
            for j in range(3):
                dwcat_ref[j] += _mm_nt(_chunks_to_lanes(dmix_b, j, nch), r["v2s"][j])
                dvn_cols.append(_lanes_to_chunks(_mm(wtcat_ref[j], _head_stack(dlo, dhi, j, nch)), nch))
            dvn = jnp.concatenate(dvn_cols, axis=1)
            xh = r["xh"]
            dlng = add(dlng, jnp.sum(dvn * xh, axis=0, keepdims=True))
            dlnb = add(dlnb, jnp.sum(dvn, axis=0, keepdims=True))
            dxh = dvn * lng
            m1 = _mm(dxh.astype(BF16), pm_)
            m2 = _mm((dxh * xh).astype(BF16), pm_)
            dgv = r["ln_rstd"] * (dxh - m1 - xh * m2)
            du = dgu * r["dgelu_u"]
            dv = dgv * r["dgelu_v"]

            dgb = dyb * r["yc"]
            dyc2 = dyb * r["gb"]
            dcv0 = add(dcv0, jnp.sum(dyc2 * r["h2"], axis=0, keepdims=True))
            dcv1 = add(dcv1, jnp.sum(dyc2 * r["h1"], axis=0, keepdims=True))
            dcv2 = add(dcv2, jnp.sum(dyc2 * r["hc"], axis=0, keepdims=True))
            ext = jnp.concatenate([dyc2, cy], axis=0)
            dhc = r["w2"] * dyc2 + r["w1"] * _shift_up(ext, 1) + r["w0"] * _shift_up(ext, 2)
            cy = dyc2[0:HALO, :]
            dgc = dhc * r["zb"]
            dzb = dhc * r["gc"]

            dscale = add(dscale, jnp.sum(dyc * r["pm"], axis=0, keepdims=True))
            dpm = (dyc * r["scale"]).astype(BF16)
            dbd_ref[...] += _mm_tn(r["pooledb"], dpm)
            dpooled = _mm_nt(dpm, bd[...])
            q = dpooled * r["inv"]
            ext = jnp.concatenate([q, cq], axis=0)
            n = run + HALO
            r2 = ext + pltpu.roll(ext, n - 1, 0)
            r4 = r2 + pltpu.roll(r2, n - 2, 0)
            r8 = r4 + pltpu.roll(r4, n - 4, 0)
            r16 = r8 + pltpu.roll(r8, n - 8, 0)
            dzc = _pool_select(r2, r4, r8, r16)[0:run, :] - dpooled
            cq = q[0:HALO, :]
            dproj_parts[c] = jnp.concatenate([du, dv, dzb, dgb, dgc, dzc], axis=1).astype(BF16)

        carry_yc[...] = cy
        carry_q[...] = cq
        dbm_acc[...] += dbm
        dlng_ref[...] += dlng
        dlnb_ref[...] += dlnb
        dscale_ref[...] += dscale
        dconv_ref[0:1, :] += dcv0
        dconv_ref[1:2, :] += dcv1
        dconv_ref[2:3, :] += dcv2
        dwout_acc[...] += _mm_tn(jnp.concatenate(ycat_parts, axis=0), dx1b)
        dproj = jnp.concatenate(dproj_parts, axis=0)
        dwin_acc[...] += _mm_tn(dproj, hb16)
        dh = _mm(dproj, win_s[...])
        dx, dg = _rms_bwd(dh, g, xhat, rstd)
        dx_ref[...] = dx1v + dx
        dg_ref[...] += dg

        @pl.when(i == nt - 1)
        def _():
            _write_out([(dwin_acc, win_s, dwin_hbm), (dwout_acc, wout_s, dwout_hbm)], load_sem)
            for j in range(3):
                dwcat_ref[j] = dwcat_ref[j] * tril_ref[...]
            acc = dbm_acc[...]
            hi = acc.astype(BF16)
            lo = (acc - hi.astype(F32)).astype(BF16)
            dsb_ref[...] = _mm(hi, sel_ref[...]) + _mm(lo, sel_ref[...])

    return pl.pallas_call(
        body, name=f"mix_bwd_{layer}", grid=(nt,),
        in_specs=[pl.BlockSpec((tm, D), lambda i: (rev(i), 0)), pl.BlockSpec((tm, D), lambda i: (rev(i), 0)),
                  pl.BlockSpec((tm, RSTD_LANES), lambda i: (rev(i), 0)), pl.BlockSpec((tm, D_IN), lambda i: (rev(i), 0)),
                  pl.BlockSpec((HALO, D_IN), lambda i: (jnp.maximum(rev(i) * hb - 1, 0), 0)),
                  pl.BlockSpec((tm, N_SAVED * D_A), lambda i: (rev(i), 0)), _full((1, D))] + _mix_param_specs()
                 + [_full((3, CHUNK, 2 * CHUNK)), _full((CHUNK, 2 * CHUNK)), _full((D_A, CHUNK)), HBM, HBM, ANY],
        out_specs=[pl.BlockSpec((tm, D), lambda i: (rev(i), 0)), _full((1, D)), _full((3, CHUNK, 2 * CHUNK)),
                   _full((CHUNK, CHUNK)), _full((1, D_A)), _full((1, D_A)), _full((3, D_B)), _full((D_C, D_C)),
                   _full((1, D_C)), HBM, HBM],
        out_shape=[jax.ShapeDtypeStruct((t, D), F32), jax.ShapeDtypeStruct((1, D), F32),
                   jax.ShapeDtypeStruct((3, CHUNK, 2 * CHUNK), F32), jax.ShapeDtypeStruct((CHUNK, CHUNK), F32),
                   jax.ShapeDtypeStruct((1, D_A), F32), jax.ShapeDtypeStruct((1, D_A), F32),
                   jax.ShapeDtypeStruct((3, D_B), F32), jax.ShapeDtypeStruct((D_C, D_C), F32),
                   jax.ShapeDtypeStruct((1, D_C), F32), pltpu.HBM((D_IN, D), BF16), pltpu.HBM((D, D), BF16)],
        scratch_shapes=[pltpu.VMEM((D_IN, D), BF16), pltpu.VMEM((D, D), BF16),
                        pltpu.VMEM((D_IN, D), F32), pltpu.VMEM((D, D), F32), pltpu.VMEM((CHUNK, D_A), F32),
                        pltpu.VMEM((HALO, D_B), F32), pltpu.VMEM((HALO, D_C), F32), pltpu.SemaphoreType.DMA((2,))],
        compiler_params=_params(),
    )(dx1, xhat0, rstd0, proj, proj, saved, g_mix, *[mp[k] for k in _MIX_PARAM_NAMES], wtcat, trilcat, headsel, _in_hbm(w_in_t), _in_hbm(w_out), after)


def _coords():
    return lax.axis_index("x"), lax.axis_index("y"), lax.axis_index("c")


EFFECT = pltpu.SideEffectType.DATAFLOW_SIDE_EFFECTING


def _peer(k):
    x, y, c = _coords()
    px, py, pc = x ^ (k >> 2), y ^ ((k >> 1) & 1), c ^ (k & 1)
    return (px, py, pc), 4 * px + 2 * py + pc


def _landing_shape(shape, mode):
    if mode == "block":
        return (N_DEV,) + shape
    if mode == "slot":
        return shape
    if mode == "cols_in":
        return (shape[0], N_DEV * shape[1])
    return (N_DEV, shape[0], shape[1] // N_DEV)


def _pieces(src, land, mode, src_idx, land_idx):
    if mode == "block":
        return src, land.at[land_idx]
    if mode == "slot":
        return src.at[src_idx], land.at[land_idx]
    if mode == "cols_in":
        cw = src.shape[1]
        return src, land.at[:, pl.ds(pl.multiple_of(land_idx * cw, 128), cw)]
    cw = land.shape[2]
    return src.at[:, pl.ds(pl.multiple_of(src_idx * cw, 128), cw)], land.at[land_idx]


def _exchange_copy(src, land, mode, send_sem, recv_sem, ai, k, starting):
    x, y, c = _coords()
    peer, pidx = _peer(k)
    s, d = _pieces(src, land, mode, pidx, 4 * x + 2 * y + c if starting else pidx)
    i = ai * (N_DEV - 1) + k - 1
    return pltpu.make_async_remote_copy(src_ref=s, dst_ref=d, send_sem=send_sem.at[i], recv_sem=recv_sem.at[i],
                                        device_id=peer, device_id_type=MESH)


def _own_copy(src, land, mode, local_sem, ai):
    x, y, c = _coords()
    me = 4 * x + 2 * y + c
    s, d = _pieces(src, land, mode, me, me)
    return pltpu.make_async_copy(s, d, local_sem.at[ai])


def _item_src(ins, item):
    a, sub = item
    return ins[a] if sub is None else ins[a].at[sub]


def _exchange_start(srcs, items, modes, groups, name):
    n, ni, ng = len(srcs), len(items), len(groups)
    shapes = [srcs[a].shape if sub is None else srcs[a].shape[1:] for a, sub in items]
    land_shapes = [pltpu.HBM(_landing_shape(sh, m), srcs[a].dtype) for sh, m, (a, _) in zip(shapes, modes, items)]

    def body(*refs):
        ins = refs[:n]
        sems = refs[n:n + 3 * ng]
        land_refs = refs[n + 3 * ng:n + 3 * ng + ni]
        token = refs[-1]
        for g, idxs in enumerate(groups):
            for ai, it in enumerate(idxs):
                src = _item_src(ins, items[it])
                _own_copy(src, land_refs[it], modes[it], sems[3 * g + 2], ai).start()
                for k in range(1, N_DEV):
                    _exchange_copy(src, land_refs[it], modes[it], sems[3 * g], sems[3 * g + 1], ai, k, True).start()
        token[...] = jnp.zeros_like(token)

    sem_shapes = []
    for idxs in groups:
        sem_shapes += [pltpu.SemaphoreType.DMA((len(idxs) * (N_DEV - 1),))] * 2 + [pltpu.SemaphoreType.DMA((len(idxs),))]
    out = pl.pallas_call(
        body, name=name,
        out_shape=tuple(sem_shapes) + tuple(land_shapes) + (jax.ShapeDtypeStruct((8, 128), F32),),
        in_specs=[HBM] * n,
        out_specs=tuple([SEM] * (3 * ng) + [HBM] * ni + [pl.BlockSpec(memory_space=pltpu.VMEM)]),
        compiler_params=pltpu.CompilerParams(has_side_effects=EFFECT),
    )(*[pltpu.with_memory_space_constraint(s, pltpu.HBM) for s in srcs])
    sems = [tuple(out[3 * g:3 * g + 3]) for g in range(ng)]
    return sems, list(out[3 * ng:3 * ng + ni]), out[-1]


def _exchange_wait(sems, srcs, items, lands, modes, groups, after, name):
    n, ni, ng = len(srcs), len(items), len(groups)

    def body(*refs):
        ins, land_refs = refs[:n], refs[n:n + ni]
        sem_refs = refs[n + ni:n + ni + 3 * ng]
        for g, idxs in enumerate(groups):
            for ai, it in enumerate(idxs):
                src = _item_src(ins, items[it])
                _own_copy(src, land_refs[it], modes[it], sem_refs[3 * g + 2], ai).wait()
                for k in range(1, N_DEV):
                    cp = _exchange_copy(src, land_refs[it], modes[it], sem_refs[3 * g], sem_refs[3 * g + 1], ai, k, False)
                    cp.wait_send()
                    cp.wait_recv()

    flat_sems = [s for trio in sems for s in trio]
    afters = list(after) if isinstance(after, (list, tuple)) else [after]
    out = pl.pallas_call(
        body, name=name,
        out_shape=tuple(pltpu.HBM(l.shape, l.dtype) for l in lands),
        in_specs=[HBM] * (n + ni) + [SEM] * (3 * ng) + [ANY] * len(afters),
        out_specs=tuple([HBM] * ni),
        input_output_aliases={n + i: i for i in range(ni)},
        compiler_params=pltpu.CompilerParams(has_side_effects=EFFECT),
    )(*srcs, *lands, *flat_sems, *afters)
    return list(out)


def _slot_sum_call(landed):
    n = len(landed)

    def body(*refs):
        for src, dst in zip(refs[:n], refs[n:]):
            tot = src[0]
            for j in range(1, N_DEV):
                tot = tot + src[j]
            dst[...] = tot

    vm = pl.BlockSpec(memory_space=pltpu.VMEM)
    return pl.pallas_call(
        body, name="small_grads_sum", in_specs=[vm] * n, out_specs=[vm] * n,
        out_shape=[jax.ShapeDtypeStruct(a.shape[1:], F32) for a in landed],
        compiler_params=pltpu.CompilerParams(vmem_limit_bytes=V7X_VMEM_LIMIT),
    )(*landed)


def _adamw(w, g, m, v):
    m = ADAM_B1 * m + (1.0 - ADAM_B1) * g
    v = ADAM_B2 * v + (1.0 - ADAM_B2) * (g * g)
    m_hat = m / (1.0 - ADAM_B1 ** ADAM_STEP)
    v_hat = v / (1.0 - ADAM_B2 ** ADAM_STEP)
    delta = -ADAM_LR * (m_hat / (jnp.sqrt(v_hat) + ADAM_EPS) + ADAM_WD * w)
    return delta, m, v


def _reduce_adamw_call(recvs, w, m, v, name):
    nl = len(recvs)
    _, r, c = recvs[0].shape
    rb = 256 if r % 256 == 0 else r
    nb = r // rb

    def body(*refs):
        recv_refs = refs[:nl]
        w_ref, m_ref, v_ref, g_ref, d_ref, nm_ref, nv_ref = refs[nl:]
        for l in range(nl):
            @pl.when(pl.program_id(0) == l)
            def _(l=l):
                g = recv_refs[l][0].astype(F32)
                for j in range(1, N_DEV):
                    g = g + recv_refs[l][j].astype(F32)
                delta, nm, nv = _adamw(w_ref[0], g, m_ref[0], v_ref[0])
                g_ref[0] = g
                d_ref[0] = delta
                nm_ref[0] = nm
                nv_ref[0] = nv

    def recv_spec(l):
        return pl.BlockSpec((N_DEV, rb, c), lambda lg, i: (0, jnp.where(lg == l, i, jnp.where(lg < l, 0, nb - 1)), 0))

    blk = pl.BlockSpec((1, rb, c), lambda lg, i: (lg, i, 0))
    shp = jax.ShapeDtypeStruct((nl, r, c), F32)
    return pl.pallas_call(
        body, name=name, grid=(nl, nb),
        in_specs=[recv_spec(l) for l in range(nl)] + [blk, blk, blk],
        out_specs=[blk, blk, blk, blk], out_shape=[shp, shp, shp, shp],
        compiler_params=pltpu.CompilerParams(dimension_semantics=("arbitrary", "arbitrary"), vmem_limit_bytes=V7X_VMEM_LIMIT),
    )(*recvs, w, m, v)


def _small_adamw_call(w, g, m, v):
    def body(w_ref, g_ref, m_ref, v_ref, d_ref, nm_ref, nv_ref):
        delta, nm, nv = _adamw(w_ref[...], g_ref[...], m_ref[...], v_ref[...])
        d_ref[...] = delta
        nm_ref[...] = nm
        nv_ref[...] = nv

    shp = jax.ShapeDtypeStruct(w.shape, F32)
    vm = pl.BlockSpec(memory_space=pltpu.VMEM)
    return pl.pallas_call(body, name="small_adamw", in_specs=[vm] * 4, out_specs=[vm] * 3, out_shape=[shp, shp, shp],
                          compiler_params=pltpu.CompilerParams(vmem_limit_bytes=V7X_VMEM_LIMIT))(w, g, m, v)


_GATHER_MODE = dict(w_in="block", w_out="block", w_ff1="cols_in", w_ff2="block", w_ple_gate="block", w_ple_proj="cols_in")
_SCATTER_MODE = dict(w_in="slot", w_out="slot", w_ff1="cols_out", w_ff2="slot", w_ple_gate="slot", w_ple_proj="cols_out")
_GROUP_A = ("w_in", "w_out")
_GROUP_B = ("w_ff1", "w_ff2", "w_ple_gate", "w_ple_proj")


def _gathered_full(k, landed):
    if _GATHER_MODE[k] == "cols_in":
        return landed
    n, r, c = landed.shape
    return landed.reshape(n * r, c)


def _grad_send(k, g):
    if _SCATTER_MODE[k] == "cols_out":
        return g
    r8, c = g.shape
    return g.reshape(N_DEV, r8 // N_DEV, c)


_SMALL_ORDER = ("norm_mix_g", "sgu_w", "sgu_b", "sgu_ln_g", "sgu_ln_b", "conv_w", "pool_w", "pool_scale",
                "norm_ff_g", "norm_ple_g", "final_g")


def _pack_small(d, order=_SMALL_ORDER):
    pieces, layout = [], []
    for k in order:
        flat = d[k].reshape(-1)
        n = flat.shape[0]
        pad = (-n) % 128
        pieces.append(jnp.pad(flat, (0, pad)))
        layout.append((k, d[k].shape, n, n + pad))
    flat = jnp.concatenate(pieces)
    pad = (-flat.shape[0]) % 1024
    return jnp.pad(flat, (0, pad)).reshape(-1, 128), layout


def _unpack_small(buf, layout):
    flat = buf.reshape(-1)
    out, off = {}, 0
    for k, shape, n, padded in layout:
        out[k] = flat[off:off + n].reshape(shape)
        off += padded
    return out


def kernel(x, p, norm_mix_g, w_in, sgu_w, sgu_b, sgu_ln_g, sgu_ln_b, conv_w, pool_w, pool_scale, w_out, norm_ff_g, w_ff1, w_ff2, norm_ple_g, w_ple_gate, w_ple_proj, final_g, loss_target, m_norm_mix_g, m_w_in, m_sgu_w, m_sgu_b, m_sgu_ln_g, m_sgu_ln_b, m_conv_w, m_pool_w, m_pool_scale, m_w_out, m_norm_ff_g, m_w_ff1, m_w_ff2, m_norm_ple_g, m_w_ple_gate, m_w_ple_proj, m_final_g, v_norm_mix_g, v_w_in, v_sgu_w, v_sgu_b, v_sgu_ln_g, v_sgu_ln_b, v_conv_w, v_pool_w, v_pool_scale, v_w_out, v_norm_ff_g, v_w_ff1, v_w_ff2, v_norm_ple_g, v_w_ple_gate, v_w_ple_proj, v_final_g):
    t = x.shape[1]
    xc, yc_, cc = _coords()
    me = 4 * xc + 2 * yc_ + cc
    tm = lambda want: min(want, t)

    shard_names = _GROUP_A + _GROUP_B
    swap = lambda a: jnp.transpose(a, (0, 2, 1))
    shard = dict(w_in=swap(w_in), w_out=w_out, w_ff1=w_ff1, w_ff2=w_ff2, w_ple_gate=w_ple_gate, w_ple_proj=w_ple_proj)
    conv_pad = jnp.zeros((16, 128), F32).at[0:DEPTH * 3, 0:D_B // N_DEV].set(conv_w.reshape(DEPTH * 3, D_B // N_DEV))
    ag_srcs = [shard[k].astype(BF16) for k in shard_names] + [conv_pad]
    ag_items, ag_modes, ag_groups = [], [], []
    for l in range(DEPTH):
        for names in (_GROUP_A, _GROUP_B):
            ag_groups.append(list(range(len(ag_items), len(ag_items) + len(names))))
            ag_items += [(shard_names.index(k), l) for k in names]
            ag_modes += [_GATHER_MODE[k] for k in names]
            if l == 0 and names is _GROUP_A:
                ag_groups[-1].append(len(ag_items))
                ag_items.append((len(shard_names), None))
                ag_modes.append("block")
    ag_sems, ag_lands, _ = _exchange_start(ag_srcs, ag_items, ag_modes, ag_groups, "weights_gather_start")

    def gathered(l, which, after):
        idxs = ag_groups[2 * l + which]
        landed = _exchange_wait([ag_sems[2 * l + which]], ag_srcs, [ag_items[i] for i in idxs], [ag_lands[i] for i in idxs],
                                [ag_modes[i] for i in idxs], [list(range(len(idxs)))], after, f"weights_gather_wait_{l}_{which}")
        full = {k: _gathered_full(k, got) for k, got in zip((_GROUP_A, _GROUP_B)[which], landed)}
        if l == 0 and which == 0:
            full["conv_w"] = jnp.transpose(landed[-1][:, 0:DEPTH * 3, 0:D_B // N_DEV].reshape(N_DEV, DEPTH, 3, D_B // N_DEV),
                                           (1, 2, 0, 3)).reshape(DEPTH, 3, D_B)
        return full

    idx = jnp.arange(D_A)
    pmat = ((idx[:, None] // 64) == (idx[None, :] // 64)).astype(BF16) * (1.0 / 64.0)
    pmat = pmat.astype(BF16)
    tril = jnp.tril(jnp.ones((CHUNK, CHUNK), F32))
    trilcat = jnp.concatenate([tril, tril], axis=1)
    headsel = ((idx[:, None] // 64) == jnp.arange(CHUNK)[None, :]).astype(BF16)
    row = lambda a: a.reshape(1, -1)

    def mix_params(l):
        wm = sgu_w[l] * tril[None]
        wcat = jnp.stack([jnp.concatenate([wm[2 * j], wm[2 * j + 1]], axis=1) for j in range(3)]).astype(BF16)
        wtcat = jnp.stack([jnp.concatenate([wm[2 * j].T, wm[2 * j + 1].T], axis=1) for j in range(3)]).astype(BF16)
        bmat = jnp.repeat(sgu_b[l].T, 64, axis=1)
        bd = jnp.zeros((D_C, D_C), F32)
        for gi in range(4):
            bd = bd.at[gi * 64:(gi + 1) * 64, gi * 64:(gi + 1) * 64].set(pool_w[l, gi])
        mp = dict(pmat=pmat, ln_g=row(sgu_ln_g[l]), ln_b=row(sgu_ln_b[l]), wcat=wcat, bmat=bmat, conv_w=conv_full[l],
                  bd=bd.astype(BF16), pool_scale=row(pool_scale[l]))
        return mp, wtcat

    xs = x.reshape(t, D)
    p_layers = p.reshape(DEPTH, t, D_PLE)
    saved, full_w = [], []
    conv_full = None
    for l in range(DEPTH):
        wa = gathered(l, 0, xs)
        if l == 0:
            conv_full = wa["conv_w"]
        mp, _ = mix_params(l)
        x1, proj, sgu_saved, xh0, rs0 = _mix_fwd_call(xs, row(norm_mix_g[l]), wa["w_in"], wa["w_out"], mp, tm(TM_MIX_FWD), tm(RUN_MIX_FWD), l)
        wb = gathered(l, 1, x1)
        x3, r, gate, xh1, rs1, xh2, rs2 = _ffn_fwd_call(x1, p_layers, row(norm_ff_g[l]), row(norm_ple_g[l]), wb["w_ff1"], wb["w_ff2"],
                                        wb["w_ple_gate"], wb["w_ple_proj"], tm(TM_FFN_FWD), l)
        saved.append((proj, sgu_saved, r, gate, (xh0, rs0), (xh1, rs1), (xh2, rs2)))
        full_w.append({**wa, **wb})
        xs = x3

    sq, dx, dfinal = _loss_call(xs, loss_target.reshape(t, D), row(final_g), tm(TM_LOSS))
    loss = lax.psum(jnp.sum(sq) * (0.5 / D), ("x", "y", "c"))

    layer_keys = tuple(k for k in _SMALL_ORDER if k != "final_g")
    small = {k: [None] * DEPTH for k in layer_keys}
    small_ex, small_layout = [None] * DEPTH, None
    ex = {}
    token = jnp.zeros((8, 128), F32)

    for l in reversed(range(DEPTH)):
        proj, sgu_saved, r, gate, norm0, norm1, norm2 = saved[l]
        fw = full_w[l]
        dx2, dgple, dwg, dwp = _ple_bwd_call(dx, *norm2, gate, p_layers, row(norm_ple_g[l]), fw["w_ple_gate"], fw["w_ple_proj"], token,
                                             tm(TM_PLE_BWD), l)
        da, dw2 = _ffn_bwd_hidden_call(dx2, r, fw["w_ff2"], tm(TM_FFN_BWD), l)
        dx1, dgff, dw1 = _ffn_bwd_input_call(da, *norm1, dx2, row(norm_ff_g[l]), fw["w_ff1"], tm(TM_FFN_BWD), l)
        sends = [_grad_send(k, g_) for k, g_ in zip(_GROUP_B, (dw1, dw2, dwg, dwp))]
        sems, lands, token = _exchange_start(sends, [(i, None) for i in range(len(sends))], [_SCATTER_MODE[k] for k in _GROUP_B],
                                             [list(range(len(sends)))], f"grads_exchange_start_{l}_1")
        ex[(l, 1)] = (sems[0], sends, lands)
        mp, wtcat = mix_params(l)
        (dx, dgmix, dwcat, dsb, dlng, dlnb, dconv, dbd, dscale, dwin, dwout) = _mix_bwd_call(
            dx1, *norm0, proj, sgu_saved, row(norm_mix_g[l]), fw["w_in"], fw["w_out"], mp, wtcat, trilcat, headsel, token,
            tm(TM_MIX_BWD), tm(RUN_MIX_BWD), l)
        small["norm_mix_g"][l] = dgmix[0]
        small["sgu_w"][l] = jnp.stack([dwcat[h // 2][:, (h % 2) * CHUNK:(h % 2 + 1) * CHUNK] for h in range(6)])
        small["sgu_b"][l] = dsb[:, 0:6].T
        small["sgu_ln_g"][l], small["sgu_ln_b"][l] = dlng[0], dlnb[0]
        small["conv_w"][l] = dconv
        small["pool_w"][l] = jnp.stack([dbd[gi * 64:(gi + 1) * 64, gi * 64:(gi + 1) * 64] for gi in range(4)])
        small["pool_scale"][l] = dscale[0]
        small["norm_ff_g"][l], small["norm_ple_g"][l] = dgff[0], dgple[0]
        layer_small = {k: small[k][l] for k in layer_keys}
        layer_small["final_g"] = dfinal[0] if l == DEPTH - 1 else jnp.zeros_like(dfinal[0])
        sbuf, small_layout = _pack_small(layer_small, layer_keys + ("final_g",))
        sends = [_grad_send("w_in", dwin), _grad_send("w_out", dwout)]
        sems, lands, token = _exchange_start([sbuf] + sends, [(i, None) for i in range(3)],
                                             ["block", _SCATTER_MODE["w_in"], _SCATTER_MODE["w_out"]], [[0], [1, 2]],
                                             f"grads_exchange_start_{l}_0")
        small_ex[l] = (sems[0], sbuf, lands[0])
        ex[(l, 0)] = (sems[1], sends, lands[1:3])
    grad_x = dx.reshape(1, t, D)

    state = dict(w_in=(swap(w_in), swap(m_w_in), swap(v_w_in)), w_out=(w_out, m_w_out, v_w_out), w_ff1=(w_ff1, m_w_ff1, v_w_ff1),
                 w_ff2=(w_ff2, m_w_ff2, v_w_ff2), w_ple_gate=(w_ple_gate, m_w_ple_gate, v_w_ple_gate),
                 w_ple_proj=(w_ple_proj, m_w_ple_proj, v_w_ple_proj))
    res = {}

    def finish_group(which, after):
        names = (_GROUP_A, _GROUP_B)[which]
        n = len(names)
        sems = [ex[(l, which)][0] for l in range(DEPTH)]
        sends = [s_ for l in range(DEPTH) for s_ in ex[(l, which)][1]]
        lands = [a_ for l in range(DEPTH) for a_ in ex[(l, which)][2]]
        landed = _exchange_wait(sems, sends, [(i, None) for i in range(DEPTH * n)], lands, [_SCATTER_MODE[k] for k in names] * DEPTH,
                                [list(range(l * n, (l + 1) * n)) for l in range(DEPTH)], after, f"grads_exchange_wait_{which}")
        for i, k in enumerate(names):
            res[k] = _reduce_adamw_call([landed[l * n + i] for l in range(DEPTH)], *state[k], "reduce_adamw_" + k)

    finish_group(1, dx)

    landed = _exchange_wait([small_ex[l][0] for l in range(DEPTH)], [small_ex[l][1] for l in range(DEPTH)],
                            [(l, None) for l in range(DEPTH)], [small_ex[l][2] for l in range(DEPTH)], ["block"] * DEPTH,
                            [[l] for l in range(DEPTH)], res[_GROUP_B[-1]][0], "small_grads_wait")
    sums = [_unpack_small(b_, small_layout) for b_ in _slot_sum_call(landed)]
    gs = {k: jnp.stack([sums[l][k] for l in range(DEPTH)]) for k in layer_keys}
    gs["final_g"] = sums[DEPTH - 1]["final_g"]
    conv_cols = lambda a: lax.dynamic_slice_in_dim(a, me * (D_B // N_DEV), D_B // N_DEV, axis=2)
    pad_conv = lambda a: jnp.zeros((DEPTH, 3, D_B), F32).at[:, :, 0:D_B // N_DEV].set(a)
    small_w = dict(norm_mix_g=norm_mix_g, sgu_w=sgu_w, sgu_b=sgu_b, sgu_ln_g=sgu_ln_g, sgu_ln_b=sgu_ln_b, conv_w=pad_conv(conv_w),
                   pool_w=pool_w, pool_scale=pool_scale, norm_ff_g=norm_ff_g, norm_ple_g=norm_ple_g, final_g=final_g)
    small_m = dict(norm_mix_g=m_norm_mix_g, sgu_w=m_sgu_w, sgu_b=m_sgu_b, sgu_ln_g=m_sgu_ln_g, sgu_ln_b=m_sgu_ln_b,
                   conv_w=pad_conv(m_conv_w), pool_w=m_pool_w, pool_scale=m_pool_scale, norm_ff_g=m_norm_ff_g,
                   norm_ple_g=m_norm_ple_g, final_g=m_final_g)
    small_v = dict(norm_mix_g=v_norm_mix_g, sgu_w=v_sgu_w, sgu_b=v_sgu_b, sgu_ln_g=v_sgu_ln_g, sgu_ln_b=v_sgu_ln_b,
                   conv_w=pad_conv(v_conv_w), pool_w=v_pool_w, pool_scale=v_pool_scale,
                   norm_ff_g=v_norm_ff_g, norm_ple_g=v_norm_ple_g, final_g=v_final_g)
    gs_local = dict(gs)
    gs_local["conv_w"] = pad_conv(conv_cols(gs["conv_w"]))
    g_loc, layout = _pack_small(gs_local)
    wbuf, _ = _pack_small(small_w)
    mbuf, _ = _pack_small(small_m)
    vbuf, _ = _pack_small(small_v)
    dbuf, nmbuf, nvbuf = _small_adamw_call(wbuf, g_loc, mbuf, vbuf)
    sd, sm, sv = _unpack_small(dbuf, layout), _unpack_small(nmbuf, layout), _unpack_small(nvbuf, layout)
    unconv = lambda a: a[:, :, 0:D_B // N_DEV]
    for dct in (gs_local, sd, sm, sv):
        dct["conv_w"] = unconv(dct["conv_w"])

    finish_group(0, [res[_GROUP_B[-1]][0], dbuf])

    order = ["norm_mix_g", "w_in", "sgu_w", "sgu_b", "sgu_ln_g", "sgu_ln_b", "conv_w", "pool_w", "pool_scale", "w_out",
             "norm_ff_g", "w_ff1", "w_ff2", "norm_ple_g", "w_ple_gate", "w_ple_proj", "final_g"]
    outs = [loss, grad_x]
    for which in range(4):
        for k in order:
            if k in res:
                outs.append(swap(res[k][which]) if k == "w_in" else res[k][which])
            else:
                outs.append((gs_local, sd, sm, sv)[which][k])
    return tuple(outs)
```

```python
import functools
import math

import jax
import jax.numpy as jnp
from jax import lax
from jax.experimental import pallas as pl
from jax.experimental.pallas import tpu as pltpu

F32 = jnp.float32
BF16 = jnp.bfloat16

D = 1024
D_IN = 2176
D_A = 384
D_B = 384
D_C = 256
D_FF = 4096
D_PLE = 256
DEPTH = 4
CHUNK = 128
HALO = 16
FF_BLK = 1024
N_DEV = 8
RMS_EPS = 1e-6
LN_EPS = 1e-5
ADAM_LR = 0.001
ADAM_B1 = 0.9
ADAM_B2 = 0.999
ADAM_EPS = 1e-08
ADAM_WD = 0.01
ADAM_STEP = 10

TM_MIX_FWD = 1024
TM_FFN_FWD = 512
TM_LOSS = 512
TM_PLE_BWD = 1024
TM_FFN_BWD = 512
TM_MIX_BWD = 512
RUN_MIX_FWD = 1024
RUN_MIX_BWD = 512
V7X_VMEM_LIMIT = 60000 * 1024

ANY = pl.BlockSpec(memory_space=pl.ANY)
HBM = pl.BlockSpec(memory_space=pltpu.HBM)
SEM = pl.BlockSpec(memory_space=pltpu.SEMAPHORE)
MESH = pl.DeviceIdType.MESH


def _params(vmem=V7X_VMEM_LIMIT):
    return pltpu.CompilerParams(dimension_semantics=("arbitrary",), vmem_limit_bytes=vmem)


def _in_hbm(a):
    return pltpu.with_memory_space_constraint(a, pltpu.HBM)


def _full(shape):
    nd = len(shape)
    return pl.BlockSpec(shape, lambda i: (0,) * nd)


def _rows(tm, cols):
    return pl.BlockSpec((tm, cols), lambda i: (i, 0))


def _layer_rows(layer, tm, cols):
    return pl.BlockSpec((None, tm, cols), lambda i: (layer, i, 0))


def _mm(a, b):
    return jnp.dot(a, b, preferred_element_type=F32)


def _mm_nt(a, b):
    return lax.dot_general(a, b, (((1,), (1,)), ((), ())), preferred_element_type=F32)


def _mm_tn(a, b):
    return lax.dot_general(a, b, (((0,), (0,)), ((), ())), preferred_element_type=F32)


def _gelu_and_grad(x):
    ax = jnp.abs(x) * (1.0 / math.sqrt(2.0))
    t = 1.0 / (1.0 + 0.3275911 * ax)
    poly = t * (0.254829592 + t * (-0.284496736 + t * (1.421413741 + t * (-1.453152027 + t * 1.061405429))))
    e = jnp.exp(-0.5 * x * x)
    half = 0.5 * poly * e
    cdf = jnp.where(x < 0, half, 1.0 - half)
    return x * cdf, cdf + x * (e * (1.0 / math.sqrt(2.0 * math.pi)))


def _rms(x, g):
    rstd = lax.rsqrt(jnp.mean(x * x, axis=-1, keepdims=True) + RMS_EPS)
    xhat = x * rstd
    return xhat * g, xhat, rstd


RSTD_LANES = 128


def _save_norm(xhat, rstd, xhat_ref, rstd_ref):
    xhat_ref[...] = xhat.astype(BF16)
    rstd_ref[...] = jnp.broadcast_to(rstd, rstd_ref.shape)


def _saved_norm(xhat_ref, rstd_ref, g):
    xhat = xhat_ref[...].astype(F32)
    return xhat * g, xhat, rstd_ref[:, 0:1]


def _rms_bwd(dy, g, xhat, rstd):
    dg = jnp.sum(dy * xhat, axis=0, keepdims=True)
    dxh = dy * g
    dx = rstd * (dxh - xhat * jnp.mean(dxh * xhat, axis=-1, keepdims=True))
    return dx, dg


def _shift_down(ext, k):
    return pltpu.roll(ext, k, 0)[HALO:, :]


def _shift_up(ext, k):
    n = ext.shape[0]
    return pltpu.roll(ext, n - k, 0)[: n - HALO, :]


def _pool_select(s2, s4, s8, s16):
    lane = lax.broadcasted_iota(jnp.int32, s2.shape, 1)
    return jnp.where(lane < 64, s2, jnp.where(lane < 128, s4, jnp.where(lane < 192, s8, s16)))


def _pool_inv_count(tile_start, tm):
    pos = lax.broadcasted_iota(jnp.int32, (tm, D_C), 0) + tile_start + 1
    lane = lax.broadcasted_iota(jnp.int32, (tm, D_C), 1)
    win = jnp.where(lane < 64, 2, jnp.where(lane < 128, 4, jnp.where(lane < 192, 8, 16)))
    return 1.0 / jnp.minimum(pos, win).astype(F32)


def _head_halves(a):
    lane = lax.broadcasted_iota(jnp.int32, a.shape, 1)
    even = (lane & 64) == 0
    return jnp.where(even, a, 0.0).astype(BF16), jnp.where(even, 0.0, a).astype(BF16)


def _head_stack(lo, hi, j, nch):
    return jnp.concatenate(
        [jnp.concatenate([lo[c * CHUNK:(c + 1) * CHUNK, j * 128:(j + 1) * 128], hi[c * CHUNK:(c + 1) * CHUNK, j * 128:(j + 1) * 128]], axis=0)
         for c in range(nch)], axis=1)


def _chunks_to_lanes(a, j, nch):
    return jnp.concatenate([a[c * CHUNK:(c + 1) * CHUNK, j * 128:(j + 1) * 128] for c in range(nch)], axis=1)


def _lanes_to_chunks(o, nch):
    return jnp.concatenate([o[:, c * CHUNK:(c + 1) * CHUNK] for c in range(nch)], axis=0)


def _loads(pairs, sem):
    return [pltpu.make_async_copy(src, dst, sem.at[n]) for n, (src, dst) in enumerate(pairs)]


def _load_all(loads, zero=()):
    for cp in loads:
        cp.start()
    for ref in zero:
        ref[...] = jnp.zeros_like(ref)
    for cp in loads:
        cp.wait()


def _write_out(triples, sem):
    copies = []
    for n, (acc, stage, out) in enumerate(triples):
        _stage_bf16(acc, stage)
        cp = pltpu.make_async_copy(stage, out, sem.at[n])
        cp.start()
        copies.append(cp)
    for cp in copies:
        cp.wait()


def _halves(ref):
    half = ref.shape[0] // 2
    return ref.at[pl.ds(0, half)], ref.at[pl.ds(half, half)]


def _stage_bf16(acc, stage):
    rows = acc.shape[0]
    strip = min(rows, 128)

    @pl.loop(0, rows // strip)
    def _(n):
        sl = pl.ds(pl.multiple_of(n * strip, strip), strip)
        stage[sl, :] = acc[sl, :].astype(BF16)


N_SAVED = 5


def _mixers_fwd(pf, halo_hc, halo_zc, tile_start, prm, saved=None):
    tm = pf.shape[0]
    nch = tm // CHUNK
    u, v = pf[:, 0:D_A], pf[:, D_A:2 * D_A]
    zb, gb, gc = pf[:, 768:1152], pf[:, 1152:1536], pf[:, 1536:1920]
    zc = pf[:, 1920:2176]
    r = {}
    if saved is None:
        gu, r["dgelu_u"] = _gelu_and_grad(u)
        gv, r["dgelu_v"] = _gelu_and_grad(v)
        pmat = prm["pmat"][...]
        mu = _mm(gv.astype(BF16), pmat)
        dv = gv - mu
        var = _mm((dv * dv).astype(BF16), pmat)
        rstd = lax.rsqrt(var + LN_EPS)
        xh = dv * rstd
        r["saved"] = jnp.concatenate([gu, r["dgelu_u"], r["dgelu_v"], xh, rstd], axis=1).astype(BF16)
    else:
        gu, r["dgelu_u"], r["dgelu_v"], xh, rstd = (saved[:, n * D_A:(n + 1) * D_A] for n in range(N_SAVED))
    vlo, vhi = _head_halves(xh * prm["ln_g"][...] + prm["ln_b"][...])
    cols, v2s = [], []
    for j in range(3):
        v2 = _head_stack(vlo, vhi, j, nch)
        v2s.append(v2)
        cols.append(_lanes_to_chunks(_mm(prm["wcat"][j], v2), nch))
    mixed = jnp.concatenate(cols, axis=1) + jnp.concatenate([prm["bmat"][...]] * nch, axis=0)
    ya = gu * mixed
    r.update(gu=gu, mixed=mixed, v2s=v2s, xh=xh, ln_rstd=rstd)
    w0, w1, w2 = prm["conv_w"][0:1, :], prm["conv_w"][1:2, :], prm["conv_w"][2:3, :]
    hc = gc * zb
    ext = jnp.concatenate([halo_hc, hc], axis=0)
    h1, h2 = _shift_down(ext, 1), _shift_down(ext, 2)
    yc = w2 * hc + w1 * h1 + w0 * h2
    yb = gb * yc
    r.update(hc=hc, h1=h1, h2=h2, yc=yc, zb=zb, gb=gb, gc=gc, w0=w0, w1=w1, w2=w2)
    ext = jnp.concatenate([halo_zc, zc], axis=0)
    s2 = ext + pltpu.roll(ext, 1, 0)
    s4 = s2 + pltpu.roll(s2, 2, 0)
    s8 = s4 + pltpu.roll(s4, 4, 0)
    s16 = s8 + pltpu.roll(s8, 8, 0)
    inv = _pool_inv_count(tile_start, tm)
    pooled = _pool_select(s2, s4, s8, s16)[HALO:, :] * inv - zc
    pooledb = pooled.astype(BF16)
    pm = _mm(pooledb, prm["bd"][...])
    scale = prm["pool_scale"][...]
    ycm = pm * scale
    r.update(inv=inv, pooledb=pooledb, pm=pm, scale=scale, zc=zc)
    r["ycat"] = jnp.concatenate([ya, yb, ycm], axis=1)
    return r


_MIX_PARAM_NAMES = ("pmat", "ln_g", "ln_b", "wcat", "bmat", "conv_w", "bd", "pool_scale")


def _mix_param_specs():
    return [_full((D_A, D_A)), _full((1, D_A)), _full((1, D_A)), _full((3, CHUNK, 2 * CHUNK)), _full((CHUNK, D_A)),
            _full((3, D_B)), _full((D_C, D_C)), _full((1, D_C))]


def _mix_fwd_call(x, g_mix, w_in_t, w_out, mp, tm, run, layer):
    t = x.shape[0]
    nt = t // tm

    def body(x_ref, g_ref, pmat, ln_g, ln_b, wcat, bmat, conv_w, bd, pool_scale, win_hbm, wout_hbm,
             x1_ref, proj_ref, saved_ref, xhat_ref, rstd_ref, win_s, wout_s, halo_hc, halo_zc, load_sem):
        i = pl.program_id(0)
        loads = _loads([(win_hbm, win_s), (wout_hbm, wout_s)], load_sem)

        @pl.when(i == 0)
        def _():
            _load_all(loads, (halo_hc, halo_zc))

        prm = dict(pmat=pmat, ln_g=ln_g, ln_b=ln_b, wcat=wcat, bmat=bmat, conv_w=conv_w, bd=bd, pool_scale=pool_scale)
        xv = x_ref[...]
        h, xhat, rstd = _rms(xv, g_ref[...])
        _save_norm(xhat, rstd, xhat_ref, rstd_ref)
        pf = _mm_nt(h.astype(BF16), win_s[...])
        proj_ref[...] = pf.astype(BF16)
        hh, hz = halo_hc[...], halo_zc[...]
        parts = []
        for c in range(tm // run):
            r = _mixers_fwd(pf[c * run:(c + 1) * run, :], hh, hz, i * tm + c * run, prm)
            hh, hz = r["hc"][run - HALO:, :], r["zc"][run - HALO:, :]
            parts.append(r["ycat"].astype(BF16))
            saved_ref[c * run:(c + 1) * run, :] = r["saved"]
        halo_hc[...] = hh
        halo_zc[...] = hz
        x1_ref[...] = xv + _mm(jnp.concatenate(parts, axis=0), wout_s[...])

    return pl.pallas_call(
        body, name=f"mix_fwd_{layer}", grid=(nt,),
        in_specs=[_rows(tm, D), _full((1, D))] + _mix_param_specs() + [HBM, HBM],
        out_specs=[_rows(tm, D), _rows(tm, D_IN), _rows(tm, N_SAVED * D_A), _rows(tm, D), _rows(tm, RSTD_LANES)],
        out_shape=[jax.ShapeDtypeStruct((t, D), F32), jax.ShapeDtypeStruct((t, D_IN), BF16),
                   jax.ShapeDtypeStruct((t, N_SAVED * D_A), BF16), jax.ShapeDtypeStruct((t, D), BF16),
                   jax.ShapeDtypeStruct((t, RSTD_LANES), F32)],
        scratch_shapes=[pltpu.VMEM((D_IN, D), BF16), pltpu.VMEM((D, D), BF16),
                        pltpu.VMEM((HALO, D_B), F32), pltpu.VMEM((HALO, D_C), F32), pltpu.SemaphoreType.DMA((2,))],
        compiler_params=_params(),
    )(x, g_mix, *[mp[k] for k in _MIX_PARAM_NAMES], _in_hbm(w_in_t), _in_hbm(w_out))


def _ffn_fwd_call(x1, p, g_ff, g_ple, w1, w2, wg, wp, tm, layer):
    t = x1.shape[0]
    nt = t // tm

    def body(x1_ref, p_ref, gff_ref, gple_ref, w1_hbm, w2_hbm, wg_hbm, wp_hbm,
             x3_ref, r_ref, gate_ref, q_ref, xh1_ref, rs1_ref, xh2_ref, rs2_ref, w1_s, w2_s, wg_s, wp_s, load_sem):
        i = pl.program_id(0)
        loads = _loads([(w1_hbm, w1_s), (w2_hbm, w2_s), (wg_hbm, wg_s), (wp_hbm, wp_s)], load_sem)

        @pl.when(i == 0)
        def _():
            _load_all(loads)

        x1v = x1_ref[...]
        h2, xhat, rstd = _rms(x1v, gff_ref[...])
        _save_norm(xhat, rstd, xh1_ref, rs1_ref)
        h2b = h2.astype(BF16)
        acc = x1v
        for j in range(D_FF // FF_BLK):
            blk = slice(j * FF_BLK, (j + 1) * FF_BLK)
            rj = jnp.maximum(_mm(h2b, w1_s[:, blk]), 0.0)
            r_ref[:, blk] = rj.astype(BF16)
            acc = acc + _mm((rj * rj).astype(BF16), w2_s[blk, :])
        n3, xhat, rstd = _rms(acc, gple_ref[...])
        _save_norm(xhat, rstd, xh2_ref, rs2_ref)
        gate = jax.nn.sigmoid(_mm(n3.astype(BF16), wg_s[...]))
        gate_ref[...] = gate.astype(BF16)
        pp = _mm(p_ref[...].astype(BF16), wp_s[...])
        q_ref[...] = (pp * gate * (1.0 - gate)).astype(BF16)
        x3_ref[...] = acc + pp * gate

    return pl.pallas_call(
        body, name=f"ffn_fwd_{layer}", grid=(nt,),
        in_specs=[_rows(tm, D), _layer_rows(layer, tm, D_PLE), _full((1, D)), _full((1, D)), HBM, HBM, HBM, HBM],
        out_specs=[_rows(tm, D), _rows(tm, D_FF), _rows(tm, D), _rows(tm, D), _rows(tm, D), _rows(tm, RSTD_LANES), _rows(tm, D),
                   _rows(tm, RSTD_LANES)],
        out_shape=[jax.ShapeDtypeStruct((t, D), F32), jax.ShapeDtypeStruct((t, D_FF), BF16), jax.ShapeDtypeStruct((t, D), BF16),
                   jax.ShapeDtypeStruct((t, D), BF16), jax.ShapeDtypeStruct((t, D), BF16), jax.ShapeDtypeStruct((t, RSTD_LANES), F32),
                   jax.ShapeDtypeStruct((t, D), BF16), jax.ShapeDtypeStruct((t, RSTD_LANES), F32)],
        scratch_shapes=[pltpu.VMEM((D, D_FF), BF16), pltpu.VMEM((D_FF, D), BF16),
                        pltpu.VMEM((D, D), BF16), pltpu.VMEM((D_PLE, D), BF16), pltpu.SemaphoreType.DMA((4,))],
        compiler_params=_params(),
    )(x1, p, g_ff, g_ple, _in_hbm(w1), _in_hbm(w2), _in_hbm(wg), _in_hbm(wp))


def _loss_call(xl, target, final_g, tm):
    t = xl.shape[0]
    nt = t // tm

    def body(x_ref, t_ref, g_ref, sq_ref, dx_ref, dg_ref):
        i = pl.program_id(0)

        @pl.when(i == 0)
        def _():
            sq_ref[...] = jnp.zeros_like(sq_ref)
            dg_ref[...] = jnp.zeros_like(dg_ref)

        g = g_ref[...]
        y, xhat, rstd = _rms(x_ref[...], g)
        err = y - t_ref[...]
        sq_ref[...] += jnp.sum(err * err, axis=0, keepdims=True)
        dx, dg = _rms_bwd(err * (1.0 / D), g, xhat, rstd)
        dx_ref[...] = dx
        dg_ref[...] += dg

    return pl.pallas_call(
        body, name="loss_head", grid=(nt,),
        in_specs=[_rows(tm, D), _rows(tm, D), _full((1, D))],
        out_specs=[_full((1, D)), _rows(tm, D), _full((1, D))],
        out_shape=[jax.ShapeDtypeStruct((1, D), F32), jax.ShapeDtypeStruct((t, D), F32), jax.ShapeDtypeStruct((1, D), F32)],
        compiler_params=_params(),
    )(xl, target, final_g)


def _ple_bwd_call(dx3, xhat2, rstd2, gate, q, p, g_ple, wg, after, tm, layer):
    t = dx3.shape[0]
    nt = t // tm

    def body(dx3_ref, xhat_ref, rstd_ref, gate_ref, q_ref, p_ref, g_ref, wg_hbm, after_ref,
             dx2_ref, dg_ref, dwg_hbm, dwp_hbm, wg_s, wp_s, dwg_acc, dwp_acc, load_sem):
        i = pl.program_id(0)
        loads = _loads([(wg_hbm, wg_s)], load_sem)

        @pl.when(i == 0)
        def _():
            _load_all(loads, (dwg_acc, dwp_acc, dg_ref))

        g = g_ref[...]
        dx3v = dx3_ref[...]
        dwp_acc[...] += _mm_tn(p_ref[...].astype(BF16), (dx3v * gate_ref[...].astype(F32)).astype(BF16))
        dgpre = (dx3v * q_ref[...].astype(F32)).astype(BF16)
        n3, xhat, rstd = _saved_norm(xhat_ref, rstd_ref, g)
        dwg_acc[...] += _mm_tn(n3.astype(BF16), dgpre)
        dn3 = _mm_nt(dgpre, wg_s[...])
        dx, dg = _rms_bwd(dn3, g, xhat, rstd)
        dx2_ref[...] = dx3v + dx
        dg_ref[...] += dg

        @pl.when(i == nt - 1)
        def _():
            _write_out([(dwg_acc, wg_s, dwg_hbm), (dwp_acc, wp_s, dwp_hbm)], load_sem)

    return pl.pallas_call(
        body, name=f"ple_bwd_{layer}", grid=(nt,),
        in_specs=[_rows(tm, D), _rows(tm, D), _rows(tm, RSTD_LANES), _rows(tm, D), _rows(tm, D), _layer_rows(layer, tm, D_PLE),
                  _full((1, D)), HBM, ANY],
        out_specs=[_rows(tm, D), _full((1, D)), HBM, HBM],
        out_shape=[jax.ShapeDtypeStruct((t, D), F32), jax.ShapeDtypeStruct((1, D), F32),
                   pltpu.HBM((D, D), BF16), pltpu.HBM((D_PLE, D), BF16)],
        scratch_shapes=[pltpu.VMEM((D, D), BF16), pltpu.VMEM((D_PLE, D), BF16),
                        pltpu.VMEM((D, D), F32), pltpu.VMEM((D_PLE, D), F32), pltpu.SemaphoreType.DMA((2,))],
        compiler_params=_params(),
    )(dx3, xhat2, rstd2, gate, q, p, g_ple, _in_hbm(wg), after)


def _ffn_bwd_hidden_call(dx2, r, w2, tm, layer):
    t = dx2.shape[0]
    nt = t // tm

    def body(dx2_ref, r_ref, w2_hbm, da_ref, dw2_hbm, w2_s, dw2_acc, load_sem):
        i = pl.program_id(0)
        loads = _loads([(w2_hbm, w2_s)], load_sem)

        @pl.when(i == 0)
        def _():
            _load_all(loads, (dw2_acc,))

        dxb = dx2_ref[...].astype(BF16)
        for j in range(D_FF // FF_BLK):
            blk = slice(j * FF_BLK, (j + 1) * FF_BLK)
            rj = r_ref[:, blk].astype(F32)
            ds = _mm_nt(dxb, w2_s[blk, :])
            da_ref[:, blk] = (2.0 * rj * ds).astype(BF16)
            dw2_acc[blk, :] += _mm_tn((rj * rj).astype(BF16), dxb)

        @pl.when(i == nt - 1)
        def _():
            _write_out(list(zip(_halves(dw2_acc), _halves(w2_s), _halves(dw2_hbm))), load_sem)

    return pl.pallas_call(
        body, name=f"ffn_bwd_hidden_{layer}", grid=(nt,),
        in_specs=[_rows(tm, D), _rows(tm, D_FF), HBM],
        out_specs=[_rows(tm, D_FF), HBM],
        out_shape=[jax.ShapeDtypeStruct((t, D_FF), BF16), pltpu.HBM((D_FF, D), BF16)],
        scratch_shapes=[pltpu.VMEM((D_FF, D), BF16), pltpu.VMEM((D_FF, D), F32), pltpu.SemaphoreType.DMA((2,))],
        compiler_params=_params(),
    )(dx2, r, _in_hbm(w2))


def _ffn_bwd_input_call(da, xhat1, rstd1, dx2, g_ff, w1, tm, layer):
    t = dx2.shape[0]
    nt = t // tm

    def body(da_ref, xhat_ref, rstd_ref, dx2_ref, g_ref, w1_hbm, dx1_ref, dg_ref, dw1_hbm, w1_s, dw1_acc, load_sem):
        i = pl.program_id(0)
        loads = _loads([(w1_hbm, w1_s)], load_sem)

        @pl.when(i == 0)
        def _():
            _load_all(loads, (dw1_acc, dg_ref))

        g = g_ref[...]
        h2, xhat, rstd = _saved_norm(xhat_ref, rstd_ref, g)
        h2b = h2.astype(BF16)
        dh2 = jnp.zeros((tm, D), F32)
        for j in range(D_FF // FF_BLK):
            blk = slice(j * FF_BLK, (j + 1) * FF_BLK)
            daj = da_ref[:, blk]
            dh2 = dh2 + _mm_nt(daj, w1_s[:, blk])
            dw1_acc[:, blk] += _mm_tn(h2b, daj)
        dx, dg = _rms_bwd(dh2, g, xhat, rstd)
        dx1_ref[...] = dx2_ref[...] + dx
        dg_ref[...] += dg

        @pl.when(i == nt - 1)
        def _():
            _write_out(list(zip(_halves(dw1_acc), _halves(w1_s), _halves(dw1_hbm))), load_sem)

    return pl.pallas_call(
        body, name=f"ffn_bwd_input_{layer}", grid=(nt,),
        in_specs=[_rows(tm, D_FF), _rows(tm, D), _rows(tm, RSTD_LANES), _rows(tm, D), _full((1, D)), HBM],
        out_specs=[_rows(tm, D), _full((1, D)), HBM],
        out_shape=[jax.ShapeDtypeStruct((t, D), F32), jax.ShapeDtypeStruct((1, D), F32), pltpu.HBM((D, D_FF), BF16)],
        scratch_shapes=[pltpu.VMEM((D, D_FF), BF16), pltpu.VMEM((D, D_FF), F32), pltpu.SemaphoreType.DMA((2,))],
        compiler_params=_params(),
    )(da, xhat1, rstd1, dx2, g_ff, _in_hbm(w1))


def _mix_bwd_call(dx1, xhat0, rstd0, proj, saved, g_mix, w_in_t, w_out, mp, wtcat, trilcat, headsel, after, tm, run, layer):
    t = dx1.shape[0]
    nt = t // tm
    nrun = tm // run
    nch = run // CHUNK
    hb = tm // HALO

    def rev(i):
        return nt - 1 - i

    def body(dx1_ref, xhat_ref, rstd_ref, proj_ref, halo_ref, saved_ref, g_ref, pmat, ln_g, ln_b, wcat, bmat, conv_w, bd, pool_scale,
             wtcat_ref, tril_ref, sel_ref, win_hbm, wout_hbm, after_ref,
             dx_ref, dg_ref, dwcat_ref, dsb_ref, dlng_ref, dlnb_ref, dconv_ref, dbd_ref, dscale_ref, dwin_hbm, dwout_hbm,
             win_s, wout_s, dwin_acc, dwout_acc, dbm_acc, carry_yc, carry_q, load_sem):
        i = pl.program_id(0)
        ri = nt - 1 - i
        loads = _loads([(wout_hbm, wout_s), (win_hbm, win_s)], load_sem)

        @pl.when(i == 0)
        def _():
            _load_all(loads, (dwin_acc, dwout_acc, dbm_acc, carry_yc, carry_q, dg_ref, dwcat_ref, dlng_ref, dlnb_ref,
                              dconv_ref, dbd_ref, dscale_ref))

        prm = dict(pmat=pmat, ln_g=ln_g, ln_b=ln_b, wcat=wcat, bmat=bmat, conv_w=conv_w, bd=bd, pool_scale=pool_scale)
        g = g_ref[...]
        h, xhat, rstd = _saved_norm(xhat_ref, rstd_ref, g)
        hb16 = h.astype(BF16)
        dx1v = dx1_ref[...]
        dx1b = dx1v.astype(BF16)
        dycat = _mm_nt(dx1b, wout_s[...])
        lng = ln_g[...]
        pm_ = pmat[...]
        cy, cq = carry_yc[...], carry_q[...]
        ycat_parts, dproj_parts = [None] * nrun, [None] * nrun
        dbm = dlng = dlnb = dscale = dcv0 = dcv1 = dcv2 = None
        add = lambda tot, v: v if tot is None else tot + v
        for c in reversed(range(nrun)):
            rows = slice(c * run, (c + 1) * run)
            pf = proj_ref[rows, :].astype(F32)
            if c > 0:
                ph = proj_ref[c * run - HALO:c * run, :].astype(F32)
            else:
                ph = halo_ref[...].astype(F32) * (ri > 0).astype(F32)
            r = _mixers_fwd(pf, ph[:, 1536:1920] * ph[:, 768:1152], ph[:, 1920:2176], ri * tm + c * run, prm,
                            saved_ref[rows, :].astype(F32))
            ycat_parts[c] = r["ycat"].astype(BF16)
            dya, dyb, dyc = dycat[rows, 0:D_A], dycat[rows, D_A:D_A + D_B], dycat[rows, D_A + D_B:D]

            dgu = dya * r["mixed"]
            dmix = dya * r["gu"]
            dmix_b = dmix.astype(BF16)
            dlo, dhi = _head_halves(dmix)
            for k in range(nch):
                dbm = add(dbm, dmix[k * CHUNK:(k + 1) * CHUNK, :])
            dvn_cols = []
            for j in range(3):
                dwcat_ref[j] += _mm_nt(_chunks_to_lanes(dmix_b, j, nch), r["v2s"][j])
                dvn_cols.append(_lanes_to_chunks(_mm(wtcat_ref[j], _head_stack(dlo, dhi, j, nch)), nch))
            dvn = jnp.concatenate(dvn_cols, axis=1)
            xh = r["xh"]
            dlng = add(dlng, jnp.sum(dvn * xh, axis=0, keepdims=True))
            dlnb = add(dlnb, jnp.sum(dvn, axis=0, keepdims=True))
            dxh = dvn * lng
            m1 = _mm(dxh.astype(BF16), pm_)
            m2 = _mm((dxh * xh).astype(BF16), pm_)
            dgv = r["ln_rstd"] * (dxh - m1 - xh * m2)
            du = dgu * r["dgelu_u"]
            dv = dgv * r["dgelu_v"]

            dgb = dyb * r["yc"]
            dyc2 = dyb * r["gb"]
            dcv0 = add(dcv0, jnp.sum(dyc2 * r["h2"], axis=0, keepdims=True))
            dcv1 = add(dcv1, jnp.sum(dyc2 * r["h1"], axis=0, keepdims=True))
            dcv2 = add(dcv2, jnp.sum(dyc2 * r["hc"], axis=0, keepdims=True))
            ext = jnp.concatenate([dyc2, cy], axis=0)
            dhc = r["w2"] * dyc2 + r["w1"] * _shift_up(ext, 1) + r["w0"] * _shift_up(ext, 2)
            cy = dyc2[0:HALO, :]
            dgc = dhc * r["zb"]
            dzb = dhc * r["gc"]

            dscale = add(dscale, jnp.sum(dyc * r["pm"], axis=0, keepdims=True))
            dpm = (dyc * r["scale"]).astype(BF16)
            dbd_ref[...] += _mm_tn(r["pooledb"], dpm)
            dpooled = _mm_nt(dpm, bd[...])
            q = dpooled * r["inv"]
            ext = jnp.concatenate([q, cq], axis=0)
            n = run + HALO
            r2 = ext + pltpu.roll(ext, n - 1, 0)
            r4 = r2 + pltpu.roll(r2, n - 2, 0)
            r8 = r4 + pltpu.roll(r4, n - 4, 0)
            r16 = r8 + pltpu.roll(r8, n - 8, 0)
            dzc = _pool_select(r2, r4, r8, r16)[0:run, :] - dpooled
            cq = q[0:HALO, :]
            dproj_parts[c] = jnp.concatenate([du, dv, dzb, dgb, dgc, dzc], axis=1).astype(BF16)

        carry_yc[...] = cy
        carry_q[...] = cq
        dbm_acc[...] += dbm
        dlng_ref[...] += dlng
        dlnb_ref[...] += dlnb
        dscale_ref[...] += dscale
        dconv_ref[0:1, :] += dcv0
        dconv_ref[1:2, :] += dcv1
        dconv_ref[2:3, :] += dcv2
        dwout_acc[...] += _mm_tn(jnp.concatenate(ycat_parts, axis=0), dx1b)
        dproj = jnp.concatenate(dproj_parts, axis=0)
        dwin_acc[...] += _mm_tn(dproj, hb16)
        dh = _mm(dproj, win_s[...])
        dx, dg = _rms_bwd(dh, g, xhat, rstd)
        dx_ref[...] = dx1v + dx
        dg_ref[...] += dg

        @pl.when(i == nt - 1)
        def _():
            _write_out([(dwin_acc, win_s, dwin_hbm), (dwout_acc, wout_s, dwout_hbm)], load_sem)
            for j in range(3):
                dwcat_ref[j] = dwcat_ref[j] * tril_ref[...]
            acc = dbm_acc[...]
            hi = acc.astype(BF16)
            lo = (acc - hi.astype(F32)).astype(BF16)
            dsb_ref[...] = _mm(hi, sel_ref[...]) + _mm(lo, sel_ref[...])

    return pl.pallas_call(
        body, name=f"mix_bwd_{layer}", grid=(nt,),
        in_specs=[pl.BlockSpec((tm, D), lambda i: (rev(i), 0)), pl.BlockSpec((tm, D), lambda i: (rev(i), 0)),
                  pl.BlockSpec((tm, RSTD_LANES), lambda i: (rev(i), 0)), pl.BlockSpec((tm, D_IN), lambda i: (rev(i), 0)),
                  pl.BlockSpec((HALO, D_IN), lambda i: (jnp.maximum(rev(i) * hb - 1, 0), 0)),
                  pl.BlockSpec((tm, N_SAVED * D_A), lambda i: (rev(i), 0)), _full((1, D))] + _mix_param_specs()
                 + [_full((3, CHUNK, 2 * CHUNK)), _full((CHUNK, 2 * CHUNK)), _full((D_A, CHUNK)), HBM, HBM, ANY],
        out_specs=[pl.BlockSpec((tm, D), lambda i: (rev(i), 0)), _full((1, D)), _full((3, CHUNK, 2 * CHUNK)),
                   _full((CHUNK, CHUNK)), _full((1, D_A)), _full((1, D_A)), _full((3, D_B)), _full((D_C, D_C)),
                   _full((1, D_C)), HBM, HBM],
        out_shape=[jax.ShapeDtypeStruct((t, D), F32), jax.ShapeDtypeStruct((1, D), F32),
                   jax.ShapeDtypeStruct((3, CHUNK, 2 * CHUNK), F32), jax.ShapeDtypeStruct((CHUNK, CHUNK), F32),
                   jax.ShapeDtypeStruct((1, D_A), F32), jax.ShapeDtypeStruct((1, D_A), F32),
                   jax.ShapeDtypeStruct((3, D_B), F32), jax.ShapeDtypeStruct((D_C, D_C), F32),
                   jax.ShapeDtypeStruct((1, D_C), F32), pltpu.HBM((D_IN, D), BF16), pltpu.HBM((D, D), BF16)],
        scratch_shapes=[pltpu.VMEM((D_IN, D), BF16), pltpu.VMEM((D, D), BF16),
                        pltpu.VMEM((D_IN, D), F32), pltpu.VMEM((D, D), F32), pltpu.VMEM((CHUNK, D_A), F32),
                        pltpu.VMEM((HALO, D_B), F32), pltpu.VMEM((HALO, D_C), F32), pltpu.SemaphoreType.DMA((2,))],
        compiler_params=_params(),
    )(dx1, xhat0, rstd0, proj, proj, saved, g_mix, *[mp[k] for k in _MIX_PARAM_NAMES], wtcat, trilcat, headsel, _in_hbm(w_in_t), _in_hbm(w_out), after)


def _coords():
    return lax.axis_index("x"), lax.axis_index("y"), lax.axis_index("c")


EFFECT = pltpu.SideEffectType.DATAFLOW_SIDE_EFFECTING


def _peer(k):
    x, y, c = _coords()
    px, py, pc = x ^ (k >> 2), y ^ ((k >> 1) & 1), c ^ (k & 1)
    return (px, py, pc), 4 * px + 2 * py + pc


def _landing_shape(shape, mode):
    if mode == "block":
        return (N_DEV,) + shape
    if mode == "slot":
        return shape
    if mode == "cols_in":
        return (shape[0], N_DEV * shape[1])
    return (N_DEV, shape[0], shape[1] // N_DEV)


def _pieces(src, land, mode, src_idx, land_idx):
    if mode == "block":
        return src, land.at[land_idx]
    if mode == "slot":
        return src.at[src_idx], land.at[land_idx]
    if mode == "cols_in":
        cw = src.shape[1]
        return src, land.at[:, pl.ds(pl.multiple_of(land_idx * cw, 128), cw)]
    cw = land.shape[2]
    return src.at[:, pl.ds(pl.multiple_of(src_idx * cw, 128), cw)], land.at[land_idx]


def _exchange_copy(src, land, mode, send_sem, recv_sem, ai, k, starting):
    x, y, c = _coords()
    peer, pidx = _peer(k)
    s, d = _pieces(src, land, mode, pidx, 4 * x + 2 * y + c if starting else pidx)
    i = ai * (N_DEV - 1) + k - 1
    return pltpu.make_async_remote_copy(src_ref=s, dst_ref=d, send_sem=send_sem.at[i], recv_sem=recv_sem.at[i],
                                        device_id=peer, device_id_type=MESH)


def _own_copy(src, land, mode, local_sem, ai):
    x, y, c = _coords()
    me = 4 * x + 2 * y + c
    s, d = _pieces(src, land, mode, me, me)
    return pltpu.make_async_copy(s, d, local_sem.at[ai])


def _item_src(ins, item):
    a, sub = item
    return ins[a] if sub is None else ins[a].at[sub]


def _exchange_start(srcs, items, modes, groups, name):
    n, ni, ng = len(srcs), len(items), len(groups)
    shapes = [srcs[a].shape if sub is None else srcs[a].shape[1:] for a, sub in items]
    land_shapes = [pltpu.HBM(_landing_shape(sh, m), srcs[a].dtype) for sh, m, (a, _) in zip(shapes, modes, items)]

    def body(*refs):
        ins = refs[:n]
        sems = refs[n:n + 3 * ng]
        land_refs = refs[n + 3 * ng:n + 3 * ng + ni]
        token = refs[-1]
        for g, idxs in enumerate(groups):
            for ai, it in enumerate(idxs):
                src = _item_src(ins, items[it])
                _own_copy(src, land_refs[it], modes[it], sems[3 * g + 2], ai).start()
                for k in range(1, N_DEV):
                    _exchange_copy(src, land_refs[it], modes[it], sems[3 * g], sems[3 * g + 1], ai, k, True).start()
        token[...] = jnp.zeros_like(token)

    sem_shapes = []
    for idxs in groups:
        sem_shapes += [pltpu.SemaphoreType.DMA((len(idxs) * (N_DEV - 1),))] * 2 + [pltpu.SemaphoreType.DMA((len(idxs),))]
    out = pl.pallas_call(
        body, name=name,
        out_shape=tuple(sem_shapes) + tuple(land_shapes) + (jax.ShapeDtypeStruct((8, 128), F32),),
        in_specs=[HBM] * n,
        out_specs=tuple([SEM] * (3 * ng) + [HBM] * ni + [pl.BlockSpec(memory_space=pltpu.VMEM)]),
        compiler_params=pltpu.CompilerParams(has_side_effects=EFFECT),
    )(*[pltpu.with_memory_space_constraint(s, pltpu.HBM) for s in srcs])
    sems = [tuple(out[3 * g:3 * g + 3]) for g in range(ng)]
    return sems, list(out[3 * ng:3 * ng + ni]), out[-1]


def _exchange_wait(sems, srcs, items, lands, modes, groups, after, name):
    n, ni, ng = len(srcs), len(items), len(groups)

    def body(*refs):
        ins, land_refs = refs[:n], refs[n:n + ni]
        sem_refs = refs[n + ni:n + ni + 3 * ng]
        for g, idxs in enumerate(groups):
            for ai, it in enumerate(idxs):
                src = _item_src(ins, items[it])
                _own_copy(src, land_refs[it], modes[it], sem_refs[3 * g + 2], ai).wait()
                for k in range(1, N_DEV):
                    cp = _exchange_copy(src, land_refs[it], modes[it], sem_refs[3 * g], sem_refs[3 * g + 1], ai, k, False)
                    cp.wait_send()
                    cp.wait_recv()

    flat_sems = [s for trio in sems for s in trio]
    afters = list(after) if isinstance(after, (list, tuple)) else [after]
    out = pl.pallas_call(
        body, name=name,
        out_shape=tuple(pltpu.HBM(l.shape, l.dtype) for l in lands),
        in_specs=[HBM] * (n + ni) + [SEM] * (3 * ng) + [ANY] * len(afters),
        out_specs=tuple([HBM] * ni),
        input_output_aliases={n + i: i for i in range(ni)},
        compiler_params=pltpu.CompilerParams(has_side_effects=EFFECT),
    )(*srcs, *lands, *flat_sems, *afters)
    return list(out)


def _slot_sum_call(landed):
    n = len(landed)

    def body(*refs):
        for src, dst in zip(refs[:n], refs[n:]):
            tot = src[0]
            for j in range(1, N_DEV):
                tot = tot + src[j]
            dst[...] = tot

    vm = pl.BlockSpec(memory_space=pltpu.VMEM)
    return pl.pallas_call(
        body, name="small_grads_sum", in_specs=[vm] * n, out_specs=[vm] * n,
        out_shape=[jax.ShapeDtypeStruct(a.shape[1:], F32) for a in landed],
        compiler_params=pltpu.CompilerParams(vmem_limit_bytes=V7X_VMEM_LIMIT),
    )(*landed)


def _adamw(w, g, m, v):
    m = ADAM_B1 * m + (1.0 - ADAM_B1) * g
    v = ADAM_B2 * v + (1.0 - ADAM_B2) * (g * g)
    m_hat = m / (1.0 - ADAM_B1 ** ADAM_STEP)
    v_hat = v / (1.0 - ADAM_B2 ** ADAM_STEP)
    delta = -ADAM_LR * (m_hat / (jnp.sqrt(v_hat) + ADAM_EPS) + ADAM_WD * w)
    return delta, m, v


def _reduce_adamw_call(recvs, w, m, v, name):
    nl = len(recvs)
    _, r, c = recvs[0].shape
    rb = 256 if r % 256 == 0 else r
    nb = r // rb

    def body(*refs):
        recv_refs = refs[:nl]
        w_ref, m_ref, v_ref, g_ref, d_ref, nm_ref, nv_ref = refs[nl:]
        for l in range(nl):
            @pl.when(pl.program_id(0) == l)
            def _(l=l):
                g = recv_refs[l][0].astype(F32)
                for j in range(1, N_DEV):
                    g = g + recv_refs[l][j].astype(F32)
                delta, nm, nv = _adamw(w_ref[0], g, m_ref[0], v_ref[0])
                g_ref[0] = g
                d_ref[0] = delta
                nm_ref[0] = nm
                nv_ref[0] = nv

    def recv_spec(l):
        return pl.BlockSpec((N_DEV, rb, c), lambda lg, i: (0, jnp.where(lg == l, i, jnp.where(lg < l, 0, nb - 1)), 0))

    blk = pl.BlockSpec((1, rb, c), lambda lg, i: (lg, i, 0))
    shp = jax.ShapeDtypeStruct((nl, r, c), F32)
    return pl.pallas_call(
        body, name=name, grid=(nl, nb),
        in_specs=[recv_spec(l) for l in range(nl)] + [blk, blk, blk],
        out_specs=[blk, blk, blk, blk], out_shape=[shp, shp, shp, shp],
        compiler_params=pltpu.CompilerParams(dimension_semantics=("arbitrary", "arbitrary"), vmem_limit_bytes=V7X_VMEM_LIMIT),
    )(*recvs, w, m, v)


def _small_adamw_call(w, g, m, v):
    def body(w_ref, g_ref, m_ref, v_ref, d_ref, nm_ref, nv_ref):
        delta, nm, nv = _adamw(w_ref[...], g_ref[...], m_ref[...], v_ref[...])
        d_ref[...] = delta
        nm_ref[...] = nm
        nv_ref[...] = nv

    shp = jax.ShapeDtypeStruct(w.shape, F32)
    vm = pl.BlockSpec(memory_space=pltpu.VMEM)
    return pl.pallas_call(body, name="small_adamw", in_specs=[vm] * 4, out_specs=[vm] * 3, out_shape=[shp, shp, shp],
                          compiler_params=pltpu.CompilerParams(vmem_limit_bytes=V7X_VMEM_LIMIT))(w, g, m, v)


_GATHER_MODE = dict(w_in="block", w_out="block", w_ff1="cols_in", w_ff2="block", w_ple_gate="block", w_ple_proj="cols_in")
_SCATTER_MODE = dict(w_in="slot", w_out="slot", w_ff1="cols_out", w_ff2="slot", w_ple_gate="slot", w_ple_proj="cols_out")
_GROUP_A = ("w_in", "w_out")
_GROUP_B = ("w_ff1", "w_ff2", "w_ple_gate", "w_ple_proj")


def _gathered_full(k, landed):
    if _GATHER_MODE[k] == "cols_in":
        return landed
    n, r, c = landed.shape
    return landed.reshape(n * r, c)


def _grad_send(k, g):
    if _SCATTER_MODE[k] == "cols_out":
        return g
    r8, c = g.shape
    return g.reshape(N_DEV, r8 // N_DEV, c)


_SMALL_ORDER = ("norm_mix_g", "sgu_w", "sgu_b", "sgu_ln_g", "sgu_ln_b", "conv_w", "pool_w", "pool_scale",
                "norm_ff_g", "norm_ple_g", "final_g")


def _pack_small(d, order=_SMALL_ORDER):
    pieces, layout = [], []
    for k in order:
        flat = d[k].reshape(-1)
        n = flat.shape[0]
        pad = (-n) % 128
        pieces.append(jnp.pad(flat, (0, pad)))
        layout.append((k, d[k].shape, n, n + pad))
    flat = jnp.concatenate(pieces)
    pad = (-flat.shape[0]) % 1024
    return jnp.pad(flat, (0, pad)).reshape(-1, 128), layout


def _unpack_small(buf, layout):
    flat = buf.reshape(-1)
    out, off = {}, 0
    for k, shape, n, padded in layout:
        out[k] = flat[off:off + n].reshape(shape)
        off += padded
    return out


def kernel(x, p, norm_mix_g, w_in, sgu_w, sgu_b, sgu_ln_g, sgu_ln_b, conv_w, pool_w, pool_scale, w_out, norm_ff_g, w_ff1, w_ff2, norm_ple_g, w_ple_gate, w_ple_proj, final_g, loss_target, m_norm_mix_g, m_w_in, m_sgu_w, m_sgu_b, m_sgu_ln_g, m_sgu_ln_b, m_conv_w, m_pool_w, m_pool_scale, m_w_out, m_norm_ff_g, m_w_ff1, m_w_ff2, m_norm_ple_g, m_w_ple_gate, m_w_ple_proj, m_final_g, v_norm_mix_g, v_w_in, v_sgu_w, v_sgu_b, v_sgu_ln_g, v_sgu_ln_b, v_conv_w, v_pool_w, v_pool_scale, v_w_out, v_norm_ff_g, v_w_ff1, v_w_ff2, v_norm_ple_g, v_w_ple_gate, v_w_ple_proj, v_final_g):
    t = x.shape[1]
    xc, yc_, cc = _coords()
    me = 4 * xc + 2 * yc_ + cc
    tm = lambda want: min(want, t)

    shard_names = _GROUP_A + _GROUP_B
    swap = lambda a: jnp.transpose(a, (0, 2, 1))
    shard = dict(w_in=swap(w_in), w_out=w_out, w_ff1=w_ff1, w_ff2=w_ff2, w_ple_gate=w_ple_gate, w_ple_proj=w_ple_proj)
    conv_pad = jnp.zeros((16, 128), F32).at[0:DEPTH * 3, 0:D_B // N_DEV].set(conv_w.reshape(DEPTH * 3, D_B // N_DEV))
    ag_srcs = [shard[k].astype(BF16) for k in shard_names] + [conv_pad]
    ag_items, ag_modes, ag_groups = [], [], []
    for l in range(DEPTH):
        for names in (_GROUP_A, _GROUP_B):
            ag_groups.append(list(range(len(ag_items), len(ag_items) + len(names))))
            ag_items += [(shard_names.index(k), l) for k in names]
            ag_modes += [_GATHER_MODE[k] for k in names]
            if l == 0 and names is _GROUP_A:
                ag_groups[-1].append(len(ag_items))
                ag_items.append((len(shard_names), None))
                ag_modes.append("block")
    ag_sems, ag_lands, _ = _exchange_start(ag_srcs, ag_items, ag_modes, ag_groups, "weights_gather_start")

    def gathered(l, which, after):
        idxs = ag_groups[2 * l + which]
        landed = _exchange_wait([ag_sems[2 * l + which]], ag_srcs, [ag_items[i] for i in idxs], [ag_lands[i] for i in idxs],
                                [ag_modes[i] for i in idxs], [list(range(len(idxs)))], after, f"weights_gather_wait_{l}_{which}")
        full = {k: _gathered_full(k, got) for k, got in zip((_GROUP_A, _GROUP_B)[which], landed)}
        if l == 0 and which == 0:
            full["conv_w"] = jnp.transpose(landed[-1][:, 0:DEPTH * 3, 0:D_B // N_DEV].reshape(N_DEV, DEPTH, 3, D_B // N_DEV),
                                           (1, 2, 0, 3)).reshape(DEPTH, 3, D_B)
        return full

    idx = jnp.arange(D_A)
    pmat = ((idx[:, None] // 64) == (idx[None, :] // 64)).astype(BF16) * (1.0 / 64.0)
    pmat = pmat.astype(BF16)
    tril = jnp.tril(jnp.ones((CHUNK, CHUNK), F32))
    trilcat = jnp.concatenate([tril, tril], axis=1)
    headsel = ((idx[:, None] // 64) == jnp.arange(CHUNK)[None, :]).astype(BF16)
    row = lambda a: a.reshape(1, -1)

    def mix_params(l):
        wm = sgu_w[l] * tril[None]
        wcat = jnp.stack([jnp.concatenate([wm[2 * j], wm[2 * j + 1]], axis=1) for j in range(3)]).astype(BF16)
        wtcat = jnp.stack([jnp.concatenate([wm[2 * j].T, wm[2 * j + 1].T], axis=1) for j in range(3)]).astype(BF16)
        bmat = jnp.repeat(sgu_b[l].T, 64, axis=1)
        bd = jnp.zeros((D_C, D_C), F32)
        for gi in range(4):
            bd = bd.at[gi * 64:(gi + 1) * 64, gi * 64:(gi + 1) * 64].set(pool_w[l, gi])
        mp = dict(pmat=pmat, ln_g=row(sgu_ln_g[l]), ln_b=row(sgu_ln_b[l]), wcat=wcat, bmat=bmat, conv_w=conv_full[l],
                  bd=bd.astype(BF16), pool_scale=row(pool_scale[l]))
        return mp, wtcat

    xs = x.reshape(t, D)
    p_layers = p.reshape(DEPTH, t, D_PLE)
    saved, full_w = [], []
    conv_full = None
    for l in range(DEPTH):
        wa = gathered(l, 0, xs)
        if l == 0:
            conv_full = wa["conv_w"]
        mp, _ = mix_params(l)
        x1, proj, sgu_saved, xh0, rs0 = _mix_fwd_call(xs, row(norm_mix_g[l]), wa["w_in"], wa["w_out"], mp, tm(TM_MIX_FWD), tm(RUN_MIX_FWD), l)
        wb = gathered(l, 1, x1)
        x3, r, gate, q, xh1, rs1, xh2, rs2 = _ffn_fwd_call(x1, p_layers, row(norm_ff_g[l]), row(norm_ple_g[l]), wb["w_ff1"], wb["w_ff2"],
                                        wb["w_ple_gate"], wb["w_ple_proj"], tm(TM_FFN_FWD), l)
        saved.append((proj, sgu_saved, r, (gate, q), (xh0, rs0), (xh1, rs1), (xh2, rs2)))
        full_w.append({**wa, **wb})
        xs = x3

    sq, dx, dfinal = _loss_call(xs, loss_target.reshape(t, D), row(final_g), tm(TM_LOSS))
    loss = lax.psum(jnp.sum(sq) * (0.5 / D), ("x", "y", "c"))

    layer_keys = tuple(k for k in _SMALL_ORDER if k != "final_g")
    small = {k: [None] * DEPTH for k in layer_keys}
    small_ex, small_layout = [None] * DEPTH, None
    ex = {}
    token = jnp.zeros((8, 128), F32)

    for l in reversed(range(DEPTH)):
        proj, sgu_saved, r, gates, norm0, norm1, norm2 = saved[l]
        fw = full_w[l]
        dx2, dgple, dwg, dwp = _ple_bwd_call(dx, *norm2, *gates, p_layers, row(norm_ple_g[l]), fw["w_ple_gate"], token,
                                             tm(TM_PLE_BWD), l)
        da, dw2 = _ffn_bwd_hidden_call(dx2, r, fw["w_ff2"], tm(TM_FFN_BWD), l)
        dx1, dgff, dw1 = _ffn_bwd_input_call(da, *norm1, dx2, row(norm_ff_g[l]), fw["w_ff1"], tm(TM_FFN_BWD), l)
        sends = [_grad_send(k, g_) for k, g_ in zip(_GROUP_B, (dw1, dw2, dwg, dwp))]
        sems, lands, token = _exchange_start(sends, [(i, None) for i in range(len(sends))], [_SCATTER_MODE[k] for k in _GROUP_B],
                                             [list(range(len(sends)))], f"grads_exchange_start_{l}_1")
        ex[(l, 1)] = (sems[0], sends, lands)
        mp, wtcat = mix_params(l)
        (dx, dgmix, dwcat, dsb, dlng, dlnb, dconv, dbd, dscale, dwin, dwout) = _mix_bwd_call(
            dx1, *norm0, proj, sgu_saved, row(norm_mix_g[l]), fw["w_in"], fw["w_out"], mp, wtcat, trilcat, headsel, token,
            tm(TM_MIX_BWD), tm(RUN_MIX_BWD), l)
        small["norm_mix_g"][l] = dgmix[0]
        small["sgu_w"][l] = jnp.stack([dwcat[h // 2][:, (h % 2) * CHUNK:(h % 2 + 1) * CHUNK] for h in range(6)])
        small["sgu_b"][l] = dsb[:, 0:6].T
        small["sgu_ln_g"][l], small["sgu_ln_b"][l] = dlng[0], dlnb[0]
        small["conv_w"][l] = dconv
        small["pool_w"][l] = jnp.stack([dbd[gi * 64:(gi + 1) * 64, gi * 64:(gi + 1) * 64] for gi in range(4)])
        small["pool_scale"][l] = dscale[0]
        small["norm_ff_g"][l], small["norm_ple_g"][l] = dgff[0], dgple[0]
        layer_small = {k: small[k][l] for k in layer_keys}
        layer_small["final_g"] = dfinal[0] if l == DEPTH - 1 else jnp.zeros_like(dfinal[0])
        sbuf, small_layout = _pack_small(layer_small, layer_keys + ("final_g",))
        sends = [_grad_send("w_in", dwin), _grad_send("w_out", dwout)]
        sems, lands, token = _exchange_start([sbuf] + sends, [(i, None) for i in range(3)],
                                             ["block", _SCATTER_MODE["w_in"], _SCATTER_MODE["w_out"]], [[0], [1, 2]],
                                             f"grads_exchange_start_{l}_0")
        small_ex[l] = (sems[0], sbuf, lands[0])
        ex[(l, 0)] = (sems[1], sends, lands[1:3])
    grad_x = dx.reshape(1, t, D)

    state = dict(w_in=(swap(w_in), swap(m_w_in), swap(v_w_in)), w_out=(w_out, m_w_out, v_w_out), w_ff1=(w_ff1, m_w_ff1, v_w_ff1),
                 w_ff2=(w_ff2, m_w_ff2, v_w_ff2), w_ple_gate=(w_ple_gate, m_w_ple_gate, v_w_ple_gate),
                 w_ple_proj=(w_ple_proj, m_w_ple_proj, v_w_ple_proj))
    res = {}

    def finish_group(which, after):
        names = (_GROUP_A, _GROUP_B)[which]
        n = len(names)
        sems = [ex[(l, which)][0] for l in range(DEPTH)]
        sends = [s_ for l in range(DEPTH) for s_ in ex[(l, which)][1]]
        lands = [a_ for l in range(DEPTH) for a_ in ex[(l, which)][2]]
        landed = _exchange_wait(sems, sends, [(i, None) for i in range(DEPTH * n)], lands, [_SCATTER_MODE[k] for k in names] * DEPTH,
                                [list(range(l * n, (l + 1) * n)) for l in range(DEPTH)], after, f"grads_exchange_wait_{which}")
        for i, k in enumerate(names):
            res[k] = _reduce_adamw_call([landed[l * n + i] for l in range(DEPTH)], *state[k], "reduce_adamw_" + k)

    finish_group(1, dx)

    landed = _exchange_wait([small_ex[l][0] for l in range(DEPTH)], [small_ex[l][1] for l in range(DEPTH)],
                            [(l, None) for l in range(DEPTH)], [small_ex[l][2] for l in range(DEPTH)], ["block"] * DEPTH,
                            [[l] for l in range(DEPTH)], res[_GROUP_B[-1]][0], "small_grads_wait")
    sums = [_unpack_small(b_, small_layout) for b_ in _slot_sum_call(landed)]
    gs = {k: jnp.stack([sums[l][k] for l in range(DEPTH)]) for k in layer_keys}
    gs["final_g"] = sums[DEPTH - 1]["final_g"]
    conv_cols = lambda a: lax.dynamic_slice_in_dim(a, me * (D_B // N_DEV), D_B // N_DEV, axis=2)
    pad_conv = lambda a: jnp.zeros((DEPTH, 3, D_B), F32).at[:, :, 0:D_B // N_DEV].set(a)
    small_w = dict(norm_mix_g=norm_mix_g, sgu_w=sgu_w, sgu_b=sgu_b, sgu_ln_g=sgu_ln_g, sgu_ln_b=sgu_ln_b, conv_w=pad_conv(conv_w),
                   pool_w=pool_w, pool_scale=pool_scale, norm_ff_g=norm_ff_g, norm_ple_g=norm_ple_g, final_g=final_g)
    small_m = dict(norm_mix_g=m_norm_mix_g, sgu_w=m_sgu_w, sgu_b=m_sgu_b, sgu_ln_g=m_sgu_ln_g, sgu_ln_b=m_sgu_ln_b,
                   conv_w=pad_conv(m_conv_w), pool_w=m_pool_w, pool_scale=m_pool_scale, norm_ff_g=m_norm_ff_g,
                   norm_ple_g=m_norm_ple_g, final_g=m_final_g)
    small_v = dict(norm_mix_g=v_norm_mix_g, sgu_w=v_sgu_w, sgu_b=v_sgu_b, sgu_ln_g=v_sgu_ln_g, sgu_ln_b=v_sgu_ln_b,
                   conv_w=pad_conv(v_conv_w), pool_w=v_pool_w, pool_scale=v_pool_scale,
                   norm_ff_g=v_norm_ff_g, norm_ple_g=v_norm_ple_g, final_g=v_final_g)
    gs_local = dict(gs)
    gs_local["conv_w"] = pad_conv(conv_cols(gs["conv_w"]))
    g_loc, layout = _pack_small(gs_local)
    wbuf, _ = _pack_small(small_w)
    mbuf, _ = _pack_small(small_m)
    vbuf, _ = _pack_small(small_v)
    dbuf, nmbuf, nvbuf = _small_adamw_call(wbuf, g_loc, mbuf, vbuf)
    sd, sm, sv = _unpack_small(dbuf, layout), _unpack_small(nmbuf, layout), _unpack_small(nvbuf, layout)
    unconv = lambda a: a[:, :, 0:D_B // N_DEV]
    for dct in (gs_local, sd, sm, sv):
        dct["conv_w"] = unconv(dct["conv_w"])

    finish_group(0, [res[_GROUP_B[-1]][0], dbuf])

    order = ["norm_mix_g", "w_in", "sgu_w", "sgu_b", "sgu_ln_g", "sgu_ln_b", "conv_w", "pool_w", "pool_scale", "w_out",
             "norm_ff_g", "w_ff1", "w_ff2", "norm_ple_g", "w_ple_gate", "w_ple_proj", "final_g"]
    outs = [loss, grad_x]
    for which in range(4):
        for k in order:
            if k in res:
                outs.append(swap(res[k][which]) if k == "w_in" else res[k][which])
            else:
                outs.append((gs_local, sd, sm, sv)[which][k])
    return tuple(outs)
```

```python
import functools
import math

import jax
import jax.numpy as jnp
from jax import lax
from jax.experimental import pallas as pl
from jax.experimental.pallas import tpu as pltpu

F32 = jnp.float32
BF16 = jnp.bfloat16

D = 1024
D_IN = 2176
D_A = 384
D_B = 384
D_C = 256
D_FF = 4096
D_PLE = 256
DEPTH = 4
CHUNK = 128
HALO = 16
FF_BLK = 1024
N_DEV = 8
RMS_EPS = 1e-6
LN_EPS = 1e-5
ADAM_LR = 0.001
ADAM_B1 = 0.9
ADAM_B2 = 0.999
ADAM_EPS = 1e-08
ADAM_WD = 0.01
ADAM_STEP = 10

TM_MIX_FWD = 1024
TM_FFN_FWD = 512
TM_PLE_BWD = 1024
TM_FFN_BWD = 512
TM_MIX_BWD = 512
RUN_MIX_FWD = 1024
RUN_MIX_BWD = 512
V7X_VMEM_LIMIT = 60000 * 1024

ANY = pl.BlockSpec(memory_space=pl.ANY)
HBM = pl.BlockSpec(memory_space=pltpu.HBM)
SEM = pl.BlockSpec(memory_space=pltpu.SEMAPHORE)
MESH = pl.DeviceIdType.MESH


def _params(vmem=V7X_VMEM_LIMIT):
    return pltpu.CompilerParams(dimension_semantics=("arbitrary",), vmem_limit_bytes=vmem)


def _in_hbm(a):
    return pltpu.with_memory_space_constraint(a, pltpu.HBM)


def _full(shape):
    nd = len(shape)
    return pl.BlockSpec(shape, lambda i: (0,) * nd)


def _rows(tm, cols):
    return pl.BlockSpec((tm, cols), lambda i: (i, 0))


def _layer_rows(layer, tm, cols):
    return pl.BlockSpec((None, tm, cols), lambda i: (layer, i, 0))


def _mm(a, b):
    return jnp.dot(a, b, preferred_element_type=F32)


def _mm_nt(a, b):
    return lax.dot_general(a, b, (((1,), (1,)), ((), ())), preferred_element_type=F32)


def _mm_tn(a, b):
    return lax.dot_general(a, b, (((0,), (0,)), ((), ())), preferred_element_type=F32)


def _gelu_and_grad(x):
    ax = jnp.abs(x) * (1.0 / math.sqrt(2.0))
    t = 1.0 / (1.0 + 0.3275911 * ax)
    poly = t * (0.254829592 + t * (-0.284496736 + t * (1.421413741 + t * (-1.453152027 + t * 1.061405429))))
    e = jnp.exp(-0.5 * x * x)
    half = 0.5 * poly * e
    cdf = jnp.where(x < 0, half, 1.0 - half)
    return x * cdf, cdf + x * (e * (1.0 / math.sqrt(2.0 * math.pi)))


def _rms(x, g):
    rstd = lax.rsqrt(jnp.mean(x * x, axis=-1, keepdims=True) + RMS_EPS)
    xhat = x * rstd
    return xhat * g, xhat, rstd


RSTD_LANES = 128


def _save_norm(xhat, rstd, xhat_ref, rstd_ref):
    xhat_ref[...] = xhat.astype(BF16)
    rstd_ref[...] = jnp.broadcast_to(rstd, rstd_ref.shape)


def _saved_norm(xhat_ref, rstd_ref, g):
    xhat = xhat_ref[...].astype(F32)
    return xhat * g, xhat, rstd_ref[:, 0:1]


def _rms_bwd(dy, g, xhat, rstd):
    dg = jnp.sum(dy * xhat, axis=0, keepdims=True)
    dxh = dy * g
    dx = rstd * (dxh - xhat * jnp.mean(dxh * xhat, axis=-1, keepdims=True))
    return dx, dg


def _shift_down(ext, k):
    return pltpu.roll(ext, k, 0)[HALO:, :]


def _shift_up(ext, k):
    n = ext.shape[0]
    return pltpu.roll(ext, n - k, 0)[: n - HALO, :]


def _pool_select(s2, s4, s8, s16):
    lane = lax.broadcasted_iota(jnp.int32, s2.shape, 1)
    return jnp.where(lane < 64, s2, jnp.where(lane < 128, s4, jnp.where(lane < 192, s8, s16)))


def _pool_inv_count(tile_start, tm):
    pos = lax.broadcasted_iota(jnp.int32, (tm, D_C), 0) + tile_start + 1
    lane = lax.broadcasted_iota(jnp.int32, (tm, D_C), 1)
    win = jnp.where(lane < 64, 2, jnp.where(lane < 128, 4, jnp.where(lane < 192, 8, 16)))
    return 1.0 / jnp.minimum(pos, win).astype(F32)


def _head_halves(a):
    lane = lax.broadcasted_iota(jnp.int32, a.shape, 1)
    even = (lane & 64) == 0
    return jnp.where(even, a, 0.0).astype(BF16), jnp.where(even, 0.0, a).astype(BF16)


def _head_stack(lo, hi, j, nch):
    return jnp.concatenate(
        [jnp.concatenate([lo[c * CHUNK:(c + 1) * CHUNK, j * 128:(j + 1) * 128], hi[c * CHUNK:(c + 1) * CHUNK, j * 128:(j + 1) * 128]], axis=0)
         for c in range(nch)], axis=1)


def _chunks_to_lanes(a, j, nch):
    return jnp.concatenate([a[c * CHUNK:(c + 1) * CHUNK, j * 128:(j + 1) * 128] for c in range(nch)], axis=1)


def _lanes_to_chunks(o, nch):
    return jnp.concatenate([o[:, c * CHUNK:(c + 1) * CHUNK] for c in range(nch)], axis=0)


def _loads(pairs, sem):
    return [pltpu.make_async_copy(src, dst, sem.at[n]) for n, (src, dst) in enumerate(pairs)]


def _load_all(loads, zero=()):
    for cp in loads:
        cp.start()
    for ref in zero:
        ref[...] = jnp.zeros_like(ref)
    for cp in loads:
        cp.wait()


def _write_out(triples, sem):
    copies = []
    for n, (acc, stage, out) in enumerate(triples):
        _stage_bf16(acc, stage)
        cp = pltpu.make_async_copy(stage, out, sem.at[n])
        cp.start()
        copies.append(cp)
    for cp in copies:
        cp.wait()


def _halves(ref):
    half = ref.shape[0] // 2
    return ref.at[pl.ds(0, half)], ref.at[pl.ds(half, half)]


def _stage_bf16(acc, stage):
    rows = acc.shape[0]
    strip = min(rows, 128)

    @pl.loop(0, rows // strip)
    def _(n):
        sl = pl.ds(pl.multiple_of(n * strip, strip), strip)
        stage[sl, :] = acc[sl, :].astype(BF16)


N_SAVED = 5


def _mixers_fwd(pf, halo_hc, halo_zc, tile_start, prm, saved=None):
    tm = pf.shape[0]
    nch = tm // CHUNK
    u, v = pf[:, 0:D_A], pf[:, D_A:2 * D_A]
    zb, gb, gc = pf[:, 768:1152], pf[:, 1152:1536], pf[:, 1536:1920]
    zc = pf[:, 1920:2176]
    r = {}
    if saved is None:
        gu, r["dgelu_u"] = _gelu_and_grad(u)
        gv, r["dgelu_v"] = _gelu_and_grad(v)
        pmat = prm["pmat"][...]
        mu = _mm(gv.astype(BF16), pmat)
        dv = gv - mu
        var = _mm((dv * dv).astype(BF16), pmat)
        rstd = lax.rsqrt(var + LN_EPS)
        xh = dv * rstd
        r["saved"] = jnp.concatenate([gu, r["dgelu_u"], r["dgelu_v"], xh, rstd], axis=1).astype(BF16)
    else:
        gu, r["dgelu_u"], r["dgelu_v"], xh, rstd = (saved[:, n * D_A:(n + 1) * D_A] for n in range(N_SAVED))
    vlo, vhi = _head_halves(xh * prm["ln_g"][...] + prm["ln_b"][...])
    cols, v2s = [], []
    for j in range(3):
        v2 = _head_stack(vlo, vhi, j, nch)
        v2s.append(v2)
        cols.append(_lanes_to_chunks(_mm(prm["wcat"][j], v2), nch))
    mixed = jnp.concatenate(cols, axis=1) + jnp.concatenate([prm["bmat"][...]] * nch, axis=0)
    ya = gu * mixed
    r.update(gu=gu, mixed=mixed, v2s=v2s, xh=xh, ln_rstd=rstd)
    w0, w1, w2 = prm["conv_w"][0:1, :], prm["conv_w"][1:2, :], prm["conv_w"][2:3, :]
    hc = gc * zb
    ext = jnp.concatenate([halo_hc, hc], axis=0)
    h1, h2 = _shift_down(ext, 1), _shift_down(ext, 2)
    yc = w2 * hc + w1 * h1 + w0 * h2
    yb = gb * yc
    r.update(hc=hc, h1=h1, h2=h2, yc=yc, zb=zb, gb=gb, gc=gc, w0=w0, w1=w1, w2=w2)
    ext = jnp.concatenate([halo_zc, zc], axis=0)
    s2 = ext + pltpu.roll(ext, 1, 0)
    s4 = s2 + pltpu.roll(s2, 2, 0)
    s8 = s4 + pltpu.roll(s4, 4, 0)
    s16 = s8 + pltpu.roll(s8, 8, 0)
    inv = _pool_inv_count(tile_start, tm)
    pooled = _pool_select(s2, s4, s8, s16)[HALO:, :] * inv - zc
    pooledb = pooled.astype(BF16)
    pm = _mm(pooledb, prm["bd"][...])
    scale = prm["pool_scale"][...]
    ycm = pm * scale
    r.update(inv=inv, pooledb=pooledb, pm=pm, scale=scale, zc=zc)
    r["ycat"] = jnp.concatenate([ya, yb, ycm], axis=1)
    return r


_MIX_PARAM_NAMES = ("pmat", "ln_g", "ln_b", "wcat", "bmat", "conv_w", "bd", "pool_scale")


def _mix_param_specs():
    return [_full((D_A, D_A)), _full((1, D_A)), _full((1, D_A)), _full((3, CHUNK, 2 * CHUNK)), _full((CHUNK, D_A)),
            _full((3, D_B)), _full((D_C, D_C)), _full((1, D_C))]


def _mix_fwd_call(x, g_mix, w_in_t, w_out, mp, tm, run, layer):
    t = x.shape[0]
    nt = t // tm

    def body(x_ref, g_ref, pmat, ln_g, ln_b, wcat, bmat, conv_w, bd, pool_scale, win_hbm, wout_hbm,
             x1_ref, proj_ref, saved_ref, xhat_ref, rstd_ref, win_s, wout_s, halo_hc, halo_zc, load_sem):
        i = pl.program_id(0)
        loads = _loads([(win_hbm, win_s), (wout_hbm, wout_s)], load_sem)

        @pl.when(i == 0)
        def _():
            _load_all(loads, (halo_hc, halo_zc))

        prm = dict(pmat=pmat, ln_g=ln_g, ln_b=ln_b, wcat=wcat, bmat=bmat, conv_w=conv_w, bd=bd, pool_scale=pool_scale)
        xv = x_ref[...]
        h, xhat, rstd = _rms(xv, g_ref[...])
        _save_norm(xhat, rstd, xhat_ref, rstd_ref)
        pf = _mm_nt(h.astype(BF16), win_s[...])
        proj_ref[...] = pf.astype(BF16)
        hh, hz = halo_hc[...], halo_zc[...]
        parts = []
        for c in range(tm // run):
            r = _mixers_fwd(pf[c * run:(c + 1) * run, :], hh, hz, i * tm + c * run, prm)
            hh, hz = r["hc"][run - HALO:, :], r["zc"][run - HALO:, :]
            parts.append(r["ycat"].astype(BF16))
            saved_ref[c * run:(c + 1) * run, :] = r["saved"]
        halo_hc[...] = hh
        halo_zc[...] = hz
        x1_ref[...] = xv + _mm(jnp.concatenate(parts, axis=0), wout_s[...])

    return pl.pallas_call(
        body, name=f"mix_fwd_{layer}", grid=(nt,),
        in_specs=[_rows(tm, D), _full((1, D))] + _mix_param_specs() + [HBM, HBM],
        out_specs=[_rows(tm, D), _rows(tm, D_IN), _rows(tm, N_SAVED * D_A), _rows(tm, D), _rows(tm, RSTD_LANES)],
        out_shape=[jax.ShapeDtypeStruct((t, D), F32), jax.ShapeDtypeStruct((t, D_IN), BF16),
                   jax.ShapeDtypeStruct((t, N_SAVED * D_A), BF16), jax.ShapeDtypeStruct((t, D), BF16),
                   jax.ShapeDtypeStruct((t, RSTD_LANES), F32)],
        scratch_shapes=[pltpu.VMEM((D_IN, D), BF16), pltpu.VMEM((D, D), BF16),
                        pltpu.VMEM((HALO, D_B), F32), pltpu.VMEM((HALO, D_C), F32), pltpu.SemaphoreType.DMA((2,))],
        compiler_params=_params(),
    )(x, g_mix, *[mp[k] for k in _MIX_PARAM_NAMES], _in_hbm(w_in_t), _in_hbm(w_out))


def _ffn_fwd_call(x1, p, g_ff, g_ple, w1, w2, wg, wp, tm, layer, head=None):
    t = x1.shape[0]
    nt = t // tm
    nh = 0 if head is None else 2

    def body(*refs):
        x1_ref, p_ref, gff_ref, gple_ref = refs[0:4]
        t_ref, fg_ref = refs[4:4 + nh] if nh else (None, None)
        w1_hbm, w2_hbm, wg_hbm, wp_hbm = refs[4 + nh:8 + nh]
        x3_ref, r_ref, gate_ref, q_ref, xh1_ref, rs1_ref, xh2_ref, rs2_ref = refs[8 + nh:16 + nh]
        sq_ref, dfg_ref = refs[16 + nh:16 + 2 * nh] if nh else (None, None)
        w1_s, w2_s, wg_s, wp_s, load_sem = refs[16 + 2 * nh:]
        i = pl.program_id(0)
        loads = _loads([(w1_hbm, w1_s), (w2_hbm, w2_s), (wg_hbm, wg_s), (wp_hbm, wp_s)], load_sem)

        @pl.when(i == 0)
        def _():
            _load_all(loads, (sq_ref, dfg_ref) if nh else ())

        x1v = x1_ref[...]
        h2, xhat, rstd = _rms(x1v, gff_ref[...])
        _save_norm(xhat, rstd, xh1_ref, rs1_ref)
        h2b = h2.astype(BF16)
        acc = x1v
        for j in range(D_FF // FF_BLK):
            blk = slice(j * FF_BLK, (j + 1) * FF_BLK)
            rj = jnp.maximum(_mm(h2b, w1_s[:, blk]), 0.0)
            r_ref[:, blk] = rj.astype(BF16)
            acc = acc + _mm((rj * rj).astype(BF16), w2_s[blk, :])
        n3, xhat, rstd = _rms(acc, gple_ref[...])
        _save_norm(xhat, rstd, xh2_ref, rs2_ref)
        gate = jax.nn.sigmoid(_mm(n3.astype(BF16), wg_s[...]))
        gate_ref[...] = gate.astype(BF16)
        pp = _mm(p_ref[...].astype(BF16), wp_s[...])
        q_ref[...] = (pp * gate * (1.0 - gate)).astype(BF16)
        x3 = acc + pp * gate
        if nh:
            gf = fg_ref[...]
            y, xhat, rstd = _rms(x3, gf)
            err = y - t_ref[...]
            sq_ref[...] += jnp.sum(err * err, axis=0, keepdims=True)
            x3, dg = _rms_bwd(err * (1.0 / D), gf, xhat, rstd)
            dfg_ref[...] += dg
        x3_ref[...] = x3

    vec = jax.ShapeDtypeStruct((1, D), F32)
    return pl.pallas_call(
        body, name=f"ffn_fwd_{layer}", grid=(nt,),
        in_specs=[_rows(tm, D), _layer_rows(layer, tm, D_PLE), _full((1, D)), _full((1, D))]
                 + ([_rows(tm, D), _full((1, D))] if nh else []) + [HBM, HBM, HBM, HBM],
        out_specs=[_rows(tm, D), _rows(tm, D_FF), _rows(tm, D), _rows(tm, D), _rows(tm, D), _rows(tm, RSTD_LANES), _rows(tm, D),
                   _rows(tm, RSTD_LANES)] + ([_full((1, D)), _full((1, D))] if nh else []),
        out_shape=[jax.ShapeDtypeStruct((t, D), F32), jax.ShapeDtypeStruct((t, D_FF), BF16), jax.ShapeDtypeStruct((t, D), BF16),
                   jax.ShapeDtypeStruct((t, D), BF16), jax.ShapeDtypeStruct((t, D), BF16), jax.ShapeDtypeStruct((t, RSTD_LANES), F32),
                   jax.ShapeDtypeStruct((t, D), BF16), jax.ShapeDtypeStruct((t, RSTD_LANES), F32)] + ([vec, vec] if nh else []),
        scratch_shapes=[pltpu.VMEM((D, D_FF), BF16), pltpu.VMEM((D_FF, D), BF16),
                        pltpu.VMEM((D, D), BF16), pltpu.VMEM((D_PLE, D), BF16), pltpu.SemaphoreType.DMA((4,))],
        compiler_params=_params(),
    )(x1, p, g_ff, g_ple, *(head or ()), _in_hbm(w1), _in_hbm(w2), _in_hbm(wg), _in_hbm(wp))


def _ple_bwd_call(dx3, xhat2, rstd2, gate, q, p, g_ple, wg, after, tm, layer):
    t = dx3.shape[0]
    nt = t // tm

    def body(dx3_ref, xhat_ref, rstd_ref, gate_ref, q_ref, p_ref, g_ref, wg_hbm, after_ref,
             dx2_ref, dg_ref, dwg_hbm, dwp_hbm, wg_s, wp_s, dwg_acc, dwp_acc, load_sem):
        i = pl.program_id(0)
        loads = _loads([(wg_hbm, wg_s)], load_sem)

        @pl.when(i == 0)
        def _():
            _load_all(loads, (dwg_acc, dwp_acc, dg_ref))

        g = g_ref[...]
        dx3v = dx3_ref[...]
        dwp_acc[...] += _mm_tn(p_ref[...].astype(BF16), (dx3v * gate_ref[...].astype(F32)).astype(BF16))
        dgpre = (dx3v * q_ref[...].astype(F32)).astype(BF16)
        n3, xhat, rstd = _saved_norm(xhat_ref, rstd_ref, g)
        dwg_acc[...] += _mm_tn(n3.astype(BF16), dgpre)
        dn3 = _mm_nt(dgpre, wg_s[...])
        dx, dg = _rms_bwd(dn3, g, xhat, rstd)
        dx2_ref[...] = dx3v + dx
        dg_ref[...] += dg

        @pl.when(i == nt - 1)
        def _():
            _write_out([(dwg_acc, wg_s, dwg_hbm), (dwp_acc, wp_s, dwp_hbm)], load_sem)

    return pl.pallas_call(
        body, name=f"ple_bwd_{layer}", grid=(nt,),
        in_specs=[_rows(tm, D), _rows(tm, D), _rows(tm, RSTD_LANES), _rows(tm, D), _rows(tm, D), _layer_rows(layer, tm, D_PLE),
                  _full((1, D)), HBM, ANY],
        out_specs=[_rows(tm, D), _full((1, D)), HBM, HBM],
        out_shape=[jax.ShapeDtypeStruct((t, D), F32), jax.ShapeDtypeStruct((1, D), F32),
                   pltpu.HBM((D, D), BF16), pltpu.HBM((D_PLE, D), BF16)],
        scratch_shapes=[pltpu.VMEM((D, D), BF16), pltpu.VMEM((D_PLE, D), BF16),
                        pltpu.VMEM((D, D), F32), pltpu.VMEM((D_PLE, D), F32), pltpu.SemaphoreType.DMA((2,))],
        compiler_params=_params(),
    )(dx3, xhat2, rstd2, gate, q, p, g_ple, _in_hbm(wg), after)


def _ffn_bwd_hidden_call(dx2, r, w2, tm, layer):
    t = dx2.shape[0]
    nt = t // tm

    def body(dx2_ref, r_ref, w2_hbm, da_ref, dw2_hbm, w2_s, dw2_acc, load_sem):
        i = pl.program_id(0)
        loads = _loads([(w2_hbm, w2_s)], load_sem)

        @pl.when(i == 0)
        def _():
            _load_all(loads, (dw2_acc,))

        dxb = dx2_ref[...].astype(BF16)
        for j in range(D_FF // FF_BLK):
            blk = slice(j * FF_BLK, (j + 1) * FF_BLK)
            rj = r_ref[:, blk].astype(F32)
            ds = _mm_nt(dxb, w2_s[blk, :])
            da_ref[:, blk] = (2.0 * rj * ds).astype(BF16)
            dw2_acc[blk, :] += _mm_tn((rj * rj).astype(BF16), dxb)

        @pl.when(i == nt - 1)
        def _():
            _write_out(list(zip(_halves(dw2_acc), _halves(w2_s), _halves(dw2_hbm))), load_sem)

    return pl.pallas_call(
        body, name=f"ffn_bwd_hidden_{layer}", grid=(nt,),
        in_specs=[_rows(tm, D), _rows(tm, D_FF), HBM],
        out_specs=[_rows(tm, D_FF), HBM],
        out_shape=[jax.ShapeDtypeStruct((t, D_FF), BF16), pltpu.HBM((D_FF, D), BF16)],
        scratch_shapes=[pltpu.VMEM((D_FF, D), BF16), pltpu.VMEM((D_FF, D), F32), pltpu.SemaphoreType.DMA((2,))],
        compiler_params=_params(),
    )(dx2, r, _in_hbm(w2))


def _ffn_bwd_input_call(da, xhat1, rstd1, dx2, g_ff, w1, tm, layer):
    t = dx2.shape[0]
    nt = t // tm

    def body(da_ref, xhat_ref, rstd_ref, dx2_ref, g_ref, w1_hbm, dx1_ref, dg_ref, dw1_hbm, w1_s, dw1_acc, load_sem):
        i = pl.program_id(0)
        loads = _loads([(w1_hbm, w1_s)], load_sem)

        @pl.when(i == 0)
        def _():
            _load_all(loads, (dw1_acc, dg_ref))

        g = g_ref[...]
        h2, xhat, rstd = _saved_norm(xhat_ref, rstd_ref, g)
        h2b = h2.astype(BF16)
        dh2 = jnp.zeros((tm, D), F32)
        for j in range(D_FF // FF_BLK):
            blk = slice(j * FF_BLK, (j + 1) * FF_BLK)
            daj = da_ref[:, blk]
            dh2 = dh2 + _mm_nt(daj, w1_s[:, blk])
            dw1_acc[:, blk] += _mm_tn(h2b, daj)
        dx, dg = _rms_bwd(dh2, g, xhat, rstd)
        dx1_ref[...] = dx2_ref[...] + dx
        dg_ref[...] += dg

        @pl.when(i == nt - 1)
        def _():
            _write_out(list(zip(_halves(dw1_acc), _halves(w1_s), _halves(dw1_hbm))), load_sem)

    return pl.pallas_call(
        body, name=f"ffn_bwd_input_{layer}", grid=(nt,),
        in_specs=[_rows(tm, D_FF), _rows(tm, D), _rows(tm, RSTD_LANES), _rows(tm, D), _full((1, D)), HBM],
        out_specs=[_rows(tm, D), _full((1, D)), HBM],
        out_shape=[jax.ShapeDtypeStruct((t, D), F32), jax.ShapeDtypeStruct((1, D), F32), pltpu.HBM((D, D_FF), BF16)],
        scratch_shapes=[pltpu.VMEM((D, D_FF), BF16), pltpu.VMEM((D, D_FF), F32), pltpu.SemaphoreType.DMA((2,))],
        compiler_params=_params(),
    )(da, xhat1, rstd1, dx2, g_ff, _in_hbm(w1))


def _mix_bwd_call(dx1, xhat0, rstd0, proj, saved, g_mix, w_in_t, w_out, mp, wtcat, trilcat, headsel, after, tm, run, layer):
    t = dx1.shape[0]
    nt = t // tm
    nrun = tm // run
    nch = run // CHUNK
    hb = tm // HALO

    def rev(i):
        return nt - 1 - i

    def body(dx1_ref, xhat_ref, rstd_ref, proj_ref, halo_ref, saved_ref, g_ref, pmat, ln_g, ln_b, wcat, bmat, conv_w, bd, pool_scale,
             wtcat_ref, tril_ref, sel_ref, win_hbm, wout_hbm, after_ref,
             dx_ref, dg_ref, dwcat_ref, dsb_ref, dlng_ref, dlnb_ref, dconv_ref, dbd_ref, dscale_ref, dwin_hbm, dwout_hbm,
             win_s, wout_s, dwin_acc, dwout_acc, dbm_acc, carry_yc, carry_q, load_sem):
        i = pl.program_id(0)
        ri = nt - 1 - i
        loads = _loads([(wout_hbm, wout_s), (win_hbm, win_s)], load_sem)

        @pl.when(i == 0)
        def _():
            _load_all(loads, (dwin_acc, dwout_acc, dbm_acc, carry_yc, carry_q, dg_ref, dwcat_ref, dlng_ref, dlnb_ref,
                              dconv_ref, dbd_ref, dscale_ref))

        prm = dict(pmat=pmat, ln_g=ln_g, ln_b=ln_b, wcat=wcat, bmat=bmat, conv_w=conv_w, bd=bd, pool_scale=pool_scale)
        g = g_ref[...]
        h, xhat, rstd = _saved_norm(xhat_ref, rstd_ref, g)
        hb16 = h.astype(BF16)
        dx1v = dx1_ref[...]
        dx1b = dx1v.astype(BF16)
        dycat = _mm_nt(dx1b, wout_s[...])
        lng = ln_g[...]
        pm_ = pmat[...]
        cy, cq = carry_yc[...], carry_q[...]
        ycat_parts, dproj_parts = [None] * nrun, [None] * nrun
        dbm = dlng = dlnb = dscale = dcv0 = dcv1 = dcv2 = None
        add = lambda tot, v: v if tot is None else tot + v
        for c in reversed(range(nrun)):
            rows = slice(c * run, (c + 1) * run)
            pf = proj_ref[rows, :].astype(F32)
            if c > 0:
                ph = proj_ref[c * run - HALO:c * run, :].astype(F32)
            else:
                ph = halo_ref[...].astype(F32) * (ri > 0).astype(F32)
            r = _mixers_fwd(pf, ph[:, 1536:1920] * ph[:, 768:1152], ph[:, 1920:2176], ri * tm + c * run, prm,
                            saved_ref[rows, :].astype(F32))
            ycat_parts[c] = r["ycat"].astype(BF16)
            dya, dyb, dyc = dycat[rows, 0:D_A], dycat[rows, D_A:D_A + D_B], dycat[rows, D_A + D_B:D]

            dgu = dya * r["mixed"]
            dmix = dya * r["gu"]
            dmix_b = dmix.astype(BF16)
            dlo, dhi = _head_halves(dmix)
            for k in range(nch):
                dbm = add(dbm, dmix[k * CHUNK:(k + 1) * CHUNK, :])
            dvn_cols = []
            for j in range(3):
                dwcat_ref[j] += _mm_nt(_chunks_to_lanes(dmix_b, j, nch), r["v2s"][j])
                dvn_cols.append(_lanes_to_chunks(_mm(wtcat_ref[j], _head_stack(dlo, dhi, j, nch)), nch))
            dvn = jnp.concatenate(dvn_cols, axis=1)
            xh = r["xh"]
            dlng = add(dlng, jnp.sum(dvn * xh, axis=0, keepdims=True))
            dlnb = add(dlnb, jnp.sum(dvn, axis=0, keepdims=True))
            dxh = dvn * lng
            m1 = _mm(dxh.astype(BF16), pm_)
            m2 = _mm((dxh * xh).astype(BF16), pm_)
            dgv = r["ln_rstd"] * (dxh - m1 - xh * m2)
            du = dgu * r["dgelu_u"]
            dv = dgv * r["dgelu_v"]

            dgb = dyb * r["yc"]
            dyc2 = dyb * r["gb"]
            dcv0 = add(dcv0, jnp.sum(dyc2 * r["h2"], axis=0, keepdims=True))
            dcv1 = add(dcv1, jnp.sum(dyc2 * r["h1"], axis=0, keepdims=True))
            dcv2 = add(dcv2, jnp.sum(dyc2 * r["hc"], axis=0, keepdims=True))
            ext = jnp.concatenate([dyc2, cy], axis=0)
            dhc = r["w2"] * dyc2 + r["w1"] * _shift_up(ext, 1) + r["w0"] * _shift_up(ext, 2)
            cy = dyc2[0:HALO, :]
            dgc = dhc * r["zb"]
            dzb = dhc * r["gc"]

            dscale = add(dscale, jnp.sum(dyc * r["pm"], axis=0, keepdims=True))
            dpm = (dyc * r["scale"]).astype(BF16)
            dbd_ref[...] += _mm_tn(r["pooledb"], dpm)
            dpooled = _mm_nt(dpm, bd[...])
            q = dpooled * r["inv"]
            ext = jnp.concatenate([q, cq], axis=0)
            n = run + HALO
            r2 = ext + pltpu.roll(ext, n - 1, 0)
            r4 = r2 + pltpu.roll(r2, n - 2, 0)
            r8 = r4 + pltpu.roll(r4, n - 4, 0)
            r16 = r8 + pltpu.roll(r8, n - 8, 0)
            dzc = _pool_select(r2, r4, r8, r16)[0:run, :] - dpooled
            cq = q[0:HALO, :]
            dproj_parts[c] = jnp.concatenate([du, dv, dzb, dgb, dgc, dzc], axis=1).astype(BF16)

        carry_yc[...] = cy
        carry_q[...] = cq
        dbm_acc[...] += dbm
        dlng_ref[...] += dlng
        dlnb_ref[...] += dlnb
        dscale_ref[...] += dscale
        dconv_ref[0:1, :] += dcv0
        dconv_ref[1:2, :] += dcv1
        dconv_ref[2:3, :] += dcv2
        dwout_acc[...] += _mm_tn(jnp.concatenate(ycat_parts, axis=0), dx1b)
        dproj = jnp.concatenate(dproj_parts, axis=0)
        dwin_acc[...] += _mm_tn(dproj, hb16)
        dh = _mm(dproj, win_s[...])
        dx, dg = _rms_bwd(dh, g, xhat, rstd)
        dx_ref[...] = dx1v + dx
        dg_ref[...] += dg

        @pl.when(i == nt - 1)
        def _():
            _write_out([(dwin_acc, win_s, dwin_hbm), (dwout_acc, wout_s, dwout_hbm)], load_sem)
            for j in range(3):
                dwcat_ref[j] = dwcat_ref[j] * tril_ref[...]
            acc = dbm_acc[...]
            hi = acc.astype(BF16)
            lo = (acc - hi.astype(F32)).astype(BF16)
            dsb_ref[...] = _mm(hi, sel_ref[...]) + _mm(lo, sel_ref[...])

    return pl.pallas_call(
        body, name=f"mix_bwd_{layer}", grid=(nt,),
        in_specs=[pl.BlockSpec((tm, D), lambda i: (rev(i), 0)), pl.BlockSpec((tm, D), lambda i: (rev(i), 0)),
                  pl.BlockSpec((tm, RSTD_LANES), lambda i: (rev(i), 0)), pl.BlockSpec((tm, D_IN), lambda i: (rev(i), 0)),
                  pl.BlockSpec((HALO, D_IN), lambda i: (jnp.maximum(rev(i) * hb - 1, 0), 0)),
                  pl.BlockSpec((tm, N_SAVED * D_A), lambda i: (rev(i), 0)), _full((1, D))] + _mix_param_specs()
                 + [_full((3, CHUNK, 2 * CHUNK)), _full((CHUNK, 2 * CHUNK)), _full((D_A, CHUNK)), HBM, HBM, ANY],
        out_specs=[pl.BlockSpec((tm, D), lambda i: (rev(i), 0)), _full((1, D)), _full((3, CHUNK, 2 * CHUNK)),
                   _full((CHUNK, CHUNK)), _full((1, D_A)), _full((1, D_A)), _full((3, D_B)), _full((D_C, D_C)),
                   _full((1, D_C)), HBM, HBM],
        out_shape=[jax.ShapeDtypeStruct((t, D), F32), jax.ShapeDtypeStruct((1, D), F32),
                   jax.ShapeDtypeStruct((3, CHUNK, 2 * CHUNK), F32), jax.ShapeDtypeStruct((CHUNK, CHUNK), F32),
                   jax.ShapeDtypeStruct((1, D_A), F32), jax.ShapeDtypeStruct((1, D_A), F32),
                   jax.ShapeDtypeStruct((3, D_B), F32), jax.ShapeDtypeStruct((D_C, D_C), F32),
                   jax.ShapeDtypeStruct((1, D_C), F32), pltpu.HBM((D_IN, D), BF16), pltpu.HBM((D, D), BF16)],
        scratch_shapes=[pltpu.VMEM((D_IN, D), BF16), pltpu.VMEM((D, D), BF16),
                        pltpu.VMEM((D_IN, D), F32), pltpu.VMEM((D, D), F32), pltpu.VMEM((CHUNK, D_A), F32),
                        pltpu.VMEM((HALO, D_B), F32), pltpu.VMEM((HALO, D_C), F32), pltpu.SemaphoreType.DMA((2,))],
        compiler_params=_params(),
    )(dx1, xhat0, rstd0, proj, proj, saved, g_mix, *[mp[k] for k in _MIX_PARAM_NAMES], wtcat, trilcat, headsel, _in_hbm(w_in_t), _in_hbm(w_out), after)


def _coords():
    return lax.axis_index("x"), lax.axis_index("y"), lax.axis_index("c")


EFFECT = pltpu.SideEffectType.DATAFLOW_SIDE_EFFECTING


def _peer(k):
    x, y, c = _coords()
    px, py, pc = x ^ (k >> 2), y ^ ((k >> 1) & 1), c ^ (k & 1)
    return (px, py, pc), 4 * px + 2 * py + pc


def _landing_shape(shape, mode):
    if mode == "block":
        return (N_DEV,) + shape
    if mode == "slot":
        return shape
    if mode == "cols_in":
        return (shape[0], N_DEV * shape[1])
    return (N_DEV, shape[0], shape[1] // N_DEV)


def _pieces(src, land, mode, src_idx, land_idx):
    if mode == "block":
        return src, land.at[land_idx]
    if mode == "slot":
        return src.at[src_idx], land.at[land_idx]
    if mode == "cols_in":
        cw = src.shape[1]
        return src, land.at[:, pl.ds(pl.multiple_of(land_idx * cw, 128), cw)]
    cw = land.shape[2]
    return src.at[:, pl.ds(pl.multiple_of(src_idx * cw, 128), cw)], land.at[land_idx]


def _exchange_copy(src, land, mode, send_sem, recv_sem, ai, k, starting):
    x, y, c = _coords()
    peer, pidx = _peer(k)
    s, d = _pieces(src, land, mode, pidx, 4 * x + 2 * y + c if starting else pidx)
    i = ai * (N_DEV - 1) + k - 1
    return pltpu.make_async_remote_copy(src_ref=s, dst_ref=d, send_sem=send_sem.at[i], recv_sem=recv_sem.at[i],
                                        device_id=peer, device_id_type=MESH)


def _own_copy(src, land, mode, local_sem, ai):
    x, y, c = _coords()
    me = 4 * x + 2 * y + c
    s, d = _pieces(src, land, mode, me, me)
    return pltpu.make_async_copy(s, d, local_sem.at[ai])


def _item_src(ins, item):
    a, sub = item
    return ins[a] if sub is None else ins[a].at[sub]


def _exchange_start(srcs, items, modes, groups, name):
    n, ni, ng = len(srcs), len(items), len(groups)
    shapes = [srcs[a].shape if sub is None else srcs[a].shape[1:] for a, sub in items]
    land_shapes = [pltpu.HBM(_landing_shape(sh, m), srcs[a].dtype) for sh, m, (a, _) in zip(shapes, modes, items)]

    def body(*refs):
        ins = refs[:n]
        sems = refs[n:n + 3 * ng]
        land_refs = refs[n + 3 * ng:n + 3 * ng + ni]
        token = refs[-1]
        for g, idxs in enumerate(groups):
            for ai, it in enumerate(idxs):
                src = _item_src(ins, items[it])
                _own_copy(src, land_refs[it], modes[it], sems[3 * g + 2], ai).start()
                for k in range(1, N_DEV):
                    _exchange_copy(src, land_refs[it], modes[it], sems[3 * g], sems[3 * g + 1], ai, k, True).start()
        token[...] = jnp.zeros_like(token)

    sem_shapes = []
    for idxs in groups:
        sem_shapes += [pltpu.SemaphoreType.DMA((len(idxs) * (N_DEV - 1),))] * 2 + [pltpu.SemaphoreType.DMA((len(idxs),))]
    out = pl.pallas_call(
        body, name=name,
        out_shape=tuple(sem_shapes) + tuple(land_shapes) + (jax.ShapeDtypeStruct((8, 128), F32),),
        in_specs=[HBM] * n,
        out_specs=tuple([SEM] * (3 * ng) + [HBM] * ni + [pl.BlockSpec(memory_space=pltpu.VMEM)]),
        compiler_params=pltpu.CompilerParams(has_side_effects=EFFECT),
    )(*[pltpu.with_memory_space_constraint(s, pltpu.HBM) for s in srcs])
    sems = [tuple(out[3 * g:3 * g + 3]) for g in range(ng)]
    return sems, list(out[3 * ng:3 * ng + ni]), out[-1]


def _exchange_wait(sems, srcs, items, lands, modes, groups, after, name):
    n, ni, ng = len(srcs), len(items), len(groups)

    def body(*refs):
        ins, land_refs = refs[:n], refs[n:n + ni]
        sem_refs = refs[n + ni:n + ni + 3 * ng]
        for g, idxs in enumerate(groups):
            for ai, it in enumerate(idxs):
                src = _item_src(ins, items[it])
                _own_copy(src, land_refs[it], modes[it], sem_refs[3 * g + 2], ai).wait()
                for k in range(1, N_DEV):
                    cp = _exchange_copy(src, land_refs[it], modes[it], sem_refs[3 * g], sem_refs[3 * g + 1], ai, k, False)
                    cp.wait_send()
                    cp.wait_recv()

    flat_sems = [s for trio in sems for s in trio]
    afters = list(after) if isinstance(after, (list, tuple)) else [after]
    out = pl.pallas_call(
        body, name=name,
        out_shape=tuple(pltpu.HBM(l.shape, l.dtype) for l in lands),
        in_specs=[HBM] * (n + ni) + [SEM] * (3 * ng) + [ANY] * len(afters),
        out_specs=tuple([HBM] * ni),
        input_output_aliases={n + i: i for i in range(ni)},
        compiler_params=pltpu.CompilerParams(has_side_effects=EFFECT),
    )(*srcs, *lands, *flat_sems, *afters)
    return list(out)


def _slot_sum_call(landed):
    n = len(landed)

    def body(*refs):
        for src, dst in zip(refs[:n], refs[n:]):
            tot = src[0]
            for j in range(1, N_DEV):
                tot = tot + src[j]
            dst[...] = tot

    vm = pl.BlockSpec(memory_space=pltpu.VMEM)
    return pl.pallas_call(
        body, name="small_grads_sum", in_specs=[vm] * n, out_specs=[vm] * n,
        out_shape=[jax.ShapeDtypeStruct(a.shape[1:], F32) for a in landed],
        compiler_params=pltpu.CompilerParams(vmem_limit_bytes=V7X_VMEM_LIMIT),
    )(*landed)


def _adamw(w, g, m, v):
    m = ADAM_B1 * m + (1.0 - ADAM_B1) * g
    v = ADAM_B2 * v + (1.0 - ADAM_B2) * (g * g)
    m_hat = m / (1.0 - ADAM_B1 ** ADAM_STEP)
    v_hat = v / (1.0 - ADAM_B2 ** ADAM_STEP)
    delta = -ADAM_LR * (m_hat / (jnp.sqrt(v_hat) + ADAM_EPS) + ADAM_WD * w)
    return delta, m, v


def _reduce_adamw_call(recvs, w, m, v, name):
    nl = len(recvs)
    _, r, c = recvs[0].shape
    rb = 256 if r % 256 == 0 else r
    nb = r // rb

    def body(*refs):
        recv_refs = refs[:nl]
        w_ref, m_ref, v_ref, g_ref, d_ref, nm_ref, nv_ref = refs[nl:]
        for l in range(nl):
            @pl.when(pl.program_id(0) == l)
            def _(l=l):
                g = recv_refs[l][0].astype(F32)
                for j in range(1, N_DEV):
                    g = g + recv_refs[l][j].astype(F32)
                delta, nm, nv = _adamw(w_ref[0], g, m_ref[0], v_ref[0])
                g_ref[0] = g
                d_ref[0] = delta
                nm_ref[0] = nm
                nv_ref[0] = nv

    def recv_spec(l):
        return pl.BlockSpec((N_DEV, rb, c), lambda lg, i: (0, jnp.where(lg == l, i, jnp.where(lg < l, 0, nb - 1)), 0))

    blk = pl.BlockSpec((1, rb, c), lambda lg, i: (lg, i, 0))
    shp = jax.ShapeDtypeStruct((nl, r, c), F32)
    return pl.pallas_call(
        body, name=name, grid=(nl, nb),
        in_specs=[recv_spec(l) for l in range(nl)] + [blk, blk, blk],
        out_specs=[blk, blk, blk, blk], out_shape=[shp, shp, shp, shp],
        compiler_params=pltpu.CompilerParams(dimension_semantics=("arbitrary", "arbitrary"), vmem_limit_bytes=V7X_VMEM_LIMIT),
    )(*recvs, w, m, v)


def _small_adamw_call(w, g, m, v):
    def body(w_ref, g_ref, m_ref, v_ref, d_ref, nm_ref, nv_ref):
        delta, nm, nv = _adamw(w_ref[...], g_ref[...], m_ref[...], v_ref[...])
        d_ref[...] = delta
        nm_ref[...] = nm
        nv_ref[...] = nv

    shp = jax.ShapeDtypeStruct(w.shape, F32)
    vm = pl.BlockSpec(memory_space=pltpu.VMEM)
    return pl.pallas_call(body, name="small_adamw", in_specs=[vm] * 4, out_specs=[vm] * 3, out_shape=[shp, shp, shp],
                          compiler_params=pltpu.CompilerParams(vmem_limit_bytes=V7X_VMEM_LIMIT))(w, g, m, v)


_GATHER_MODE = dict(w_in="block", w_out="block", w_ff1="cols_in", w_ff2="block", w_ple_gate="block", w_ple_proj="cols_in")
_SCATTER_MODE = dict(w_in="slot", w_out="slot", w_ff1="cols_out", w_ff2="slot", w_ple_gate="slot", w_ple_proj="cols_out")
_GROUP_A = ("w_in", "w_out")
_GROUP_B = ("w_ff1", "w_ff2", "w_ple_gate", "w_ple_proj")


def _gathered_full(k, landed):
    if _GATHER_MODE[k] == "cols_in":
        return landed
    n, r, c = landed.shape
    return landed.reshape(n * r, c)


def _grad_send(k, g):
    if _SCATTER_MODE[k] == "cols_out":
        return g
    r8, c = g.shape
    return g.reshape(N_DEV, r8 // N_DEV, c)


_SMALL_ORDER = ("norm_mix_g", "sgu_w", "sgu_b", "sgu_ln_g", "sgu_ln_b", "conv_w", "pool_w", "pool_scale",
                "norm_ff_g", "norm_ple_g", "final_g")


def _pack_small(d, order=_SMALL_ORDER):
    pieces, layout = [], []
    for k in order:
        flat = d[k].reshape(-1)
        n = flat.shape[0]
        pad = (-n) % 128
        pieces.append(jnp.pad(flat, (0, pad)))
        layout.append((k, d[k].shape, n, n + pad))
    flat = jnp.concatenate(pieces)
    pad = (-flat.shape[0]) % 1024
    return jnp.pad(flat, (0, pad)).reshape(-1, 128), layout


def _unpack_small(buf, layout):
    flat = buf.reshape(-1)
    out, off = {}, 0
    for k, shape, n, padded in layout:
        out[k] = flat[off:off + n].reshape(shape)
        off += padded
    return out


def kernel(x, p, norm_mix_g, w_in, sgu_w, sgu_b, sgu_ln_g, sgu_ln_b, conv_w, pool_w, pool_scale, w_out, norm_ff_g, w_ff1, w_ff2, norm_ple_g, w_ple_gate, w_ple_proj, final_g, loss_target, m_norm_mix_g, m_w_in, m_sgu_w, m_sgu_b, m_sgu_ln_g, m_sgu_ln_b, m_conv_w, m_pool_w, m_pool_scale, m_w_out, m_norm_ff_g, m_w_ff1, m_w_ff2, m_norm_ple_g, m_w_ple_gate, m_w_ple_proj, m_final_g, v_norm_mix_g, v_w_in, v_sgu_w, v_sgu_b, v_sgu_ln_g, v_sgu_ln_b, v_conv_w, v_pool_w, v_pool_scale, v_w_out, v_norm_ff_g, v_w_ff1, v_w_ff2, v_norm_ple_g, v_w_ple_gate, v_w_ple_proj, v_final_g):
    t = x.shape[1]
    xc, yc_, cc = _coords()
    me = 4 * xc + 2 * yc_ + cc
    tm = lambda want: min(want, t)

    shard_names = _GROUP_A + _GROUP_B
    swap = lambda a: jnp.transpose(a, (0, 2, 1))
    shard = dict(w_in=swap(w_in), w_out=w_out, w_ff1=w_ff1, w_ff2=w_ff2, w_ple_gate=w_ple_gate, w_ple_proj=w_ple_proj)
    conv_pad = jnp.zeros((16, 128), F32).at[0:DEPTH * 3, 0:D_B // N_DEV].set(conv_w.reshape(DEPTH * 3, D_B // N_DEV))
    ag_srcs = [shard[k].astype(BF16) for k in shard_names] + [conv_pad]
    ag_items, ag_modes, ag_groups = [], [], []
    for l in range(DEPTH):
        for names in (_GROUP_A, _GROUP_B):
            ag_groups.append(list(range(len(ag_items), len(ag_items) + len(names))))
            ag_items += [(shard_names.index(k), l) for k in names]
            ag_modes += [_GATHER_MODE[k] for k in names]
            if l == 0 and names is _GROUP_A:
                ag_groups[-1].append(len(ag_items))
                ag_items.append((len(shard_names), None))
                ag_modes.append("block")
    ag_sems, ag_lands, _ = _exchange_start(ag_srcs, ag_items, ag_modes, ag_groups, "weights_gather_start")

    def gathered(l, which, after):
        idxs = ag_groups[2 * l + which]
        landed = _exchange_wait([ag_sems[2 * l + which]], ag_srcs, [ag_items[i] for i in idxs], [ag_lands[i] for i in idxs],
                                [ag_modes[i] for i in idxs], [list(range(len(idxs)))], after, f"weights_gather_wait_{l}_{which}")
        full = {k: _gathered_full(k, got) for k, got in zip((_GROUP_A, _GROUP_B)[which], landed)}
        if l == 0 and which == 0:
            full["conv_w"] = jnp.transpose(landed[-1][:, 0:DEPTH * 3, 0:D_B // N_DEV].reshape(N_DEV, DEPTH, 3, D_B // N_DEV),
                                           (1, 2, 0, 3)).reshape(DEPTH, 3, D_B)
        return full

    idx = jnp.arange(D_A)
    pmat = ((idx[:, None] // 64) == (idx[None, :] // 64)).astype(BF16) * (1.0 / 64.0)
    pmat = pmat.astype(BF16)
    tril = jnp.tril(jnp.ones((CHUNK, CHUNK), F32))
    trilcat = jnp.concatenate([tril, tril], axis=1)
    headsel = ((idx[:, None] // 64) == jnp.arange(CHUNK)[None, :]).astype(BF16)
    row = lambda a: a.reshape(1, -1)

    def mix_params(l):
        wm = sgu_w[l] * tril[None]
        wcat = jnp.stack([jnp.concatenate([wm[2 * j], wm[2 * j + 1]], axis=1) for j in range(3)]).astype(BF16)
        wtcat = jnp.stack([jnp.concatenate([wm[2 * j].T, wm[2 * j + 1].T], axis=1) for j in range(3)]).astype(BF16)
        bmat = jnp.repeat(sgu_b[l].T, 64, axis=1)
        bd = jnp.zeros((D_C, D_C), F32)
        for gi in range(4):
            bd = bd.at[gi * 64:(gi + 1) * 64, gi * 64:(gi + 1) * 64].set(pool_w[l, gi])
        mp = dict(pmat=pmat, ln_g=row(sgu_ln_g[l]), ln_b=row(sgu_ln_b[l]), wcat=wcat, bmat=bmat, conv_w=conv_full[l],
                  bd=bd.astype(BF16), pool_scale=row(pool_scale[l]))
        return mp, wtcat

    xs = x.reshape(t, D)
    p_layers = p.reshape(DEPTH, t, D_PLE)
    saved, full_w = [], []
    conv_full = None
    for l in range(DEPTH):
        wa = gathered(l, 0, xs)
        if l == 0:
            conv_full = wa["conv_w"]
        mp, _ = mix_params(l)
        x1, proj, sgu_saved, xh0, rs0 = _mix_fwd_call(xs, row(norm_mix_g[l]), wa["w_in"], wa["w_out"], mp, tm(TM_MIX_FWD), tm(RUN_MIX_FWD), l)
        wb = gathered(l, 1, x1)
        head = (loss_target.reshape(t, D), row(final_g)) if l == DEPTH - 1 else None
        x3, r, gate, q, xh1, rs1, xh2, rs2, *head_out = _ffn_fwd_call(
            x1, p_layers, row(norm_ff_g[l]), row(norm_ple_g[l]), wb["w_ff1"], wb["w_ff2"], wb["w_ple_gate"], wb["w_ple_proj"],
            tm(TM_FFN_FWD), l, head)
        saved.append((proj, sgu_saved, r, (gate, q), (xh0, rs0), (xh1, rs1), (xh2, rs2)))
        full_w.append({**wa, **wb})
        xs = x3

    dx, (sq, dfinal) = xs, head_out
    loss = lax.psum(jnp.sum(sq) * (0.5 / D), ("x", "y", "c"))

    layer_keys = tuple(k for k in _SMALL_ORDER if k != "final_g")
    small = {k: [None] * DEPTH for k in layer_keys}
    small_ex, small_layout = [None] * DEPTH, None
    ex = {}
    token = jnp.zeros((8, 128), F32)

    for l in reversed(range(DEPTH)):
        proj, sgu_saved, r, gates, norm0, norm1, norm2 = saved[l]
        fw = full_w[l]
        dx2, dgple, dwg, dwp = _ple_bwd_call(dx, *norm2, *gates, p_layers, row(norm_ple_g[l]), fw["w_ple_gate"], token,
                                             tm(TM_PLE_BWD), l)
        da, dw2 = _ffn_bwd_hidden_call(dx2, r, fw["w_ff2"], tm(TM_FFN_BWD), l)
        dx1, dgff, dw1 = _ffn_bwd_input_call(da, *norm1, dx2, row(norm_ff_g[l]), fw["w_ff1"], tm(TM_FFN_BWD), l)
        sends = [_grad_send(k, g_) for k, g_ in zip(_GROUP_B, (dw1, dw2, dwg, dwp))]
        sems, lands, token = _exchange_start(sends, [(i, None) for i in range(len(sends))], [_SCATTER_MODE[k] for k in _GROUP_B],
                                             [list(range(len(sends)))], f"grads_exchange_start_{l}_1")
        ex[(l, 1)] = (sems[0], sends, lands)
        mp, wtcat = mix_params(l)
        (dx, dgmix, dwcat, dsb, dlng, dlnb, dconv, dbd, dscale, dwin, dwout) = _mix_bwd_call(
            dx1, *norm0, proj, sgu_saved, row(norm_mix_g[l]), fw["w_in"], fw["w_out"], mp, wtcat, trilcat, headsel, token,
            tm(TM_MIX_BWD), tm(RUN_MIX_BWD), l)
        small["norm_mix_g"][l] = dgmix[0]
        small["sgu_w"][l] = jnp.stack([dwcat[h // 2][:, (h % 2) * CHUNK:(h % 2 + 1) * CHUNK] for h in range(6)])
        small["sgu_b"][l] = dsb[:, 0:6].T
        small["sgu_ln_g"][l], small["sgu_ln_b"][l] = dlng[0], dlnb[0]
        small["conv_w"][l] = dconv
        small["pool_w"][l] = jnp.stack([dbd[gi * 64:(gi + 1) * 64, gi * 64:(gi + 1) * 64] for gi in range(4)])
        small["pool_scale"][l] = dscale[0]
        small["norm_ff_g"][l], small["norm_ple_g"][l] = dgff[0], dgple[0]
        layer_small = {k: small[k][l] for k in layer_keys}
        layer_small["final_g"] = dfinal[0] if l == DEPTH - 1 else jnp.zeros_like(dfinal[0])
        sbuf, small_layout = _pack_small(layer_small, layer_keys + ("final_g",))
        sends = [_grad_send("w_in", dwin), _grad_send("w_out", dwout)]
        sems, lands, token = _exchange_start([sbuf] + sends, [(i, None) for i in range(3)],
                                             ["block", _SCATTER_MODE["w_in"], _SCATTER_MODE["w_out"]], [[0], [1, 2]],
                                             f"grads_exchange_start_{l}_0")
        small_ex[l] = (sems[0], sbuf, lands[0])
        ex[(l, 0)] = (sems[1], sends, lands[1:3])
    grad_x = dx.reshape(1, t, D)

    state = dict(w_in=(swap(w_in), swap(m_w_in), swap(v_w_in)), w_out=(w_out, m_w_out, v_w_out), w_ff1=(w_ff1, m_w_ff1, v_w_ff1),
                 w_ff2=(w_ff2, m_w_ff2, v_w_ff2), w_ple_gate=(w_ple_gate, m_w_ple_gate, v_w_ple_gate),
                 w_ple_proj=(w_ple_proj, m_w_ple_proj, v_w_ple_proj))
    res = {}

    def finish_group(which, after):
        names = (_GROUP_A, _GROUP_B)[which]
        n = len(names)
        sems = [ex[(l, which)][0] for l in range(DEPTH)]
        sends = [s_ for l in range(DEPTH) for s_ in ex[(l, which)][1]]
        lands = [a_ for l in range(DEPTH) for a_ in ex[(l, which)][2]]
        landed = _exchange_wait(sems, sends, [(i, None) for i in range(DEPTH * n)], lands, [_SCATTER_MODE[k] for k in names] * DEPTH,
                                [list(range(l * n, (l + 1) * n)) for l in range(DEPTH)], after, f"grads_exchange_wait_{which}")
        for i, k in enumerate(names):
            res[k] = _reduce_adamw_call([landed[l * n + i] for l in range(DEPTH)], *state[k], "reduce_adamw_" + k)

    finish_group(1, dx)

    landed = _exchange_wait([small_ex[l][0] for l in range(DEPTH)], [small_ex[l][1] for l in range(DEPTH)],
                            [(l, None) for l in range(DEPTH)], [small_ex[l][2] for l in range(DEPTH)], ["block"] * DEPTH,
                            [[l] for l in range(DEPTH)], res[_GROUP_B[-1]][0], "small_grads_wait")
    sums = [_unpack_small(b_, small_layout) for b_ in _slot_sum_call(landed)]
    gs = {k: jnp.stack([sums[l][k] for l in range(DEPTH)]) for k in layer_keys}
    gs["final_g"] = sums[DEPTH - 1]["final_g"]
    conv_cols = lambda a: lax.dynamic_slice_in_dim(a, me * (D_B // N_DEV), D_B // N_DEV, axis=2)
    pad_conv = lambda a: jnp.zeros((DEPTH, 3, D_B), F32).at[:, :, 0:D_B // N_DEV].set(a)
    small_w = dict(norm_mix_g=norm_mix_g, sgu_w=sgu_w, sgu_b=sgu_b, sgu_ln_g=sgu_ln_g, sgu_ln_b=sgu_ln_b, conv_w=pad_conv(conv_w),
                   pool_w=pool_w, pool_scale=pool_scale, norm_ff_g=norm_ff_g, norm_ple_g=norm_ple_g, final_g=final_g)
    small_m = dict(norm_mix_g=m_norm_mix_g, sgu_w=m_sgu_w, sgu_b=m_sgu_b, sgu_ln_g=m_sgu_ln_g, sgu_ln_b=m_sgu_ln_b,
                   conv_w=pad_conv(m_conv_w), pool_w=m_pool_w, pool_scale=m_pool_scale, norm_ff_g=m_norm_ff_g,
                   norm_ple_g=m_norm_ple_g, final_g=m_final_g)
    small_v = dict(norm_mix_g=v_norm_mix_g, sgu_w=v_sgu_w, sgu_b=v_sgu_b, sgu_ln_g=v_sgu_ln_g, sgu_ln_b=v_sgu_ln_b,
                   conv_w=pad_conv(v_conv_w), pool_w=v_pool_w, pool_scale=v_pool_scale,
                   norm_ff_g=v_norm_ff_g, norm_ple_g=v_norm_ple_g, final_g=v_final_g)
    gs_local = dict(gs)
    gs_local["conv_w"] = pad_conv(conv_cols(gs["conv_w"]))
    g_loc, layout = _pack_small(gs_local)
    wbuf, _ = _pack_small(small_w)
    mbuf, _ = _pack_small(small_m)
    vbuf, _ = _pack_small(small_v)
    dbuf, nmbuf, nvbuf = _small_adamw_call(wbuf, g_loc, mbuf, vbuf)
    sd, sm, sv = _unpack_small(dbuf, layout), _unpack_small(nmbuf, layout), _unpack_small(nvbuf, layout)
    unconv = lambda a: a[:, :, 0:D_B // N_DEV]
    for dct in (gs_local, sd, sm, sv):
        dct["conv_w"] = unconv(dct["conv_w"])

    finish_group(0, [res[_GROUP_B[-1]][0], dbuf])

    order = ["norm_mix_g", "w_in", "sgu_w", "sgu_b", "sgu_ln_g", "sgu_ln_b", "conv_w", "pool_w", "pool_scale", "w_out",
             "norm_ff_g", "w_ff1", "w_ff2", "norm_ple_g", "w_ple_gate", "w_ple_proj", "final_g"]
    outs = [loss, grad_x]
    for which in range(4):
        for k in order:
            if k in res:
                outs.append(swap(res[k][which]) if k == "w_in" else res[k][which])
            else:
                outs.append((gs_local, sd, sm, sv)[which][k])
    return tuple(outs)
```
